```python
import math
import jax
import jax.numpy as jnp
from jax import lax
import numpy as np

D_MODEL = 1024
BATCH = 8
SEQ = 2048
DEPTH = 1

CTX_LEN = 256
GRID_W = 64
MIX_WIDTH = D_MODEL
HY_WIDTH = MIX_WIDTH // 2
HY_ORDER = 2
HY_EMB_BANDS = 16
HY_EMB_DIM = 1 + 2 * HY_EMB_BANDS
HY_FILTER_WIDTH = 64
HY_FAST_DECAY = 0.3
HY_SLOW_DECAY = 1.5
HY_TARGET = 1e-2
HY_FILTER_INIT = 0.01
RET_WIDTH = MIX_WIDTH - HY_WIDTH
RET_HEADS = 4
RET_QK_DIM = 64
RET_V_DIM = RET_WIDTH // RET_HEADS
RET_CHUNK = 128
ROPE_BASE = 10000.0
D_FF = 2816
EPS = 1e-6

HY_COLS = (HY_ORDER + 1) * HY_WIDTH
RET_QK_WIDTH = RET_HEADS * RET_QK_DIM
Q_OFF = HY_COLS
K_OFF = Q_OFF + RET_QK_WIDTH
V_OFF = K_OFF + RET_QK_WIDTH
G_OFF = V_OFF + RET_WIDTH
IN_COLS = G_OFF + RET_WIDTH

kernel_name = 'hyena_retention_hybrid_dit'


def rms_norm(x, gain):
    xf = x.astype(jnp.float32)
    y = xf * lax.rsqrt(jnp.mean(xf * xf, axis=-1, keepdims=True) + EPS)
    return (y * gain.astype(jnp.float32)).astype(x.dtype)


def modulate(h, shift, scale):
    return h * (1.0 + scale) + shift


def depthwise_conv3(x, w, b):
    seq = x.shape[1]
    xp = jnp.pad(x, ((0, 0), (1, 1), (0, 0)))
    return xp[:, :seq] * w[0] + xp[:, 1:seq + 1] * w[1] + xp[:, 2:] * w[2] + b


def hyena_filters(seq, w1, b1, f1, w2, b2, f2, w3):
    pos = jnp.arange(seq, dtype=jnp.float32)[:, None]
    t = jnp.linspace(0.0, 1.0, seq, dtype=jnp.float32)[:, None]
    bands = jnp.linspace(1e-4, HY_EMB_BANDS - 1, HY_EMB_BANDS, dtype=jnp.float32)[None, :]
    ang = 2.0 * math.pi * pos * bands / seq
    z = jnp.concatenate([t, jnp.cos(ang), -jnp.sin(ang)], axis=-1)
    hid = jnp.sin(f1 * (z @ w1 + b1))
    hid = jnp.sin(f2 * (hid @ w2 + b2))
    filt = (hid @ w3).reshape(seq, 2, HY_ORDER, HY_WIDTH)
    max_decay = math.log(HY_TARGET) / HY_FAST_DECAY
    min_decay = math.log(HY_TARGET) / HY_SLOW_DECAY
    deltas = jnp.linspace(min_decay, max_decay, HY_WIDTH, dtype=jnp.float32)
    window = jnp.exp(-t * jnp.abs(deltas)[None, :])
    return filt * window[:, None, None, :]


def bidirectional_fftconv(u, k_fwd, k_bwd, skip):
    seq = u.shape[1]
    kern = jnp.concatenate([k_fwd, jnp.zeros_like(k_fwd[:1]), k_bwd[:0:-1]], axis=0)
    k_f = jnp.fft.rfft(kern.astype(jnp.float32), n=2 * seq, axis=0)
    u32 = u.astype(jnp.float32)
    u_f = jnp.fft.rfft(u32, n=2 * seq, axis=1)
    y = jnp.fft.irfft(u_f * k_f[None], n=2 * seq, axis=1)[:, :seq]
    return (y + u32 * skip.astype(jnp.float32)).astype(u.dtype)


def hyena_mixer(z, p):
    z = depthwise_conv3(z, p['hy_conv_w'], p['hy_conv_b'])
    v, *gates = jnp.split(z, HY_ORDER + 1, axis=-1)
    filt = hyena_filters(z.shape[1], p['hy_w1'], p['hy_b1'], p['hy_f1'],
                         p['hy_w2'], p['hy_b2'], p['hy_f2'], p['hy_w3'])
    y = v
    for o in range(HY_ORDER):
        y = gates[o] * bidirectional_fftconv(y, filt[:, 0, o], filt[:, 1, o], p['hy_bias'][o])
    return y


def split_heads(t):
    b, seq, _ = t.shape
    return t.reshape(b, seq, RET_HEADS, -1).transpose(0, 2, 1, 3)


def rope_2d(x, row, col):
    half = x.shape[-1] // 2
    quarter = half // 2
    inv_freq = ROPE_BASE ** (-jnp.arange(quarter, dtype=jnp.float32) / quarter)
    ang = jnp.concatenate([row[:, None] * inv_freq, col[:, None] * inv_freq], axis=-1)
    cos, sin = jnp.cos(ang), jnp.sin(ang)
    x1, x2 = x[..., :half], x[..., half:]
    return jnp.concatenate([x1 * cos - x2 * sin, x1 * sin + x2 * cos], axis=-1)


def retention_chunkwise(q, k, v, log_gamma, s0, inclusive):
    b, h, seq, dk = q.shape
    dv = v.shape[-1]
    n_chunks = seq // RET_CHUNK
    qc = q.astype(jnp.float32).reshape(b, h, n_chunks, RET_CHUNK, dk)
    kc = k.astype(jnp.float32).reshape(b, h, n_chunks, RET_CHUNK, dk)
    vc = v.astype(jnp.float32).reshape(b, h, n_chunks, RET_CHUNK, dv)
    pos = jnp.arange(RET_CHUNK, dtype=jnp.float32)
    diff = pos[:, None] - pos[None, :]
    mask = (diff >= 0) if inclusive else (diff > 0)
    lg = log_gamma.astype(jnp.float32)[:, None]
    intra = jnp.where(mask[None], jnp.exp(lg[:, :, None] * jnp.maximum(diff, 0.0)[None]), 0.0)
    scores = jnp.einsum('bhnid,bhnjd->bhnij', qc, kc) * intra[None, :, None]
    o_intra = jnp.einsum('bhnij,bhnje->bhnie', scores, vc)
    k_w = jnp.exp(lg * (RET_CHUNK - 1.0 - pos))
    kv = jnp.einsum('bhnjd,hj,bhnje->nbhde', kc, k_w, vc)
    chunk_decay = jnp.exp(lg * RET_CHUNK)[None, :, :, None]

    def step(state, kv_n):
        return state * chunk_decay + kv_n, state

    _, s_prev = lax.scan(step, s0.astype(jnp.float32), kv)
    q_w = jnp.exp(lg * (pos + 1.0))
    o_cross = jnp.einsum('bhnid,nbhde,hi->bhnie', qc, s_prev, q_w)
    return (o_intra + o_cross).reshape(b, h, seq, dv)


def retention_final_state(k, v, log_gamma, reverse):
    seq = k.shape[2]
    pos = jnp.arange(seq, dtype=jnp.float32)
    steps = pos if reverse else (seq - 1.0 - pos)
    w = jnp.exp(log_gamma.astype(jnp.float32)[:, None] * steps[None, :])
    return jnp.einsum('bhld,hl,bhle->bhde', k.astype(jnp.float32), w, v.astype(jnp.float32))


def bidirectional_retention(q, k, v, lg_f, lg_b, s_f, s_b):
    o_f = retention_chunkwise(q, k, v, lg_f, s_f, True)
    o_b = retention_chunkwise(q[:, :, ::-1], k[:, :, ::-1], v[:, :, ::-1], lg_b, s_b, False)
    return o_f + o_b[:, :, ::-1]


def retention_output(o, g):
    of = o.astype(jnp.float32)
    of = of * lax.rsqrt(jnp.mean(of * of, axis=-1, keepdims=True) + EPS)
    b, h, seq, dv = o.shape
    of = of.transpose(0, 2, 1, 3).reshape(b, seq, h * dv)
    return (jax.nn.silu(g.astype(jnp.float32)) * of).astype(g.dtype)


def conv_ffn(h, p):
    a = depthwise_conv3(h @ p['ffn_w_up'], p['ffn_conv_w'], p['ffn_conv_b'])
    val, gate = jnp.split(a, 2, axis=-1)
    return (jax.nn.silu(gate) * val) @ p['ffn_w_down']


def trunk_layer(x, xc, c, c_ctx, p, ctx_out):
    mod = jax.nn.silu(c) @ p['w_mod'] + p['b_mod']
    mod_c = jax.nn.silu(c_ctx) @ p['w_mod'] + p['b_mod']
    sh1, sc1, g1, sh2, sc2, g2 = jnp.split(mod[:, None, :], 6, axis=-1)
    shc1, scc1, gc1, shc2, scc2, gc2 = jnp.split(mod_c, 6)
    seq = x.shape[1]
    rows = seq // GRID_W
    row = jnp.broadcast_to(jnp.arange(rows, dtype=jnp.float32)[:, None], (rows, GRID_W)).reshape(seq)
    col = jnp.broadcast_to(jnp.arange(GRID_W, dtype=jnp.float32)[None, :], (rows, GRID_W)).reshape(seq)
    lg_f = jax.nn.log_sigmoid(p['ret_logit_f'].astype(jnp.float32))
    lg_b = jax.nn.log_sigmoid(p['ret_logit_b'].astype(jnp.float32))
    k_scale = RET_QK_DIM ** -0.5

    h = modulate(rms_norm(x, p['norm1']), sh1, sc1)
    hc = modulate(rms_norm(xc, p['norm1']), shc1, scc1)
    u = h @ p['w_in']
    w_c = p['w_in'] if ctx_out else p['w_in'][:, K_OFF:G_OFF]
    off = 0 if ctx_out else K_OFF
    uc = hc @ w_c
    kc = split_heads(uc[..., K_OFF - off:V_OFF - off]) * k_scale
    vc = split_heads(uc[..., V_OFF - off:G_OFF - off])
    s_f = retention_final_state(kc, vc, lg_f, False)
    s_b = retention_final_state(kc, vc, lg_b, True)

    q = rope_2d(split_heads(u[..., Q_OFF:K_OFF]), row, col)
    k = rope_2d(split_heads(u[..., K_OFF:V_OFF]), row, col) * k_scale
    v = split_heads(u[..., V_OFF:G_OFF])
    y_hy = hyena_mixer(u[..., :HY_COLS], p)
    y_ret = retention_output(bidirectional_retention(q, k, v, lg_f, lg_b, s_f, s_b), u[..., G_OFF:])
    x = x + g1 * (jnp.concatenate([y_hy, y_ret], axis=-1) @ p['w_out'])

    if ctx_out:
        qc = split_heads(uc[..., Q_OFF:K_OFF])
        zero = jnp.zeros_like(s_f)
        yc_hy = hyena_mixer(uc[..., :HY_COLS], p)
        yc_ret = retention_output(bidirectional_retention(qc, kc, vc, lg_f, lg_b, zero, zero), uc[..., G_OFF:])
        xc = xc + gc1 * (jnp.concatenate([yc_hy, yc_ret], axis=-1) @ p['w_out'])

    x = x + g2 * conv_ffn(modulate(rms_norm(x, p['norm2']), sh2, sc2), p)
    if ctx_out:
        xc = xc + gc2 * conv_ffn(modulate(rms_norm(xc, p['norm2']), shc2, scc2), p)
    return x, xc


def setup_inputs(seed: int = 0) -> dict:
    key = jax.random.key(seed)
    ks = jax.random.split(key, 27)

    def nrm(k, shape, scale):
        return jax.random.normal(k, shape, jnp.float32) * scale

    base_logit = jnp.log(2.0 ** (5.0 + jnp.arange(RET_HEADS, dtype=jnp.float32)) - 1.0)
    return {
        'x': nrm(ks[0], (BATCH, SEQ, D_MODEL), 1.0),
        'c': nrm(ks[1], (BATCH, D_MODEL), 1.0),
        'ctx': nrm(ks[2], (BATCH, CTX_LEN, D_MODEL), 1.0),
        'c_ctx': nrm(ks[3], (D_MODEL,), 1.0),
        'w_mod': nrm(ks[4], (DEPTH, D_MODEL, 6 * D_MODEL), 0.5 * D_MODEL ** -0.5),
        'b_mod': nrm(ks[5], (DEPTH, 6 * D_MODEL), 0.02),
        'norm1': 1.0 + nrm(ks[6], (DEPTH, D_MODEL), 0.02),
        'w_in': nrm(ks[7], (DEPTH, D_MODEL, IN_COLS), D_MODEL ** -0.5),
        'hy_conv_w': nrm(ks[8], (DEPTH, 3, HY_COLS), 3 ** -0.5),
        'hy_conv_b': nrm(ks[9], (DEPTH, HY_COLS), 0.02),
        'hy_w1': nrm(ks[10], (DEPTH, HY_EMB_DIM, HY_FILTER_WIDTH), 2.0 * HY_EMB_DIM ** -0.5),
        'hy_b1': nrm(ks[11], (DEPTH, HY_FILTER_WIDTH), 0.02),
        'hy_f1': 1.0 + nrm(ks[12], (DEPTH, HY_FILTER_WIDTH), 0.02),
        'hy_w2': nrm(ks[13], (DEPTH, HY_FILTER_WIDTH, HY_FILTER_WIDTH), 2.0 * HY_FILTER_WIDTH ** -0.5),
        'hy_b2': nrm(ks[14], (DEPTH, HY_FILTER_WIDTH), 0.02),
        'hy_f2': 1.0 + nrm(ks[15], (DEPTH, HY_FILTER_WIDTH), 0.02),
        'hy_w3': nrm(ks[16], (DEPTH, HY_FILTER_WIDTH, 2 * HY_ORDER * HY_WIDTH), HY_FILTER_INIT),
        'hy_bias': nrm(ks[17], (DEPTH, HY_ORDER, HY_WIDTH), 0.5),
        'ret_logit_f': base_logit + nrm(ks[18], (DEPTH, RET_HEADS), 0.05),
        'ret_logit_b': base_logit + nrm(ks[19], (DEPTH, RET_HEADS), 0.05),
        'w_out': nrm(ks[20], (DEPTH, MIX_WIDTH, D_MODEL), MIX_WIDTH ** -0.5),
        'norm2': 1.0 + nrm(ks[21], (DEPTH, D_MODEL), 0.02),
        'ffn_w_up': nrm(ks[22], (DEPTH, D_MODEL, 2 * D_FF), D_MODEL ** -0.5),
        'ffn_conv_w': nrm(ks[23], (DEPTH, 3, 2 * D_FF), 3 ** -0.5),
        'ffn_conv_b': nrm(ks[24], (DEPTH, 2 * D_FF), 0.02),
        'ffn_w_down': nrm(ks[25], (DEPTH, D_FF, D_MODEL), D_FF ** -0.5),
        'norm_f': 1.0 + nrm(ks[26], (D_MODEL,), 0.02),
    }


def reference(x, c, ctx, c_ctx, w_mod, b_mod, norm1, w_in, hy_conv_w, hy_conv_b, hy_w1, hy_b1, hy_f1,
              hy_w2, hy_b2, hy_f2, hy_w3, hy_bias, ret_logit_f, ret_logit_b, w_out, norm2,
              ffn_w_up, ffn_conv_w, ffn_conv_b, ffn_w_down, norm_f):
    xc = ctx
    for layer in range(DEPTH):
        p = {
            'w_mod': w_mod[layer], 'b_mod': b_mod[layer], 'norm1': norm1[layer], 'w_in': w_in[layer],
            'hy_conv_w': hy_conv_w[layer], 'hy_conv_b': hy_conv_b[layer],
            'hy_w1': hy_w1[layer], 'hy_b1': hy_b1[layer], 'hy_f1': hy_f1[layer],
            'hy_w2': hy_w2[layer], 'hy_b2': hy_b2[layer], 'hy_f2': hy_f2[layer],
            'hy_w3': hy_w3[layer], 'hy_bias': hy_bias[layer],
            'ret_logit_f': ret_logit_f[layer], 'ret_logit_b': ret_logit_b[layer],
            'w_out': w_out[layer], 'norm2': norm2[layer],
            'ffn_w_up': ffn_w_up[layer], 'ffn_conv_w': ffn_conv_w[layer],
            'ffn_conv_b': ffn_conv_b[layer], 'ffn_w_down': ffn_w_down[layer],
        }
        x, xc = trunk_layer(x, xc, c, c_ctx, p, layer < DEPTH - 1)
    return rms_norm(x, norm_f)
```

```python
import functools
import math

import numpy as np
import jax
import jax.numpy as jnp
from jax import lax
from jax.experimental import pallas as pl
from jax.experimental.pallas import tpu as pltpu

F32 = jnp.float32
BF16 = jnp.bfloat16

D_MODEL = 1024
BATCH = 8
SEQ = 2048
CTX_LEN = 256
GRID_W = 64
HY_WIDTH = 512
HY_ORDER = 2
HY_EMB_BANDS = 16
HY_FILTER_WIDTH = 64
HY_FAST_DECAY = 0.3
HY_SLOW_DECAY = 1.5
HY_TARGET = 1e-2
RET_WIDTH = 512
RET_HEADS = 4
RET_QK_DIM = 64
RET_V_DIM = 128
ROPE_BASE = 10000.0
D_FF = 2816
EPS = 1e-6
HY_COLS = (HY_ORDER + 1) * HY_WIDTH
Q_OFF = HY_COLS
K_OFF = Q_OFF + RET_HEADS * RET_QK_DIM
V_OFF = K_OFF + RET_HEADS * RET_QK_DIM
G_OFF = V_OFF + RET_WIDTH
K_SCALE = RET_QK_DIM ** -0.5

MOD_ROWS = 16
TOK_TILE = 512
HALO = 16
CONV_BLOCK = 512
N_CONV_BLOCKS = SEQ // CONV_BLOCK
N_OFFSETS = 2 * N_CONV_BLOCKS - 1
FREQ_ROWS = 16
RET_CHUNK = 256
FFN_COLS = 256
VMEM_LIMIT = 56 * 1024 * 1024

_NT = (((1,), (1,)), ((), ()))


def _dot(a, b):
    return jnp.dot(a, b, preferred_element_type=F32)


def _dot_hi(a, b):
    return jnp.dot(a, b, preferred_element_type=F32, precision=lax.Precision.HIGHEST)


def _silu(x):
    return x * (1.0 / (1.0 + jnp.exp(-x)))


def _norm_mod(x, gain, shift, scale):
    y = x * lax.rsqrt(jnp.mean(x * x, axis=-1, keepdims=True) + EPS)
    return (y * gain) * (1.0 + scale) + shift


def _resident(shape):
    nd = len(shape)
    return pl.BlockSpec(shape, lambda *_: (0,) * nd, pipeline_mode=pl.Buffered(1))


@functools.lru_cache(maxsize=None)
def _rope_tables():
    pos = np.arange(SEQ)
    row = (pos // GRID_W).astype(np.float64)
    col = (pos % GRID_W).astype(np.float64)
    quarter = RET_QK_DIM // 4
    inv_freq = ROPE_BASE ** (-np.arange(quarter, dtype=np.float64) / quarter)
    ang = np.concatenate([row[:, None] * inv_freq, col[:, None] * inv_freq], axis=-1)
    cos, sin = np.cos(ang), np.sin(ang)
    cos_h = np.concatenate([cos, cos], axis=-1)
    sin_h = np.concatenate([-sin, sin], axis=-1)
    cos_t = np.tile(cos_h, (1, RET_HEADS))
    sin_t = np.tile(sin_h, (1, RET_HEADS))
    return (cos_t.astype(np.float32), sin_t.astype(np.float32),
            np.ascontiguousarray((cos_t * K_SCALE).T).astype(np.float32),
            np.ascontiguousarray((sin_t * K_SCALE).T).astype(np.float32))


@functools.lru_cache(maxsize=None)
def _filter_features():
    lag = np.abs(np.arange(2 * SEQ) - SEQ).astype(np.float64)
    t = lag / (SEQ - 1)
    bands = np.linspace(1e-4, HY_EMB_BANDS - 1, HY_EMB_BANDS)
    ang = 2.0 * math.pi * lag[:, None] * bands[None, :] / SEQ
    z = np.concatenate([t[:, None], np.cos(ang), -np.sin(ang)], axis=-1)
    zp = np.zeros((2 * SEQ, HY_FILTER_WIDTH), np.float64)
    zp[:, :z.shape[1]] = z
    max_decay = math.log(HY_TARGET) / HY_FAST_DECAY
    min_decay = math.log(HY_TARGET) / HY_SLOW_DECAY
    absdelta = np.abs(np.linspace(min_decay, max_decay, HY_WIDTH))[None, :]
    return zp.astype(np.float32), absdelta.astype(np.float32)


@functools.lru_cache(maxsize=None)
def _dft_matrices():
    p = CONV_BLOCK
    n = 2 * p
    f = np.arange(p, dtype=np.float64)[:, None] + 0.5
    t = np.arange(n, dtype=np.float64)[None, :]
    theta = 2.0 * math.pi * f * t / n
    fwd = np.concatenate([np.cos(theta), -np.sin(theta)], axis=0)
    th_out = theta[:, p:].T
    inv = np.concatenate([np.cos(th_out), -np.sin(th_out)], axis=1) / p
    return fwd.astype(np.float32), inv.astype(np.float32)


def _mod_kernel(c_ref, w_ref, b_ref, o_ref):
    s = _silu(c_ref[...]).astype(BF16)
    o_ref[...] = _dot(s, w_ref[...].astype(BF16)) + b_ref[...]


def _mod_call(cc, w_mod, b_mod):
    ncol = 6 * D_MODEL
    blk = ncol // 4
    return pl.pallas_call(
        _mod_kernel,
        grid=(ncol // blk,),
        in_specs=[pl.BlockSpec((MOD_ROWS, D_MODEL), lambda j: (0, 0)),
                  pl.BlockSpec((D_MODEL, blk), lambda j: (0, j)),
                  pl.BlockSpec((1, blk), lambda j: (0, j))],
        out_specs=pl.BlockSpec((MOD_ROWS, blk), lambda j: (0, j)),
        out_shape=jax.ShapeDtypeStruct((MOD_ROWS, ncol), F32),
        compiler_params=pltpu.CompilerParams(vmem_limit_bytes=VMEM_LIMIT),
        name="mod",
    )(cc, w_mod, b_mod)


def _ctx_kernel(lg_ref, x_ref, sh_ref, sc_ref, n1_ref, wkt_ref, wv_ref, s_ref):
    h = _norm_mod(x_ref[0], n1_ref[...], sh_ref[0], sc_ref[0]).astype(BF16)
    kt = lax.dot_general(wkt_ref[...], h, _NT, preferred_element_type=F32) * K_SCALE
    v = _dot(h, wv_ref[...])
    pos = lax.broadcasted_iota(jnp.int32, (1, CTX_LEN), 1).astype(F32)
    for hh in range(RET_HEADS):
        w_f = jnp.exp(lg_ref[0, hh] * (CTX_LEN - 1.0 - pos))
        w_b = jnp.exp(lg_ref[1, hh] * pos)
        kth = kt[hh * RET_QK_DIM:(hh + 1) * RET_QK_DIM, :]
        vh = v[:, hh * RET_V_DIM:(hh + 1) * RET_V_DIM].astype(BF16)
        s_ref[0, hh, 0:RET_QK_DIM, :] = _dot((kth * w_f).astype(BF16), vh)
        s_ref[0, hh, RET_QK_DIM:2 * RET_QK_DIM, :] = _dot((kth * w_b).astype(BF16), vh)


def _ctx_call(lg, ctx, mod3, norm1, w_kt, w_v):
    return pl.pallas_call(
        _ctx_kernel,
        grid=(BATCH,),
        in_specs=[pl.BlockSpec(memory_space=pltpu.SMEM),
                  pl.BlockSpec((1, CTX_LEN, D_MODEL), lambda b: (b, 0, 0)),
                  pl.BlockSpec((1, 1, D_MODEL), lambda b: (BATCH, 0, 0)),
                  pl.BlockSpec((1, 1, D_MODEL), lambda b: (BATCH, 0, 1)),
                  _resident((1, D_MODEL)),
                  _resident((RET_HEADS * RET_QK_DIM, D_MODEL)),
                  _resident((D_MODEL, RET_WIDTH))],
        out_specs=pl.BlockSpec((1, RET_HEADS, 2 * RET_QK_DIM, RET_V_DIM), lambda b: (b, 0, 0, 0)),
        out_shape=jax.ShapeDtypeStruct((BATCH, RET_HEADS, 2 * RET_QK_DIM, RET_V_DIM), F32),
        compiler_params=pltpu.CompilerParams(vmem_limit_bytes=VMEM_LIMIT),
        name="ctx",
    )(lg, ctx, mod3, mod3, norm1, w_kt, w_v)


def _swap_halves(x, axis):
    n = x.shape[axis]
    half = RET_QK_DIM // 2
    idx = lax.broadcasted_iota(jnp.int32, x.shape, axis)
    first = (idx & (RET_QK_DIM - 1)) < half
    return jnp.where(first, pltpu.roll(x, n - half, axis), pltpu.roll(x, half, axis))


def _inproj_kernel(x_ref, sh_ref, sc_ref, n1_ref, why_ref, wq_ref, wv_ref, wg_ref, wkt_ref,
                   cq_ref, sq_ref, ck_ref, sk_ref,
                   zhy_ref, q_ref, kt_ref, v_ref, g_ref):
    hb = _norm_mod(x_ref[0], n1_ref[...], sh_ref[0], sc_ref[0]).astype(BF16)
    zhy_ref[0] = _dot(hb, why_ref[...]).astype(BF16)
    v_ref[0] = _dot(hb, wv_ref[...]).astype(BF16)
    g_ref[0] = _dot(hb, wg_ref[...]).astype(BF16)
    q = _dot(hb, wq_ref[...])
    q = q * cq_ref[...] + _swap_halves(q, 1) * sq_ref[...]
    for hh in range(RET_HEADS):
        q_ref[0, hh] = q[:, hh * RET_QK_DIM:(hh + 1) * RET_QK_DIM].astype(BF16)
    kt = lax.dot_general(wkt_ref[...], hb, _NT, preferred_element_type=F32)
    kt = kt * ck_ref[...] + _swap_halves(kt, 0) * sk_ref[...]
    kt_ref[0] = kt.astype(BF16)


def _inproj_call(x, mod3, norm1, w_hy, w_q, w_v, w_g, w_kt, rope):
    cq, sq, ck, sk = rope
    t = TOK_TILE
    qk = RET_HEADS * RET_QK_DIM
    return pl.pallas_call(
        _inproj_kernel,
        grid=(BATCH, SEQ // t),
        in_specs=[pl.BlockSpec((1, t, D_MODEL), lambda b, i: (b, i, 0)),
                  pl.BlockSpec((1, 1, D_MODEL), lambda b, i: (b, 0, 0)),
                  pl.BlockSpec((1, 1, D_MODEL), lambda b, i: (b, 0, 1)),
                  _resident((1, D_MODEL)),
                  _resident((D_MODEL, HY_COLS)),
                  _resident((D_MODEL, qk)),
                  _resident((D_MODEL, RET_WIDTH)),
                  _resident((D_MODEL, RET_WIDTH)),
                  _resident((qk, D_MODEL)),
                  pl.BlockSpec((t, qk), lambda b, i: (i, 0)),
                  pl.BlockSpec((t, qk), lambda b, i: (i, 0)),
                  pl.BlockSpec((qk, t), lambda b, i: (0, i)),
                  pl.BlockSpec((qk, t), lambda b, i: (0, i))],
        out_specs=[pl.BlockSpec((1, t, HY_COLS), lambda b, i: (b, i, 0)),
                   pl.BlockSpec((1, RET_HEADS, t, RET_QK_DIM), lambda b, i: (b, 0, i, 0)),
                   pl.BlockSpec((1, qk, t), lambda b, i: (b, 0, i)),
                   pl.BlockSpec((1, t, RET_WIDTH), lambda b, i: (b, i, 0)),
                   pl.BlockSpec((1, t, RET_WIDTH), lambda b, i: (b, i, 0))],
        out_shape=[jax.ShapeDtypeStruct((BATCH, SEQ, HY_COLS), BF16),
                   jax.ShapeDtypeStruct((BATCH, RET_HEADS, SEQ, RET_QK_DIM), BF16),
                   jax.ShapeDtypeStruct((BATCH, qk, SEQ), BF16),
                   jax.ShapeDtypeStruct((BATCH, SEQ, RET_WIDTH), BF16),
                   jax.ShapeDtypeStruct((BATCH, SEQ, RET_WIDTH), BF16)],
        compiler_params=pltpu.CompilerParams(vmem_limit_bytes=VMEM_LIMIT),
        name="inproj",
    )(x, mod3, mod3, norm1, w_hy, w_q, w_v, w_g, w_kt, cq, sq, ck, sk)


def _filter_kernel(z_ref, w1_ref, b1_ref, f1_ref, w2_ref, b2_ref, f2_ref, w3b_ref, w3f_ref,
                   adel_ref, fwd_ref, g_ref, k_scr):
    rows = 512
    for r in range(2 * SEQ // rows):
        z = z_ref[r * rows:(r + 1) * rows, :]
        hid = jnp.sin(f1_ref[...] * (_dot_hi(z, w1_ref[...]) + b1_ref[...]))
        hid = jnp.sin(f2_ref[...] * (_dot_hi(hid, w2_ref[...]) + b2_ref[...]))
        w3 = w3b_ref[...] if r * rows < SEQ else w3f_ref[...]
        window = jnp.exp(-z[:, 0:1] * adel_ref[...])
        k_scr[r * rows:(r + 1) * rows, :] = (_dot_hi(hid, w3) * window).astype(BF16)
    p = CONV_BLOCK
    for e in range(N_OFFSETS):
        g_ref[0, e] = _dot(fwd_ref[...], k_scr[e * p:(e + 2) * p, :])


def _filter_call(zf, w1p, b1, f1, w2, b2, f2, w3, absdelta, fwd):
    p = CONV_BLOCK
    fw = HY_FILTER_WIDTH
    return pl.pallas_call(
        _filter_kernel,
        grid=(HY_ORDER,),
        in_specs=[_resident((2 * SEQ, fw)),
                  _resident((fw, fw)), _resident((1, fw)), _resident((1, fw)),
                  _resident((fw, fw)), _resident((1, fw)), _resident((1, fw)),
                  pl.BlockSpec((fw, HY_WIDTH), lambda o: (0, HY_ORDER + o)),
                  pl.BlockSpec((fw, HY_WIDTH), lambda o: (0, o)),
                  _resident((1, HY_WIDTH)),
                  _resident((2 * p, 2 * p))],
        out_specs=pl.BlockSpec((1, N_OFFSETS, 2 * p, HY_WIDTH), lambda o: (o, 0, 0, 0)),
        out_shape=jax.ShapeDtypeStruct((HY_ORDER, N_OFFSETS, 2 * p, HY_WIDTH), F32),
        scratch_shapes=[pltpu.VMEM((2 * SEQ, HY_WIDTH), BF16)],
        compiler_params=pltpu.CompilerParams(vmem_limit_bytes=VMEM_LIMIT),
        name="filters",
    )(zf, w1p, b1, f1, w2, b2, f2, w3, w3, absdelta, fwd)


def _conv3_rows(ref, j, n_blocks, rows, w, b):
    main = ref[0, j * rows:(j + 1) * rows, :].astype(F32)
    cols = main.shape[1]
    zeros = jnp.zeros((HALO, cols), F32)
    prev = ref[0, j * rows - HALO:j * rows, :].astype(F32) if j > 0 else zeros
    nxt = ref[0, (j + 1) * rows:(j + 1) * rows + HALO, :].astype(F32) if j < n_blocks - 1 else zeros
    ext = jnp.concatenate([prev, main, nxt], axis=0)
    n = rows + 2 * HALO
    before = pltpu.roll(ext, 1, 0)[HALO:HALO + rows]
    after = pltpu.roll(ext, n - 1, 0)[HALO:HALO + rows]
    return before * w[0:1] + main * w[1:2] + after * w[2:3] + b


def _hyena_kernel(u_ref, zg_ref, cwu_ref, cbu_ref, cwg_ref, cbg_ref, skip_ref, g_ref, fwd_ref, inv_ref,
                  o_ref, uf_scr, ut_scr, y_scr, *, conv_u):
    p = CONV_BLOCK
    nb = N_CONV_BLOCKS
    for j in range(nb):
        if conv_u:
            uj = _conv3_rows(u_ref, j, nb, p, cwu_ref[...], cbu_ref[...])
        else:
            uj = u_ref[0, j * p:(j + 1) * p, :].astype(F32)
        ut_scr[j * p:(j + 1) * p, :] = uj
        uf_scr[j] = _dot(fwd_ref[...], uj.astype(BF16))

    for i in range(nb):
        def freq_step(r, carry):
            rs = pl.multiple_of(r * FREQ_ROWS, FREQ_ROWS)
            acc_r = jnp.zeros((FREQ_ROWS, HY_WIDTH), F32)
            acc_i = jnp.zeros((FREQ_ROWS, HY_WIDTH), F32)
            for j in range(nb):
                e = i - j + nb - 1
                g_r = g_ref[0, e, pl.ds(rs, FREQ_ROWS), :]
                g_i = g_ref[0, e, pl.ds(p + rs, FREQ_ROWS), :]
                u_r = uf_scr[j, pl.ds(rs, FREQ_ROWS), :]
                u_i = uf_scr[j, pl.ds(p + rs, FREQ_ROWS), :]
                acc_r = acc_r + (g_r * u_r - g_i * u_i)
                acc_i = acc_i + (g_r * u_i + g_i * u_r)
            y_scr[pl.ds(rs, FREQ_ROWS), :] = acc_r.astype(BF16)
            y_scr[pl.ds(p + rs, FREQ_ROWS), :] = acc_i.astype(BF16)
            return carry

        lax.fori_loop(0, p // FREQ_ROWS, freq_step, 0)
        y = _dot(inv_ref[...], y_scr[...])
        gate = _conv3_rows(zg_ref, i, nb, p, cwg_ref[...], cbg_ref[...])
        o_ref[0, i * p:(i + 1) * p, :] = (
            gate * (y + ut_scr[i * p:(i + 1) * p, :] * skip_ref[...])).astype(BF16)


def _hyena_call(u, u_col, zhy, gate_col, conv_w, conv_b, skip, g_all, order, fwd_u, inv, conv_u):
    p = CONV_BLOCK
    c = HY_WIDTH
    ucol = u_col if conv_u else 0
    cwu = conv_w[:, ucol * c:(ucol + 1) * c]
    cbu = conv_b[:, ucol * c:(ucol + 1) * c]
    cwg = conv_w[:, gate_col * c:(gate_col + 1) * c]
    cbg = conv_b[:, gate_col * c:(gate_col + 1) * c]
    return pl.pallas_call(
        functools.partial(_hyena_kernel, conv_u=conv_u),
        grid=(BATCH,),
        in_specs=[pl.BlockSpec((1, SEQ, c), lambda b: (b, 0, u_col)),
                  pl.BlockSpec((1, SEQ, c), lambda b: (b, 0, gate_col)),
                  _resident((3, c)), _resident((1, c)), _resident((3, c)), _resident((1, c)),
                  _resident((1, c)),
                  pl.BlockSpec((1, N_OFFSETS, 2 * p, c), lambda b: (order, 0, 0, 0),
                               pipeline_mode=pl.Buffered(1)),
                  _resident((2 * p, p)),
                  _resident((p, 2 * p))],
        out_specs=pl.BlockSpec((1, SEQ, c), lambda b: (b, 0, 0)),
        out_shape=jax.ShapeDtypeStruct((BATCH, SEQ, c), BF16),
        scratch_shapes=[pltpu.VMEM((N_CONV_BLOCKS, 2 * p, c), F32),
                        pltpu.VMEM((SEQ, c), F32),
                        pltpu.VMEM((2 * p, c), BF16)],
        compiler_params=pltpu.CompilerParams(vmem_limit_bytes=VMEM_LIMIT),
        name="hyena%d" % order,
    )(u, zhy, cwu, cbu, cwg, cbg, skip, g_all, fwd_u, inv)


def _ret_kernel(lg_ref, q_ref, kt_ref, v_ref, g_ref, s_ref, o_ref, b_scr):
    c = RET_CHUNK
    nc = SEQ // c
    dk = RET_QK_DIM
    h = pl.program_id(1)
    lg_f = lg_ref[0, h]
    lg_b = lg_ref[1, h]
    ii = lax.broadcasted_iota(jnp.int32, (c, c), 0).astype(F32)
    jj = lax.broadcasted_iota(jnp.int32, (c, c), 1).astype(F32)
    dif = ii - jj
    decay = jnp.where(dif >= 0.0, jnp.exp(lg_f * jnp.maximum(dif, 0.0)),
                      jnp.exp(lg_b * jnp.maximum(-dif, 0.0)))
    pos_r = lax.broadcasted_iota(jnp.int32, (1, c), 1).astype(F32)
    pos_c = lax.broadcasted_iota(jnp.int32, (c, 1), 0).astype(F32)
    kw_f = jnp.exp(lg_f * (c - 1.0 - pos_r))
    kw_b = jnp.exp(lg_b * pos_r)
    qw_f = jnp.exp(lg_f * (pos_c + 1.0))
    qw_b = jnp.exp(lg_b * (c - pos_c))
    ones = jnp.ones((1, RET_V_DIM), F32)
    dec_f = jnp.exp(lg_f * float(c) * ones)
    dec_b = jnp.exp(lg_b * float(c) * ones)

    state = s_ref[0, 0, dk:2 * dk, :]
    b_scr[nc - 1] = state
    for n in range(nc - 1, 0, -1):
        ktn = (kt_ref[0, :, n * c:(n + 1) * c].astype(F32) * kw_b).astype(BF16)
        state = state * dec_b + _dot(ktn, v_ref[0, n * c:(n + 1) * c, :])
        b_scr[n - 1] = state

    state = s_ref[0, 0, 0:dk, :]
    for n in range(nc):
        qn = q_ref[0, 0, n * c:(n + 1) * c, :]
        ktn = kt_ref[0, :, n * c:(n + 1) * c]
        vn = v_ref[0, n * c:(n + 1) * c, :]
        scores = (_dot(qn, ktn) * decay).astype(BF16)
        qf = qn.astype(F32)
        o = _dot(scores, vn)
        o = o + _dot((qf * qw_f).astype(BF16), state.astype(BF16))
        o = o + _dot((qf * qw_b).astype(BF16), b_scr[n].astype(BF16))
        state = state * dec_f + _dot((ktn.astype(F32) * kw_f).astype(BF16), vn)
        o = o * lax.rsqrt(jnp.mean(o * o, axis=-1, keepdims=True) + EPS)
        gate = g_ref[0, n * c:(n + 1) * c, :].astype(F32)
        o_ref[0, n * c:(n + 1) * c, :] = (_silu(gate) * o).astype(BF16)


def _ret_call(lg, q, kt, v, g, s):
    dk, dv = RET_QK_DIM, RET_V_DIM
    return pl.pallas_call(
        _ret_kernel,
        grid=(BATCH, RET_HEADS),
        in_specs=[pl.BlockSpec(memory_space=pltpu.SMEM),
                  pl.BlockSpec((1, 1, SEQ, dk), lambda b, h: (b, h, 0, 0)),
                  pl.BlockSpec((1, dk, SEQ), lambda b, h: (b, h, 0)),
                  pl.BlockSpec((1, SEQ, dv), lambda b, h: (b, 0, h)),
                  pl.BlockSpec((1, SEQ, dv), lambda b, h: (b, 0, h)),
                  pl.BlockSpec((1, 1, 2 * dk, dv), lambda b, h: (b, h, 0, 0))],
        out_specs=pl.BlockSpec((1, SEQ, dv), lambda b, h: (b, 0, h)),
        out_shape=jax.ShapeDtypeStruct((BATCH, SEQ, RET_WIDTH), BF16),
        scratch_shapes=[pltpu.VMEM((SEQ // RET_CHUNK, dk, dv), F32)],
        compiler_params=pltpu.CompilerParams(vmem_limit_bytes=VMEM_LIMIT),
        name="ret",
    )(lg, q, kt, v, g, s)


def _ffn_kernel(x_ref, xp_ref, xn_ref, yh_ref, yhp_ref, yhn_ref, yr_ref, yrp_ref, yrn_ref,
                g1_ref, sh_ref, sc_ref, g2_ref, n2_ref, nf_ref,
                woh_ref, wor_ref, wup_ref, cw_ref, cb_ref, wdn_ref,
                o_ref, hb_scr, acc_scr):
    t = TOK_TILE
    i = pl.program_id(1)
    nt = pl.num_programs(1)

    def mixed(xr, yh, yr):
        return xr[0] + g1_ref[0] * (_dot(yh[0], woh_ref[...]) + _dot(yr[0], wor_ref[...]))

    def hidden(x1):
        return _norm_mod(x1, n2_ref[...], sh_ref[0], sc_ref[0])

    x1 = mixed(x_ref, yh_ref, yr_ref)
    hb_scr[HALO:HALO + t, :] = hidden(x1).astype(BF16)
    hp = hidden(mixed(xp_ref, yhp_ref, yrp_ref))
    hb_scr[0:HALO, :] = jnp.where(i > 0, hp, 0.0).astype(BF16)
    hn = hidden(mixed(xn_ref, yhn_ref, yrn_ref))
    hb_scr[HALO + t:2 * HALO + t, :] = jnp.where(i < nt - 1, hn, 0.0).astype(BF16)

    n = t + 2 * HALO

    def conv(a, col):
        w = cw_ref[:, col:col + FFN_COLS]
        before = pltpu.roll(a, 1, 0)[HALO:HALO + t]
        after = pltpu.roll(a, n - 1, 0)[HALO:HALO + t]
        return (before * w[0:1] + a[HALO:HALO + t] * w[1:2] + after * w[2:3]
                + cb_ref[:, col:col + FFN_COLS])

    for cblk in range(D_FF // FFN_COLS):
        c0 = cblk * FFN_COLS
        hb = hb_scr[...]
        val = conv(_dot(hb, wup_ref[:, c0:c0 + FFN_COLS]), c0)
        gate = conv(_dot(hb, wup_ref[:, D_FF + c0:D_FF + c0 + FFN_COLS]), D_FF + c0)
        act = (_silu(gate) * val).astype(BF16)
        part = _dot(act, wdn_ref[c0:c0 + FFN_COLS, :])
        if cblk == 0:
            acc_scr[...] = part
        else:
            acc_scr[...] += part

    x2 = x1 + g2_ref[0] * acc_scr[...]
    o_ref[0] = x2 * lax.rsqrt(jnp.mean(x2 * x2, axis=-1, keepdims=True) + EPS) * nf_ref[...]


def _ffn_call(x, y_hy, y_ret, mod3, norm2, norm_f, w_oh, w_or, w_up, conv_w, conv_b, w_dn):
    t = TOK_TILE
    r = t // HALO
    last = SEQ // HALO - 1

    def main(width):
        return pl.BlockSpec((1, t, width), lambda b, i: (b, i, 0))

    def prev(width):
        return pl.BlockSpec((1, HALO, width), lambda b, i: (b, jnp.maximum(i * r - 1, 0), 0))

    def nxt(width):
        return pl.BlockSpec((1, HALO, width), lambda b, i: (b, jnp.minimum((i + 1) * r, last), 0))

    def modrow(k):
        return pl.BlockSpec((1, 1, D_MODEL), lambda b, i: (b, 0, k))

    return pl.pallas_call(
        _ffn_kernel,
        grid=(BATCH, SEQ // t),
        in_specs=[main(D_MODEL), prev(D_MODEL), nxt(D_MODEL),
                  main(HY_WIDTH), prev(HY_WIDTH), nxt(HY_WIDTH),
                  main(RET_WIDTH), prev(RET_WIDTH), nxt(RET_WIDTH),
                  modrow(2), modrow(3), modrow(4), modrow(5),
                  _resident((1, D_MODEL)), _resident((1, D_MODEL)),
                  _resident((HY_WIDTH, D_MODEL)), _resident((RET_WIDTH, D_MODEL)),
                  _resident((D_MODEL, 2 * D_FF)),
                  _resident((3, 2 * D_FF)), _resident((1, 2 * D_FF)),
                  _resident((D_FF, D_MODEL))],
        out_specs=pl.BlockSpec((1, t, D_MODEL), lambda b, i: (b, i, 0)),
        out_shape=jax.ShapeDtypeStruct((BATCH, SEQ, D_MODEL), F32),
        scratch_shapes=[pltpu.VMEM((t + 2 * HALO, D_MODEL), BF16),
                        pltpu.VMEM((t, D_MODEL), F32)],
        compiler_params=pltpu.CompilerParams(vmem_limit_bytes=VMEM_LIMIT),
        name="ffn",
    )(x, x, x, y_hy, y_hy, y_hy, y_ret, y_ret, y_ret, mod3, mod3, mod3, mod3,
      norm2, norm_f, w_oh, w_or, w_up, conv_w, conv_b, w_dn)


def kernel(x, c, ctx, c_ctx, w_mod, b_mod, norm1, w_in, hy_conv_w, hy_conv_b, hy_w1, hy_b1, hy_f1,
           hy_w2, hy_b2, hy_f2, hy_w3, hy_bias, ret_logit_f, ret_logit_b, w_out, norm2,
           ffn_w_up, ffn_conv_w, ffn_conv_b, ffn_w_down, norm_f):
    layer = 0
    rope = tuple(jnp.asarray(a) for a in _rope_tables())
    zf_np, absdelta_np = _filter_features()
    fwd_np, inv_np = _dft_matrices()
    zf, absdelta = jnp.asarray(zf_np), jnp.asarray(absdelta_np)
    fwd = jnp.asarray(fwd_np).astype(BF16)
    fwd_u = fwd[:, :CONV_BLOCK]
    inv = jnp.asarray(inv_np).astype(BF16)

    w_in_l = w_in[layer]
    w_hy = w_in_l[:, :HY_COLS].astype(BF16)
    w_q = w_in_l[:, Q_OFF:K_OFF].astype(BF16)
    w_kt = w_in_l[:, K_OFF:V_OFF].T.astype(BF16)
    w_v = w_in_l[:, V_OFF:G_OFF].astype(BF16)
    w_g = w_in_l[:, G_OFF:].astype(BF16)
    w_oh = w_out[layer][:HY_WIDTH].astype(BF16)
    w_or = w_out[layer][HY_WIDTH:].astype(BF16)
    w_up = ffn_w_up[layer].astype(BF16)
    w_dn = ffn_w_down[layer].astype(BF16)
    row = lambda a: a.reshape(1, -1)
    w1p = jnp.pad(hy_w1[layer], ((0, HY_FILTER_WIDTH - hy_w1.shape[1]), (0, 0)))
    lg = jnp.stack([jax.nn.log_sigmoid(ret_logit_f[layer].astype(F32)),
                    jax.nn.log_sigmoid(ret_logit_b[layer].astype(F32))])

    cc = jnp.concatenate([c, c_ctx[None, :], jnp.zeros((MOD_ROWS - BATCH - 1, D_MODEL), F32)], axis=0)
    mod = _mod_call(cc, w_mod[layer], row(b_mod[layer]))
    mod3 = mod.reshape(MOD_ROWS, 1, 6 * D_MODEL)
    norm1_r = row(norm1[layer])

    s_ctx = _ctx_call(lg, ctx, mod3, norm1_r, w_kt, w_v)
    zhy, q, kt, v, g = _inproj_call(x, mod3, norm1_r, w_hy, w_q, w_v, w_g, w_kt, rope)

    g_all = _filter_call(zf, w1p, row(hy_b1[layer]), row(hy_f1[layer]), hy_w2[layer], row(hy_b2[layer]),
                         row(hy_f2[layer]), hy_w3[layer], absdelta, fwd)
    conv_w, conv_b = hy_conv_w[layer], row(hy_conv_b[layer])
    y1 = _hyena_call(zhy, 0, zhy, 1, conv_w, conv_b, hy_bias[layer][0:1], g_all, 0, fwd_u, inv, True)
    y_hy = _hyena_call(y1, 0, zhy, 2, conv_w, conv_b, hy_bias[layer][1:2], g_all, 1, fwd_u, inv, False)

    y_ret = _ret_call(lg, q, kt, v, g, s_ctx)

    return _ffn_call(x, y_hy, y_ret, mod3, row(norm2[layer]), row(norm_f), w_oh, w_or, w_up,
                     ffn_conv_w[layer], row(ffn_conv_b[layer]), w_dn)
```

```python
import functools
import math

import numpy as np
import jax
import jax.numpy as jnp
from jax import lax
from jax.experimental import pallas as pl
from jax.experimental.pallas import tpu as pltpu

F32 = jnp.float32
BF16 = jnp.bfloat16

D_MODEL = 1024
BATCH = 8
SEQ = 2048
CTX_LEN = 256
GRID_W = 64
HY_WIDTH = 512
HY_ORDER = 2
HY_EMB_BANDS = 16
HY_FILTER_WIDTH = 64
HY_FAST_DECAY = 0.3
HY_SLOW_DECAY = 1.5
HY_TARGET = 1e-2
RET_WIDTH = 512
RET_HEADS = 4
RET_QK_DIM = 64
RET_V_DIM = 128
ROPE_BASE = 10000.0
D_FF = 2816
EPS = 1e-6
HY_COLS = (HY_ORDER + 1) * HY_WIDTH
Q_OFF = HY_COLS
K_OFF = Q_OFF + RET_HEADS * RET_QK_DIM
V_OFF = K_OFF + RET_HEADS * RET_QK_DIM
G_OFF = V_OFF + RET_WIDTH
K_SCALE = RET_QK_DIM ** -0.5

MOD_ROWS = 16
TOK_TILE = 512
HALO = 16
CONV_BLOCK = 512
N_CONV_BLOCKS = SEQ // CONV_BLOCK
N_OFFSETS = 2 * N_CONV_BLOCKS - 1
FREQ_ROWS = 16
RET_CHUNK = 256
FFN_COLS = 256
VMEM_LIMIT = 56 * 1024 * 1024

_NT = (((1,), (1,)), ((), ()))


def _dot(a, b):
    return jnp.dot(a, b, preferred_element_type=F32)


def _dot_hi(a, b):
    return jnp.dot(a, b, preferred_element_type=F32, precision=lax.Precision.HIGHEST)


def _silu(x):
    return x * (1.0 / (1.0 + jnp.exp(-x)))


def _norm_mod(x, gain, shift, scale):
    y = x * lax.rsqrt(jnp.mean(x * x, axis=-1, keepdims=True) + EPS)
    return (y * gain) * (1.0 + scale) + shift


def _resident(shape):
    nd = len(shape)
    return pl.BlockSpec(shape, lambda *_: (0,) * nd, pipeline_mode=pl.Buffered(1))


@functools.lru_cache(maxsize=None)
def _rope_tables():
    pos = np.arange(SEQ)
    row = (pos // GRID_W).astype(np.float64)
    col = (pos % GRID_W).astype(np.float64)
    quarter = RET_QK_DIM // 4
    inv_freq = ROPE_BASE ** (-np.arange(quarter, dtype=np.float64) / quarter)
    ang = np.concatenate([row[:, None] * inv_freq, col[:, None] * inv_freq], axis=-1)
    cos, sin = np.cos(ang), np.sin(ang)
    cos_h = np.concatenate([cos, cos], axis=-1)
    sin_h = np.concatenate([-sin, sin], axis=-1)
    cos_t = np.tile(cos_h, (1, RET_HEADS))
    sin_t = np.tile(sin_h, (1, RET_HEADS))
    return (cos_t.astype(np.float32), sin_t.astype(np.float32),
            np.ascontiguousarray((cos_t * K_SCALE).T).astype(np.float32),
            np.ascontiguousarray((sin_t * K_SCALE).T).astype(np.float32))


@functools.lru_cache(maxsize=None)
def _filter_features():
    lag = np.abs(np.arange(2 * SEQ) - SEQ).astype(np.float64)
    t = lag / (SEQ - 1)
    bands = np.linspace(1e-4, HY_EMB_BANDS - 1, HY_EMB_BANDS)
    ang = 2.0 * math.pi * lag[:, None] * bands[None, :] / SEQ
    z = np.concatenate([t[:, None], np.cos(ang), -np.sin(ang)], axis=-1)
    zp = np.zeros((2 * SEQ, HY_FILTER_WIDTH), np.float64)
    zp[:, :z.shape[1]] = z
    max_decay = math.log(HY_TARGET) / HY_FAST_DECAY
    min_decay = math.log(HY_TARGET) / HY_SLOW_DECAY
    absdelta = np.abs(np.linspace(min_decay, max_decay, HY_WIDTH))[:, None]
    return (np.ascontiguousarray(zp.T).astype(np.float32), t[None, :].astype(np.float32),
            absdelta.astype(np.float32))


@functools.lru_cache(maxsize=None)
def _dft_matrices():
    p = CONV_BLOCK
    n = 2 * p
    f = np.arange(p, dtype=np.float64)[:, None] + 0.5
    t = np.arange(n, dtype=np.float64)[None, :]
    theta = 2.0 * math.pi * f * t / n
    fwd = np.concatenate([np.cos(theta), -np.sin(theta)], axis=0)
    th_out = theta[:, p:].T
    inv = np.concatenate([np.cos(th_out), -np.sin(th_out)], axis=1) / p
    return fwd.astype(np.float32), inv.astype(np.float32)


def _mod_kernel(c_ref, w_ref, b_ref, o_ref):
    s = _silu(c_ref[...]).astype(BF16)
    o_ref[...] = _dot(s, w_ref[...].astype(BF16)) + b_ref[...]


def _mod_call(cc, w_mod, b_mod):
    ncol = 6 * D_MODEL
    blk = ncol // 4
    return pl.pallas_call(
        _mod_kernel,
        grid=(ncol // blk,),
        in_specs=[pl.BlockSpec((MOD_ROWS, D_MODEL), lambda j: (0, 0)),
                  pl.BlockSpec((D_MODEL, blk), lambda j: (0, j)),
                  pl.BlockSpec((1, blk), lambda j: (0, j))],
        out_specs=pl.BlockSpec((MOD_ROWS, blk), lambda j: (0, j)),
        out_shape=jax.ShapeDtypeStruct((MOD_ROWS, ncol), F32),
        compiler_params=pltpu.CompilerParams(vmem_limit_bytes=VMEM_LIMIT),
        name="mod",
    )(cc, w_mod, b_mod)


def _ctx_kernel(lg_ref, x_ref, sh_ref, sc_ref, n1_ref, wkt_ref, wv_ref, s_ref):
    h = _norm_mod(x_ref[0], n1_ref[...], sh_ref[0], sc_ref[0]).astype(BF16)
    kt = lax.dot_general(wkt_ref[...], h, _NT, preferred_element_type=F32) * K_SCALE
    v = _dot(h, wv_ref[...])
    pos = lax.broadcasted_iota(jnp.int32, (1, CTX_LEN), 1).astype(F32)
    for hh in range(RET_HEADS):
        w_f = jnp.exp(lg_ref[0, hh] * (CTX_LEN - 1.0 - pos))
        w_b = jnp.exp(lg_ref[1, hh] * pos)
        kth = kt[hh * RET_QK_DIM:(hh + 1) * RET_QK_DIM, :]
        vh = v[:, hh * RET_V_DIM:(hh + 1) * RET_V_DIM].astype(BF16)
        s_ref[0, hh, 0:RET_QK_DIM, :] = _dot((kth * w_f).astype(BF16), vh)
        s_ref[0, hh, RET_QK_DIM:2 * RET_QK_DIM, :] = _dot((kth * w_b).astype(BF16), vh)


def _ctx_call(lg, ctx, mod3, norm1, w_kt, w_v):
    return pl.pallas_call(
        _ctx_kernel,
        grid=(BATCH,),
        in_specs=[pl.BlockSpec(memory_space=pltpu.SMEM),
                  pl.BlockSpec((1, CTX_LEN, D_MODEL), lambda b: (b, 0, 0)),
                  pl.BlockSpec((1, 1, D_MODEL), lambda b: (BATCH, 0, 0)),
                  pl.BlockSpec((1, 1, D_MODEL), lambda b: (BATCH, 0, 1)),
                  _resident((1, D_MODEL)),
                  _resident((RET_HEADS * RET_QK_DIM, D_MODEL)),
                  _resident((D_MODEL, RET_WIDTH))],
        out_specs=pl.BlockSpec((1, RET_HEADS, 2 * RET_QK_DIM, RET_V_DIM), lambda b: (b, 0, 0, 0)),
        out_shape=jax.ShapeDtypeStruct((BATCH, RET_HEADS, 2 * RET_QK_DIM, RET_V_DIM), F32),
        compiler_params=pltpu.CompilerParams(vmem_limit_bytes=VMEM_LIMIT),
        name="ctx",
    )(lg, ctx, mod3, mod3, norm1, w_kt, w_v)


def _swap_halves(x, axis):
    n = x.shape[axis]
    half = RET_QK_DIM // 2
    idx = lax.broadcasted_iota(jnp.int32, x.shape, axis)
    first = (idx & (RET_QK_DIM - 1)) < half
    return jnp.where(first, pltpu.roll(x, n - half, axis), pltpu.roll(x, half, axis))


def _inproj_kernel(x_ref, sh_ref, sc_ref, n1_ref, why_ref, wq_ref, wv_ref, wg_ref, wkt_ref,
                   cq_ref, sq_ref, ck_ref, sk_ref,
                   zhy_ref, q_ref, kt_ref, v_ref, g_ref):
    hb = _norm_mod(x_ref[0], n1_ref[...], sh_ref[0], sc_ref[0]).astype(BF16)
    zhy_ref[0] = _dot(hb, why_ref[...]).astype(BF16)
    v_ref[0] = _dot(hb, wv_ref[...]).astype(BF16)
    g_ref[0] = _dot(hb, wg_ref[...]).astype(BF16)
    q = _dot(hb, wq_ref[...])
    q = q * cq_ref[...] + _swap_halves(q, 1) * sq_ref[...]
    for hh in range(RET_HEADS):
        q_ref[0, hh] = q[:, hh * RET_QK_DIM:(hh + 1) * RET_QK_DIM].astype(BF16)
    kt = lax.dot_general(wkt_ref[...], hb, _NT, preferred_element_type=F32)
    kt = kt * ck_ref[...] + _swap_halves(kt, 0) * sk_ref[...]
    kt_ref[0] = kt.astype(BF16)


def _inproj_call(x, mod3, norm1, w_hy, w_q, w_v, w_g, w_kt, rope):
    cq, sq, ck, sk = rope
    t = TOK_TILE
    qk = RET_HEADS * RET_QK_DIM
    return pl.pallas_call(
        _inproj_kernel,
        grid=(BATCH, SEQ // t),
        in_specs=[pl.BlockSpec((1, t, D_MODEL), lambda b, i: (b, i, 0)),
                  pl.BlockSpec((1, 1, D_MODEL), lambda b, i: (b, 0, 0)),
                  pl.BlockSpec((1, 1, D_MODEL), lambda b, i: (b, 0, 1)),
                  _resident((1, D_MODEL)),
                  _resident((D_MODEL, HY_COLS)),
                  _resident((D_MODEL, qk)),
                  _resident((D_MODEL, RET_WIDTH)),
                  _resident((D_MODEL, RET_WIDTH)),
                  _resident((qk, D_MODEL)),
                  pl.BlockSpec((t, qk), lambda b, i: (i, 0)),
                  pl.BlockSpec((t, qk), lambda b, i: (i, 0)),
                  pl.BlockSpec((qk, t), lambda b, i: (0, i)),
                  pl.BlockSpec((qk, t), lambda b, i: (0, i))],
        out_specs=[pl.BlockSpec((1, t, HY_COLS), lambda b, i: (b, i, 0)),
                   pl.BlockSpec((1, RET_HEADS, t, RET_QK_DIM), lambda b, i: (b, 0, i, 0)),
                   pl.BlockSpec((1, qk, t), lambda b, i: (b, 0, i)),
                   pl.BlockSpec((1, t, RET_WIDTH), lambda b, i: (b, i, 0)),
                   pl.BlockSpec((1, t, RET_WIDTH), lambda b, i: (b, i, 0))],
        out_shape=[jax.ShapeDtypeStruct((BATCH, SEQ, HY_COLS), BF16),
                   jax.ShapeDtypeStruct((BATCH, RET_HEADS, SEQ, RET_QK_DIM), BF16),
                   jax.ShapeDtypeStruct((BATCH, qk, SEQ), BF16),
                   jax.ShapeDtypeStruct((BATCH, SEQ, RET_WIDTH), BF16),
                   jax.ShapeDtypeStruct((BATCH, SEQ, RET_WIDTH), BF16)],
        compiler_params=pltpu.CompilerParams(vmem_limit_bytes=VMEM_LIMIT),
        name="inproj",
    )(x, mod3, mod3, norm1, w_hy, w_q, w_v, w_g, w_kt, cq, sq, ck, sk)


def _filter_mlp_kernel(zt_ref, t_ref, w1t_ref, b1_ref, f1_ref, w2t_ref, b2_ref, f2_ref, w3t_ref, adel_ref,
                       kt_ref):
    hid = jnp.sin(f1_ref[...] * (_dot_hi(w1t_ref[...], zt_ref[...]) + b1_ref[...]))
    hid = jnp.sin(f2_ref[...] * (_dot_hi(w2t_ref[...], hid) + b2_ref[...])).astype(BF16)
    c = HY_WIDTH
    for half in range(2):
        lags = slice(half * SEQ, (half + 1) * SEQ)
        window = jnp.exp(-adel_ref[...] * t_ref[:, lags])
        direction = 1 - half
        for o in range(HY_ORDER):
            r0 = (direction * HY_ORDER + o) * c
            w3 = w3t_ref[r0:r0 + c, :].astype(BF16)
            kt_ref[o, :, lags] = (_dot(w3, hid[:, lags]) * window).astype(BF16)


def _filter_mlp_call(zt, t_row, w1t, b1, f1, w2t, b2, f2, w3t, absdelta):
    fw = HY_FILTER_WIDTH
    n_out = 2 * HY_ORDER * HY_WIDTH
    return pl.pallas_call(
        _filter_mlp_kernel,
        grid=(1,),
        in_specs=[_resident((fw, 2 * SEQ)), _resident((1, 2 * SEQ)),
                  _resident((fw, fw)), _resident((fw, 1)), _resident((fw, 1)),
                  _resident((fw, fw)), _resident((fw, 1)), _resident((fw, 1)),
                  _resident((n_out, fw)), _resident((HY_WIDTH, 1))],
        out_specs=pl.BlockSpec((HY_ORDER, HY_WIDTH, 2 * SEQ), lambda i: (0, 0, 0)),
        out_shape=jax.ShapeDtypeStruct((HY_ORDER, HY_WIDTH, 2 * SEQ), BF16),
        compiler_params=pltpu.CompilerParams(vmem_limit_bytes=VMEM_LIMIT),
        name="filter_mlp",
    )(zt, t_row, w1t, b1, f1, w2t, b2, f2, w3t, absdelta)


def _filter_dft_kernel(kt_ref, fwd_ref, g_ref, prev_scr):
    p = CONV_BLOCK
    e = pl.program_id(1)
    cur = lax.dot_general(fwd_ref[...], kt_ref[0], _NT, preferred_element_type=F32)
    row = lax.broadcasted_iota(jnp.int32, (p, 1), 0)
    sign = (1 - 2 * (row & 1)).astype(F32)

    @pl.when(e > 0)
    def _():
        g_ref[0, 0, 0:p, :] = prev_scr[0:p, :] + sign * cur[p:2 * p, :]
        g_ref[0, 0, p:2 * p, :] = prev_scr[p:2 * p, :] - sign * cur[0:p, :]

    @pl.when(e == 0)
    def _():
        g_ref[0, 0] = cur

    prev_scr[...] = cur


def _filter_dft_call(kt, fwd_u):
    p = CONV_BLOCK
    return pl.pallas_call(
        _filter_dft_kernel,
        grid=(HY_ORDER, 2 * N_CONV_BLOCKS),
        in_specs=[pl.BlockSpec((1, HY_WIDTH, p), lambda o, e: (o, 0, e)),
                  _resident((2 * p, p))],
        out_specs=pl.BlockSpec((1, 1, 2 * p, HY_WIDTH), lambda o, e: (o, jnp.maximum(e - 1, 0), 0, 0)),
        out_shape=jax.ShapeDtypeStruct((HY_ORDER, N_OFFSETS, 2 * p, HY_WIDTH), F32),
        scratch_shapes=[pltpu.VMEM((2 * p, HY_WIDTH), F32)],
        compiler_params=pltpu.CompilerParams(vmem_limit_bytes=VMEM_LIMIT),
        name="filter_dft",
    )(kt, fwd_u)


def _conv3_rows(ref, j, n_blocks, rows, w, b):
    main = ref[0, j * rows:(j + 1) * rows, :].astype(F32)
    cols = main.shape[1]
    zeros = jnp.zeros((HALO, cols), F32)
    prev = ref[0, j * rows - HALO:j * rows, :].astype(F32) if j > 0 else zeros
    nxt = ref[0, (j + 1) * rows:(j + 1) * rows + HALO, :].astype(F32) if j < n_blocks - 1 else zeros
    ext = jnp.concatenate([prev, main, nxt], axis=0)
    n = rows + 2 * HALO
    before = pltpu.roll(ext, 1, 0)[HALO:HALO + rows]
    after = pltpu.roll(ext, n - 1, 0)[HALO:HALO + rows]
    return before * w[0:1] + main * w[1:2] + after * w[2:3] + b


def _hyena_kernel(u_ref, zg_ref, cwu_ref, cbu_ref, cwg_ref, cbg_ref, skip_ref, g_ref, fwd_ref, inv_ref,
                  o_ref, uf_scr, ut_scr, y_scr, *, conv_u):
    p = CONV_BLOCK
    nb = N_CONV_BLOCKS
    for j in range(nb):
        if conv_u:
            uj = _conv3_rows(u_ref, j, nb, p, cwu_ref[...], cbu_ref[...])
        else:
            uj = u_ref[0, j * p:(j + 1) * p, :].astype(F32)
        ut_scr[j * p:(j + 1) * p, :] = uj
        uf_scr[j] = _dot(fwd_ref[...], uj.astype(BF16))

    for i in range(nb):
        def freq_step(r, carry):
            rs = pl.multiple_of(r * FREQ_ROWS, FREQ_ROWS)
            acc_r = jnp.zeros((FREQ_ROWS, HY_WIDTH), F32)
            acc_i = jnp.zeros((FREQ_ROWS, HY_WIDTH), F32)
            for j in range(nb):
                e = i - j + nb - 1
                g_r = g_ref[0, e, pl.ds(rs, FREQ_ROWS), :]
                g_i = g_ref[0, e, pl.ds(p + rs, FREQ_ROWS), :]
                u_r = uf_scr[j, pl.ds(rs, FREQ_ROWS), :]
                u_i = uf_scr[j, pl.ds(p + rs, FREQ_ROWS), :]
                acc_r = acc_r + (g_r * u_r - g_i * u_i)
                acc_i = acc_i + (g_r * u_i + g_i * u_r)
            y_scr[pl.ds(rs, FREQ_ROWS), :] = acc_r.astype(BF16)
            y_scr[pl.ds(p + rs, FREQ_ROWS), :] = acc_i.astype(BF16)
            return carry

        lax.fori_loop(0, p // FREQ_ROWS, freq_step, 0)
        y = _dot(inv_ref[...], y_scr[...])
        gate = _conv3_rows(zg_ref, i, nb, p, cwg_ref[...], cbg_ref[...])
        o_ref[0, i * p:(i + 1) * p, :] = (
            gate * (y + ut_scr[i * p:(i + 1) * p, :] * skip_ref[...])).astype(BF16)


def _hyena_call(u, u_col, zhy, gate_col, conv_w, conv_b, skip, g_all, order, fwd_u, inv, conv_u):
    p = CONV_BLOCK
    c = HY_WIDTH
    ucol = u_col if conv_u else 0
    cwu = conv_w[:, ucol * c:(ucol + 1) * c]
    cbu = conv_b[:, ucol * c:(ucol + 1) * c]
    cwg = conv_w[:, gate_col * c:(gate_col + 1) * c]
    cbg = conv_b[:, gate_col * c:(gate_col + 1) * c]
    return pl.pallas_call(
        functools.partial(_hyena_kernel, conv_u=conv_u),
        grid=(BATCH,),
        in_specs=[pl.BlockSpec((1, SEQ, c), lambda b: (b, 0, u_col)),
                  pl.BlockSpec((1, SEQ, c), lambda b: (b, 0, gate_col)),
                  _resident((3, c)), _resident((1, c)), _resident((3, c)), _resident((1, c)),
                  _resident((1, c)),
                  pl.BlockSpec((1, N_OFFSETS, 2 * p, c), lambda b: (order, 0, 0, 0),
                               pipeline_mode=pl.Buffered(1)),
                  _resident((2 * p, p)),
                  _resident((p, 2 * p))],
        out_specs=pl.BlockSpec((1, SEQ, c), lambda b: (b, 0, 0)),
        out_shape=jax.ShapeDtypeStruct((BATCH, SEQ, c), BF16),
        scratch_shapes=[pltpu.VMEM((N_CONV_BLOCKS, 2 * p, c), F32),
                        pltpu.VMEM((SEQ, c), F32),
                        pltpu.VMEM((2 * p, c), BF16)],
        compiler_params=pltpu.CompilerParams(vmem_limit_bytes=VMEM_LIMIT),
        name="hyena%d" % order,
    )(u, zhy, cwu, cbu, cwg, cbg, skip, g_all, fwd_u, inv)


def _ret_kernel(lg_ref, q_ref, kt_ref, v_ref, g_ref, s_ref, o_ref, b_scr):
    c = RET_CHUNK
    nc = SEQ // c
    dk = RET_QK_DIM
    h = pl.program_id(1)
    lg_f = lg_ref[0, h]
    lg_b = lg_ref[1, h]
    ii = lax.broadcasted_iota(jnp.int32, (c, c), 0).astype(F32)
    jj = lax.broadcasted_iota(jnp.int32, (c, c), 1).astype(F32)
    dif = ii - jj
    decay = jnp.where(dif >= 0.0, jnp.exp(lg_f * jnp.maximum(dif, 0.0)),
                      jnp.exp(lg_b * jnp.maximum(-dif, 0.0)))
    pos_r = lax.broadcasted_iota(jnp.int32, (1, c), 1).astype(F32)
    pos_c = lax.broadcasted_iota(jnp.int32, (c, 1), 0).astype(F32)
    kw_f = jnp.exp(lg_f * (c - 1.0 - pos_r))
    kw_b = jnp.exp(lg_b * pos_r)
    qw_f = jnp.exp(lg_f * (pos_c + 1.0))
    qw_b = jnp.exp(lg_b * (c - pos_c))
    ones = jnp.ones((1, RET_V_DIM), F32)
    dec_f = jnp.exp(lg_f * float(c) * ones)
    dec_b = jnp.exp(lg_b * float(c) * ones)

    state = s_ref[0, 0, dk:2 * dk, :]
    b_scr[nc - 1] = state
    for n in range(nc - 1, 0, -1):
        ktn = (kt_ref[0, :, n * c:(n + 1) * c].astype(F32) * kw_b).astype(BF16)
        state = state * dec_b + _dot(ktn, v_ref[0, n * c:(n + 1) * c, :])
        b_scr[n - 1] = state

    state = s_ref[0, 0, 0:dk, :]
    for n in range(nc):
        qn = q_ref[0, 0, n * c:(n + 1) * c, :]
        ktn = kt_ref[0, :, n * c:(n + 1) * c]
        vn = v_ref[0, n * c:(n + 1) * c, :]
        scores = (_dot(qn, ktn) * decay).astype(BF16)
        qf = qn.astype(F32)
        o = _dot(scores, vn)
        o = o + _dot((qf * qw_f).astype(BF16), state.astype(BF16))
        o = o + _dot((qf * qw_b).astype(BF16), b_scr[n].astype(BF16))
        state = state * dec_f + _dot((ktn.astype(F32) * kw_f).astype(BF16), vn)
        o = o * lax.rsqrt(jnp.mean(o * o, axis=-1, keepdims=True) + EPS)
        gate = g_ref[0, n * c:(n + 1) * c, :].astype(F32)
        o_ref[0, n * c:(n + 1) * c, :] = (_silu(gate) * o).astype(BF16)


def _ret_call(lg, q, kt, v, g, s):
    dk, dv = RET_QK_DIM, RET_V_DIM
    return pl.pallas_call(
        _ret_kernel,
        grid=(BATCH, RET_HEADS),
        in_specs=[pl.BlockSpec(memory_space=pltpu.SMEM),
                  pl.BlockSpec((1, 1, SEQ, dk), lambda b, h: (b, h, 0, 0)),
                  pl.BlockSpec((1, dk, SEQ), lambda b, h: (b, h, 0)),
                  pl.BlockSpec((1, SEQ, dv), lambda b, h: (b, 0, h)),
                  pl.BlockSpec((1, SEQ, dv), lambda b, h: (b, 0, h)),
                  pl.BlockSpec((1, 1, 2 * dk, dv), lambda b, h: (b, h, 0, 0))],
        out_specs=pl.BlockSpec((1, SEQ, dv), lambda b, h: (b, 0, h)),
        out_shape=jax.ShapeDtypeStruct((BATCH, SEQ, RET_WIDTH), BF16),
        scratch_shapes=[pltpu.VMEM((SEQ // RET_CHUNK, dk, dv), F32)],
        compiler_params=pltpu.CompilerParams(vmem_limit_bytes=VMEM_LIMIT),
        name="ret",
    )(lg, q, kt, v, g, s)


def _ffn_kernel(x_ref, xp_ref, xn_ref, yh_ref, yhp_ref, yhn_ref, yr_ref, yrp_ref, yrn_ref,
                g1_ref, sh_ref, sc_ref, g2_ref, n2_ref, nf_ref,
                woh_ref, wor_ref, wup_ref, cw_ref, cb_ref, wdn_ref,
                o_ref, hb_scr, x1_scr, av_scr, ag_scr, act_scr):
    t = TOK_TILE
    i = pl.program_id(1)
    nt = pl.num_programs(1)

    def mixed(xr, yh, yr):
        return xr[0] + g1_ref[0] * (_dot(yh[0], woh_ref[...]) + _dot(yr[0], wor_ref[...]))

    def hidden(x1):
        return _norm_mod(x1, n2_ref[...], sh_ref[0], sc_ref[0])

    x1_scr[...] = mixed(x_ref, yh_ref, yr_ref)
    hb_scr[HALO:HALO + t, :] = hidden(x1_scr[...]).astype(BF16)
    hp = hidden(mixed(xp_ref, yhp_ref, yrp_ref))
    hb_scr[0:HALO, :] = jnp.where(i > 0, hp, 0.0).astype(BF16)
    hn = hidden(mixed(xn_ref, yhn_ref, yrn_ref))
    hb_scr[HALO + t:2 * HALO + t, :] = jnp.where(i < nt - 1, hn, 0.0).astype(BF16)

    half = (t + 2 * HALO) // 2

    def up(slot, cblk):
        c0 = cblk * FFN_COLS
        for r in range(2):
            rows = slice(r * half, (r + 1) * half)
            hb = hb_scr[rows, :]
            av_scr[slot, rows, :] = _dot(hb, wup_ref[:, c0:c0 + FFN_COLS])
            ag_scr[slot, rows, :] = _dot(hb, wup_ref[:, D_FF + c0:D_FF + c0 + FFN_COLS])

    def conv(scr, slot, col):
        w = cw_ref[:, col:col + FFN_COLS]
        return (scr[slot, HALO - 1:HALO - 1 + t, :] * w[0:1] + scr[slot, HALO:HALO + t, :] * w[1:2]
                + scr[slot, HALO + 1:HALO + 1 + t, :] * w[2:3] + cb_ref[:, col:col + FFN_COLS])

    n_blk = D_FF // FFN_COLS
    up(0, 0)
    for cblk in range(n_blk):
        c0 = cblk * FFN_COLS
        slot = cblk % 2
        if cblk + 1 < n_blk:
            up(1 - slot, cblk + 1)
        act_scr[:, c0:c0 + FFN_COLS] = (
            _silu(conv(ag_scr, slot, D_FF + c0)) * conv(av_scr, slot, c0)).astype(BF16)

    x2 = x1_scr[...] + g2_ref[0] * _dot(act_scr[...], wdn_ref[...])
    o_ref[0] = x2 * lax.rsqrt(jnp.mean(x2 * x2, axis=-1, keepdims=True) + EPS) * nf_ref[...]


def _ffn_call(x, y_hy, y_ret, mod3, norm2, norm_f, w_oh, w_or, w_up, conv_w, conv_b, w_dn):
    t = TOK_TILE
    r = t // HALO
    last = SEQ // HALO - 1

    def main(width):
        return pl.BlockSpec((1, t, width), lambda b, i: (b, i, 0))

    def prev(width):
        return pl.BlockSpec((1, HALO, width), lambda b, i: (b, jnp.maximum(i * r - 1, 0), 0))

    def nxt(width):
        return pl.BlockSpec((1, HALO, width), lambda b, i: (b, jnp.minimum((i + 1) * r, last), 0))

    def modrow(k):
        return pl.BlockSpec((1, 1, D_MODEL), lambda b, i: (b, 0, k))

    return pl.pallas_call(
        _ffn_kernel,
        grid=(BATCH, SEQ // t),
        in_specs=[main(D_MODEL), prev(D_MODEL), nxt(D_MODEL),
                  main(HY_WIDTH), prev(HY_WIDTH), nxt(HY_WIDTH),
                  main(RET_WIDTH), prev(RET_WIDTH), nxt(RET_WIDTH),
                  modrow(2), modrow(3), modrow(4), modrow(5),
                  _resident((1, D_MODEL)), _resident((1, D_MODEL)),
                  _resident((HY_WIDTH, D_MODEL)), _resident((RET_WIDTH, D_MODEL)),
                  _resident((D_MODEL, 2 * D_FF)),
                  _resident((3, 2 * D_FF)), _resident((1, 2 * D_FF)),
                  _resident((D_FF, D_MODEL))],
        out_specs=pl.BlockSpec((1, t, D_MODEL), lambda b, i: (b, i, 0)),
        out_shape=jax.ShapeDtypeStruct((BATCH, SEQ, D_MODEL), F32),
        scratch_shapes=[pltpu.VMEM((t + 2 * HALO, D_MODEL), BF16),
                        pltpu.VMEM((t, D_MODEL), F32),
                        pltpu.VMEM((2, t + 2 * HALO, FFN_COLS), F32),
                        pltpu.VMEM((2, t + 2 * HALO, FFN_COLS), F32),
                        pltpu.VMEM((t, D_FF), BF16)],
        compiler_params=pltpu.CompilerParams(vmem_limit_bytes=VMEM_LIMIT),
        name="ffn",
    )(x, x, x, y_hy, y_hy, y_hy, y_ret, y_ret, y_ret, mod3, mod3, mod3, mod3,
      norm2, norm_f, w_oh, w_or, w_up, conv_w, conv_b, w_dn)


def kernel(x, c, ctx, c_ctx, w_mod, b_mod, norm1, w_in, hy_conv_w, hy_conv_b, hy_w1, hy_b1, hy_f1,
           hy_w2, hy_b2, hy_f2, hy_w3, hy_bias, ret_logit_f, ret_logit_b, w_out, norm2,
           ffn_w_up, ffn_conv_w, ffn_conv_b, ffn_w_down, norm_f):
    layer = 0
    rope = tuple(jnp.asarray(a) for a in _rope_tables())
    zt, t_row, absdelta = (jnp.asarray(a) for a in _filter_features())
    fwd_np, inv_np = _dft_matrices()
    fwd_u = jnp.asarray(fwd_np[:, :CONV_BLOCK]).astype(BF16)
    inv = jnp.asarray(inv_np).astype(BF16)

    w_in_l = w_in[layer]
    w_hy = w_in_l[:, :HY_COLS].astype(BF16)
    w_q = w_in_l[:, Q_OFF:K_OFF].astype(BF16)
    w_kt = w_in_l[:, K_OFF:V_OFF].T.astype(BF16)
    w_v = w_in_l[:, V_OFF:G_OFF].astype(BF16)
    w_g = w_in_l[:, G_OFF:].astype(BF16)
    w_oh = w_out[layer][:HY_WIDTH].astype(BF16)
    w_or = w_out[layer][HY_WIDTH:].astype(BF16)
    w_up = ffn_w_up[layer].astype(BF16)
    w_dn = ffn_w_down[layer].astype(BF16)
    row = lambda a: a.reshape(1, -1)
    w1p = jnp.pad(hy_w1[layer], ((0, HY_FILTER_WIDTH - hy_w1.shape[1]), (0, 0)))
    lg = jnp.stack([jax.nn.log_sigmoid(ret_logit_f[layer].astype(F32)),
                    jax.nn.log_sigmoid(ret_logit_b[layer].astype(F32))])

    cc = jnp.concatenate([c, c_ctx[None, :], jnp.zeros((MOD_ROWS - BATCH - 1, D_MODEL), F32)], axis=0)
    mod = _mod_call(cc, w_mod[layer], row(b_mod[layer]))
    mod3 = mod.reshape(MOD_ROWS, 1, 6 * D_MODEL)
    norm1_r = row(norm1[layer])

    s_ctx = _ctx_call(lg, ctx, mod3, norm1_r, w_kt, w_v)
    zhy, q, kt, v, g = _inproj_call(x, mod3, norm1_r, w_hy, w_q, w_v, w_g, w_kt, rope)

    col = lambda a: a.reshape(-1, 1)
    k_two_sided = _filter_mlp_call(zt, t_row, w1p.T, col(hy_b1[layer]), col(hy_f1[layer]), hy_w2[layer].T,
                                   col(hy_b2[layer]), col(hy_f2[layer]), hy_w3[layer].T, absdelta)
    g_all = _filter_dft_call(k_two_sided, fwd_u)
    conv_w, conv_b = hy_conv_w[layer], row(hy_conv_b[layer])
    y1 = _hyena_call(zhy, 0, zhy, 1, conv_w, conv_b, hy_bias[layer][0:1], g_all, 0, fwd_u, inv, True)
    y_hy = _hyena_call(y1, 0, zhy, 2, conv_w, conv_b, hy_bias[layer][1:2], g_all, 1, fwd_u, inv, False)

    y_ret = _ret_call(lg, q, kt, v, g, s_ctx)

    return _ffn_call(x, y_hy, y_ret, mod3, row(norm2[layer]), row(norm_f), w_oh, w_or, w_up,
                     ffn_conv_w[layer], row(ffn_conv_b[layer]), w_dn)
```

```python
import functools
import math

import numpy as np
import jax
import jax.numpy as jnp
from jax import lax
from jax.experimental import pallas as pl
from jax.experimental.pallas import tpu as pltpu

F32 = jnp.float32
BF16 = jnp.bfloat16

D_MODEL = 1024
BATCH = 8
SEQ = 2048
CTX_LEN = 256
GRID_W = 64
HY_WIDTH = 512
HY_ORDER = 2
HY_EMB_BANDS = 16
HY_FILTER_WIDTH = 64
HY_FAST_DECAY = 0.3
HY_SLOW_DECAY = 1.5
HY_TARGET = 1e-2
RET_WIDTH = 512
RET_HEADS = 4
RET_QK_DIM = 64
RET_V_DIM = 128
ROPE_BASE = 10000.0
D_FF = 2816
EPS = 1e-6
HY_COLS = (HY_ORDER + 1) * HY_WIDTH
Q_OFF = HY_COLS
K_OFF = Q_OFF + RET_HEADS * RET_QK_DIM
V_OFF = K_OFF + RET_HEADS * RET_QK_DIM
G_OFF = V_OFF + RET_WIDTH
K_SCALE = RET_QK_DIM ** -0.5

MOD_ROWS = 16
TOK_TILE = 512
SUBLANES = 8
LANES = 128
LANE_BLOCKS = D_MODEL // LANES
HALO = 16
CONV_BLOCK = 512
N_CONV_BLOCKS = SEQ // CONV_BLOCK
N_OFFSETS = 2 * N_CONV_BLOCKS - 1
FREQ_ROWS = 16
FREQ_LANES = 128
DFT_ROWS = 256
RET_CHUNK = 256
FFN_COLS = 256
VMEM_LIMIT = 56 * 1024 * 1024
HYENA_VMEM_LIMIT = 62 * 1024 * 1024

_NT = (((1,), (1,)), ((), ()))


def _dot(a, b):
    return jnp.dot(a, b, preferred_element_type=F32)


def _dot_hi(a, b):
    return jnp.dot(a, b, preferred_element_type=F32, precision=lax.Precision.HIGHEST)


def _silu(x):
    return x * (1.0 / (1.0 + jnp.exp(-x)))


def _norm_mod(x, gain, shift, scale):
    y = x * lax.rsqrt(jnp.mean(x * x, axis=-1, keepdims=True) + EPS)
    return (y * gain) * (1.0 + scale) + shift


def _resident(shape):
    nd = len(shape)
    return pl.BlockSpec(shape, lambda *_: (0,) * nd, pipeline_mode=pl.Buffered(1))


@functools.lru_cache(maxsize=None)
def _rope_tables():
    pos = np.arange(SEQ)
    row = (pos // GRID_W).astype(np.float64)
    col = (pos % GRID_W).astype(np.float64)
    quarter = RET_QK_DIM // 4
    inv_freq = ROPE_BASE ** (-np.arange(quarter, dtype=np.float64) / quarter)
    ang = np.concatenate([row[:, None] * inv_freq, col[:, None] * inv_freq], axis=-1)
    cos, sin = np.cos(ang), np.sin(ang)
    cos_h = np.concatenate([cos, cos], axis=-1)
    sin_h = np.concatenate([-sin, sin], axis=-1)
    cos_t = np.tile(cos_h, (1, RET_HEADS))
    sin_t = np.tile(sin_h, (1, RET_HEADS))
    return (cos_t.astype(np.float32), sin_t.astype(np.float32),
            np.ascontiguousarray((cos_t * K_SCALE).T).astype(np.float32),
            np.ascontiguousarray((sin_t * K_SCALE).T).astype(np.float32))


@functools.lru_cache(maxsize=None)
def _filter_features():
    lag = np.abs(np.arange(2 * SEQ) - SEQ).astype(np.float64)
    t = lag / (SEQ - 1)
    bands = np.linspace(1e-4, HY_EMB_BANDS - 1, HY_EMB_BANDS)
    ang = 2.0 * math.pi * lag[:, None] * bands[None, :] / SEQ
    z = np.concatenate([t[:, None], np.cos(ang), -np.sin(ang)], axis=-1)
    zp = np.zeros((2 * SEQ, HY_FILTER_WIDTH), np.float64)
    zp[:, :z.shape[1]] = z
    max_decay = math.log(HY_TARGET) / HY_FAST_DECAY
    min_decay = math.log(HY_TARGET) / HY_SLOW_DECAY
    absdelta = np.abs(np.linspace(min_decay, max_decay, HY_WIDTH))[:, None]
    return (np.ascontiguousarray(zp.T).astype(np.float32), t[None, :].astype(np.float32),
            absdelta.astype(np.float32))


@functools.lru_cache(maxsize=None)
def _dft_matrices():
    p = CONV_BLOCK
    n = 2 * p
    f = np.arange(p, dtype=np.float64)[:, None] + 0.5
    t = np.arange(n, dtype=np.float64)[None, :]
    theta = 2.0 * math.pi * f * t / n
    fwd = np.concatenate([np.cos(theta), -np.sin(theta)], axis=0)
    th_out = theta[:, p:].T
    inv = np.concatenate([np.cos(th_out), -np.sin(th_out)], axis=1) / p
    return fwd.astype(np.float32), inv.astype(np.float32)


def _mod_kernel(c_ref, w_ref, b_ref, o_ref):
    s = _silu(c_ref[...]).astype(BF16)
    o_ref[...] = _dot(s, w_ref[...].astype(BF16)) + b_ref[...]


def _mod_call(cc, w_mod, b_mod):
    ncol = 6 * D_MODEL
    blk = ncol // 4
    return pl.pallas_call(
        _mod_kernel,
        grid=(ncol // blk,),
        in_specs=[pl.BlockSpec((MOD_ROWS, D_MODEL), lambda j: (0, 0)),
                  pl.BlockSpec((D_MODEL, blk), lambda j: (0, j)),
                  pl.BlockSpec((1, blk), lambda j: (0, j))],
        out_specs=pl.BlockSpec((MOD_ROWS, blk), lambda j: (0, j)),
        out_shape=jax.ShapeDtypeStruct((MOD_ROWS, ncol), F32),
        compiler_params=pltpu.CompilerParams(vmem_limit_bytes=VMEM_LIMIT),
        name="mod",
    )(cc, w_mod, b_mod)


def _ctx_kernel(lg_ref, x_ref, sh_ref, sc_ref, n1_ref, wkt_ref, wv_ref, s_ref):
    h = _norm_mod(x_ref[0], n1_ref[...], sh_ref[0], sc_ref[0]).astype(BF16)
    kt = lax.dot_general(wkt_ref[...], h, _NT, preferred_element_type=F32) * K_SCALE
    v = _dot(h, wv_ref[...])
    pos = lax.broadcasted_iota(jnp.int32, (1, CTX_LEN), 1).astype(F32)
    for hh in range(RET_HEADS):
        w_f = jnp.exp(lg_ref[0, hh] * (CTX_LEN - 1.0 - pos))
        w_b = jnp.exp(lg_ref[1, hh] * pos)
        kth = kt[hh * RET_QK_DIM:(hh + 1) * RET_QK_DIM, :]
        vh = v[:, hh * RET_V_DIM:(hh + 1) * RET_V_DIM].astype(BF16)
        s_ref[0, hh, 0:RET_QK_DIM, :] = _dot((kth * w_f).astype(BF16), vh)
        s_ref[0, hh, RET_QK_DIM:2 * RET_QK_DIM, :] = _dot((kth * w_b).astype(BF16), vh)


def _ctx_call(lg, ctx, mod3, norm1, w_kt, w_v):
    return pl.pallas_call(
        _ctx_kernel,
        grid=(BATCH,),
        in_specs=[pl.BlockSpec(memory_space=pltpu.SMEM),
                  pl.BlockSpec((1, CTX_LEN, D_MODEL), lambda b: (b, 0, 0)),
                  pl.BlockSpec((1, 1, D_MODEL), lambda b: (BATCH, 0, 0)),
                  pl.BlockSpec((1, 1, D_MODEL), lambda b: (BATCH, 0, 1)),
                  _resident((1, D_MODEL)),
                  _resident((RET_HEADS * RET_QK_DIM, D_MODEL)),
                  _resident((D_MODEL, RET_WIDTH))],
        out_specs=pl.BlockSpec((1, RET_HEADS, 2 * RET_QK_DIM, RET_V_DIM), lambda b: (b, 0, 0, 0)),
        out_shape=jax.ShapeDtypeStruct((BATCH, RET_HEADS, 2 * RET_QK_DIM, RET_V_DIM), F32),
        compiler_params=pltpu.CompilerParams(vmem_limit_bytes=VMEM_LIMIT),
        name="ctx",
    )(lg, ctx, mod3, mod3, norm1, w_kt, w_v)


def _swap_halves(x, axis):
    n = x.shape[axis]
    half = RET_QK_DIM // 2
    idx = lax.broadcasted_iota(jnp.int32, x.shape, axis)
    first = (idx & (RET_QK_DIM - 1)) < half
    return jnp.where(first, pltpu.roll(x, n - half, axis), pltpu.roll(x, half, axis))


def _inproj_kernel(x_ref, sh_ref, sc_ref, n1_ref, why_ref, wq_ref, wv_ref, wg_ref, wkt_ref,
                   cq_ref, sq_ref, ck_ref, sk_ref,
                   zhy_ref, q_ref, kt_ref, v_ref, g_ref):
    hb = _norm_mod(x_ref[0], n1_ref[...], sh_ref[0], sc_ref[0]).astype(BF16)
    zhy_ref[0] = _dot(hb, why_ref[...]).astype(BF16)
    v_ref[0] = _dot(hb, wv_ref[...]).astype(BF16)
    g_ref[0] = _dot(hb, wg_ref[...]).astype(BF16)
    q = _dot(hb, wq_ref[...])
    q = q * cq_ref[...] + _swap_halves(q, 1) * sq_ref[...]
    for hh in range(RET_HEADS):
        q_ref[0, hh] = q[:, hh * RET_QK_DIM:(hh + 1) * RET_QK_DIM].astype(BF16)
    kt = lax.dot_general(wkt_ref[...], hb, _NT, preferred_element_type=F32)
    kt = kt * ck_ref[...] + _swap_halves(kt, 0) * sk_ref[...]
    kt_ref[0] = kt.astype(BF16)


def _inproj_call(x, mod3, norm1, w_hy, w_q, w_v, w_g, w_kt, rope):
    cq, sq, ck, sk = rope
    t = TOK_TILE
    qk = RET_HEADS * RET_QK_DIM
    return pl.pallas_call(
        _inproj_kernel,
        grid=(BATCH, SEQ // t),
        in_specs=[pl.BlockSpec((1, t, D_MODEL), lambda b, i: (b, i, 0)),
                  pl.BlockSpec((1, 1, D_MODEL), lambda b, i: (b, 0, 0)),
                  pl.BlockSpec((1, 1, D_MODEL), lambda b, i: (b, 0, 1)),
                  _resident((1, D_MODEL)),
                  _resident((D_MODEL, HY_COLS)),
                  _resident((D_MODEL, qk)),
                  _resident((D_MODEL, RET_WIDTH)),
                  _resident((D_MODEL, RET_WIDTH)),
                  _resident((qk, D_MODEL)),
                  pl.BlockSpec((t, qk), lambda b, i: (i, 0)),
                  pl.BlockSpec((t, qk), lambda b, i: (i, 0)),
                  pl.BlockSpec((qk, t), lambda b, i: (0, i)),
                  pl.BlockSpec((qk, t), lambda b, i: (0, i))],
        out_specs=[pl.BlockSpec((1, t, HY_COLS), lambda b, i: (b, i, 0)),
                   pl.BlockSpec((1, RET_HEADS, t, RET_QK_DIM), lambda b, i: (b, 0, i, 0)),
                   pl.BlockSpec((1, qk, t), lambda b, i: (b, 0, i)),
                   pl.BlockSpec((1, t, RET_WIDTH), lambda b, i: (b, i, 0)),
                   pl.BlockSpec((1, t, RET_WIDTH), lambda b, i: (b, i, 0))],
        out_shape=[jax.ShapeDtypeStruct((BATCH, SEQ, HY_COLS), BF16),
                   jax.ShapeDtypeStruct((BATCH, RET_HEADS, SEQ, RET_QK_DIM), BF16),
                   jax.ShapeDtypeStruct((BATCH, qk, SEQ), BF16),
                   jax.ShapeDtypeStruct((BATCH, SEQ, RET_WIDTH), BF16),
                   jax.ShapeDtypeStruct((BATCH, SEQ, RET_WIDTH), BF16)],
        compiler_params=pltpu.CompilerParams(vmem_limit_bytes=VMEM_LIMIT),
        name="inproj",
    )(x, mod3, mod3, norm1, w_hy, w_q, w_v, w_g, w_kt, cq, sq, ck, sk)


def _filter_mlp_kernel(zt_ref, t_ref, w1t_ref, b1_ref, f1_ref, w2t_ref, b2_ref, f2_ref, w3t_ref, adel_ref,
                       kt_ref):
    hid = jnp.sin(f1_ref[...] * (_dot_hi(w1t_ref[...], zt_ref[...]) + b1_ref[...]))
    hid = jnp.sin(f2_ref[...] * (_dot_hi(w2t_ref[...], hid) + b2_ref[...])).astype(BF16)
    c = HY_WIDTH
    for half in range(2):
        lags = slice(half * SEQ, (half + 1) * SEQ)
        window = jnp.exp(-adel_ref[...] * t_ref[:, lags])
        direction = 1 - half
        for o in range(HY_ORDER):
            r0 = (direction * HY_ORDER + o) * c
            w3 = w3t_ref[r0:r0 + c, :].astype(BF16)
            kt_ref[o, :, lags] = (_dot(w3, hid[:, lags]) * window).astype(BF16)


def _filter_mlp_call(zt, t_row, w1t, b1, f1, w2t, b2, f2, w3t, absdelta):
    fw = HY_FILTER_WIDTH
    n_out = 2 * HY_ORDER * HY_WIDTH
    return pl.pallas_call(
        _filter_mlp_kernel,
        grid=(1,),
        in_specs=[_resident((fw, 2 * SEQ)), _resident((1, 2 * SEQ)),
                  _resident((fw, fw)), _resident((fw, 1)), _resident((fw, 1)),
                  _resident((fw, fw)), _resident((fw, 1)), _resident((fw, 1)),
                  _resident((n_out, fw)), _resident((HY_WIDTH, 1))],
        out_specs=pl.BlockSpec((HY_ORDER, HY_WIDTH, 2 * SEQ), lambda i: (0, 0, 0)),
        out_shape=jax.ShapeDtypeStruct((HY_ORDER, HY_WIDTH, 2 * SEQ), BF16),
        compiler_params=pltpu.CompilerParams(vmem_limit_bytes=VMEM_LIMIT),
        name="filter_mlp",
    )(zt, t_row, w1t, b1, f1, w2t, b2, f2, w3t, absdelta)


_FILTER_PLANES = (
    {0: 1},
    {-1: 1, 0: -1},
    {1: 1, 0: -1},
    {-2: 1, 0: -1},
    {-3: 1, -1: -1, -2: -1, 0: 1},
    {-1: 1, 1: -1, -2: -1, 0: 1},
    {2: 1, 0: -1},
    {1: 1, -1: -1, 2: -1, 0: 1},
    {3: 1, 1: -1, 2: -1, 0: 1},
)


def _filter_dft_kernel(kt_ref, fwd_ref, h_ref, t_scr):
    p = CONV_BLOCK
    n_blk = 2 * N_CONV_BLOCKS
    step = pl.program_id(1)

    @pl.when(step < n_blk)
    def _():
        cur = lax.dot_general(fwd_ref[...], kt_ref[0], _NT, preferred_element_type=F32)
        t_scr[step] = cur
        h_ref[0, 0] = cur

    row = lax.broadcasted_iota(jnp.int32, (p, 1), 0)
    sign = (1 - 2 * (row & 1)).astype(F32)
    re, im = slice(0, p), slice(p, 2 * p)

    for k, coefs in enumerate(_FILTER_PLANES):
        @pl.when(step == n_blk + k)
        def _(coefs=coefs):
            def comb(shift, rows):
                acc = None
                for d, c in coefs.items():
                    term = t_scr[d + N_CONV_BLOCKS - 1 + shift, rows, :]
                    if acc is None:
                        acc = term
                    else:
                        acc = acc + term if c > 0 else acc - term
                return acc

            h_ref[0, 0, re, :] = comb(0, re) + sign * comb(1, im)
            h_ref[0, 0, im, :] = comb(0, im) - sign * comb(1, re)


def _filter_dft_call(kt, fwd_u):
    p = CONV_BLOCK
    n_blk = 2 * N_CONV_BLOCKS
    n_planes = len(_FILTER_PLANES)
    return pl.pallas_call(
        _filter_dft_kernel,
        grid=(HY_ORDER, n_blk + n_planes),
        in_specs=[pl.BlockSpec((1, HY_WIDTH, p), lambda o, s: (o, 0, jnp.minimum(s, n_blk - 1))),
                  _resident((2 * p, p))],
        out_specs=pl.BlockSpec((1, 1, 2 * p, HY_WIDTH), lambda o, s: (o, jnp.maximum(s - n_blk, 0), 0, 0)),
        out_shape=jax.ShapeDtypeStruct((HY_ORDER, n_planes, 2 * p, HY_WIDTH), F32),
        scratch_shapes=[pltpu.VMEM((n_blk, 2 * p, HY_WIDTH), F32)],
        compiler_params=pltpu.CompilerParams(vmem_limit_bytes=VMEM_LIMIT),
        name="filter_dft",
    )(kt, fwd_u)


def _conv3_rows(ref, j, n_blocks, rows, w, b):
    main = ref[0, j * rows:(j + 1) * rows, :].astype(F32)
    cols = main.shape[1]
    zeros = jnp.zeros((HALO, cols), F32)
    prev = ref[0, j * rows - HALO:j * rows, :].astype(F32) if j > 0 else zeros
    nxt = ref[0, (j + 1) * rows:(j + 1) * rows + HALO, :].astype(F32) if j < n_blocks - 1 else zeros
    ext = jnp.concatenate([prev, main, nxt], axis=0)
    n = rows + 2 * HALO
    before = pltpu.roll(ext, 1, 0)[HALO:HALO + rows]
    after = pltpu.roll(ext, n - 1, 0)[HALO:HALO + rows]
    return before * w[0:1] + main * w[1:2] + after * w[2:3] + b


def _hyena_kernel(u_ref, zg_ref, cwu_ref, cbu_ref, cwg_ref, cbg_ref, skip_ref, h_ref, fwd_ref, inv_ref,
                  o_ref, uf_scr, ub_scr, y_scr, *, conv_u):
    p = CONV_BLOCK
    nb = N_CONV_BLOCKS

    def prepare(j):
        if conv_u:
            ub_scr[j * p:(j + 1) * p, :] = _conv3_rows(u_ref, j, nb, p, cwu_ref[...], cbu_ref[...]).astype(BF16)

    def u_block(j):
        return ub_scr[j * p:(j + 1) * p, :] if conv_u else u_ref[0, j * p:(j + 1) * p, :]

    prepare(0)
    for j in range(nb):
        if j + 1 < nb:
            prepare(j + 1)
        for r in range(2 * p // DFT_ROWS):
            rows = slice(r * DFT_ROWS, (r + 1) * DFT_ROWS)
            uf_scr[j, rows, :] = _dot(fwd_ref[rows, :], u_block(j))

    def cadd(a, b):
        return a[0] + b[0], a[1] + b[1]

    def cmul(m, x):
        return m[0] * x[0] - m[1] * x[1], m[0] * x[1] + m[1] * x[0]

    def toeplitz2(k0, x0, x1, re, im, ln):
        m0, mu, ml = ((h_ref[0, k0 + t, re, ln], h_ref[0, k0 + t, im, ln]) for t in range(3))
        p1 = cmul(m0, cadd(x0, x1))
        return cadd(p1, cmul(mu, x1)), cadd(p1, cmul(ml, x0))

    assert nb == 4
    for r in range(p // FREQ_ROWS):
        re = slice(r * FREQ_ROWS, (r + 1) * FREQ_ROWS)
        im = slice(p + r * FREQ_ROWS, p + (r + 1) * FREQ_ROWS)
        for cb in range(HY_WIDTH // FREQ_LANES):
            ln = slice(cb * FREQ_LANES, (cb + 1) * FREQ_LANES)
            u = [(uf_scr[j, re, ln], uf_scr[j, im, ln]) for j in range(nb)]
            d0, d1 = toeplitz2(0, cadd(u[0], u[2]), cadd(u[1], u[3]), re, im, ln)
            b0, b1 = toeplitz2(3, u[2], u[3], re, im, ln)
            c0, c1 = toeplitz2(6, u[0], u[1], re, im, ln)
            for i, yi in enumerate((cadd(d0, b0), cadd(d1, b1), cadd(d0, c0), cadd(d1, c1))):
                y_scr[i, re, ln] = yi[0].astype(BF16)
                y_scr[i, im, ln] = yi[1].astype(BF16)

    for i in range(nb):
        gate = _conv3_rows(zg_ref, i, nb, p, cwg_ref[...], cbg_ref[...])
        y = _dot(inv_ref[...], y_scr[i])
        o_ref[0, i * p:(i + 1) * p, :] = (
            gate * (y + u_block(i).astype(F32) * skip_ref[...])).astype(BF16)


def _hyena_call(u, u_col, zhy, gate_col, conv_w, conv_b, skip, g_all, order, fwd_u, inv, conv_u):
    p = CONV_BLOCK
    c = HY_WIDTH
    ucol = u_col if conv_u else 0
    cwu = conv_w[:, ucol * c:(ucol + 1) * c]
    cbu = conv_b[:, ucol * c:(ucol + 1) * c]
    cwg = conv_w[:, gate_col * c:(gate_col + 1) * c]
    cbg = conv_b[:, gate_col * c:(gate_col + 1) * c]
    return pl.pallas_call(
        functools.partial(_hyena_kernel, conv_u=conv_u),
        grid=(BATCH,),
        in_specs=[pl.BlockSpec((1, SEQ, c), lambda b: (b, 0, u_col)),
                  pl.BlockSpec((1, SEQ, c), lambda b: (b, 0, gate_col)),
                  _resident((3, c)), _resident((1, c)), _resident((3, c)), _resident((1, c)),
                  _resident((1, c)),
                  pl.BlockSpec((1, len(_FILTER_PLANES), 2 * p, c), lambda b: (order, 0, 0, 0),
                               pipeline_mode=pl.Buffered(1)),
                  _resident((2 * p, p)),
                  _resident((p, 2 * p))],
        out_specs=pl.BlockSpec((1, SEQ, c), lambda b: (b, 0, 0)),
        out_shape=jax.ShapeDtypeStruct((BATCH, SEQ, c), BF16),
        scratch_shapes=[pltpu.VMEM((N_CONV_BLOCKS, 2 * p, c), F32),
                        pltpu.VMEM((SEQ, c) if conv_u else (SUBLANES * 2, LANES), BF16),
                        pltpu.VMEM((N_CONV_BLOCKS, 2 * p, c), BF16)],
        compiler_params=pltpu.CompilerParams(vmem_limit_bytes=HYENA_VMEM_LIMIT),
        name="hyena%d" % order,
    )(u, zhy, cwu, cbu, cwg, cbg, skip, g_all, fwd_u, inv)


def _ret_kernel(lg_ref, q_ref, kt_ref, v_ref, g_ref, s_ref, o_ref, b_scr):
    c = RET_CHUNK
    nc = SEQ // c
    dk = RET_QK_DIM
    h = pl.program_id(1)
    lg_f = lg_ref[0, h]
    lg_b = lg_ref[1, h]
    ii = lax.broadcasted_iota(jnp.int32, (c, c), 0).astype(F32)
    jj = lax.broadcasted_iota(jnp.int32, (c, c), 1).astype(F32)
    dif = ii - jj
    decay = jnp.where(dif >= 0.0, jnp.exp(lg_f * jnp.maximum(dif, 0.0)),
                      jnp.exp(lg_b * jnp.maximum(-dif, 0.0)))
    pos_r = lax.broadcasted_iota(jnp.int32, (1, c), 1).astype(F32)
    pos_c = lax.broadcasted_iota(jnp.int32, (c, 1), 0).astype(F32)
    kw_f = jnp.exp(lg_f * (c - 1.0 - pos_r))
    kw_b = jnp.exp(lg_b * pos_r)
    qw_f = jnp.exp(lg_f * (pos_c + 1.0))
    qw_b = jnp.exp(lg_b * (c - pos_c))
    ones = jnp.ones((1, RET_V_DIM), F32)
    dec_f = jnp.exp(lg_f * float(c) * ones)
    dec_b = jnp.exp(lg_b * float(c) * ones)

    state = s_ref[0, 0, dk:2 * dk, :]
    b_scr[nc - 1] = state
    for n in range(nc - 1, 0, -1):
        ktn = (kt_ref[0, :, n * c:(n + 1) * c].astype(F32) * kw_b).astype(BF16)
        state = state * dec_b + _dot(ktn, v_ref[0, n * c:(n + 1) * c, :])
        b_scr[n - 1] = state

    state = s_ref[0, 0, 0:dk, :]
    for n in range(nc):
        qn = q_ref[0, 0, n * c:(n + 1) * c, :]
        ktn = kt_ref[0, :, n * c:(n + 1) * c]
        vn = v_ref[0, n * c:(n + 1) * c, :]
        scores = (_dot(qn, ktn) * decay).astype(BF16)
        qf = qn.astype(F32)
        o = _dot(scores, vn)
        o = o + _dot((qf * qw_f).astype(BF16), state.astype(BF16))
        o = o + _dot((qf * qw_b).astype(BF16), b_scr[n].astype(BF16))
        state = state * dec_f + _dot((ktn.astype(F32) * kw_f).astype(BF16), vn)
        o = o * lax.rsqrt(jnp.mean(o * o, axis=-1, keepdims=True) + EPS)
        gate = g_ref[0, n * c:(n + 1) * c, :].astype(F32)
        o_ref[0, n * c:(n + 1) * c, :] = (_silu(gate) * o).astype(BF16)


def _ret_call(lg, q, kt, v, g, s):
    dk, dv = RET_QK_DIM, RET_V_DIM
    return pl.pallas_call(
        _ret_kernel,
        grid=(BATCH, RET_HEADS),
        in_specs=[pl.BlockSpec(memory_space=pltpu.SMEM),
                  pl.BlockSpec((1, 1, SEQ, dk), lambda b, h: (b, h, 0, 0)),
                  pl.BlockSpec((1, dk, SEQ), lambda b, h: (b, h, 0)),
                  pl.BlockSpec((1, SEQ, dv), lambda b, h: (b, 0, h)),
                  pl.BlockSpec((1, SEQ, dv), lambda b, h: (b, 0, h)),
                  pl.BlockSpec((1, 1, 2 * dk, dv), lambda b, h: (b, h, 0, 0))],
        out_specs=pl.BlockSpec((1, SEQ, dv), lambda b, h: (b, 0, h)),
        out_shape=jax.ShapeDtypeStruct((BATCH, SEQ, RET_WIDTH), BF16),
        scratch_shapes=[pltpu.VMEM((SEQ // RET_CHUNK, dk, dv), F32)],
        compiler_params=pltpu.CompilerParams(vmem_limit_bytes=VMEM_LIMIT),
        name="ret",
    )(lg, q, kt, v, g, s)


def _ffn_kernel(x_ref, xp_ref, xn_ref, yh_ref, yhp_ref, yhn_ref, yr_ref, yrp_ref, yrn_ref,
                g1_ref, sh_ref, sc_ref, g2_ref, n2_ref, nf_ref,
                woh_ref, wor_ref, wup_ref, cw_ref, cb_ref, wdn_ref,
                o_ref, hb_scr, x1_scr, av_scr, ag_scr, act_scr):
    t = TOK_TILE
    i = pl.program_id(1)
    nt = pl.num_programs(1)

    def mixed(xr, yh, yr):
        return xr[0] + g1_ref[0] * (_dot(yh[0], woh_ref[...]) + _dot(yr[0], wor_ref[...]))

    def hidden(x1):
        return _norm_mod(x1, n2_ref[...], sh_ref[0], sc_ref[0])

    x1_scr[...] = mixed(x_ref, yh_ref, yr_ref)
    hb_scr[HALO:HALO + t, :] = hidden(x1_scr[...]).astype(BF16)
    hp = hidden(mixed(xp_ref, yhp_ref, yrp_ref))
    hb_scr[0:HALO, :] = jnp.where(i > 0, hp, 0.0).astype(BF16)
    hn = hidden(mixed(xn_ref, yhn_ref, yrn_ref))
    hb_scr[HALO + t:2 * HALO + t, :] = jnp.where(i < nt - 1, hn, 0.0).astype(BF16)

    half = (t + 2 * HALO) // 2

    def up(slot, cblk):
        c0 = cblk * FFN_COLS
        for r in range(2):
            rows = slice(r * half, (r + 1) * half)
            hb = hb_scr[rows, :]
            av_scr[slot, rows, :] = _dot(hb, wup_ref[:, c0:c0 + FFN_COLS])
            ag_scr[slot, rows, :] = _dot(hb, wup_ref[:, D_FF + c0:D_FF + c0 + FFN_COLS])

    def conv(scr, slot, col):
        w = cw_ref[:, col:col + FFN_COLS]
        return (scr[slot, HALO - 1:HALO - 1 + t, :] * w[0:1] + scr[slot, HALO:HALO + t, :] * w[1:2]
                + scr[slot, HALO + 1:HALO + 1 + t, :] * w[2:3] + cb_ref[:, col:col + FFN_COLS])

    n_blk = D_FF // FFN_COLS
    up(0, 0)
    for cblk in range(n_blk):
        c0 = cblk * FFN_COLS
        slot = cblk % 2
        if cblk + 1 < n_blk:
            up(1 - slot, cblk + 1)
        act_scr[:, c0:c0 + FFN_COLS] = (
            _silu(conv(ag_scr, slot, D_FF + c0)) * conv(av_scr, slot, c0)).astype(BF16)

    x2 = x1_scr[...] + g2_ref[0] * _dot(act_scr[...], wdn_ref[...])
    o_ref[0] = x2 * lax.rsqrt(jnp.mean(x2 * x2, axis=-1, keepdims=True) + EPS) * nf_ref[...]


def _ffn_call(x, y_hy, y_ret, mod3, norm2, norm_f, w_oh, w_or, w_up, conv_w, conv_b, w_dn):
    t = TOK_TILE
    r = t // HALO
    last = SEQ // HALO - 1

    def main(width):
        return pl.BlockSpec((1, t, width), lambda b, i: (b, i, 0))

    def prev(width):
        return pl.BlockSpec((1, HALO, width), lambda b, i: (b, jnp.maximum(i * r - 1, 0), 0))

    def nxt(width):
        return pl.BlockSpec((1, HALO, width), lambda b, i: (b, jnp.minimum((i + 1) * r, last), 0))

    def modrow(k):
        return pl.BlockSpec((1, 1, D_MODEL), lambda b, i: (b, 0, k))

    return pl.pallas_call(
        _ffn_kernel,
        grid=(BATCH, SEQ // t),
        in_specs=[main(D_MODEL), prev(D_MODEL), nxt(D_MODEL),
                  main(HY_WIDTH), prev(HY_WIDTH), nxt(HY_WIDTH),
                  main(RET_WIDTH), prev(RET_WIDTH), nxt(RET_WIDTH),
                  modrow(2), modrow(3), modrow(4), modrow(5),
                  _resident((1, D_MODEL)), _resident((1, D_MODEL)),
                  _resident((HY_WIDTH, D_MODEL)), _resident((RET_WIDTH, D_MODEL)),
                  _resident((D_MODEL, 2 * D_FF)),
                  _resident((3, 2 * D_FF)), _resident((1, 2 * D_FF)),
                  _resident((D_FF, D_MODEL))],
        out_specs=pl.BlockSpec((1, t, D_MODEL), lambda b, i: (b, i, 0)),
        out_shape=jax.ShapeDtypeStruct((BATCH, SEQ, D_MODEL), F32),
        scratch_shapes=[pltpu.VMEM((t + 2 * HALO, D_MODEL), BF16),
                        pltpu.VMEM((t, D_MODEL), F32),
                        pltpu.VMEM((2, t + 2 * HALO, FFN_COLS), F32),
                        pltpu.VMEM((2, t + 2 * HALO, FFN_COLS), F32),
                        pltpu.VMEM((t, D_FF), BF16)],
        compiler_params=pltpu.CompilerParams(vmem_limit_bytes=VMEM_LIMIT),
        name="ffn",
    )(x, x, x, y_hy, y_hy, y_hy, y_ret, y_ret, y_ret, mod3, mod3, mod3, mod3,
      norm2, norm_f, w_oh, w_or, w_up, conv_w, conv_b, w_dn)


def kernel(x, c, ctx, c_ctx, w_mod, b_mod, norm1, w_in, hy_conv_w, hy_conv_b, hy_w1, hy_b1, hy_f1,
           hy_w2, hy_b2, hy_f2, hy_w3, hy_bias, ret_logit_f, ret_logit_b, w_out, norm2,
           ffn_w_up, ffn_conv_w, ffn_conv_b, ffn_w_down, norm_f):
    layer = 0
    rope = tuple(jnp.asarray(a) for a in _rope_tables())
    zt, t_row, absdelta = (jnp.asarray(a) for a in _filter_features())
    fwd_np, inv_np = _dft_matrices()
    fwd_u = jnp.asarray(fwd_np[:, :CONV_BLOCK]).astype(BF16)
    inv = jnp.asarray(inv_np).astype(BF16)

    w_in_l = w_in[layer]
    w_hy = w_in_l[:, :HY_COLS].astype(BF16)
    w_q = w_in_l[:, Q_OFF:K_OFF].astype(BF16)
    w_kt = w_in_l[:, K_OFF:V_OFF].T.astype(BF16)
    w_v = w_in_l[:, V_OFF:G_OFF].astype(BF16)
    w_g = w_in_l[:, G_OFF:].astype(BF16)
    w_oh = w_out[layer][:HY_WIDTH].astype(BF16)
    w_or = w_out[layer][HY_WIDTH:].astype(BF16)
    w_up = ffn_w_up[layer].astype(BF16)
    w_dn = ffn_w_down[layer].astype(BF16)
    row = lambda a: a.reshape(1, -1)
    w1p = jnp.pad(hy_w1[layer], ((0, HY_FILTER_WIDTH - hy_w1.shape[1]), (0, 0)))
    lg = jnp.stack([jax.nn.log_sigmoid(ret_logit_f[layer].astype(F32)),
                    jax.nn.log_sigmoid(ret_logit_b[layer].astype(F32))])

    cc = jnp.concatenate([c, c_ctx[None, :], jnp.zeros((MOD_ROWS - BATCH - 1, D_MODEL), F32)], axis=0)
    mod = _mod_call(cc, w_mod[layer], row(b_mod[layer]))
    mod3 = mod.reshape(MOD_ROWS, 1, 6 * D_MODEL)
    norm1_r = row(norm1[layer])

    s_ctx = _ctx_call(lg, ctx, mod3, norm1_r, w_kt, w_v)
    zhy, q, kt, v, g = _inproj_call(x, mod3, norm1_r, w_hy, w_q, w_v, w_g, w_kt, rope)

    col = lambda a: a.reshape(-1, 1)
    k_two_sided = _filter_mlp_call(zt, t_row, w1p.T, col(hy_b1[layer]), col(hy_f1[layer]), hy_w2[layer].T,
                                   col(hy_b2[layer]), col(hy_f2[layer]), hy_w3[layer].T, absdelta)
    g_all = _filter_dft_call(k_two_sided, fwd_u)
    conv_w, conv_b = hy_conv_w[layer], row(hy_conv_b[layer])
    y1 = _hyena_call(zhy, 0, zhy, 1, conv_w, conv_b, hy_bias[layer][0:1], g_all, 0, fwd_u, inv, True)
    y_hy = _hyena_call(y1, 0, zhy, 2, conv_w, conv_b, hy_bias[layer][1:2], g_all, 1, fwd_u, inv, False)

    y_ret = _ret_call(lg, q, kt, v, g, s_ctx)

    return _ffn_call(x, y_hy, y_ret, mod3, row(norm2[layer]), row(norm_f), w_oh, w_or, w_up,
                     ffn_conv_w[layer], row(ffn_conv_b[layer]), w_dn)
```

```python
import functools
import math

import numpy as np
import jax
import jax.numpy as jnp
from jax import lax
from jax.experimental import pallas as pl
from jax.experimental.pallas import tpu as pltpu

F32 = jnp.float32
BF16 = jnp.bfloat16

D_MODEL = 1024
BATCH = 8
SEQ = 2048
CTX_LEN = 256
GRID_W = 64
HY_WIDTH = 512
HY_ORDER = 2
HY_EMB_BANDS = 16
HY_FILTER_WIDTH = 64
HY_FAST_DECAY = 0.3
HY_SLOW_DECAY = 1.5
HY_TARGET = 1e-2
RET_WIDTH = 512
RET_HEADS = 4
RET_QK_DIM = 64
RET_V_DIM = 128
ROPE_BASE = 10000.0
D_FF = 2816
EPS = 1e-6
HY_COLS = (HY_ORDER + 1) * HY_WIDTH
Q_OFF = HY_COLS
K_OFF = Q_OFF + RET_HEADS * RET_QK_DIM
V_OFF = K_OFF + RET_HEADS * RET_QK_DIM
G_OFF = V_OFF + RET_WIDTH
K_SCALE = RET_QK_DIM ** -0.5

MOD_ROWS = 16
TOK_TILE = 512
SUBLANES = 8
LANES = 128
HALO = 16
CONV_BLOCK = 512
N_CONV_BLOCKS = SEQ // CONV_BLOCK
FREQ_ROWS = 16
FREQ_LANES = 128
DFT_ROWS = 256
RET_CHUNK = 256
FFN_COLS = 256
VMEM_LIMIT = 56 * 1024 * 1024
HYENA_VMEM_LIMIT = 62 * 1024 * 1024

_NT = (((1,), (1,)), ((), ()))


def _dot(a, b):
    return jnp.dot(a, b, preferred_element_type=F32)


def _dot_hi(a, b):
    return jnp.dot(a, b, preferred_element_type=F32, precision=lax.Precision.HIGHEST)


def _silu(x):
    return x * (1.0 / (1.0 + jnp.exp(-x)))


def _norm_mod(x, gain, shift, scale):
    y = x * lax.rsqrt(jnp.mean(x * x, axis=-1, keepdims=True) + EPS)
    return (y * gain) * (1.0 + scale) + shift


def _resident(shape):
    nd = len(shape)
    return pl.BlockSpec(shape, lambda *_: (0,) * nd, pipeline_mode=pl.Buffered(1))


@functools.lru_cache(maxsize=None)
def _rope_tables():
    pos = np.arange(SEQ)
    row = (pos // GRID_W).astype(np.float64)
    col = (pos % GRID_W).astype(np.float64)
    quarter = RET_QK_DIM // 4
    inv_freq = ROPE_BASE ** (-np.arange(quarter, dtype=np.float64) / quarter)
    ang = np.concatenate([row[:, None] * inv_freq, col[:, None] * inv_freq], axis=-1)
    cos, sin = np.cos(ang), np.sin(ang)
    cos_h = np.concatenate([cos, cos], axis=-1)
    sin_h = np.concatenate([-sin, sin], axis=-1)
    cos_t = np.tile(cos_h, (1, RET_HEADS))
    sin_t = np.tile(sin_h, (1, RET_HEADS))
    return (cos_t.astype(np.float32), sin_t.astype(np.float32),
            np.ascontiguousarray((cos_t * K_SCALE).T).astype(np.float32),
            np.ascontiguousarray((sin_t * K_SCALE).T).astype(np.float32))


@functools.lru_cache(maxsize=None)
def _filter_features():
    lag = np.abs(np.arange(2 * SEQ) - SEQ).astype(np.float64)
    t = lag / (SEQ - 1)
    bands = np.linspace(1e-4, HY_EMB_BANDS - 1, HY_EMB_BANDS)
    ang = 2.0 * math.pi * lag[:, None] * bands[None, :] / SEQ
    z = np.concatenate([t[:, None], np.cos(ang), -np.sin(ang)], axis=-1)
    zp = np.zeros((2 * SEQ, HY_FILTER_WIDTH), np.float64)
    zp[:, :z.shape[1]] = z
    max_decay = math.log(HY_TARGET) / HY_FAST_DECAY
    min_decay = math.log(HY_TARGET) / HY_SLOW_DECAY
    absdelta = np.abs(np.linspace(min_decay, max_decay, HY_WIDTH))[:, None]
    return (np.ascontiguousarray(zp.T).astype(np.float32), t[None, :].astype(np.float32),
            absdelta.astype(np.float32))


@functools.lru_cache(maxsize=None)
def _dft_matrices():
    p = CONV_BLOCK
    n = 2 * p
    f = np.arange(p, dtype=np.float64)[:, None] + 0.5
    t = np.arange(n, dtype=np.float64)[None, :]
    theta = 2.0 * math.pi * f * t / n
    fwd = np.concatenate([np.cos(theta), -np.sin(theta)], axis=0)
    th_out = theta[:, p:].T
    inv = np.concatenate([np.cos(th_out), -np.sin(th_out)], axis=1) / p
    return fwd.astype(np.float32), inv.astype(np.float32)


def _mod_kernel(c_ref, w_ref, b_ref, o_ref):
    s = _silu(c_ref[...]).astype(BF16)
    o_ref[...] = _dot(s, w_ref[...].astype(BF16)) + b_ref[...]


def _mod_call(cc, w_mod, b_mod):
    ncol = 6 * D_MODEL
    blk = ncol // 4
    return pl.pallas_call(
        _mod_kernel,
        grid=(ncol // blk,),
        in_specs=[pl.BlockSpec((MOD_ROWS, D_MODEL), lambda j: (0, 0)),
                  pl.BlockSpec((D_MODEL, blk), lambda j: (0, j)),
                  pl.BlockSpec((1, blk), lambda j: (0, j))],
        out_specs=pl.BlockSpec((MOD_ROWS, blk), lambda j: (0, j)),
        out_shape=jax.ShapeDtypeStruct((MOD_ROWS, ncol), F32),
        compiler_params=pltpu.CompilerParams(vmem_limit_bytes=VMEM_LIMIT),
        name="mod",
    )(cc, w_mod, b_mod)


def _ctx_kernel(lg_ref, x_ref, sh_ref, sc_ref, n1_ref, wkt_ref, wv_ref, s_ref):
    h = _norm_mod(x_ref[0], n1_ref[...], sh_ref[0], sc_ref[0]).astype(BF16)
    kt = lax.dot_general(wkt_ref[...], h, _NT, preferred_element_type=F32) * K_SCALE
    v = _dot(h, wv_ref[...])
    pos = lax.broadcasted_iota(jnp.int32, (1, CTX_LEN), 1).astype(F32)
    for hh in range(RET_HEADS):
        w_f = jnp.exp(lg_ref[0, hh] * (CTX_LEN - 1.0 - pos))
        w_b = jnp.exp(lg_ref[1, hh] * pos)
        kth = kt[hh * RET_QK_DIM:(hh + 1) * RET_QK_DIM, :]
        vh = v[:, hh * RET_V_DIM:(hh + 1) * RET_V_DIM].astype(BF16)
        s_ref[0, hh, 0:RET_QK_DIM, :] = _dot((kth * w_f).astype(BF16), vh)
        s_ref[0, hh, RET_QK_DIM:2 * RET_QK_DIM, :] = _dot((kth * w_b).astype(BF16), vh)


def _ctx_call(lg, ctx, mod3, norm1, w_kt, w_v):
    return pl.pallas_call(
        _ctx_kernel,
        grid=(BATCH,),
        in_specs=[pl.BlockSpec(memory_space=pltpu.SMEM),
                  pl.BlockSpec((1, CTX_LEN, D_MODEL), lambda b: (b, 0, 0)),
                  pl.BlockSpec((1, 1, D_MODEL), lambda b: (BATCH, 0, 0)),
                  pl.BlockSpec((1, 1, D_MODEL), lambda b: (BATCH, 0, 1)),
                  _resident((1, D_MODEL)),
                  _resident((RET_HEADS * RET_QK_DIM, D_MODEL)),
                  _resident((D_MODEL, RET_WIDTH))],
        out_specs=pl.BlockSpec((1, RET_HEADS, 2 * RET_QK_DIM, RET_V_DIM), lambda b: (b, 0, 0, 0)),
        out_shape=jax.ShapeDtypeStruct((BATCH, RET_HEADS, 2 * RET_QK_DIM, RET_V_DIM), F32),
        compiler_params=pltpu.CompilerParams(vmem_limit_bytes=VMEM_LIMIT),
        name="ctx",
    )(lg, ctx, mod3, mod3, norm1, w_kt, w_v)


def _swap_halves(x, axis):
    n = x.shape[axis]
    half = RET_QK_DIM // 2
    idx = lax.broadcasted_iota(jnp.int32, x.shape, axis)
    first = (idx & (RET_QK_DIM - 1)) < half
    return jnp.where(first, pltpu.roll(x, n - half, axis), pltpu.roll(x, half, axis))


def _inproj_kernel(x_ref, sh_ref, sc_ref, n1_ref, why_ref, wq_ref, wv_ref, wg_ref, wkt_ref,
                   cq_ref, sq_ref, ck_ref, sk_ref,
                   zhy_ref, q_ref, kt_ref, v_ref, g_ref):
    th = TOK_TILE // 2
    halves = (slice(0, th), slice(th, TOK_TILE))
    hbs = [_norm_mod(x_ref[0, rows, :], n1_ref[...], sh_ref[0], sc_ref[0]).astype(BF16) for rows in halves]
    for rows, hb in zip(halves, hbs):
        zhy_ref[0, rows, :] = _dot(hb, why_ref[...]).astype(BF16)
        v_ref[0, rows, :] = _dot(hb, wv_ref[...]).astype(BF16)
        g_ref[0, rows, :] = _dot(hb, wg_ref[...]).astype(BF16)
        q = _dot(hb, wq_ref[...])
        q = q * cq_ref[rows, :] + _swap_halves(q, 1) * sq_ref[rows, :]
        for hh in range(RET_HEADS):
            q_ref[0, hh, rows, :] = q[:, hh * RET_QK_DIM:(hh + 1) * RET_QK_DIM].astype(BF16)
        kt = lax.dot_general(wkt_ref[...], hb, _NT, preferred_element_type=F32)
        kt = kt * ck_ref[:, rows] + _swap_halves(kt, 0) * sk_ref[:, rows]
        kt_ref[0, :, rows] = kt.astype(BF16)


def _inproj_call(x, mod3, norm1, w_hy, w_q, w_v, w_g, w_kt, rope):
    cq, sq, ck, sk = rope
    t = TOK_TILE
    qk = RET_HEADS * RET_QK_DIM
    return pl.pallas_call(
        _inproj_kernel,
        grid=(BATCH, SEQ // t),
        in_specs=[pl.BlockSpec((1, t, D_MODEL), lambda b, i: (b, i, 0)),
                  pl.BlockSpec((1, 1, D_MODEL), lambda b, i: (b, 0, 0)),
                  pl.BlockSpec((1, 1, D_MODEL), lambda b, i: (b, 0, 1)),
                  _resident((1, D_MODEL)),
                  _resident((D_MODEL, HY_COLS)),
                  _resident((D_MODEL, qk)),
                  _resident((D_MODEL, RET_WIDTH)),
                  _resident((D_MODEL, RET_WIDTH)),
                  _resident((qk, D_MODEL)),
                  pl.BlockSpec((t, qk), lambda b, i: (i, 0)),
                  pl.BlockSpec((t, qk), lambda b, i: (i, 0)),
                  pl.BlockSpec((qk, t), lambda b, i: (0, i)),
                  pl.BlockSpec((qk, t), lambda b, i: (0, i))],
        out_specs=[pl.BlockSpec((1, t, HY_COLS), lambda b, i: (b, i, 0)),
                   pl.BlockSpec((1, RET_HEADS, t, RET_QK_DIM), lambda b, i: (b, 0, i, 0)),
                   pl.BlockSpec((1, qk, t), lambda b, i: (b, 0, i)),
                   pl.BlockSpec((1, t, RET_WIDTH), lambda b, i: (b, i, 0)),
                   pl.BlockSpec((1, t, RET_WIDTH), lambda b, i: (b, i, 0))],
        out_shape=[jax.ShapeDtypeStruct((BATCH, SEQ, HY_COLS), BF16),
                   jax.ShapeDtypeStruct((BATCH, RET_HEADS, SEQ, RET_QK_DIM), BF16),
                   jax.ShapeDtypeStruct((BATCH, qk, SEQ), BF16),
                   jax.ShapeDtypeStruct((BATCH, SEQ, RET_WIDTH), BF16),
                   jax.ShapeDtypeStruct((BATCH, SEQ, RET_WIDTH), BF16)],
        compiler_params=pltpu.CompilerParams(vmem_limit_bytes=VMEM_LIMIT),
        name="inproj",
    )(x, mod3, mod3, norm1, w_hy, w_q, w_v, w_g, w_kt, cq, sq, ck, sk)


def _filter_mlp_kernel(zt_ref, t_ref, w1t_ref, b1_ref, f1_ref, w2t_ref, b2_ref, f2_ref, w3t_ref, adel_ref,
                       kt_ref):
    hid = jnp.sin(f1_ref[...] * (_dot_hi(w1t_ref[...], zt_ref[...]) + b1_ref[...]))
    hid = jnp.sin(f2_ref[...] * (_dot_hi(w2t_ref[...], hid) + b2_ref[...])).astype(BF16)
    c = HY_WIDTH
    for half in range(2):
        lags = slice(half * SEQ, (half + 1) * SEQ)
        window = jnp.exp(-adel_ref[...] * t_ref[:, lags])
        direction = 1 - half
        for o in range(HY_ORDER):
            r0 = (direction * HY_ORDER + o) * c
            w3 = w3t_ref[r0:r0 + c, :].astype(BF16)
            kt_ref[o, :, lags] = (_dot(w3, hid[:, lags]) * window).astype(BF16)


def _filter_mlp_call(zt, t_row, w1t, b1, f1, w2t, b2, f2, w3t, absdelta):
    fw = HY_FILTER_WIDTH
    n_out = 2 * HY_ORDER * HY_WIDTH
    return pl.pallas_call(
        _filter_mlp_kernel,
        grid=(1,),
        in_specs=[_resident((fw, 2 * SEQ)), _resident((1, 2 * SEQ)),
                  _resident((fw, fw)), _resident((fw, 1)), _resident((fw, 1)),
                  _resident((fw, fw)), _resident((fw, 1)), _resident((fw, 1)),
                  _resident((n_out, fw)), _resident((HY_WIDTH, 1))],
        out_specs=pl.BlockSpec((HY_ORDER, HY_WIDTH, 2 * SEQ), lambda i: (0, 0, 0)),
        out_shape=jax.ShapeDtypeStruct((HY_ORDER, HY_WIDTH, 2 * SEQ), BF16),
        compiler_params=pltpu.CompilerParams(vmem_limit_bytes=VMEM_LIMIT),
        name="filter_mlp",
    )(zt, t_row, w1t, b1, f1, w2t, b2, f2, w3t, absdelta)


_FILTER_PLANES = (
    {0: 1},
    {-1: 1, 0: -1},
    {1: 1, 0: -1},
    {-2: 1, 0: -1},
    {-3: 1, -1: -1, -2: -1, 0: 1},
    {-1: 1, 1: -1, -2: -1, 0: 1},
    {2: 1, 0: -1},
    {1: 1, -1: -1, 2: -1, 0: 1},
    {3: 1, 1: -1, 2: -1, 0: 1},
)
DFT_BLOCKS_PER_STEP = 2
PLANES_PER_STEP = 3


def _filter_dft_kernel(kt_ref, fwd_ref, h_ref, t_scr):
    p = CONV_BLOCK
    n_dft_steps = 2 * N_CONV_BLOCKS // DFT_BLOCKS_PER_STEP
    step = pl.program_id(1)

    for s in range(n_dft_steps):
        @pl.when(step == s)
        def _(s=s):
            for k in range(DFT_BLOCKS_PER_STEP):
                t_scr[s * DFT_BLOCKS_PER_STEP + k] = lax.dot_general(
                    fwd_ref[...], kt_ref[0, :, k * p:(k + 1) * p], _NT, preferred_element_type=F32)
            for k in range(PLANES_PER_STEP):
                h_ref[0, k] = t_scr[s * DFT_BLOCKS_PER_STEP]

    row = lax.broadcasted_iota(jnp.int32, (p, 1), 0)
    sign = (1 - 2 * (row & 1)).astype(F32)
    re, im = slice(0, p), slice(p, 2 * p)

    def plane(k, coefs):
        def comb(shift, rows):
            acc = None
            for d, c in coefs.items():
                term = t_scr[d + N_CONV_BLOCKS - 1 + shift, rows, :]
                if acc is None:
                    acc = term
                else:
                    acc = acc + term if c > 0 else acc - term
            return acc

        h_ref[0, k, re, :] = comb(0, re) + sign * comb(1, im)
        h_ref[0, k, im, :] = comb(0, im) - sign * comb(1, re)

    for s in range(len(_FILTER_PLANES) // PLANES_PER_STEP):
        @pl.when(step == n_dft_steps + s)
        def _(s=s):
            for k in range(PLANES_PER_STEP):
                plane(k, _FILTER_PLANES[s * PLANES_PER_STEP + k])


def _filter_dft_call(kt, fwd_u):
    p = CONV_BLOCK
    n_blk = 2 * N_CONV_BLOCKS
    n_planes = len(_FILTER_PLANES)
    n_dft_steps = n_blk // DFT_BLOCKS_PER_STEP
    return pl.pallas_call(
        _filter_dft_kernel,
        grid=(HY_ORDER, n_dft_steps + n_planes // PLANES_PER_STEP),
        in_specs=[pl.BlockSpec((1, HY_WIDTH, DFT_BLOCKS_PER_STEP * p),
                               lambda o, s: (o, 0, jnp.minimum(s, n_dft_steps - 1))),
                  _resident((2 * p, p))],
        out_specs=pl.BlockSpec((1, PLANES_PER_STEP, 2 * p, HY_WIDTH),
                               lambda o, s: (o, jnp.maximum(s - n_dft_steps, 0), 0, 0)),
        out_shape=jax.ShapeDtypeStruct((HY_ORDER, n_planes, 2 * p, HY_WIDTH), F32),
        scratch_shapes=[pltpu.VMEM((n_blk, 2 * p, HY_WIDTH), F32)],
        compiler_params=pltpu.CompilerParams(vmem_limit_bytes=VMEM_LIMIT),
        name="filter_dft",
    )(kt, fwd_u)


def _conv3_rows(ref, j, n_blocks, rows, w, b):
    main = ref[0, j * rows:(j + 1) * rows, :].astype(F32)
    cols = main.shape[1]
    zeros = jnp.zeros((HALO, cols), F32)
    prev = ref[0, j * rows - HALO:j * rows, :].astype(F32) if j > 0 else zeros
    nxt = ref[0, (j + 1) * rows:(j + 1) * rows + HALO, :].astype(F32) if j < n_blocks - 1 else zeros
    ext = jnp.concatenate([prev, main, nxt], axis=0)
    n = rows + 2 * HALO
    before = pltpu.roll(ext, 1, 0)[HALO:HALO + rows]
    after = pltpu.roll(ext, n - 1, 0)[HALO:HALO + rows]
    return before * w[0:1] + main * w[1:2] + after * w[2:3] + b


def _hyena_kernel(u_ref, zg_ref, cwu_ref, cbu_ref, cwg_ref, cbg_ref, skip_ref, h_ref, fwd_ref, inv_ref,
                  o_ref, uf_scr, ub_scr, y_scr, *, conv_u):
    p = CONV_BLOCK
    nb = N_CONV_BLOCKS

    def prepare(j):
        if conv_u:
            ub_scr[j * p:(j + 1) * p, :] = _conv3_rows(u_ref, j, nb, p, cwu_ref[...], cbu_ref[...]).astype(BF16)

    def u_block(j):
        return ub_scr[j * p:(j + 1) * p, :] if conv_u else u_ref[0, j * p:(j + 1) * p, :]

    prepare(0)
    for j in range(nb):
        if j + 1 < nb:
            prepare(j + 1)
        for r in range(2 * p // DFT_ROWS):
            rows = slice(r * DFT_ROWS, (r + 1) * DFT_ROWS)
            uf_scr[j, rows, :] = _dot(fwd_ref[rows, :], u_block(j))

    def cadd(a, b):
        return a[0] + b[0], a[1] + b[1]

    def cmul(m, x):
        return m[0] * x[0] - m[1] * x[1], m[0] * x[1] + m[1] * x[0]

    def toeplitz2(k0, x0, x1, re, im, ln):
        m0, mu, ml = ((h_ref[0, k0 + t, re, ln], h_ref[0, k0 + t, im, ln]) for t in range(3))
        p1 = cmul(m0, cadd(x0, x1))
        return cadd(p1, cmul(mu, x1)), cadd(p1, cmul(ml, x0))

    assert nb == 4
    for r in range(p // FREQ_ROWS):
        re = slice(r * FREQ_ROWS, (r + 1) * FREQ_ROWS)
        im = slice(p + r * FREQ_ROWS, p + (r + 1) * FREQ_ROWS)
        for cb in range(HY_WIDTH // FREQ_LANES):
            ln = slice(cb * FREQ_LANES, (cb + 1) * FREQ_LANES)
            u = [(uf_scr[j, re, ln], uf_scr[j, im, ln]) for j in range(nb)]
            d0, d1 = toeplitz2(0, cadd(u[0], u[2]), cadd(u[1], u[3]), re, im, ln)
            b0, b1 = toeplitz2(3, u[2], u[3], re, im, ln)
            c0, c1 = toeplitz2(6, u[0], u[1], re, im, ln)
            for i, yi in enumerate((cadd(d0, b0), cadd(d1, b1), cadd(d0, c0), cadd(d1, c1))):
                y_scr[i, re, ln] = yi[0].astype(BF16)
                y_scr[i, im, ln] = yi[1].astype(BF16)

    for i in range(nb):
        gate = _conv3_rows(zg_ref, i, nb, p, cwg_ref[...], cbg_ref[...])
        y = _dot(inv_ref[...], y_scr[i])
        o_ref[0, i * p:(i + 1) * p, :] = (
            gate * (y + u_block(i).astype(F32) * skip_ref[...])).astype(BF16)


def _hyena_call(u, u_col, zhy, gate_col, conv_w, conv_b, skip, g_all, order, fwd_u, inv, conv_u):
    p = CONV_BLOCK
    c = HY_WIDTH
    ucol = u_col if conv_u else 0
    cwu = conv_w[:, ucol * c:(ucol + 1) * c]
    cbu = conv_b[:, ucol * c:(ucol + 1) * c]
    cwg = conv_w[:, gate_col * c:(gate_col + 1) * c]
    cbg = conv_b[:, gate_col * c:(gate_col + 1) * c]
    return pl.pallas_call(
        functools.partial(_hyena_kernel, conv_u=conv_u),
        grid=(BATCH,),
        in_specs=[pl.BlockSpec((1, SEQ, c), lambda b: (b, 0, u_col)),
                  pl.BlockSpec((1, SEQ, c), lambda b: (b, 0, gate_col)),
                  _resident((3, c)), _resident((1, c)), _resident((3, c)), _resident((1, c)),
                  _resident((1, c)),
                  pl.BlockSpec((1, len(_FILTER_PLANES), 2 * p, c), lambda b: (order, 0, 0, 0),
                               pipeline_mode=pl.Buffered(1)),
                  _resident((2 * p, p)),
                  _resident((p, 2 * p))],
        out_specs=pl.BlockSpec((1, SEQ, c), lambda b: (b, 0, 0)),
        out_shape=jax.ShapeDtypeStruct((BATCH, SEQ, c), BF16),
        scratch_shapes=[pltpu.VMEM((N_CONV_BLOCKS, 2 * p, c), F32),
                        pltpu.VMEM((SEQ, c) if conv_u else (SUBLANES * 2, LANES), BF16),
                        pltpu.VMEM((N_CONV_BLOCKS, 2 * p, c), BF16)],
        compiler_params=pltpu.CompilerParams(vmem_limit_bytes=HYENA_VMEM_LIMIT),
        name="hyena%d" % order,
    )(u, zhy, cwu, cbu, cwg, cbg, skip, g_all, fwd_u, inv)


def _ret_kernel(lg_ref, q_ref, kt_ref, v_ref, g_ref, s_ref, o_ref, b_scr):
    c = RET_CHUNK
    nc = SEQ // c
    dk = RET_QK_DIM
    h = pl.program_id(1)
    lg_f = lg_ref[0, h]
    lg_b = lg_ref[1, h]
    ii = lax.broadcasted_iota(jnp.int32, (c, c), 0).astype(F32)
    jj = lax.broadcasted_iota(jnp.int32, (c, c), 1).astype(F32)
    dif = ii - jj
    decay = jnp.where(dif >= 0.0, jnp.exp(lg_f * jnp.maximum(dif, 0.0)),
                      jnp.exp(lg_b * jnp.maximum(-dif, 0.0)))
    pos_r = lax.broadcasted_iota(jnp.int32, (1, c), 1).astype(F32)
    pos_c = lax.broadcasted_iota(jnp.int32, (c, 1), 0).astype(F32)
    kw_f = jnp.exp(lg_f * (c - 1.0 - pos_r))
    kw_b = jnp.exp(lg_b * pos_r)
    qw_f = jnp.exp(lg_f * (pos_c + 1.0))
    qw_b = jnp.exp(lg_b * (c - pos_c))
    ones = jnp.ones((1, RET_V_DIM), F32)
    dec_f = jnp.exp(lg_f * float(c) * ones)
    dec_b = jnp.exp(lg_b * float(c) * ones)

    state = s_ref[0, 0, dk:2 * dk, :]
    b_scr[nc - 1] = state
    for n in range(nc - 1, 0, -1):
        ktn = (kt_ref[0, :, n * c:(n + 1) * c].astype(F32) * kw_b).astype(BF16)
        state = state * dec_b + _dot(ktn, v_ref[0, n * c:(n + 1) * c, :])
        b_scr[n - 1] = state

    state = s_ref[0, 0, 0:dk, :]
    for n in range(nc):
        qn = q_ref[0, 0, n * c:(n + 1) * c, :]
        ktn = kt_ref[0, :, n * c:(n + 1) * c]
        vn = v_ref[0, n * c:(n + 1) * c, :]
        scores = (_dot(qn, ktn) * decay).astype(BF16)
        qf = qn.astype(F32)
        o = _dot(scores, vn)
        o = o + _dot((qf * qw_f).astype(BF16), state.astype(BF16))
        o = o + _dot((qf * qw_b).astype(BF16), b_scr[n].astype(BF16))
        state = state * dec_f + _dot((ktn.astype(F32) * kw_f).astype(BF16), vn)
        o = o * lax.rsqrt(jnp.mean(o * o, axis=-1, keepdims=True) + EPS)
        gate = g_ref[0, n * c:(n + 1) * c, :].astype(F32)
        o_ref[0, n * c:(n + 1) * c, :] = (_silu(gate) * o).astype(BF16)


def _ret_call(lg, q, kt, v, g, s):
    dk, dv = RET_QK_DIM, RET_V_DIM
    return pl.pallas_call(
        _ret_kernel,
        grid=(BATCH, RET_HEADS),
        in_specs=[pl.BlockSpec(memory_space=pltpu.SMEM),
                  pl.BlockSpec((1, 1, SEQ, dk), lambda b, h: (b, h, 0, 0)),
                  pl.BlockSpec((1, dk, SEQ), lambda b, h: (b, h, 0)),
                  pl.BlockSpec((1, SEQ, dv), lambda b, h: (b, 0, h)),
                  pl.BlockSpec((1, SEQ, dv), lambda b, h: (b, 0, h)),
                  pl.BlockSpec((1, 1, 2 * dk, dv), lambda b, h: (b, h, 0, 0))],
        out_specs=pl.BlockSpec((1, SEQ, dv), lambda b, h: (b, 0, h)),
        out_shape=jax.ShapeDtypeStruct((BATCH, SEQ, RET_WIDTH), BF16),
        scratch_shapes=[pltpu.VMEM((SEQ // RET_CHUNK, dk, dv), F32)],
        compiler_params=pltpu.CompilerParams(vmem_limit_bytes=VMEM_LIMIT),
        name="ret",
    )(lg, q, kt, v, g, s)


def _ffn_kernel(x_ref, xp_ref, xn_ref, yh_ref, yhp_ref, yhn_ref, yr_ref, yrp_ref, yrn_ref,
                g1_ref, sh_ref, sc_ref, g2_ref, n2_ref, nf_ref,
                woh_ref, wor_ref, wup_ref, cw_ref, cb_ref, wdn_ref,
                o_ref, hb_scr, x1_scr, av_scr, ag_scr, act_scr):
    t = TOK_TILE
    i = pl.program_id(1)
    nt = pl.num_programs(1)
    th = t // 2
    tile_halves = (slice(0, th), slice(th, t))

    def mixed(xr, yh, yr, rows=slice(None)):
        return xr[0, rows, :] + g1_ref[0] * (_dot(yh[0, rows, :], woh_ref[...]) + _dot(yr[0, rows, :], wor_ref[...]))

    def hidden(x1):
        return _norm_mod(x1, n2_ref[...], sh_ref[0], sc_ref[0])

    half = (t + 2 * HALO) // 2
    up_halves = (slice(0, half), slice(half, 2 * half))

    def up(slot, cblk, rows):
        c0 = cblk * FFN_COLS
        hb = hb_scr[rows, :]
        av_scr[slot, rows, :] = _dot(hb, wup_ref[:, c0:c0 + FFN_COLS])
        ag_scr[slot, rows, :] = _dot(hb, wup_ref[:, D_FF + c0:D_FF + c0 + FFN_COLS])

    for rows in tile_halves:
        x1_scr[rows, :] = mixed(x_ref, yh_ref, yr_ref, rows)
    hp = hidden(mixed(xp_ref, yhp_ref, yrp_ref))
    hn = hidden(mixed(xn_ref, yhn_ref, yrn_ref))
    hb_scr[0:HALO, :] = jnp.where(i > 0, hp, 0.0).astype(BF16)
    hb_scr[HALO + t:2 * HALO + t, :] = jnp.where(i < nt - 1, hn, 0.0).astype(BF16)
    hb_scr[HALO:HALO + th, :] = hidden(x1_scr[tile_halves[0], :]).astype(BF16)
    up(0, 0, up_halves[0])
    hb_scr[HALO + th:HALO + t, :] = hidden(x1_scr[tile_halves[1], :]).astype(BF16)
    up(0, 0, up_halves[1])

    def conv(scr, slot, col):
        w = cw_ref[:, col:col + FFN_COLS]
        return (scr[slot, HALO - 1:HALO - 1 + t, :] * w[0:1] + scr[slot, HALO:HALO + t, :] * w[1:2]
                + scr[slot, HALO + 1:HALO + 1 + t, :] * w[2:3] + cb_ref[:, col:col + FFN_COLS])

    n_blk = D_FF // FFN_COLS
    for cblk in range(n_blk):
        c0 = cblk * FFN_COLS
        slot = cblk % 2
        if cblk + 1 < n_blk:
            for rows in up_halves:
                up(1 - slot, cblk + 1, rows)
        act_scr[:, c0:c0 + FFN_COLS] = (
            _silu(conv(ag_scr, slot, D_FF + c0)) * conv(av_scr, slot, c0)).astype(BF16)

    down = [_dot(act_scr[rows, :], wdn_ref[...]) for rows in tile_halves]
    for rows, ffn in zip(tile_halves, down):
        x2 = x1_scr[rows, :] + g2_ref[0] * ffn
        o_ref[0, rows, :] = x2 * lax.rsqrt(jnp.mean(x2 * x2, axis=-1, keepdims=True) + EPS) * nf_ref[...]


def _ffn_call(x, y_hy, y_ret, mod3, norm2, norm_f, w_oh, w_or, w_up, conv_w, conv_b, w_dn):
    t = TOK_TILE
    r = t // HALO
    last = SEQ // HALO - 1

    def main(width):
        return pl.BlockSpec((1, t, width), lambda b, i: (b, i, 0))

    def prev(width):
        return pl.BlockSpec((1, HALO, width), lambda b, i: (b, jnp.maximum(i * r - 1, 0), 0))

    def nxt(width):
        return pl.BlockSpec((1, HALO, width), lambda b, i: (b, jnp.minimum((i + 1) * r, last), 0))

    def modrow(k):
        return pl.BlockSpec((1, 1, D_MODEL), lambda b, i: (b, 0, k))

    return pl.pallas_call(
        _ffn_kernel,
        grid=(BATCH, SEQ // t),
        in_specs=[main(D_MODEL), prev(D_MODEL), nxt(D_MODEL),
                  main(HY_WIDTH), prev(HY_WIDTH), nxt(HY_WIDTH),
                  main(RET_WIDTH), prev(RET_WIDTH), nxt(RET_WIDTH),
                  modrow(2), modrow(3), modrow(4), modrow(5),
                  _resident((1, D_MODEL)), _resident((1, D_MODEL)),
                  _resident((HY_WIDTH, D_MODEL)), _resident((RET_WIDTH, D_MODEL)),
                  _resident((D_MODEL, 2 * D_FF)),
                  _resident((3, 2 * D_FF)), _resident((1, 2 * D_FF)),
                  _resident((D_FF, D_MODEL))],
        out_specs=pl.BlockSpec((1, t, D_MODEL), lambda b, i: (b, i, 0)),
        out_shape=jax.ShapeDtypeStruct((BATCH, SEQ, D_MODEL), F32),
        scratch_shapes=[pltpu.VMEM((t + 2 * HALO, D_MODEL), BF16),
                        pltpu.VMEM((t, D_MODEL), F32),
                        pltpu.VMEM((2, t + 2 * HALO, FFN_COLS), F32),
                        pltpu.VMEM((2, t + 2 * HALO, FFN_COLS), F32),
                        pltpu.VMEM((t, D_FF), BF16)],
        compiler_params=pltpu.CompilerParams(vmem_limit_bytes=VMEM_LIMIT),
        name="ffn",
    )(x, x, x, y_hy, y_hy, y_hy, y_ret, y_ret, y_ret, mod3, mod3, mod3, mod3,
      norm2, norm_f, w_oh, w_or, w_up, conv_w, conv_b, w_dn)


def kernel(x, c, ctx, c_ctx, w_mod, b_mod, norm1, w_in, hy_conv_w, hy_conv_b, hy_w1, hy_b1, hy_f1,
           hy_w2, hy_b2, hy_f2, hy_w3, hy_bias, ret_logit_f, ret_logit_b, w_out, norm2,
           ffn_w_up, ffn_conv_w, ffn_conv_b, ffn_w_down, norm_f):
    layer = 0
    rope = tuple(jnp.asarray(a) for a in _rope_tables())
    zt, t_row, absdelta = (jnp.asarray(a) for a in _filter_features())
    fwd_np, inv_np = _dft_matrices()
    fwd_u = jnp.asarray(fwd_np[:, :CONV_BLOCK]).astype(BF16)
    inv = jnp.asarray(inv_np).astype(BF16)

    w_in_l = w_in[layer]
    w_hy = w_in_l[:, :HY_COLS].astype(BF16)
    w_q = w_in_l[:, Q_OFF:K_OFF].astype(BF16)
    w_kt = w_in_l[:, K_OFF:V_OFF].T.astype(BF16)
    w_v = w_in_l[:, V_OFF:G_OFF].astype(BF16)
    w_g = w_in_l[:, G_OFF:].astype(BF16)
    w_oh = w_out[layer][:HY_WIDTH].astype(BF16)
    w_or = w_out[layer][HY_WIDTH:].astype(BF16)
    w_up = ffn_w_up[layer].astype(BF16)
    w_dn = ffn_w_down[layer].astype(BF16)
    row = lambda a: a.reshape(1, -1)
    col = lambda a: a.reshape(-1, 1)
    w1p = jnp.pad(hy_w1[layer], ((0, HY_FILTER_WIDTH - hy_w1.shape[1]), (0, 0)))
    lg = jnp.stack([jax.nn.log_sigmoid(ret_logit_f[layer].astype(F32)),
                    jax.nn.log_sigmoid(ret_logit_b[layer].astype(F32))])

    cc = jnp.concatenate([c, c_ctx[None, :], jnp.zeros((MOD_ROWS - BATCH - 1, D_MODEL), F32)], axis=0)
    mod = _mod_call(cc, w_mod[layer], row(b_mod[layer]))
    mod3 = mod.reshape(MOD_ROWS, 1, 6 * D_MODEL)
    norm1_r = row(norm1[layer])

    s_ctx = _ctx_call(lg, ctx, mod3, norm1_r, w_kt, w_v)
    zhy, q, kt, v, g = _inproj_call(x, mod3, norm1_r, w_hy, w_q, w_v, w_g, w_kt, rope)

    k_two_sided = _filter_mlp_call(zt, t_row, w1p.T, col(hy_b1[layer]), col(hy_f1[layer]), hy_w2[layer].T,
                                   col(hy_b2[layer]), col(hy_f2[layer]), hy_w3[layer].T, absdelta)
    g_all = _filter_dft_call(k_two_sided, fwd_u)
    conv_w, conv_b = hy_conv_w[layer], row(hy_conv_b[layer])
    y1 = _hyena_call(zhy, 0, zhy, 1, conv_w, conv_b, hy_bias[layer][0:1], g_all, 0, fwd_u, inv, True)
    y_hy = _hyena_call(y1, 0, zhy, 2, conv_w, conv_b, hy_bias[layer][1:2], g_all, 1, fwd_u, inv, False)

    y_ret = _ret_call(lg, q, kt, v, g, s_ctx)

    return _ffn_call(x, y_hy, y_ret, mod3, row(norm2[layer]), row(norm_f), w_oh, w_or, w_up,
                     ffn_conv_w[layer], row(ffn_conv_b[layer]), w_dn)
```

```python
import functools
import math

import numpy as np
import jax
import jax.numpy as jnp
from jax import lax
from jax.experimental import pallas as pl
from jax.experimental.pallas import tpu as pltpu

F32 = jnp.float32
BF16 = jnp.bfloat16

D_MODEL = 1024
BATCH = 8
SEQ = 2048
CTX_LEN = 256
GRID_W = 64
HY_WIDTH = 512
HY_ORDER = 2
HY_EMB_BANDS = 16
HY_FILTER_WIDTH = 64
HY_FAST_DECAY = 0.3
HY_SLOW_DECAY = 1.5
HY_TARGET = 1e-2
RET_WIDTH = 512
RET_HEADS = 4
RET_QK_DIM = 64
RET_V_DIM = 128
ROPE_BASE = 10000.0
D_FF = 2816
EPS = 1e-6
HY_COLS = (HY_ORDER + 1) * HY_WIDTH
Q_OFF = HY_COLS
K_OFF = Q_OFF + RET_HEADS * RET_QK_DIM
V_OFF = K_OFF + RET_HEADS * RET_QK_DIM
G_OFF = V_OFF + RET_WIDTH
K_SCALE = RET_QK_DIM ** -0.5

MOD_ROWS = 16
TOK_TILE = 512
IN_TILE = 1024
IN_PIECE = 256
SUBLANES = 8
LANES = 128
HALO = 16
CONV_BLOCK = 512
N_CONV_BLOCKS = SEQ // CONV_BLOCK
FREQ_ROWS = 16
FREQ_LANES = 128
DFT_ROWS = 256
RET_CHUNK = 256
FFN_COLS = 256
VMEM_LIMIT = 56 * 1024 * 1024
HYENA_VMEM_LIMIT = 62 * 1024 * 1024

_NT = (((1,), (1,)), ((), ()))


def _dot(a, b):
    return jnp.dot(a, b, preferred_element_type=F32)


def _dot_hi(a, b):
    return jnp.dot(a, b, preferred_element_type=F32, precision=lax.Precision.HIGHEST)


def _silu(x):
    return x * (1.0 / (1.0 + jnp.exp(-x)))


def _norm_mod(x, gain, shift, scale):
    y = x * lax.rsqrt(jnp.mean(x * x, axis=-1, keepdims=True) + EPS)
    return (y * gain) * (1.0 + scale) + shift


def _resident(shape):
    nd = len(shape)
    return pl.BlockSpec(shape, lambda *_: (0,) * nd, pipeline_mode=pl.Buffered(1))


@functools.lru_cache(maxsize=None)
def _rope_tables():
    pos = np.arange(SEQ)
    row = (pos // GRID_W).astype(np.float64)
    col = (pos % GRID_W).astype(np.float64)
    quarter = RET_QK_DIM // 4
    inv_freq = ROPE_BASE ** (-np.arange(quarter, dtype=np.float64) / quarter)
    ang = np.concatenate([row[:, None] * inv_freq, col[:, None] * inv_freq], axis=-1)
    cos, sin = np.cos(ang), np.sin(ang)
    cos_h = np.concatenate([cos, cos], axis=-1)
    sin_h = np.concatenate([-sin, sin], axis=-1)
    cos_t = np.tile(cos_h, (1, RET_HEADS))
    sin_t = np.tile(sin_h, (1, RET_HEADS))
    return (cos_t.astype(np.float32), sin_t.astype(np.float32),
            np.ascontiguousarray((cos_t * K_SCALE).T).astype(np.float32),
            np.ascontiguousarray((sin_t * K_SCALE).T).astype(np.float32))


@functools.lru_cache(maxsize=None)
def _filter_features():
    lag = np.abs(np.arange(2 * SEQ) - SEQ).astype(np.float64)
    t = lag / (SEQ - 1)
    bands = np.linspace(1e-4, HY_EMB_BANDS - 1, HY_EMB_BANDS)
    ang = 2.0 * math.pi * lag[:, None] * bands[None, :] / SEQ
    z = np.concatenate([t[:, None], np.cos(ang), -np.sin(ang)], axis=-1)
    zp = np.zeros((2 * SEQ, HY_FILTER_WIDTH), np.float64)
    zp[:, :z.shape[1]] = z
    max_decay = math.log(HY_TARGET) / HY_FAST_DECAY
    min_decay = math.log(HY_TARGET) / HY_SLOW_DECAY
    absdelta = np.abs(np.linspace(min_decay, max_decay, HY_WIDTH))[:, None]
    return (np.ascontiguousarray(zp.T).astype(np.float32), t[None, :].astype(np.float32),
            absdelta.astype(np.float32))


@functools.lru_cache(maxsize=None)
def _dft_matrices():
    p = CONV_BLOCK
    n = 2 * p
    f = np.arange(p, dtype=np.float64)[:, None] + 0.5
    t = np.arange(n, dtype=np.float64)[None, :]
    theta = 2.0 * math.pi * f * t / n
    fwd = np.concatenate([np.cos(theta), -np.sin(theta)], axis=0)
    th_out = theta[:, p:].T
    inv = np.concatenate([np.cos(th_out), -np.sin(th_out)], axis=1) / p
    return fwd.astype(np.float32), inv.astype(np.float32)


def _mod_kernel(c_ref, w_ref, b_ref, o_ref):
    s = _silu(c_ref[...]).astype(BF16)
    o_ref[...] = _dot(s, w_ref[...].astype(BF16)) + b_ref[...]


def _mod_call(cc, w_mod, b_mod):
    ncol = 6 * D_MODEL
    blk = ncol // 4
    return pl.pallas_call(
        _mod_kernel,
        grid=(ncol // blk,),
        in_specs=[pl.BlockSpec((MOD_ROWS, D_MODEL), lambda j: (0, 0)),
                  pl.BlockSpec((D_MODEL, blk), lambda j: (0, j)),
                  pl.BlockSpec((1, blk), lambda j: (0, j))],
        out_specs=pl.BlockSpec((MOD_ROWS, blk), lambda j: (0, j)),
        out_shape=jax.ShapeDtypeStruct((MOD_ROWS, ncol), F32),
        compiler_params=pltpu.CompilerParams(vmem_limit_bytes=VMEM_LIMIT),
        name="mod",
    )(cc, w_mod, b_mod)


def _ctx_kernel(lg_ref, x_ref, sh_ref, sc_ref, n1_ref, wkt_ref, wv_ref, s_ref):
    h = _norm_mod(x_ref[0], n1_ref[...], sh_ref[0], sc_ref[0]).astype(BF16)
    kt = lax.dot_general(wkt_ref[...], h, _NT, preferred_element_type=F32) * K_SCALE
    v = _dot(h, wv_ref[...])
    pos = lax.broadcasted_iota(jnp.int32, (1, CTX_LEN), 1).astype(F32)
    for hh in range(RET_HEADS):
        w_f = jnp.exp(lg_ref[0, hh] * (CTX_LEN - 1.0 - pos))
        w_b = jnp.exp(lg_ref[1, hh] * pos)
        kth = kt[hh * RET_QK_DIM:(hh + 1) * RET_QK_DIM, :]
        vh = v[:, hh * RET_V_DIM:(hh + 1) * RET_V_DIM].astype(BF16)
        s_ref[0, hh, 0:RET_QK_DIM, :] = _dot((kth * w_f).astype(BF16), vh)
        s_ref[0, hh, RET_QK_DIM:2 * RET_QK_DIM, :] = _dot((kth * w_b).astype(BF16), vh)


def _ctx_call(lg, ctx, mod3, norm1, w_kt, w_v):
    return pl.pallas_call(
        _ctx_kernel,
        grid=(BATCH,),
        in_specs=[pl.BlockSpec(memory_space=pltpu.SMEM),
                  pl.BlockSpec((1, CTX_LEN, D_MODEL), lambda b: (b, 0, 0)),
                  pl.BlockSpec((1, 1, D_MODEL), lambda b: (BATCH, 0, 0)),
                  pl.BlockSpec((1, 1, D_MODEL), lambda b: (BATCH, 0, 1)),
                  _resident((1, D_MODEL)),
                  _resident((RET_HEADS * RET_QK_DIM, D_MODEL)),
                  _resident((D_MODEL, RET_WIDTH))],
        out_specs=pl.BlockSpec((1, RET_HEADS, 2 * RET_QK_DIM, RET_V_DIM), lambda b: (b, 0, 0, 0)),
        out_shape=jax.ShapeDtypeStruct((BATCH, RET_HEADS, 2 * RET_QK_DIM, RET_V_DIM), F32),
        compiler_params=pltpu.CompilerParams(vmem_limit_bytes=VMEM_LIMIT),
        name="ctx",
    )(lg, ctx, mod3, mod3, norm1, w_kt, w_v)


def _swap_halves(x, axis):
    n = x.shape[axis]
    half = RET_QK_DIM // 2
    idx = lax.broadcasted_iota(jnp.int32, x.shape, axis)
    first = (idx & (RET_QK_DIM - 1)) < half
    return jnp.where(first, pltpu.roll(x, n - half, axis), pltpu.roll(x, half, axis))


def _inproj_kernel(x_ref, sh_ref, sc_ref, n1_ref, why_ref, wq_ref, wv_ref, wg_ref, wkt_ref,
                   cq_ref, sq_ref, ck_ref, sk_ref,
                   zhy_ref, q_ref, kt_ref, v_ref, g_ref):
    pieces = [slice(r, r + IN_PIECE) for r in range(0, IN_TILE, IN_PIECE)]
    hbs = [_norm_mod(x_ref[0, rows, :], n1_ref[...], sh_ref[0], sc_ref[0]).astype(BF16) for rows in pieces]
    for rows, hb in zip(pieces, hbs):
        zhy_ref[0, rows, :] = _dot(hb, why_ref[...]).astype(BF16)
        v_ref[0, rows, :] = _dot(hb, wv_ref[...]).astype(BF16)
        g_ref[0, rows, :] = _dot(hb, wg_ref[...]).astype(BF16)
        q = _dot(hb, wq_ref[...])
        q = q * cq_ref[rows, :] + _swap_halves(q, 1) * sq_ref[rows, :]
        for hh in range(RET_HEADS):
            q_ref[0, hh, rows, :] = q[:, hh * RET_QK_DIM:(hh + 1) * RET_QK_DIM].astype(BF16)
        kt = lax.dot_general(wkt_ref[...], hb, _NT, preferred_element_type=F32)
        kt = kt * ck_ref[:, rows] + _swap_halves(kt, 0) * sk_ref[:, rows]
        kt_ref[0, :, rows] = kt.astype(BF16)


def _inproj_call(x, mod3, norm1, w_hy, w_q, w_v, w_g, w_kt, rope):
    cq, sq, ck, sk = rope
    t = IN_TILE
    qk = RET_HEADS * RET_QK_DIM
    return pl.pallas_call(
        _inproj_kernel,
        grid=(BATCH, SEQ // t),
        in_specs=[pl.BlockSpec((1, t, D_MODEL), lambda b, i: (b, i, 0)),
                  pl.BlockSpec((1, 1, D_MODEL), lambda b, i: (b, 0, 0)),
                  pl.BlockSpec((1, 1, D_MODEL), lambda b, i: (b, 0, 1)),
                  _resident((1, D_MODEL)),
                  _resident((D_MODEL, HY_COLS)),
                  _resident((D_MODEL, qk)),
                  _resident((D_MODEL, RET_WIDTH)),
                  _resident((D_MODEL, RET_WIDTH)),
                  _resident((qk, D_MODEL)),
                  pl.BlockSpec((t, qk), lambda b, i: (i, 0)),
                  pl.BlockSpec((t, qk), lambda b, i: (i, 0)),
                  pl.BlockSpec((qk, t), lambda b, i: (0, i)),
                  pl.BlockSpec((qk, t), lambda b, i: (0, i))],
        out_specs=[pl.BlockSpec((1, t, HY_COLS), lambda b, i: (b, i, 0)),
                   pl.BlockSpec((1, RET_HEADS, t, RET_QK_DIM), lambda b, i: (b, 0, i, 0)),
                   pl.BlockSpec((1, qk, t), lambda b, i: (b, 0, i)),
                   pl.BlockSpec((1, t, RET_WIDTH), lambda b, i: (b, i, 0)),
                   pl.BlockSpec((1, t, RET_WIDTH), lambda b, i: (b, i, 0))],
        out_shape=[jax.ShapeDtypeStruct((BATCH, SEQ, HY_COLS), BF16),
                   jax.ShapeDtypeStruct((BATCH, RET_HEADS, SEQ, RET_QK_DIM), BF16),
                   jax.ShapeDtypeStruct((BATCH, qk, SEQ), BF16),
                   jax.ShapeDtypeStruct((BATCH, SEQ, RET_WIDTH), BF16),
                   jax.ShapeDtypeStruct((BATCH, SEQ, RET_WIDTH), BF16)],
        compiler_params=pltpu.CompilerParams(vmem_limit_bytes=VMEM_LIMIT),
        name="inproj",
    )(x, mod3, mod3, norm1, w_hy, w_q, w_v, w_g, w_kt, cq, sq, ck, sk)


def _filter_mlp_kernel(zt_ref, t_ref, w1t_ref, b1_ref, f1_ref, w2t_ref, b2_ref, f2_ref, w3t_ref, adel_ref,
                       kt_ref):
    hid = jnp.sin(f1_ref[...] * (_dot_hi(w1t_ref[...], zt_ref[...]) + b1_ref[...]))
    hid = jnp.sin(f2_ref[...] * (_dot_hi(w2t_ref[...], hid) + b2_ref[...])).astype(BF16)
    c = HY_WIDTH
    for half in range(2):
        lags = slice(half * SEQ, (half + 1) * SEQ)
        window = jnp.exp(-adel_ref[...] * t_ref[:, lags])
        direction = 1 - half
        for o in range(HY_ORDER):
            r0 = (direction * HY_ORDER + o) * c
            w3 = w3t_ref[r0:r0 + c, :].astype(BF16)
            kt_ref[o, :, lags] = (_dot(w3, hid[:, lags]) * window).astype(BF16)


def _filter_mlp_call(zt, t_row, w1t, b1, f1, w2t, b2, f2, w3t, absdelta):
    fw = HY_FILTER_WIDTH
    n_out = 2 * HY_ORDER * HY_WIDTH
    return pl.pallas_call(
        _filter_mlp_kernel,
        grid=(1,),
        in_specs=[_resident((fw, 2 * SEQ)), _resident((1, 2 * SEQ)),
                  _resident((fw, fw)), _resident((fw, 1)), _resident((fw, 1)),
                  _resident((fw, fw)), _resident((fw, 1)), _resident((fw, 1)),
                  _resident((n_out, fw)), _resident((HY_WIDTH, 1))],
        out_specs=pl.BlockSpec((HY_ORDER, HY_WIDTH, 2 * SEQ), lambda i: (0, 0, 0)),
        out_shape=jax.ShapeDtypeStruct((HY_ORDER, HY_WIDTH, 2 * SEQ), BF16),
        compiler_params=pltpu.CompilerParams(vmem_limit_bytes=VMEM_LIMIT),
        name="filter_mlp",
    )(zt, t_row, w1t, b1, f1, w2t, b2, f2, w3t, absdelta)


_FILTER_PLANES = (
    {0: 1},
    {-1: 1, 0: -1},
    {1: 1, 0: -1},
    {-2: 1, 0: -1},
    {-3: 1, -1: -1, -2: -1, 0: 1},
    {-1: 1, 1: -1, -2: -1, 0: 1},
    {2: 1, 0: -1},
    {1: 1, -1: -1, 2: -1, 0: 1},
    {3: 1, 1: -1, 2: -1, 0: 1},
)
DFT_BLOCKS_PER_STEP = 2
PLANES_PER_STEP = 3


def _filter_dft_kernel(kt_ref, fwd_ref, h_ref, t_scr):
    p = CONV_BLOCK
    n_dft_steps = 2 * N_CONV_BLOCKS // DFT_BLOCKS_PER_STEP
    step = pl.program_id(1)

    for s in range(n_dft_steps):
        @pl.when(step == s)
        def _(s=s):
            for k in range(DFT_BLOCKS_PER_STEP):
                t_scr[s * DFT_BLOCKS_PER_STEP + k] = lax.dot_general(
                    fwd_ref[...], kt_ref[0, :, k * p:(k + 1) * p], _NT, preferred_element_type=F32)
            for k in range(PLANES_PER_STEP):
                h_ref[0, k] = t_scr[s * DFT_BLOCKS_PER_STEP]

    row = lax.broadcasted_iota(jnp.int32, (p, 1), 0)
    sign = (1 - 2 * (row & 1)).astype(F32)
    re, im = slice(0, p), slice(p, 2 * p)

    def plane(k, coefs):
        def comb(shift, rows):
            acc = None
            for d, c in coefs.items():
                term = t_scr[d + N_CONV_BLOCKS - 1 + shift, rows, :]
                if acc is None:
                    acc = term
                else:
                    acc = acc + term if c > 0 else acc - term
            return acc

        h_ref[0, k, re, :] = comb(0, re) + sign * comb(1, im)
        h_ref[0, k, im, :] = comb(0, im) - sign * comb(1, re)

    for s in range(len(_FILTER_PLANES) // PLANES_PER_STEP):
        @pl.when(step == n_dft_steps + s)
        def _(s=s):
            for k in range(PLANES_PER_STEP):
                plane(k, _FILTER_PLANES[s * PLANES_PER_STEP + k])


def _filter_dft_call(kt, fwd_u):
    p = CONV_BLOCK
    n_blk = 2 * N_CONV_BLOCKS
    n_planes = len(_FILTER_PLANES)
    n_dft_steps = n_blk // DFT_BLOCKS_PER_STEP
    return pl.pallas_call(
        _filter_dft_kernel,
        grid=(HY_ORDER, n_dft_steps + n_planes // PLANES_PER_STEP),
        in_specs=[pl.BlockSpec((1, HY_WIDTH, DFT_BLOCKS_PER_STEP * p),
                               lambda o, s: (o, 0, jnp.minimum(s, n_dft_steps - 1))),
                  _resident((2 * p, p))],
        out_specs=pl.BlockSpec((1, PLANES_PER_STEP, 2 * p, HY_WIDTH),
                               lambda o, s: (o, jnp.maximum(s - n_dft_steps, 0), 0, 0)),
        out_shape=jax.ShapeDtypeStruct((HY_ORDER, n_planes, 2 * p, HY_WIDTH), F32),
        scratch_shapes=[pltpu.VMEM((n_blk, 2 * p, HY_WIDTH), F32)],
        compiler_params=pltpu.CompilerParams(vmem_limit_bytes=VMEM_LIMIT),
        name="filter_dft",
    )(kt, fwd_u)


def _conv3_rows(ref, j, n_blocks, rows, w, b):
    main = ref[0, j * rows:(j + 1) * rows, :].astype(F32)
    cols = main.shape[1]
    zeros = jnp.zeros((HALO, cols), F32)
    prev = ref[0, j * rows - HALO:j * rows, :].astype(F32) if j > 0 else zeros
    nxt = ref[0, (j + 1) * rows:(j + 1) * rows + HALO, :].astype(F32) if j < n_blocks - 1 else zeros
    ext = jnp.concatenate([prev, main, nxt], axis=0)
    n = rows + 2 * HALO
    before = pltpu.roll(ext, 1, 0)[HALO:HALO + rows]
    after = pltpu.roll(ext, n - 1, 0)[HALO:HALO + rows]
    return before * w[0:1] + main * w[1:2] + after * w[2:3] + b


def _hyena_kernel(u_ref, zg_ref, cwu_ref, cbu_ref, cwg_ref, cbg_ref, skip_ref, h_ref, fwd_ref, inv_ref,
                  o_ref, uf_scr, ub_scr, y_scr, *, conv_u):
    p = CONV_BLOCK
    nb = N_CONV_BLOCKS

    def prepare(j):
        if conv_u:
            ub_scr[j * p:(j + 1) * p, :] = _conv3_rows(u_ref, j, nb, p, cwu_ref[...], cbu_ref[...]).astype(BF16)

    def u_block(j):
        return ub_scr[j * p:(j + 1) * p, :] if conv_u else u_ref[0, j * p:(j + 1) * p, :]

    prepare(0)
    for j in range(nb):
        if j + 1 < nb:
            prepare(j + 1)
        for r in range(2 * p // DFT_ROWS):
            rows = slice(r * DFT_ROWS, (r + 1) * DFT_ROWS)
            uf_scr[j, rows, :] = _dot(fwd_ref[rows, :], u_block(j))

    def cadd(a, b):
        return a[0] + b[0], a[1] + b[1]

    def cmul(m, x):
        return m[0] * x[0] - m[1] * x[1], m[0] * x[1] + m[1] * x[0]

    def toeplitz2(k0, x0, x1, re, im, ln):
        m0, mu, ml = ((h_ref[0, k0 + t, re, ln], h_ref[0, k0 + t, im, ln]) for t in range(3))
        p1 = cmul(m0, cadd(x0, x1))
        return cadd(p1, cmul(mu, x1)), cadd(p1, cmul(ml, x0))

    assert nb == 4
    for r in range(p // FREQ_ROWS):
        re = slice(r * FREQ_ROWS, (r + 1) * FREQ_ROWS)
        im = slice(p + r * FREQ_ROWS, p + (r + 1) * FREQ_ROWS)
        for cb in range(HY_WIDTH // FREQ_LANES):
            ln = slice(cb * FREQ_LANES, (cb + 1) * FREQ_LANES)
            u = [(uf_scr[j, re, ln], uf_scr[j, im, ln]) for j in range(nb)]
            d0, d1 = toeplitz2(0, cadd(u[0], u[2]), cadd(u[1], u[3]), re, im, ln)
            b0, b1 = toeplitz2(3, u[2], u[3], re, im, ln)
            c0, c1 = toeplitz2(6, u[0], u[1], re, im, ln)
            for i, yi in enumerate((cadd(d0, b0), cadd(d1, b1), cadd(d0, c0), cadd(d1, c1))):
                y_scr[i, re, ln] = yi[0].astype(BF16)
                y_scr[i, im, ln] = yi[1].astype(BF16)

    for i in range(nb):
        gate = _conv3_rows(zg_ref, i, nb, p, cwg_ref[...], cbg_ref[...])
        y = _dot(inv_ref[...], y_scr[i])
        o_ref[0, i * p:(i + 1) * p, :] = (
            gate * (y + u_block(i).astype(F32) * skip_ref[...])).astype(BF16)


def _hyena_call(u, u_col, zhy, gate_col, conv_w, conv_b, skip, g_all, order, fwd_u, inv, conv_u):
    p = CONV_BLOCK
    c = HY_WIDTH
    ucol = u_col if conv_u else 0
    cwu = conv_w[:, ucol * c:(ucol + 1) * c]
    cbu = conv_b[:, ucol * c:(ucol + 1) * c]
    cwg = conv_w[:, gate_col * c:(gate_col + 1) * c]
    cbg = conv_b[:, gate_col * c:(gate_col + 1) * c]
    return pl.pallas_call(
        functools.partial(_hyena_kernel, conv_u=conv_u),
        grid=(BATCH,),
        in_specs=[pl.BlockSpec((1, SEQ, c), lambda b: (b, 0, u_col)),
                  pl.BlockSpec((1, SEQ, c), lambda b: (b, 0, gate_col)),
                  _resident((3, c)), _resident((1, c)), _resident((3, c)), _resident((1, c)),
                  _resident((1, c)),
                  pl.BlockSpec((1, len(_FILTER_PLANES), 2 * p, c), lambda b: (order, 0, 0, 0),
                               pipeline_mode=pl.Buffered(1)),
                  _resident((2 * p, p)),
                  _resident((p, 2 * p))],
        out_specs=pl.BlockSpec((1, SEQ, c), lambda b: (b, 0, 0)),
        out_shape=jax.ShapeDtypeStruct((BATCH, SEQ, c), BF16),
        scratch_shapes=[pltpu.VMEM((N_CONV_BLOCKS, 2 * p, c), F32),
                        pltpu.VMEM((SEQ, c) if conv_u else (SUBLANES * 2, LANES), BF16),
                        pltpu.VMEM((N_CONV_BLOCKS, 2 * p, c), BF16)],
        compiler_params=pltpu.CompilerParams(vmem_limit_bytes=HYENA_VMEM_LIMIT),
        name="hyena%d" % order,
    )(u, zhy, cwu, cbu, cwg, cbg, skip, g_all, fwd_u, inv)


def _ret_kernel(lg_ref, q_ref, kt_ref, v_ref, g_ref, s_ref, o_ref, b_scr, decay_scr, qw_scr):
    c = RET_CHUNK
    nc = SEQ // c
    dk = RET_QK_DIM
    h = pl.program_id(0)
    lg_f = lg_ref[0, h]
    lg_b = lg_ref[1, h]

    @pl.when(pl.program_id(1) == 0)
    def _():
        ii = lax.broadcasted_iota(jnp.int32, (c, c), 0).astype(F32)
        jj = lax.broadcasted_iota(jnp.int32, (c, c), 1).astype(F32)
        dif = ii - jj
        decay_scr[...] = jnp.where(dif >= 0.0, jnp.exp(lg_f * jnp.maximum(dif, 0.0)),
                                   jnp.exp(lg_b * jnp.maximum(-dif, 0.0)))
        pos_q = lax.broadcasted_iota(jnp.int32, (c, dk), 0).astype(F32)
        qw_scr[0] = jnp.exp(lg_f * (pos_q + 1.0))
        qw_scr[1] = jnp.exp(lg_b * (c - pos_q))

    decay = decay_scr[...]
    qw_f = qw_scr[0]
    qw_b = qw_scr[1]
    pos_r = lax.broadcasted_iota(jnp.int32, (1, c), 1).astype(F32)
    kw_f = jnp.exp(lg_f * (c - 1.0 - pos_r))
    kw_b = jnp.exp(lg_b * pos_r)
    ones = jnp.ones((1, RET_V_DIM), F32)
    dec_f = jnp.exp(lg_f * float(c) * ones)
    dec_b = jnp.exp(lg_b * float(c) * ones)

    state = s_ref[0, 0, dk:2 * dk, :]
    b_scr[nc - 1] = state
    for n in range(nc - 1, 0, -1):
        ktn = (kt_ref[0, :, n * c:(n + 1) * c].astype(F32) * kw_b).astype(BF16)
        state = state * dec_b + _dot(ktn, v_ref[0, n * c:(n + 1) * c, :])
        b_scr[n - 1] = state

    state = s_ref[0, 0, 0:dk, :]
    for n in range(nc):
        qn = q_ref[0, 0, n * c:(n + 1) * c, :]
        ktn = kt_ref[0, :, n * c:(n + 1) * c]
        vn = v_ref[0, n * c:(n + 1) * c, :]
        scores = (_dot(qn, ktn) * decay).astype(BF16)
        qf = qn.astype(F32)
        o = _dot(scores, vn)
        o = o + _dot((qf * qw_f).astype(BF16), state.astype(BF16))
        o = o + _dot((qf * qw_b).astype(BF16), b_scr[n].astype(BF16))
        state = state * dec_f + _dot((ktn.astype(F32) * kw_f).astype(BF16), vn)
        o = o * lax.rsqrt(jnp.mean(o * o, axis=-1, keepdims=True) + EPS)
        gate = g_ref[0, n * c:(n + 1) * c, :].astype(F32)
        o_ref[0, n * c:(n + 1) * c, :] = (_silu(gate) * o).astype(BF16)


def _ret_call(lg, q, kt, v, g, s):
    dk, dv = RET_QK_DIM, RET_V_DIM
    return pl.pallas_call(
        _ret_kernel,
        grid=(RET_HEADS, BATCH),
        in_specs=[pl.BlockSpec(memory_space=pltpu.SMEM),
                  pl.BlockSpec((1, 1, SEQ, dk), lambda h, b: (b, h, 0, 0)),
                  pl.BlockSpec((1, dk, SEQ), lambda h, b: (b, h, 0)),
                  pl.BlockSpec((1, SEQ, dv), lambda h, b: (b, 0, h)),
                  pl.BlockSpec((1, SEQ, dv), lambda h, b: (b, 0, h)),
                  pl.BlockSpec((1, 1, 2 * dk, dv), lambda h, b: (b, h, 0, 0))],
        out_specs=pl.BlockSpec((1, SEQ, dv), lambda h, b: (b, 0, h)),
        out_shape=jax.ShapeDtypeStruct((BATCH, SEQ, RET_WIDTH), BF16),
        scratch_shapes=[pltpu.VMEM((SEQ // RET_CHUNK, dk, dv), F32),
                        pltpu.VMEM((RET_CHUNK, RET_CHUNK), F32),
                        pltpu.VMEM((2, RET_CHUNK, dk), F32)],
        compiler_params=pltpu.CompilerParams(vmem_limit_bytes=VMEM_LIMIT),
        name="ret",
    )(lg, q, kt, v, g, s)


def _ffn_kernel(x_ref, xp_ref, xn_ref, yh_ref, yhp_ref, yhn_ref, yr_ref, yrp_ref, yrn_ref,
                g1_ref, sh_ref, sc_ref, g2_ref, n2_ref, nf_ref,
                woh_ref, wor_ref, wup_ref, cw_ref, cb_ref, wdn_ref,
                o_ref, hb_scr, x1_scr, av_scr, ag_scr, act_scr):
    t = TOK_TILE
    i = pl.program_id(1)
    nt = pl.num_programs(1)
    th = t // 2
    tile_halves = (slice(0, th), slice(th, t))

    def mixed(xr, yh, yr, rows=slice(None)):
        return xr[0, rows, :] + g1_ref[0] * (_dot(yh[0, rows, :], woh_ref[...]) + _dot(yr[0, rows, :], wor_ref[...]))

    def hidden(x1):
        return _norm_mod(x1, n2_ref[...], sh_ref[0], sc_ref[0])

    half = (t + 2 * HALO) // 2
    up_halves = (slice(0, half), slice(half, 2 * half))

    def up(slot, cblk, rows):
        c0 = cblk * FFN_COLS
        hb = hb_scr[rows, :]
        av_scr[slot, rows, :] = _dot(hb, wup_ref[:, c0:c0 + FFN_COLS])
        ag_scr[slot, rows, :] = _dot(hb, wup_ref[:, D_FF + c0:D_FF + c0 + FFN_COLS])

    for rows in tile_halves:
        x1_scr[rows, :] = mixed(x_ref, yh_ref, yr_ref, rows)
    hp = hidden(mixed(xp_ref, yhp_ref, yrp_ref))
    hn = hidden(mixed(xn_ref, yhn_ref, yrn_ref))
    hb_scr[0:HALO, :] = jnp.where(i > 0, hp, 0.0).astype(BF16)
    hb_scr[HALO + t:2 * HALO + t, :] = jnp.where(i < nt - 1, hn, 0.0).astype(BF16)
    hb_scr[HALO:HALO + th, :] = hidden(x1_scr[tile_halves[0], :]).astype(BF16)
    up(0, 0, up_halves[0])
    hb_scr[HALO + th:HALO + t, :] = hidden(x1_scr[tile_halves[1], :]).astype(BF16)
    up(0, 0, up_halves[1])

    def conv(scr, slot, col):
        w = cw_ref[:, col:col + FFN_COLS]
        return (scr[slot, HALO - 1:HALO - 1 + t, :] * w[0:1] + scr[slot, HALO:HALO + t, :] * w[1:2]
                + scr[slot, HALO + 1:HALO + 1 + t, :] * w[2:3] + cb_ref[:, col:col + FFN_COLS])

    n_blk = D_FF // FFN_COLS
    for cblk in range(n_blk):
        c0 = cblk * FFN_COLS
        slot = cblk % 2
        if cblk + 1 < n_blk:
            for rows in up_halves:
                up(1 - slot, cblk + 1, rows)
        act_scr[:, c0:c0 + FFN_COLS] = (
            _silu(conv(ag_scr, slot, D_FF + c0)) * conv(av_scr, slot, c0)).astype(BF16)

    down = [_dot(act_scr[rows, :], wdn_ref[...]) for rows in tile_halves]
    for rows, ffn in zip(tile_halves, down):
        x2 = x1_scr[rows, :] + g2_ref[0] * ffn
        o_ref[0, rows, :] = x2 * lax.rsqrt(jnp.mean(x2 * x2, axis=-1, keepdims=True) + EPS) * nf_ref[...]


def _ffn_call(x, y_hy, y_ret, mod3, norm2, norm_f, w_oh, w_or, w_up, conv_w, conv_b, w_dn):
    t = TOK_TILE
    r = t // HALO
    last = SEQ // HALO - 1

    def main(width):
        return pl.BlockSpec((1, t, width), lambda b, i: (b, i, 0))

    def prev(width):
        return pl.BlockSpec((1, HALO, width), lambda b, i: (b, jnp.maximum(i * r - 1, 0), 0))

    def nxt(width):
        return pl.BlockSpec((1, HALO, width), lambda b, i: (b, jnp.minimum((i + 1) * r, last), 0))

    def modrow(k):
        return pl.BlockSpec((1, 1, D_MODEL), lambda b, i: (b, 0, k))

    return pl.pallas_call(
        _ffn_kernel,
        grid=(BATCH, SEQ // t),
        in_specs=[main(D_MODEL), prev(D_MODEL), nxt(D_MODEL),
                  main(HY_WIDTH), prev(HY_WIDTH), nxt(HY_WIDTH),
                  main(RET_WIDTH), prev(RET_WIDTH), nxt(RET_WIDTH),
                  modrow(2), modrow(3), modrow(4), modrow(5),
                  _resident((1, D_MODEL)), _resident((1, D_MODEL)),
                  _resident((HY_WIDTH, D_MODEL)), _resident((RET_WIDTH, D_MODEL)),
                  _resident((D_MODEL, 2 * D_FF)),
                  _resident((3, 2 * D_FF)), _resident((1, 2 * D_FF)),
                  _resident((D_FF, D_MODEL))],
        out_specs=pl.BlockSpec((1, t, D_MODEL), lambda b, i: (b, i, 0)),
        out_shape=jax.ShapeDtypeStruct((BATCH, SEQ, D_MODEL), F32),
        scratch_shapes=[pltpu.VMEM((t + 2 * HALO, D_MODEL), BF16),
                        pltpu.VMEM((t, D_MODEL), F32),
                        pltpu.VMEM((2, t + 2 * HALO, FFN_COLS), F32),
                        pltpu.VMEM((2, t + 2 * HALO, FFN_COLS), F32),
                        pltpu.VMEM((t, D_FF), BF16)],
        compiler_params=pltpu.CompilerParams(vmem_limit_bytes=VMEM_LIMIT),
        name="ffn",
    )(x, x, x, y_hy, y_hy, y_hy, y_ret, y_ret, y_ret, mod3, mod3, mod3, mod3,
      norm2, norm_f, w_oh, w_or, w_up, conv_w, conv_b, w_dn)


def kernel(x, c, ctx, c_ctx, w_mod, b_mod, norm1, w_in, hy_conv_w, hy_conv_b, hy_w1, hy_b1, hy_f1,
           hy_w2, hy_b2, hy_f2, hy_w3, hy_bias, ret_logit_f, ret_logit_b, w_out, norm2,
           ffn_w_up, ffn_conv_w, ffn_conv_b, ffn_w_down, norm_f):
    layer = 0
    rope = tuple(jnp.asarray(a) for a in _rope_tables())
    zt, t_row, absdelta = (jnp.asarray(a) for a in _filter_features())
    fwd_np, inv_np = _dft_matrices()
    fwd_u = jnp.asarray(fwd_np[:, :CONV_BLOCK]).astype(BF16)
    inv = jnp.asarray(inv_np).astype(BF16)

    w_in_l = w_in[layer]
    w_hy = w_in_l[:, :HY_COLS].astype(BF16)
    w_q = w_in_l[:, Q_OFF:K_OFF].astype(BF16)
    w_kt = w_in_l[:, K_OFF:V_OFF].T.astype(BF16)
    w_v = w_in_l[:, V_OFF:G_OFF].astype(BF16)
    w_g = w_in_l[:, G_OFF:].astype(BF16)
    w_oh = w_out[layer][:HY_WIDTH].astype(BF16)
    w_or = w_out[layer][HY_WIDTH:].astype(BF16)
    w_up = ffn_w_up[layer].astype(BF16)
    w_dn = ffn_w_down[layer].astype(BF16)
    row = lambda a: a.reshape(1, -1)
    col = lambda a: a.reshape(-1, 1)
    w1p = jnp.pad(hy_w1[layer], ((0, HY_FILTER_WIDTH - hy_w1.shape[1]), (0, 0)))
    lg = jnp.stack([jax.nn.log_sigmoid(ret_logit_f[layer].astype(F32)),
                    jax.nn.log_sigmoid(ret_logit_b[layer].astype(F32))])

    cc = jnp.concatenate([c, c_ctx[None, :], jnp.zeros((MOD_ROWS - BATCH - 1, D_MODEL), F32)], axis=0)
    mod = _mod_call(cc, w_mod[layer], row(b_mod[layer]))
    mod3 = mod.reshape(MOD_ROWS, 1, 6 * D_MODEL)
    norm1_r = row(norm1[layer])

    s_ctx = _ctx_call(lg, ctx, mod3, norm1_r, w_kt, w_v)
    zhy, q, kt, v, g = _inproj_call(x, mod3, norm1_r, w_hy, w_q, w_v, w_g, w_kt, rope)

    k_two_sided = _filter_mlp_call(zt, t_row, w1p.T, col(hy_b1[layer]), col(hy_f1[layer]), hy_w2[layer].T,
                                   col(hy_b2[layer]), col(hy_f2[layer]), hy_w3[layer].T, absdelta)
    g_all = _filter_dft_call(k_two_sided, fwd_u)
    conv_w, conv_b = hy_conv_w[layer], row(hy_conv_b[layer])
    y1 = _hyena_call(zhy, 0, zhy, 1, conv_w, conv_b, hy_bias[layer][0:1], g_all, 0, fwd_u, inv, True)
    y_hy = _hyena_call(y1, 0, zhy, 2, conv_w, conv_b, hy_bias[layer][1:2], g_all, 1, fwd_u, inv, False)

    y_ret = _ret_call(lg, q, kt, v, g, s_ctx)

    return _ffn_call(x, y_hy, y_ret, mod3, row(norm2[layer]), row(norm_f), w_oh, w_or, w_up,
                     ffn_conv_w[layer], row(ffn_conv_b[layer]), w_dn)
```

```python
import functools
import math

import numpy as np
import jax
import jax.numpy as jnp
from jax import lax
from jax.experimental import pallas as pl
from jax.experimental.pallas import tpu as pltpu

F32 = jnp.float32
BF16 = jnp.bfloat16

D_MODEL = 1024
BATCH = 8
SEQ = 2048
CTX_LEN = 256
GRID_W = 64
HY_WIDTH = 512
HY_ORDER = 2
HY_EMB_BANDS = 16
HY_FILTER_WIDTH = 64
HY_FAST_DECAY = 0.3
HY_SLOW_DECAY = 1.5
HY_TARGET = 1e-2
RET_WIDTH = 512
RET_HEADS = 4
RET_QK_DIM = 64
RET_V_DIM = 128
ROPE_BASE = 10000.0
D_FF = 2816
EPS = 1e-6
HY_COLS = (HY_ORDER + 1) * HY_WIDTH
Q_OFF = HY_COLS
K_OFF = Q_OFF + RET_HEADS * RET_QK_DIM
V_OFF = K_OFF + RET_HEADS * RET_QK_DIM
G_OFF = V_OFF + RET_WIDTH
K_SCALE = RET_QK_DIM ** -0.5

MOD_ROWS = 16
TOK_TILE = 512
IN_TILE = 1024
IN_PIECE = 256
SUBLANES = 8
LANES = 128
HALO = 16
CONV_BLOCK = 512
N_CONV_BLOCKS = SEQ // CONV_BLOCK
FREQ_ROWS = 16
FREQ_LANES = 128
DFT_ROWS = 256
RET_CHUNK = 256
RET_HEADS_PER_STEP = 4
FFN_COLS = 256
VMEM_LIMIT = 56 * 1024 * 1024
HYENA_VMEM_LIMIT = 62 * 1024 * 1024

_NT = (((1,), (1,)), ((), ()))


def _dot(a, b):
    return jnp.dot(a, b, preferred_element_type=F32)


def _dot_hi(a, b):
    return jnp.dot(a, b, preferred_element_type=F32, precision=lax.Precision.HIGHEST)


def _silu(x):
    return x * (1.0 / (1.0 + jnp.exp(-x)))


def _norm_mod(x, gain, shift, scale):
    y = x * lax.rsqrt(jnp.mean(x * x, axis=-1, keepdims=True) + EPS)
    return (y * gain) * (1.0 + scale) + shift


def _resident(shape):
    nd = len(shape)
    return pl.BlockSpec(shape, lambda *_: (0,) * nd, pipeline_mode=pl.Buffered(1))


@functools.lru_cache(maxsize=None)
def _rope_tables():
    pos = np.arange(SEQ)
    row = (pos // GRID_W).astype(np.float64)
    col = (pos % GRID_W).astype(np.float64)
    quarter = RET_QK_DIM // 4
    inv_freq = ROPE_BASE ** (-np.arange(quarter, dtype=np.float64) / quarter)
    ang = np.concatenate([row[:, None] * inv_freq, col[:, None] * inv_freq], axis=-1)
    cos, sin = np.cos(ang), np.sin(ang)
    cos_h = np.concatenate([cos, cos], axis=-1)
    sin_h = np.concatenate([-sin, sin], axis=-1)
    cos_t = np.tile(cos_h, (1, RET_HEADS))
    sin_t = np.tile(sin_h, (1, RET_HEADS))
    return (cos_t.astype(np.float32), sin_t.astype(np.float32),
            np.ascontiguousarray((cos_t * K_SCALE).T).astype(np.float32),
            np.ascontiguousarray((sin_t * K_SCALE).T).astype(np.float32))


@functools.lru_cache(maxsize=None)
def _filter_features():
    lag = np.abs(np.arange(2 * SEQ) - SEQ).astype(np.float64)
    t = lag / (SEQ - 1)
    bands = np.linspace(1e-4, HY_EMB_BANDS - 1, HY_EMB_BANDS)
    ang = 2.0 * math.pi * lag[:, None] * bands[None, :] / SEQ
    z = np.concatenate([t[:, None], np.cos(ang), -np.sin(ang)], axis=-1)
    zp = np.zeros((2 * SEQ, HY_FILTER_WIDTH), np.float64)
    zp[:, :z.shape[1]] = z
    max_decay = math.log(HY_TARGET) / HY_FAST_DECAY
    min_decay = math.log(HY_TARGET) / HY_SLOW_DECAY
    absdelta = np.abs(np.linspace(min_decay, max_decay, HY_WIDTH))[:, None]
    return (np.ascontiguousarray(zp.T).astype(np.float32), t[None, :].astype(np.float32),
            absdelta.astype(np.float32))


@functools.lru_cache(maxsize=None)
def _dft_matrices():
    p = CONV_BLOCK
    n = 2 * p
    f = np.arange(p, dtype=np.float64)[:, None] + 0.5
    t = np.arange(n, dtype=np.float64)[None, :]
    theta = 2.0 * math.pi * f * t / n
    fwd = np.concatenate([np.cos(theta), -np.sin(theta)], axis=0)
    th_out = theta[:, p:].T
    inv = np.concatenate([np.cos(th_out), -np.sin(th_out)], axis=1) / p
    return fwd.astype(np.float32), inv.astype(np.float32)


def _mod_kernel(c_ref, w_ref, b_ref, o_ref):
    s = _silu(c_ref[...]).astype(BF16)
    o_ref[...] = _dot(s, w_ref[...].astype(BF16)) + b_ref[...]


def _mod_call(cc, w_mod, b_mod):
    ncol = 6 * D_MODEL
    blk = ncol // 4
    return pl.pallas_call(
        _mod_kernel,
        grid=(ncol // blk,),
        in_specs=[pl.BlockSpec((MOD_ROWS, D_MODEL), lambda j: (0, 0)),
                  pl.BlockSpec((D_MODEL, blk), lambda j: (0, j)),
                  pl.BlockSpec((1, blk), lambda j: (0, j))],
        out_specs=pl.BlockSpec((MOD_ROWS, blk), lambda j: (0, j)),
        out_shape=jax.ShapeDtypeStruct((MOD_ROWS, ncol), F32),
        compiler_params=pltpu.CompilerParams(vmem_limit_bytes=VMEM_LIMIT),
        name="mod",
    )(cc, w_mod, b_mod)


def _ctx_kernel(lg_ref, x_ref, sh_ref, sc_ref, n1_ref, wkt_ref, wv_ref, s_ref):
    h = _norm_mod(x_ref[0], n1_ref[...], sh_ref[0], sc_ref[0]).astype(BF16)
    kt = lax.dot_general(wkt_ref[...], h, _NT, preferred_element_type=F32) * K_SCALE
    v = _dot(h, wv_ref[...])
    pos = lax.broadcasted_iota(jnp.int32, (1, CTX_LEN), 1).astype(F32)
    for hh in range(RET_HEADS):
        w_f = jnp.exp(lg_ref[0, hh] * (CTX_LEN - 1.0 - pos))
        w_b = jnp.exp(lg_ref[1, hh] * pos)
        kth = kt[hh * RET_QK_DIM:(hh + 1) * RET_QK_DIM, :]
        vh = v[:, hh * RET_V_DIM:(hh + 1) * RET_V_DIM].astype(BF16)
        s_ref[0, hh, 0:RET_QK_DIM, :] = _dot((kth * w_f).astype(BF16), vh)
        s_ref[0, hh, RET_QK_DIM:2 * RET_QK_DIM, :] = _dot((kth * w_b).astype(BF16), vh)


def _ctx_call(lg, ctx, mod3, norm1, w_kt, w_v):
    return pl.pallas_call(
        _ctx_kernel,
        grid=(BATCH,),
        in_specs=[pl.BlockSpec(memory_space=pltpu.SMEM),
                  pl.BlockSpec((1, CTX_LEN, D_MODEL), lambda b: (b, 0, 0)),
                  pl.BlockSpec((1, 1, D_MODEL), lambda b: (BATCH, 0, 0)),
                  pl.BlockSpec((1, 1, D_MODEL), lambda b: (BATCH, 0, 1)),
                  _resident((1, D_MODEL)),
                  _resident((RET_HEADS * RET_QK_DIM, D_MODEL)),
                  _resident((D_MODEL, RET_WIDTH))],
        out_specs=pl.BlockSpec((1, RET_HEADS, 2 * RET_QK_DIM, RET_V_DIM), lambda b: (b, 0, 0, 0)),
        out_shape=jax.ShapeDtypeStruct((BATCH, RET_HEADS, 2 * RET_QK_DIM, RET_V_DIM), F32),
        compiler_params=pltpu.CompilerParams(vmem_limit_bytes=VMEM_LIMIT),
        name="ctx",
    )(lg, ctx, mod3, mod3, norm1, w_kt, w_v)


def _swap_halves(x, axis):
    n = x.shape[axis]
    half = RET_QK_DIM // 2
    idx = lax.broadcasted_iota(jnp.int32, x.shape, axis)
    first = (idx & (RET_QK_DIM - 1)) < half
    return jnp.where(first, pltpu.roll(x, n - half, axis), pltpu.roll(x, half, axis))


def _inproj_kernel(x_ref, sh_ref, sc_ref, n1_ref, why_ref, wq_ref, wv_ref, wg_ref, wkt_ref,
                   cq_ref, sq_ref, ck_ref, sk_ref,
                   zhy_ref, q_ref, kt_ref, v_ref, g_ref):
    pieces = [slice(r, r + IN_PIECE) for r in range(0, IN_TILE, IN_PIECE)]
    hbs = [_norm_mod(x_ref[0, rows, :], n1_ref[...], sh_ref[0], sc_ref[0]).astype(BF16) for rows in pieces]
    for rows, hb in zip(pieces, hbs):
        zhy_ref[0, rows, :] = _dot(hb, why_ref[...]).astype(BF16)
        v_ref[0, rows, :] = _dot(hb, wv_ref[...]).astype(BF16)
        g_ref[0, rows, :] = _dot(hb, wg_ref[...]).astype(BF16)
        q = _dot(hb, wq_ref[...])
        q = q * cq_ref[rows, :] + _swap_halves(q, 1) * sq_ref[rows, :]
        for hh in range(RET_HEADS):
            q_ref[0, hh, rows, :] = q[:, hh * RET_QK_DIM:(hh + 1) * RET_QK_DIM].astype(BF16)
        kt = lax.dot_general(wkt_ref[...], hb, _NT, preferred_element_type=F32)
        kt = kt * ck_ref[:, rows] + _swap_halves(kt, 0) * sk_ref[:, rows]
        kt_ref[0, :, rows] = kt.astype(BF16)


def _inproj_call(x, mod3, norm1, w_hy, w_q, w_v, w_g, w_kt, rope):
    cq, sq, ck, sk = rope
    t = IN_TILE
    qk = RET_HEADS * RET_QK_DIM
    return pl.pallas_call(
        _inproj_kernel,
        grid=(BATCH, SEQ // t),
        in_specs=[pl.BlockSpec((1, t, D_MODEL), lambda b, i: (b, i, 0)),
                  pl.BlockSpec((1, 1, D_MODEL), lambda b, i: (b, 0, 0)),
                  pl.BlockSpec((1, 1, D_MODEL), lambda b, i: (b, 0, 1)),
                  _resident((1, D_MODEL)),
                  _resident((D_MODEL, HY_COLS)),
                  _resident((D_MODEL, qk)),
                  _resident((D_MODEL, RET_WIDTH)),
                  _resident((D_MODEL, RET_WIDTH)),
                  _resident((qk, D_MODEL)),
                  pl.BlockSpec((t, qk), lambda b, i: (i, 0)),
                  pl.BlockSpec((t, qk), lambda b, i: (i, 0)),
                  pl.BlockSpec((qk, t), lambda b, i: (0, i)),
                  pl.BlockSpec((qk, t), lambda b, i: (0, i))],
        out_specs=[pl.BlockSpec((1, t, HY_COLS), lambda b, i: (b, i, 0)),
                   pl.BlockSpec((1, RET_HEADS, t, RET_QK_DIM), lambda b, i: (b, 0, i, 0)),
                   pl.BlockSpec((1, qk, t), lambda b, i: (b, 0, i)),
                   pl.BlockSpec((1, t, RET_WIDTH), lambda b, i: (b, i, 0)),
                   pl.BlockSpec((1, t, RET_WIDTH), lambda b, i: (b, i, 0))],
        out_shape=[jax.ShapeDtypeStruct((BATCH, SEQ, HY_COLS), BF16),
                   jax.ShapeDtypeStruct((BATCH, RET_HEADS, SEQ, RET_QK_DIM), BF16),
                   jax.ShapeDtypeStruct((BATCH, qk, SEQ), BF16),
                   jax.ShapeDtypeStruct((BATCH, SEQ, RET_WIDTH), BF16),
                   jax.ShapeDtypeStruct((BATCH, SEQ, RET_WIDTH), BF16)],
        compiler_params=pltpu.CompilerParams(vmem_limit_bytes=VMEM_LIMIT),
        name="inproj",
    )(x, mod3, mod3, norm1, w_hy, w_q, w_v, w_g, w_kt, cq, sq, ck, sk)


def _filter_mlp_kernel(zt_ref, t_ref, w1t_ref, b1_ref, f1_ref, w2t_ref, b2_ref, f2_ref, w3t_ref, adel_ref,
                       kt_ref):
    hid = jnp.sin(f1_ref[...] * (_dot_hi(w1t_ref[...], zt_ref[...]) + b1_ref[...]))
    hid = jnp.sin(f2_ref[...] * (_dot_hi(w2t_ref[...], hid) + b2_ref[...])).astype(BF16)
    c = HY_WIDTH
    for half in range(2):
        lags = slice(half * SEQ, (half + 1) * SEQ)
        window = jnp.exp(-adel_ref[...] * t_ref[:, lags])
        direction = 1 - half
        for o in range(HY_ORDER):
            r0 = (direction * HY_ORDER + o) * c
            w3 = w3t_ref[r0:r0 + c, :].astype(BF16)
            kt_ref[o, :, lags] = (_dot(w3, hid[:, lags]) * window).astype(BF16)


def _filter_mlp_call(zt, t_row, w1t, b1, f1, w2t, b2, f2, w3t, absdelta):
    fw = HY_FILTER_WIDTH
    n_out = 2 * HY_ORDER * HY_WIDTH
    return pl.pallas_call(
        _filter_mlp_kernel,
        grid=(1,),
        in_specs=[_resident((fw, 2 * SEQ)), _resident((1, 2 * SEQ)),
                  _resident((fw, fw)), _resident((fw, 1)), _resident((fw, 1)),
                  _resident((fw, fw)), _resident((fw, 1)), _resident((fw, 1)),
                  _resident((n_out, fw)), _resident((HY_WIDTH, 1))],
        out_specs=pl.BlockSpec((HY_ORDER, HY_WIDTH, 2 * SEQ), lambda i: (0, 0, 0)),
        out_shape=jax.ShapeDtypeStruct((HY_ORDER, HY_WIDTH, 2 * SEQ), BF16),
        compiler_params=pltpu.CompilerParams(vmem_limit_bytes=VMEM_LIMIT),
        name="filter_mlp",
    )(zt, t_row, w1t, b1, f1, w2t, b2, f2, w3t, absdelta)


_FILTER_PLANES = (
    {0: 1},
    {-1: 1, 0: -1},
    {1: 1, 0: -1},
    {-2: 1, 0: -1},
    {-3: 1, -1: -1, -2: -1, 0: 1},
    {-1: 1, 1: -1, -2: -1, 0: 1},
    {2: 1, 0: -1},
    {1: 1, -1: -1, 2: -1, 0: 1},
    {3: 1, 1: -1, 2: -1, 0: 1},
)
DFT_BLOCKS_PER_STEP = 2
PLANES_PER_STEP = 3


def _filter_dft_kernel(kt_ref, fwd_ref, h_ref, t_scr):
    p = CONV_BLOCK
    n_dft_steps = 2 * N_CONV_BLOCKS // DFT_BLOCKS_PER_STEP
    step = pl.program_id(1)

    for s in range(n_dft_steps):
        @pl.when(step == s)
        def _(s=s):
            for k in range(DFT_BLOCKS_PER_STEP):
                t_scr[s * DFT_BLOCKS_PER_STEP + k] = lax.dot_general(
                    fwd_ref[...], kt_ref[0, :, k * p:(k + 1) * p], _NT, preferred_element_type=F32)
            for k in range(PLANES_PER_STEP):
                h_ref[0, k] = t_scr[s * DFT_BLOCKS_PER_STEP]

    row = lax.broadcasted_iota(jnp.int32, (p, 1), 0)
    sign = (1 - 2 * (row & 1)).astype(F32)
    re, im = slice(0, p), slice(p, 2 * p)

    def plane(k, coefs):
        def comb(shift, rows):
            acc = None
            for d, c in coefs.items():
                term = t_scr[d + N_CONV_BLOCKS - 1 + shift, rows, :]
                if acc is None:
                    acc = term
                else:
                    acc = acc + term if c > 0 else acc - term
            return acc

        h_ref[0, k, re, :] = comb(0, re) + sign * comb(1, im)
        h_ref[0, k, im, :] = comb(0, im) - sign * comb(1, re)

    for s in range(len(_FILTER_PLANES) // PLANES_PER_STEP):
        @pl.when(step == n_dft_steps + s)
        def _(s=s):
            for k in range(PLANES_PER_STEP):
                plane(k, _FILTER_PLANES[s * PLANES_PER_STEP + k])


def _filter_dft_call(kt, fwd_u):
    p = CONV_BLOCK
    n_blk = 2 * N_CONV_BLOCKS
    n_planes = len(_FILTER_PLANES)
    n_dft_steps = n_blk // DFT_BLOCKS_PER_STEP
    return pl.pallas_call(
        _filter_dft_kernel,
        grid=(HY_ORDER, n_dft_steps + n_planes // PLANES_PER_STEP),
        in_specs=[pl.BlockSpec((1, HY_WIDTH, DFT_BLOCKS_PER_STEP * p),
                               lambda o, s: (o, 0, jnp.minimum(s, n_dft_steps - 1))),
                  _resident((2 * p, p))],
        out_specs=pl.BlockSpec((1, PLANES_PER_STEP, 2 * p, HY_WIDTH),
                               lambda o, s: (o, jnp.maximum(s - n_dft_steps, 0), 0, 0)),
        out_shape=jax.ShapeDtypeStruct((HY_ORDER, n_planes, 2 * p, HY_WIDTH), F32),
        scratch_shapes=[pltpu.VMEM((n_blk, 2 * p, HY_WIDTH), F32)],
        compiler_params=pltpu.CompilerParams(vmem_limit_bytes=VMEM_LIMIT),
        name="filter_dft",
    )(kt, fwd_u)


def _conv3_rows(ref, j, n_blocks, rows, w, b):
    main = ref[0, j * rows:(j + 1) * rows, :].astype(F32)
    cols = main.shape[1]
    zeros = jnp.zeros((HALO, cols), F32)
    prev = ref[0, j * rows - HALO:j * rows, :].astype(F32) if j > 0 else zeros
    nxt = ref[0, (j + 1) * rows:(j + 1) * rows + HALO, :].astype(F32) if j < n_blocks - 1 else zeros
    ext = jnp.concatenate([prev, main, nxt], axis=0)
    n = rows + 2 * HALO
    before = pltpu.roll(ext, 1, 0)[HALO:HALO + rows]
    after = pltpu.roll(ext, n - 1, 0)[HALO:HALO + rows]
    return before * w[0:1] + main * w[1:2] + after * w[2:3] + b


def _hyena_kernel(u_ref, zg_ref, cwu_ref, cbu_ref, cwg_ref, cbg_ref, skip_ref, h_ref, fwd_ref, inv_ref,
                  o_ref, uf_scr, ub_scr, y_scr, *, conv_u):
    p = CONV_BLOCK
    nb = N_CONV_BLOCKS

    def prepare(j):
        if conv_u:
            ub_scr[j * p:(j + 1) * p, :] = _conv3_rows(u_ref, j, nb, p, cwu_ref[...], cbu_ref[...]).astype(BF16)

    def u_block(j):
        return ub_scr[j * p:(j + 1) * p, :] if conv_u else u_ref[0, j * p:(j + 1) * p, :]

    prepare(0)
    for j in range(nb):
        if j + 1 < nb:
            prepare(j + 1)
        for r in range(2 * p // DFT_ROWS):
            rows = slice(r * DFT_ROWS, (r + 1) * DFT_ROWS)
            uf_scr[j, rows, :] = _dot(fwd_ref[rows, :], u_block(j))

    def cadd(a, b):
        return a[0] + b[0], a[1] + b[1]

    def cmul(m, x):
        return m[0] * x[0] - m[1] * x[1], m[0] * x[1] + m[1] * x[0]

    def toeplitz2(k0, x0, x1, re, im, ln):
        m0, mu, ml = ((h_ref[0, k0 + t, re, ln], h_ref[0, k0 + t, im, ln]) for t in range(3))
        p1 = cmul(m0, cadd(x0, x1))
        return cadd(p1, cmul(mu, x1)), cadd(p1, cmul(ml, x0))

    assert nb == 4
    for r in range(p // FREQ_ROWS):
        re = slice(r * FREQ_ROWS, (r + 1) * FREQ_ROWS)
        im = slice(p + r * FREQ_ROWS, p + (r + 1) * FREQ_ROWS)
        for cb in range(HY_WIDTH // FREQ_LANES):
            ln = slice(cb * FREQ_LANES, (cb + 1) * FREQ_LANES)
            u = [(uf_scr[j, re, ln], uf_scr[j, im, ln]) for j in range(nb)]
            d0, d1 = toeplitz2(0, cadd(u[0], u[2]), cadd(u[1], u[3]), re, im, ln)
            b0, b1 = toeplitz2(3, u[2], u[3], re, im, ln)
            c0, c1 = toeplitz2(6, u[0], u[1], re, im, ln)
            for i, yi in enumerate((cadd(d0, b0), cadd(d1, b1), cadd(d0, c0), cadd(d1, c1))):
                y_scr[i, re, ln] = yi[0].astype(BF16)
                y_scr[i, im, ln] = yi[1].astype(BF16)

    for i in range(nb):
        gate = _conv3_rows(zg_ref, i, nb, p, cwg_ref[...], cbg_ref[...])
        y = _dot(inv_ref[...], y_scr[i])
        o_ref[0, i * p:(i + 1) * p, :] = (
            gate * (y + u_block(i).astype(F32) * skip_ref[...])).astype(BF16)


def _hyena_call(u, u_col, zhy, gate_col, conv_w, conv_b, skip, g_all, order, fwd_u, inv, conv_u):
    p = CONV_BLOCK
    c = HY_WIDTH
    ucol = u_col if conv_u else 0
    cwu = conv_w[:, ucol * c:(ucol + 1) * c]
    cbu = conv_b[:, ucol * c:(ucol + 1) * c]
    cwg = conv_w[:, gate_col * c:(gate_col + 1) * c]
    cbg = conv_b[:, gate_col * c:(gate_col + 1) * c]
    return pl.pallas_call(
        functools.partial(_hyena_kernel, conv_u=conv_u),
        grid=(BATCH,),
        in_specs=[pl.BlockSpec((1, SEQ, c), lambda b: (b, 0, u_col)),
                  pl.BlockSpec((1, SEQ, c), lambda b: (b, 0, gate_col)),
                  _resident((3, c)), _resident((1, c)), _resident((3, c)), _resident((1, c)),
                  _resident((1, c)),
                  pl.BlockSpec((1, len(_FILTER_PLANES), 2 * p, c), lambda b: (order, 0, 0, 0),
                               pipeline_mode=pl.Buffered(1)),
                  _resident((2 * p, p)),
                  _resident((p, 2 * p))],
        out_specs=pl.BlockSpec((1, SEQ, c), lambda b: (b, 0, 0)),
        out_shape=jax.ShapeDtypeStruct((BATCH, SEQ, c), BF16),
        scratch_shapes=[pltpu.VMEM((N_CONV_BLOCKS, 2 * p, c), F32),
                        pltpu.VMEM((SEQ, c) if conv_u else (SUBLANES * 2, LANES), BF16),
                        pltpu.VMEM((N_CONV_BLOCKS, 2 * p, c), BF16)],
        compiler_params=pltpu.CompilerParams(vmem_limit_bytes=HYENA_VMEM_LIMIT),
        name="hyena%d" % order,
    )(u, zhy, cwu, cbu, cwg, cbg, skip, g_all, fwd_u, inv)


def _ret_kernel(lg_ref, q_ref, kt_ref, v_ref, g_ref, s_ref, o_ref, b_scr, decay_scr, qw_scr):
    c = RET_CHUNK
    nc = SEQ // c
    dk = RET_QK_DIM
    dv = RET_V_DIM
    heads = range(RET_HEADS_PER_STEP)
    lg_f = [lg_ref[0, pl.program_id(0) * RET_HEADS_PER_STEP + hh] for hh in heads]
    lg_b = [lg_ref[1, pl.program_id(0) * RET_HEADS_PER_STEP + hh] for hh in heads]

    @pl.when(pl.program_id(1) == 0)
    def _():
        ii = lax.broadcasted_iota(jnp.int32, (c, c), 0).astype(F32)
        jj = lax.broadcasted_iota(jnp.int32, (c, c), 1).astype(F32)
        dif = ii - jj
        pos_q = lax.broadcasted_iota(jnp.int32, (c, dk), 0).astype(F32)
        for hh in heads:
            decay_scr[hh] = jnp.where(dif >= 0.0, jnp.exp(lg_f[hh] * jnp.maximum(dif, 0.0)),
                                      jnp.exp(lg_b[hh] * jnp.maximum(-dif, 0.0)))
            qw_scr[hh, 0] = jnp.exp(lg_f[hh] * (pos_q + 1.0))
            qw_scr[hh, 1] = jnp.exp(lg_b[hh] * (c - pos_q))

    pos_r = lax.broadcasted_iota(jnp.int32, (1, c), 1).astype(F32)
    ones = jnp.ones((1, dv), F32)
    kw_f = [jnp.exp(lg_f[hh] * (c - 1.0 - pos_r)) for hh in heads]
    kw_b = [jnp.exp(lg_b[hh] * pos_r) for hh in heads]
    dec_f = [jnp.exp(lg_f[hh] * float(c) * ones) for hh in heads]
    dec_b = [jnp.exp(lg_b[hh] * float(c) * ones) for hh in heads]

    def kt_chunk(hh, n):
        return kt_ref[0, hh * dk:(hh + 1) * dk, n * c:(n + 1) * c]

    def v_chunk(hh, n):
        return v_ref[0, n * c:(n + 1) * c, hh * dv:(hh + 1) * dv]

    state = [s_ref[0, hh, dk:2 * dk, :] for hh in heads]
    for hh in heads:
        b_scr[hh, nc - 1] = state[hh]
    for n in range(nc - 1, 0, -1):
        for hh in heads:
            ktn = (kt_chunk(hh, n).astype(F32) * kw_b[hh]).astype(BF16)
            state[hh] = state[hh] * dec_b[hh] + _dot(ktn, v_chunk(hh, n))
            b_scr[hh, n - 1] = state[hh]

    state = [s_ref[0, hh, 0:dk, :] for hh in heads]
    for n in range(nc):
        for hh in heads:
            qn = q_ref[0, hh, n * c:(n + 1) * c, :]
            ktn = kt_chunk(hh, n)
            vn = v_chunk(hh, n)
            scores = (_dot(qn, ktn) * decay_scr[hh]).astype(BF16)
            qf = qn.astype(F32)
            o = _dot(scores, vn)
            o = o + _dot((qf * qw_scr[hh, 0]).astype(BF16), state[hh].astype(BF16))
            o = o + _dot((qf * qw_scr[hh, 1]).astype(BF16), b_scr[hh, n].astype(BF16))
            state[hh] = state[hh] * dec_f[hh] + _dot((ktn.astype(F32) * kw_f[hh]).astype(BF16), vn)
            o = o * lax.rsqrt(jnp.mean(o * o, axis=-1, keepdims=True) + EPS)
            gate = g_ref[0, n * c:(n + 1) * c, hh * dv:(hh + 1) * dv].astype(F32)
            o_ref[0, n * c:(n + 1) * c, hh * dv:(hh + 1) * dv] = (_silu(gate) * o).astype(BF16)


def _ret_call(lg, q, kt, v, g, s):
    dk, dv = RET_QK_DIM, RET_V_DIM
    hp = RET_HEADS_PER_STEP
    return pl.pallas_call(
        _ret_kernel,
        grid=(RET_HEADS // hp, BATCH),
        in_specs=[pl.BlockSpec(memory_space=pltpu.SMEM),
                  pl.BlockSpec((1, hp, SEQ, dk), lambda h, b: (b, h, 0, 0)),
                  pl.BlockSpec((1, hp * dk, SEQ), lambda h, b: (b, h, 0)),
                  pl.BlockSpec((1, SEQ, hp * dv), lambda h, b: (b, 0, h)),
                  pl.BlockSpec((1, SEQ, hp * dv), lambda h, b: (b, 0, h)),
                  pl.BlockSpec((1, hp, 2 * dk, dv), lambda h, b: (b, h, 0, 0))],
        out_specs=pl.BlockSpec((1, SEQ, hp * dv), lambda h, b: (b, 0, h)),
        out_shape=jax.ShapeDtypeStruct((BATCH, SEQ, RET_WIDTH), BF16),
        scratch_shapes=[pltpu.VMEM((hp, SEQ // RET_CHUNK, dk, dv), F32),
                        pltpu.VMEM((hp, RET_CHUNK, RET_CHUNK), F32),
                        pltpu.VMEM((hp, 2, RET_CHUNK, dk), F32)],
        compiler_params=pltpu.CompilerParams(vmem_limit_bytes=VMEM_LIMIT),
        name="ret",
    )(lg, q, kt, v, g, s)


def _ffn_kernel(x_ref, xp_ref, xn_ref, yh_ref, yhp_ref, yhn_ref, yr_ref, yrp_ref, yrn_ref,
                g1_ref, sh_ref, sc_ref, g2_ref, n2_ref, nf_ref,
                woh_ref, wor_ref, wup_ref, cw_ref, cb_ref, wdn_ref,
                o_ref, hb_scr, x1_scr, av_scr, ag_scr, act_scr):
    t = TOK_TILE
    i = pl.program_id(1)
    nt = pl.num_programs(1)
    th = t // 2
    tile_halves = (slice(0, th), slice(th, t))

    def mixed(xr, yh, yr, rows=slice(None)):
        return xr[0, rows, :] + g1_ref[0] * (_dot(yh[0, rows, :], woh_ref[...]) + _dot(yr[0, rows, :], wor_ref[...]))

    def hidden(x1):
        return _norm_mod(x1, n2_ref[...], sh_ref[0], sc_ref[0])

    half = (t + 2 * HALO) // 2
    up_halves = (slice(0, half), slice(half, 2 * half))

    def up(slot, cblk, rows):
        c0 = cblk * FFN_COLS
        hb = hb_scr[rows, :]
        av_scr[slot, rows, :] = _dot(hb, wup_ref[:, c0:c0 + FFN_COLS])
        ag_scr[slot, rows, :] = _dot(hb, wup_ref[:, D_FF + c0:D_FF + c0 + FFN_COLS])

    for rows in tile_halves:
        x1_scr[rows, :] = mixed(x_ref, yh_ref, yr_ref, rows)
    hp = hidden(mixed(xp_ref, yhp_ref, yrp_ref))
    hn = hidden(mixed(xn_ref, yhn_ref, yrn_ref))
    hb_scr[0:HALO, :] = jnp.where(i > 0, hp, 0.0).astype(BF16)
    hb_scr[HALO + t:2 * HALO + t, :] = jnp.where(i < nt - 1, hn, 0.0).astype(BF16)
    hb_scr[HALO:HALO + th, :] = hidden(x1_scr[tile_halves[0], :]).astype(BF16)
    up(0, 0, up_halves[0])
    hb_scr[HALO + th:HALO + t, :] = hidden(x1_scr[tile_halves[1], :]).astype(BF16)
    up(0, 0, up_halves[1])

    def conv(scr, slot, col):
        w = cw_ref[:, col:col + FFN_COLS]
        return (scr[slot, HALO - 1:HALO - 1 + t, :] * w[0:1] + scr[slot, HALO:HALO + t, :] * w[1:2]
                + scr[slot, HALO + 1:HALO + 1 + t, :] * w[2:3] + cb_ref[:, col:col + FFN_COLS])

    n_blk = D_FF // FFN_COLS
    for cblk in range(n_blk):
        c0 = cblk * FFN_COLS
        slot = cblk % 2
        if cblk + 1 < n_blk:
            for rows in up_halves:
                up(1 - slot, cblk + 1, rows)
        act_scr[:, c0:c0 + FFN_COLS] = (
            _silu(conv(ag_scr, slot, D_FF + c0)) * conv(av_scr, slot, c0)).astype(BF16)

    down = [_dot(act_scr[rows, :], wdn_ref[...]) for rows in tile_halves]
    for rows, ffn in zip(tile_halves, down):
        x2 = x1_scr[rows, :] + g2_ref[0] * ffn
        o_ref[0, rows, :] = x2 * lax.rsqrt(jnp.mean(x2 * x2, axis=-1, keepdims=True) + EPS) * nf_ref[...]


def _ffn_call(x, y_hy, y_ret, mod3, norm2, norm_f, w_oh, w_or, w_up, conv_w, conv_b, w_dn):
    t = TOK_TILE
    r = t // HALO
    last = SEQ // HALO - 1

    def main(width):
        return pl.BlockSpec((1, t, width), lambda b, i: (b, i, 0))

    def prev(width):
        return pl.BlockSpec((1, HALO, width), lambda b, i: (b, jnp.maximum(i * r - 1, 0), 0))

    def nxt(width):
        return pl.BlockSpec((1, HALO, width), lambda b, i: (b, jnp.minimum((i + 1) * r, last), 0))

    def modrow(k):
        return pl.BlockSpec((1, 1, D_MODEL), lambda b, i: (b, 0, k))

    return pl.pallas_call(
        _ffn_kernel,
        grid=(BATCH, SEQ // t),
        in_specs=[main(D_MODEL), prev(D_MODEL), nxt(D_MODEL),
                  main(HY_WIDTH), prev(HY_WIDTH), nxt(HY_WIDTH),
                  main(RET_WIDTH), prev(RET_WIDTH), nxt(RET_WIDTH),
                  modrow(2), modrow(3), modrow(4), modrow(5),
                  _resident((1, D_MODEL)), _resident((1, D_MODEL)),
                  _resident((HY_WIDTH, D_MODEL)), _resident((RET_WIDTH, D_MODEL)),
                  _resident((D_MODEL, 2 * D_FF)),
                  _resident((3, 2 * D_FF)), _resident((1, 2 * D_FF)),
                  _resident((D_FF, D_MODEL))],
        out_specs=pl.BlockSpec((1, t, D_MODEL), lambda b, i: (b, i, 0)),
        out_shape=jax.ShapeDtypeStruct((BATCH, SEQ, D_MODEL), F32),
        scratch_shapes=[pltpu.VMEM((t + 2 * HALO, D_MODEL), BF16),
                        pltpu.VMEM((t, D_MODEL), F32),
                        pltpu.VMEM((2, t + 2 * HALO, FFN_COLS), F32),
                        pltpu.VMEM((2, t + 2 * HALO, FFN_COLS), F32),
                        pltpu.VMEM((t, D_FF), BF16)],
        compiler_params=pltpu.CompilerParams(vmem_limit_bytes=VMEM_LIMIT),
        name="ffn",
    )(x, x, x, y_hy, y_hy, y_hy, y_ret, y_ret, y_ret, mod3, mod3, mod3, mod3,
      norm2, norm_f, w_oh, w_or, w_up, conv_w, conv_b, w_dn)


def kernel(x, c, ctx, c_ctx, w_mod, b_mod, norm1, w_in, hy_conv_w, hy_conv_b, hy_w1, hy_b1, hy_f1,
           hy_w2, hy_b2, hy_f2, hy_w3, hy_bias, ret_logit_f, ret_logit_b, w_out, norm2,
           ffn_w_up, ffn_conv_w, ffn_conv_b, ffn_w_down, norm_f):
    layer = 0
    rope = tuple(jnp.asarray(a) for a in _rope_tables())
    zt, t_row, absdelta = (jnp.asarray(a) for a in _filter_features())
    fwd_np, inv_np = _dft_matrices()
    fwd_u = jnp.asarray(fwd_np[:, :CONV_BLOCK]).astype(BF16)
    inv = jnp.asarray(inv_np).astype(BF16)

    w_in_l = w_in[layer]
    w_hy = w_in_l[:, :HY_COLS].astype(BF16)
    w_q = w_in_l[:, Q_OFF:K_OFF].astype(BF16)
    w_kt = w_in_l[:, K_OFF:V_OFF].T.astype(BF16)
    w_v = w_in_l[:, V_OFF:G_OFF].astype(BF16)
    w_g = w_in_l[:, G_OFF:].astype(BF16)
    w_oh = w_out[layer][:HY_WIDTH].astype(BF16)
    w_or = w_out[layer][HY_WIDTH:].astype(BF16)
    w_up = ffn_w_up[layer].astype(BF16)
    w_dn = ffn_w_down[layer].astype(BF16)
    row = lambda a: a.reshape(1, -1)
    col = lambda a: a.reshape(-1, 1)
    w1p = jnp.pad(hy_w1[layer], ((0, HY_FILTER_WIDTH - hy_w1.shape[1]), (0, 0)))
    lg = jnp.stack([jax.nn.log_sigmoid(ret_logit_f[layer].astype(F32)),
                    jax.nn.log_sigmoid(ret_logit_b[layer].astype(F32))])

    cc = jnp.concatenate([c, c_ctx[None, :], jnp.zeros((MOD_ROWS - BATCH - 1, D_MODEL), F32)], axis=0)
    mod = _mod_call(cc, w_mod[layer], row(b_mod[layer]))
    mod3 = mod.reshape(MOD_ROWS, 1, 6 * D_MODEL)
    norm1_r = row(norm1[layer])

    s_ctx = _ctx_call(lg, ctx, mod3, norm1_r, w_kt, w_v)
    zhy, q, kt, v, g = _inproj_call(x, mod3, norm1_r, w_hy, w_q, w_v, w_g, w_kt, rope)

    k_two_sided = _filter_mlp_call(zt, t_row, w1p.T, col(hy_b1[layer]), col(hy_f1[layer]), hy_w2[layer].T,
                                   col(hy_b2[layer]), col(hy_f2[layer]), hy_w3[layer].T, absdelta)
    g_all = _filter_dft_call(k_two_sided, fwd_u)
    conv_w, conv_b = hy_conv_w[layer], row(hy_conv_b[layer])
    y1 = _hyena_call(zhy, 0, zhy, 1, conv_w, conv_b, hy_bias[layer][0:1], g_all, 0, fwd_u, inv, True)
    y_hy = _hyena_call(y1, 0, zhy, 2, conv_w, conv_b, hy_bias[layer][1:2], g_all, 1, fwd_u, inv, False)

    y_ret = _ret_call(lg, q, kt, v, g, s_ctx)

    return _ffn_call(x, y_hy, y_ret, mod3, row(norm2[layer]), row(norm_f), w_oh, w_or, w_up,
                     ffn_conv_w[layer], row(ffn_conv_b[layer]), w_dn)
```

```python
import functools
import math

import numpy as np
import jax
import jax.numpy as jnp
from jax import lax
from jax.experimental import pallas as pl
from jax.experimental.pallas import tpu as pltpu

F32 = jnp.float32
BF16 = jnp.bfloat16

D_MODEL = 1024
BATCH = 8
SEQ = 2048
CTX_LEN = 256
GRID_W = 64
HY_WIDTH = 512
HY_ORDER = 2
HY_EMB_BANDS = 16
HY_FILTER_WIDTH = 64
HY_FAST_DECAY = 0.3
HY_SLOW_DECAY = 1.5
HY_TARGET = 1e-2
RET_WIDTH = 512
RET_HEADS = 4
RET_QK_DIM = 64
RET_V_DIM = 128
ROPE_BASE = 10000.0
D_FF = 2816
EPS = 1e-6
HY_COLS = (HY_ORDER + 1) * HY_WIDTH
Q_OFF = HY_COLS
K_OFF = Q_OFF + RET_HEADS * RET_QK_DIM
V_OFF = K_OFF + RET_HEADS * RET_QK_DIM
G_OFF = V_OFF + RET_WIDTH
K_SCALE = RET_QK_DIM ** -0.5

MOD_ROWS = 16
TOK_TILE = 512
IN_TILE = 1024
IN_PIECE = 256
SUBLANES = 8
LANES = 128
HALO = 16
CONV_BLOCK = 512
N_CONV_BLOCKS = SEQ // CONV_BLOCK
FREQ_ROWS = 16
FREQ_LANES = 128
DFT_ROWS = 256
CTX_PER_STEP = 4
RET_CHUNK = 256
RET_HEADS_PER_STEP = 4
FFN_COLS = 256
FFN_SLOTS = 3
VMEM_LIMIT = 56 * 1024 * 1024
HYENA_VMEM_LIMIT = 62 * 1024 * 1024

_NT = (((1,), (1,)), ((), ()))


def _dot(a, b):
    return jnp.dot(a, b, preferred_element_type=F32)


def _dot_hi(a, b):
    return jnp.dot(a, b, preferred_element_type=F32, precision=lax.Precision.HIGHEST)


def _silu(x):
    return x * (1.0 / (1.0 + jnp.exp(-x)))


def _norm_mod(x, gain, shift, scale):
    y = x * lax.rsqrt(jnp.mean(x * x, axis=-1, keepdims=True) + EPS)
    return (y * gain) * (1.0 + scale) + shift


def _resident(shape):
    nd = len(shape)
    return pl.BlockSpec(shape, lambda *_: (0,) * nd, pipeline_mode=pl.Buffered(1))


@functools.lru_cache(maxsize=None)
def _rope_tables():
    pos = np.arange(SEQ)
    row = (pos // GRID_W).astype(np.float64)
    col = (pos % GRID_W).astype(np.float64)
    quarter = RET_QK_DIM // 4
    inv_freq = ROPE_BASE ** (-np.arange(quarter, dtype=np.float64) / quarter)
    ang = np.concatenate([row[:, None] * inv_freq, col[:, None] * inv_freq], axis=-1)
    cos, sin = np.cos(ang), np.sin(ang)
    cos_h = np.concatenate([cos, cos], axis=-1)
    sin_h = np.concatenate([-sin, sin], axis=-1)
    cos_t = np.tile(cos_h, (1, RET_HEADS))
    sin_t = np.tile(sin_h, (1, RET_HEADS))
    return (cos_t.astype(np.float32), sin_t.astype(np.float32),
            np.ascontiguousarray((cos_t * K_SCALE).T).astype(np.float32),
            np.ascontiguousarray((sin_t * K_SCALE).T).astype(np.float32))


@functools.lru_cache(maxsize=None)
def _filter_features():
    lag = np.abs(np.arange(2 * SEQ) - SEQ).astype(np.float64)
    t = lag / (SEQ - 1)
    bands = np.linspace(1e-4, HY_EMB_BANDS - 1, HY_EMB_BANDS)
    ang = 2.0 * math.pi * lag[:, None] * bands[None, :] / SEQ
    z = np.concatenate([t[:, None], np.cos(ang), -np.sin(ang)], axis=-1)
    zp = np.zeros((2 * SEQ, HY_FILTER_WIDTH), np.float64)
    zp[:, :z.shape[1]] = z
    max_decay = math.log(HY_TARGET) / HY_FAST_DECAY
    min_decay = math.log(HY_TARGET) / HY_SLOW_DECAY
    absdelta = np.abs(np.linspace(min_decay, max_decay, HY_WIDTH))[:, None]
    return (np.ascontiguousarray(zp.T).astype(np.float32), t[None, :].astype(np.float32),
            absdelta.astype(np.float32))


@functools.lru_cache(maxsize=None)
def _dft_matrices():
    p = CONV_BLOCK
    n = 2 * p
    f = np.arange(p, dtype=np.float64)[:, None] + 0.5
    t = np.arange(n, dtype=np.float64)[None, :]
    theta = 2.0 * math.pi * f * t / n
    fwd = np.concatenate([np.cos(theta), -np.sin(theta)], axis=0)
    th_out = theta[:, p:].T
    inv = np.concatenate([np.cos(th_out), -np.sin(th_out)], axis=1) / p
    return fwd.astype(np.float32), inv.astype(np.float32)


def _mod_kernel(c_ref, w_ref, b_ref, o_ref):
    s = _silu(c_ref[...]).astype(BF16)
    o_ref[...] = _dot(s, w_ref[...].astype(BF16)) + b_ref[...]


def _mod_call(cc, w_mod, b_mod):
    ncol = 6 * D_MODEL
    blk = ncol // 4
    return pl.pallas_call(
        _mod_kernel,
        grid=(ncol // blk,),
        in_specs=[pl.BlockSpec((MOD_ROWS, D_MODEL), lambda j: (0, 0)),
                  pl.BlockSpec((D_MODEL, blk), lambda j: (0, j)),
                  pl.BlockSpec((1, blk), lambda j: (0, j))],
        out_specs=pl.BlockSpec((MOD_ROWS, blk), lambda j: (0, j)),
        out_shape=jax.ShapeDtypeStruct((MOD_ROWS, ncol), F32),
        compiler_params=pltpu.CompilerParams(vmem_limit_bytes=VMEM_LIMIT),
        name="mod",
    )(cc, w_mod, b_mod)


def _ctx_kernel(lg_ref, x_ref, sh_ref, sc_ref, n1_ref, wkt_ref, wv_ref, s_ref):
    pos = lax.broadcasted_iota(jnp.int32, (1, CTX_LEN), 1).astype(F32)
    w_f = [jnp.exp(lg_ref[0, hh] * (CTX_LEN - 1.0 - pos)) for hh in range(RET_HEADS)]
    w_b = [jnp.exp(lg_ref[1, hh] * pos) for hh in range(RET_HEADS)]
    for bb in range(CTX_PER_STEP):
        h = _norm_mod(x_ref[bb], n1_ref[...], sh_ref[0], sc_ref[0]).astype(BF16)
        kt = lax.dot_general(wkt_ref[...], h, _NT, preferred_element_type=F32) * K_SCALE
        v = _dot(h, wv_ref[...])
        for hh in range(RET_HEADS):
            kth = kt[hh * RET_QK_DIM:(hh + 1) * RET_QK_DIM, :]
            vh = v[:, hh * RET_V_DIM:(hh + 1) * RET_V_DIM].astype(BF16)
            s_ref[bb, hh, 0:RET_QK_DIM, :] = _dot((kth * w_f[hh]).astype(BF16), vh)
            s_ref[bb, hh, RET_QK_DIM:2 * RET_QK_DIM, :] = _dot((kth * w_b[hh]).astype(BF16), vh)


def _ctx_call(lg, ctx, mod3, norm1, w_kt, w_v):
    return pl.pallas_call(
        _ctx_kernel,
        grid=(BATCH // CTX_PER_STEP,),
        in_specs=[pl.BlockSpec(memory_space=pltpu.SMEM),
                  pl.BlockSpec((CTX_PER_STEP, CTX_LEN, D_MODEL), lambda b: (b, 0, 0)),
                  pl.BlockSpec((1, 1, D_MODEL), lambda b: (BATCH, 0, 0)),
                  pl.BlockSpec((1, 1, D_MODEL), lambda b: (BATCH, 0, 1)),
                  _resident((1, D_MODEL)),
                  _resident((RET_HEADS * RET_QK_DIM, D_MODEL)),
                  _resident((D_MODEL, RET_WIDTH))],
        out_specs=pl.BlockSpec((CTX_PER_STEP, RET_HEADS, 2 * RET_QK_DIM, RET_V_DIM), lambda b: (b, 0, 0, 0)),
        out_shape=jax.ShapeDtypeStruct((BATCH, RET_HEADS, 2 * RET_QK_DIM, RET_V_DIM), F32),
        compiler_params=pltpu.CompilerParams(vmem_limit_bytes=VMEM_LIMIT),
        name="ctx",
    )(lg, ctx, mod3, mod3, norm1, w_kt, w_v)


def _swap_halves(x, axis):
    n = x.shape[axis]
    half = RET_QK_DIM // 2
    idx = lax.broadcasted_iota(jnp.int32, x.shape, axis)
    first = (idx & (RET_QK_DIM - 1)) < half
    return jnp.where(first, pltpu.roll(x, n - half, axis), pltpu.roll(x, half, axis))


def _inproj_kernel(x_ref, sh_ref, sc_ref, n1_ref, why_ref, wq_ref, wv_ref, wg_ref, wkt_ref,
                   cq_ref, sq_ref, ck_ref, sk_ref,
                   zhy_ref, q_ref, kt_ref, v_ref, g_ref):
    pieces = [slice(r, r + IN_PIECE) for r in range(0, IN_TILE, IN_PIECE)]
    hbs = [_norm_mod(x_ref[0, rows, :], n1_ref[...], sh_ref[0], sc_ref[0]).astype(BF16) for rows in pieces]
    for rows, hb in zip(pieces, hbs):
        zhy_ref[0, rows, :] = _dot(hb, why_ref[...]).astype(BF16)
        v_ref[0, rows, :] = _dot(hb, wv_ref[...]).astype(BF16)
        g_ref[0, rows, :] = _dot(hb, wg_ref[...]).astype(BF16)
        q = _dot(hb, wq_ref[...])
        q = q * cq_ref[rows, :] + _swap_halves(q, 1) * sq_ref[rows, :]
        for hh in range(RET_HEADS):
            q_ref[0, hh, rows, :] = q[:, hh * RET_QK_DIM:(hh + 1) * RET_QK_DIM].astype(BF16)
        kt = lax.dot_general(wkt_ref[...], hb, _NT, preferred_element_type=F32)
        kt = kt * ck_ref[:, rows] + _swap_halves(kt, 0) * sk_ref[:, rows]
        kt_ref[0, :, rows] = kt.astype(BF16)


def _inproj_call(x, mod3, norm1, w_hy, w_q, w_v, w_g, w_kt, rope):
    cq, sq, ck, sk = rope
    t = IN_TILE
    qk = RET_HEADS * RET_QK_DIM
    return pl.pallas_call(
        _inproj_kernel,
        grid=(BATCH, SEQ // t),
        in_specs=[pl.BlockSpec((1, t, D_MODEL), lambda b, i: (b, i, 0)),
                  pl.BlockSpec((1, 1, D_MODEL), lambda b, i: (b, 0, 0)),
                  pl.BlockSpec((1, 1, D_MODEL), lambda b, i: (b, 0, 1)),
                  _resident((1, D_MODEL)),
                  _resident((D_MODEL, HY_COLS)),
                  _resident((D_MODEL, qk)),
                  _resident((D_MODEL, RET_WIDTH)),
                  _resident((D_MODEL, RET_WIDTH)),
                  _resident((qk, D_MODEL)),
                  pl.BlockSpec((t, qk), lambda b, i: (i, 0)),
                  pl.BlockSpec((t, qk), lambda b, i: (i, 0)),
                  pl.BlockSpec((qk, t), lambda b, i: (0, i)),
                  pl.BlockSpec((qk, t), lambda b, i: (0, i))],
        out_specs=[pl.BlockSpec((1, t, HY_COLS), lambda b, i: (b, i, 0)),
                   pl.BlockSpec((1, RET_HEADS, t, RET_QK_DIM), lambda b, i: (b, 0, i, 0)),
                   pl.BlockSpec((1, qk, t), lambda b, i: (b, 0, i)),
                   pl.BlockSpec((1, t, RET_WIDTH), lambda b, i: (b, i, 0)),
                   pl.BlockSpec((1, t, RET_WIDTH), lambda b, i: (b, i, 0))],
        out_shape=[jax.ShapeDtypeStruct((BATCH, SEQ, HY_COLS), BF16),
                   jax.ShapeDtypeStruct((BATCH, RET_HEADS, SEQ, RET_QK_DIM), BF16),
                   jax.ShapeDtypeStruct((BATCH, qk, SEQ), BF16),
                   jax.ShapeDtypeStruct((BATCH, SEQ, RET_WIDTH), BF16),
                   jax.ShapeDtypeStruct((BATCH, SEQ, RET_WIDTH), BF16)],
        compiler_params=pltpu.CompilerParams(vmem_limit_bytes=VMEM_LIMIT),
        name="inproj",
    )(x, mod3, mod3, norm1, w_hy, w_q, w_v, w_g, w_kt, cq, sq, ck, sk)


def _filter_mlp_kernel(zt_ref, t_ref, w1t_ref, b1_ref, f1_ref, w2t_ref, b2_ref, f2_ref, w3t_ref, adel_ref,
                       kt_ref):
    hid = jnp.sin(f1_ref[...] * (_dot_hi(w1t_ref[...], zt_ref[...]) + b1_ref[...]))
    hid = jnp.sin(f2_ref[...] * (_dot_hi(w2t_ref[...], hid) + b2_ref[...])).astype(BF16)
    c = HY_WIDTH
    for half in range(2):
        lags = slice(half * SEQ, (half + 1) * SEQ)
        window = jnp.exp(-adel_ref[...] * t_ref[:, lags])
        direction = 1 - half
        for o in range(HY_ORDER):
            r0 = (direction * HY_ORDER + o) * c
            w3 = w3t_ref[r0:r0 + c, :].astype(BF16)
            kt_ref[o, :, lags] = (_dot(w3, hid[:, lags]) * window).astype(BF16)


def _filter_mlp_call(zt, t_row, w1t, b1, f1, w2t, b2, f2, w3t, absdelta):
    fw = HY_FILTER_WIDTH
    n_out = 2 * HY_ORDER * HY_WIDTH
    return pl.pallas_call(
        _filter_mlp_kernel,
        grid=(1,),
        in_specs=[_resident((fw, 2 * SEQ)), _resident((1, 2 * SEQ)),
                  _resident((fw, fw)), _resident((fw, 1)), _resident((fw, 1)),
                  _resident((fw, fw)), _resident((fw, 1)), _resident((fw, 1)),
                  _resident((n_out, fw)), _resident((HY_WIDTH, 1))],
        out_specs=pl.BlockSpec((HY_ORDER, HY_WIDTH, 2 * SEQ), lambda i: (0, 0, 0)),
        out_shape=jax.ShapeDtypeStruct((HY_ORDER, HY_WIDTH, 2 * SEQ), BF16),
        compiler_params=pltpu.CompilerParams(vmem_limit_bytes=VMEM_LIMIT),
        name="filter_mlp",
    )(zt, t_row, w1t, b1, f1, w2t, b2, f2, w3t, absdelta)


_FILTER_PLANES = (
    {0: 1},
    {-1: 1, 0: -1},
    {1: 1, 0: -1},
    {-2: 1, 0: -1},
    {-3: 1, -1: -1, -2: -1, 0: 1},
    {-1: 1, 1: -1, -2: -1, 0: 1},
    {2: 1, 0: -1},
    {1: 1, -1: -1, 2: -1, 0: 1},
    {3: 1, 1: -1, 2: -1, 0: 1},
)
DFT_BLOCKS_PER_STEP = 2
PLANES_PER_STEP = 3


def _filter_dft_kernel(kt_ref, fwd_ref, h_ref, t_scr):
    p = CONV_BLOCK
    n_dft_steps = 2 * N_CONV_BLOCKS // DFT_BLOCKS_PER_STEP
    step = pl.program_id(1)

    for s in range(n_dft_steps):
        @pl.when(step == s)
        def _(s=s):
            for k in range(DFT_BLOCKS_PER_STEP):
                t_scr[s * DFT_BLOCKS_PER_STEP + k] = lax.dot_general(
                    fwd_ref[...], kt_ref[0, :, k * p:(k + 1) * p], _NT, preferred_element_type=F32)
            for k in range(PLANES_PER_STEP):
                h_ref[0, k] = t_scr[s * DFT_BLOCKS_PER_STEP]

    row = lax.broadcasted_iota(jnp.int32, (p, 1), 0)
    sign = (1 - 2 * (row & 1)).astype(F32)
    re, im = slice(0, p), slice(p, 2 * p)

    def plane(k, coefs):
        def comb(shift, rows):
            acc = None
            for d, c in coefs.items():
                term = t_scr[d + N_CONV_BLOCKS - 1 + shift, rows, :]
                if acc is None:
                    acc = term
                else:
                    acc = acc + term if c > 0 else acc - term
            return acc

        h_ref[0, k, re, :] = comb(0, re) + sign * comb(1, im)
        h_ref[0, k, im, :] = comb(0, im) - sign * comb(1, re)

    for s in range(len(_FILTER_PLANES) // PLANES_PER_STEP):
        @pl.when(step == n_dft_steps + s)
        def _(s=s):
            for k in range(PLANES_PER_STEP):
                plane(k, _FILTER_PLANES[s * PLANES_PER_STEP + k])


def _filter_dft_call(kt, fwd_u):
    p = CONV_BLOCK
    n_blk = 2 * N_CONV_BLOCKS
    n_planes = len(_FILTER_PLANES)
    n_dft_steps = n_blk // DFT_BLOCKS_PER_STEP
    return pl.pallas_call(
        _filter_dft_kernel,
        grid=(HY_ORDER, n_dft_steps + n_planes // PLANES_PER_STEP),
        in_specs=[pl.BlockSpec((1, HY_WIDTH, DFT_BLOCKS_PER_STEP * p),
                               lambda o, s: (o, 0, jnp.minimum(s, n_dft_steps - 1))),
                  _resident((2 * p, p))],
        out_specs=pl.BlockSpec((1, PLANES_PER_STEP, 2 * p, HY_WIDTH),
                               lambda o, s: (o, jnp.maximum(s - n_dft_steps, 0), 0, 0)),
        out_shape=jax.ShapeDtypeStruct((HY_ORDER, n_planes, 2 * p, HY_WIDTH), F32),
        scratch_shapes=[pltpu.VMEM((n_blk, 2 * p, HY_WIDTH), F32)],
        compiler_params=pltpu.CompilerParams(vmem_limit_bytes=VMEM_LIMIT),
        name="filter_dft",
    )(kt, fwd_u)


def _conv3_rows(ref, j, n_blocks, rows, w, b):
    main = ref[0, j * rows:(j + 1) * rows, :].astype(F32)
    cols = main.shape[1]
    zeros = jnp.zeros((HALO, cols), F32)
    prev = ref[0, j * rows - HALO:j * rows, :].astype(F32) if j > 0 else zeros
    nxt = ref[0, (j + 1) * rows:(j + 1) * rows + HALO, :].astype(F32) if j < n_blocks - 1 else zeros
    ext = jnp.concatenate([prev, main, nxt], axis=0)
    n = rows + 2 * HALO
    before = pltpu.roll(ext, 1, 0)[HALO:HALO + rows]
    after = pltpu.roll(ext, n - 1, 0)[HALO:HALO + rows]
    return before * w[0:1] + main * w[1:2] + after * w[2:3] + b


def _hyena_kernel(u_ref, zg_ref, cwu_ref, cbu_ref, cwg_ref, cbg_ref, skip_ref, h_ref, fwd_ref, inv_ref,
                  o_ref, uf_scr, ub_scr, y_scr, *, conv_u):
    p = CONV_BLOCK
    nb = N_CONV_BLOCKS

    def prepare(j):
        if conv_u:
            ub_scr[j * p:(j + 1) * p, :] = _conv3_rows(u_ref, j, nb, p, cwu_ref[...], cbu_ref[...]).astype(BF16)

    def u_block(j):
        return ub_scr[j * p:(j + 1) * p, :] if conv_u else u_ref[0, j * p:(j + 1) * p, :]

    prepare(0)
    for j in range(nb):
        if j + 1 < nb:
            prepare(j + 1)
        for r in range(2 * p // DFT_ROWS):
            rows = slice(r * DFT_ROWS, (r + 1) * DFT_ROWS)
            uf_scr[j, rows, :] = _dot(fwd_ref[rows, :], u_block(j))

    def cadd(a, b):
        return a[0] + b[0], a[1] + b[1]

    def cmul(m, x):
        return m[0] * x[0] - m[1] * x[1], m[0] * x[1] + m[1] * x[0]

    def toeplitz2(k0, x0, x1, re, im, ln):
        m0, mu, ml = ((h_ref[0, k0 + t, re, ln], h_ref[0, k0 + t, im, ln]) for t in range(3))
        p1 = cmul(m0, cadd(x0, x1))
        return cadd(p1, cmul(mu, x1)), cadd(p1, cmul(ml, x0))

    assert nb == 4
    for r in range(p // FREQ_ROWS):
        re = slice(r * FREQ_ROWS, (r + 1) * FREQ_ROWS)
        im = slice(p + r * FREQ_ROWS, p + (r + 1) * FREQ_ROWS)
        for cb in range(HY_WIDTH // FREQ_LANES):
            ln = slice(cb * FREQ_LANES, (cb + 1) * FREQ_LANES)
            u = [(uf_scr[j, re, ln], uf_scr[j, im, ln]) for j in range(nb)]
            d0, d1 = toeplitz2(0, cadd(u[0], u[2]), cadd(u[1], u[3]), re, im, ln)
            b0, b1 = toeplitz2(3, u[2], u[3], re, im, ln)
            c0, c1 = toeplitz2(6, u[0], u[1], re, im, ln)
            for i, yi in enumerate((cadd(d0, b0), cadd(d1, b1), cadd(d0, c0), cadd(d1, c1))):
                y_scr[i, re, ln] = yi[0].astype(BF16)
                y_scr[i, im, ln] = yi[1].astype(BF16)

    for i in range(nb):
        gate = _conv3_rows(zg_ref, i, nb, p, cwg_ref[...], cbg_ref[...])
        y = _dot(inv_ref[...], y_scr[i])
        o_ref[0, i * p:(i + 1) * p, :] = (
            gate * (y + u_block(i).astype(F32) * skip_ref[...])).astype(BF16)


def _hyena_call(u, u_col, zhy, gate_col, conv_w, conv_b, skip, g_all, order, fwd_u, inv, conv_u):
    p = CONV_BLOCK
    c = HY_WIDTH
    ucol = u_col if conv_u else 0
    cwu = conv_w[:, ucol * c:(ucol + 1) * c]
    cbu = conv_b[:, ucol * c:(ucol + 1) * c]
    cwg = conv_w[:, gate_col * c:(gate_col + 1) * c]
    cbg = conv_b[:, gate_col * c:(gate_col + 1) * c]
    return pl.pallas_call(
        functools.partial(_hyena_kernel, conv_u=conv_u),
        grid=(BATCH,),
        in_specs=[pl.BlockSpec((1, SEQ, c), lambda b: (b, 0, u_col)),
                  pl.BlockSpec((1, SEQ, c), lambda b: (b, 0, gate_col)),
                  _resident((3, c)), _resident((1, c)), _resident((3, c)), _resident((1, c)),
                  _resident((1, c)),
                  pl.BlockSpec((1, len(_FILTER_PLANES), 2 * p, c), lambda b: (order, 0, 0, 0),
                               pipeline_mode=pl.Buffered(1)),
                  _resident((2 * p, p)),
                  _resident((p, 2 * p))],
        out_specs=pl.BlockSpec((1, SEQ, c), lambda b: (b, 0, 0)),
        out_shape=jax.ShapeDtypeStruct((BATCH, SEQ, c), BF16),
        scratch_shapes=[pltpu.VMEM((N_CONV_BLOCKS, 2 * p, c), F32),
                        pltpu.VMEM((SEQ, c) if conv_u else (SUBLANES * 2, LANES), BF16),
                        pltpu.VMEM((N_CONV_BLOCKS, 2 * p, c), BF16)],
        compiler_params=pltpu.CompilerParams(vmem_limit_bytes=HYENA_VMEM_LIMIT),
        name="hyena%d" % order,
    )(u, zhy, cwu, cbu, cwg, cbg, skip, g_all, fwd_u, inv)


def _ret_kernel(lg_ref, q_ref, kt_ref, v_ref, g_ref, s_ref, o_ref, b_scr, decay_scr, qw_scr):
    c = RET_CHUNK
    nc = SEQ // c
    dk = RET_QK_DIM
    dv = RET_V_DIM
    heads = range(RET_HEADS_PER_STEP)
    lg_f = [lg_ref[0, pl.program_id(0) * RET_HEADS_PER_STEP + hh] for hh in heads]
    lg_b = [lg_ref[1, pl.program_id(0) * RET_HEADS_PER_STEP + hh] for hh in heads]

    @pl.when(pl.program_id(1) == 0)
    def _():
        ii = lax.broadcasted_iota(jnp.int32, (c, c), 0).astype(F32)
        jj = lax.broadcasted_iota(jnp.int32, (c, c), 1).astype(F32)
        dif = ii - jj
        pos_q = lax.broadcasted_iota(jnp.int32, (c, dk), 0).astype(F32)
        for hh in heads:
            decay_scr[hh] = jnp.where(dif >= 0.0, jnp.exp(lg_f[hh] * jnp.maximum(dif, 0.0)),
                                      jnp.exp(lg_b[hh] * jnp.maximum(-dif, 0.0)))
            qw_scr[hh, 0] = jnp.exp(lg_f[hh] * (pos_q + 1.0))
            qw_scr[hh, 1] = jnp.exp(lg_b[hh] * (c - pos_q))

    pos_r = lax.broadcasted_iota(jnp.int32, (1, c), 1).astype(F32)
    ones = jnp.ones((1, dv), F32)
    kw_f = [jnp.exp(lg_f[hh] * (c - 1.0 - pos_r)) for hh in heads]
    kw_b = [jnp.exp(lg_b[hh] * pos_r) for hh in heads]
    dec_f = [jnp.exp(lg_f[hh] * float(c) * ones) for hh in heads]
    dec_b = [jnp.exp(lg_b[hh] * float(c) * ones) for hh in heads]

    def kt_chunk(hh, n):
        return kt_ref[0, hh * dk:(hh + 1) * dk, n * c:(n + 1) * c]

    def v_chunk(hh, n):
        return v_ref[0, n * c:(n + 1) * c, hh * dv:(hh + 1) * dv]

    state = [s_ref[0, hh, dk:2 * dk, :] for hh in heads]
    for hh in heads:
        b_scr[hh, nc - 1] = state[hh]
    for n in range(nc - 1, 0, -1):
        for hh in heads:
            ktn = (kt_chunk(hh, n).astype(F32) * kw_b[hh]).astype(BF16)
            state[hh] = state[hh] * dec_b[hh] + _dot(ktn, v_chunk(hh, n))
            b_scr[hh, n - 1] = state[hh]

    state = [s_ref[0, hh, 0:dk, :] for hh in heads]
    for n in range(nc):
        for hh in heads:
            qn = q_ref[0, hh, n * c:(n + 1) * c, :]
            ktn = kt_chunk(hh, n)
            vn = v_chunk(hh, n)
            scores = (_dot(qn, ktn) * decay_scr[hh]).astype(BF16)
            qf = qn.astype(F32)
            o = _dot(scores, vn)
            o = o + _dot((qf * qw_scr[hh, 0]).astype(BF16), state[hh].astype(BF16))
            o = o + _dot((qf * qw_scr[hh, 1]).astype(BF16), b_scr[hh, n].astype(BF16))
            state[hh] = state[hh] * dec_f[hh] + _dot((ktn.astype(F32) * kw_f[hh]).astype(BF16), vn)
            o = o * lax.rsqrt(jnp.mean(o * o, axis=-1, keepdims=True) + EPS)
            gate = g_ref[0, n * c:(n + 1) * c, hh * dv:(hh + 1) * dv].astype(F32)
            o_ref[0, n * c:(n + 1) * c, hh * dv:(hh + 1) * dv] = (_silu(gate) * o).astype(BF16)


def _ret_call(lg, q, kt, v, g, s):
    dk, dv = RET_QK_DIM, RET_V_DIM
    hp = RET_HEADS_PER_STEP
    return pl.pallas_call(
        _ret_kernel,
        grid=(RET_HEADS // hp, BATCH),
        in_specs=[pl.BlockSpec(memory_space=pltpu.SMEM),
                  pl.BlockSpec((1, hp, SEQ, dk), lambda h, b: (b, h, 0, 0)),
                  pl.BlockSpec((1, hp * dk, SEQ), lambda h, b: (b, h, 0)),
                  pl.BlockSpec((1, SEQ, hp * dv), lambda h, b: (b, 0, h)),
                  pl.BlockSpec((1, SEQ, hp * dv), lambda h, b: (b, 0, h)),
                  pl.BlockSpec((1, hp, 2 * dk, dv), lambda h, b: (b, h, 0, 0))],
        out_specs=pl.BlockSpec((1, SEQ, hp * dv), lambda h, b: (b, 0, h)),
        out_shape=jax.ShapeDtypeStruct((BATCH, SEQ, RET_WIDTH), BF16),
        scratch_shapes=[pltpu.VMEM((hp, SEQ // RET_CHUNK, dk, dv), F32),
                        pltpu.VMEM((hp, RET_CHUNK, RET_CHUNK), F32),
                        pltpu.VMEM((hp, 2, RET_CHUNK, dk), F32)],
        compiler_params=pltpu.CompilerParams(vmem_limit_bytes=VMEM_LIMIT),
        name="ret",
    )(lg, q, kt, v, g, s)


def _ffn_kernel(x_ref, xp_ref, xn_ref, yh_ref, yhp_ref, yhn_ref, yr_ref, yrp_ref, yrn_ref,
                g1_ref, sh_ref, sc_ref, g2_ref, n2_ref, nf_ref,
                woh_ref, wor_ref, wup_ref, cw_ref, cb_ref, wdn_ref,
                o_ref, hb_scr, x1_scr, av_scr, ag_scr, act_scr):
    t = TOK_TILE
    i = pl.program_id(1)
    nt = pl.num_programs(1)
    th = t // 2
    tile_halves = (slice(0, th), slice(th, t))

    def mixed(xr, yh, yr, rows=slice(None)):
        return xr[0, rows, :] + g1_ref[0] * (_dot(yh[0, rows, :], woh_ref[...]) + _dot(yr[0, rows, :], wor_ref[...]))

    def hidden(x1):
        return _norm_mod(x1, n2_ref[...], sh_ref[0], sc_ref[0])

    half = (t + 2 * HALO) // 2
    up_halves = (slice(0, half), slice(half, 2 * half))

    def up(slot, cblk, rows):
        c0 = cblk * FFN_COLS
        hb = hb_scr[rows, :]
        av_scr[slot, rows, :] = _dot(hb, wup_ref[:, c0:c0 + FFN_COLS])
        ag_scr[slot, rows, :] = _dot(hb, wup_ref[:, D_FF + c0:D_FF + c0 + FFN_COLS])

    for rows in tile_halves:
        x1_scr[rows, :] = mixed(x_ref, yh_ref, yr_ref, rows)
    hp = hidden(mixed(xp_ref, yhp_ref, yrp_ref))
    hn = hidden(mixed(xn_ref, yhn_ref, yrn_ref))
    hb_scr[0:HALO, :] = jnp.where(i > 0, hp, 0.0).astype(BF16)
    hb_scr[HALO + t:2 * HALO + t, :] = jnp.where(i < nt - 1, hn, 0.0).astype(BF16)
    hb_scr[HALO:HALO + th, :] = hidden(x1_scr[tile_halves[0], :]).astype(BF16)
    up(0, 0, up_halves[0])
    hb_scr[HALO + th:HALO + t, :] = hidden(x1_scr[tile_halves[1], :]).astype(BF16)
    up(0, 0, up_halves[1])

    def conv(scr, slot, col):
        w = cw_ref[:, col:col + FFN_COLS]
        return (scr[slot, HALO - 1:HALO - 1 + t, :] * w[0:1] + scr[slot, HALO:HALO + t, :] * w[1:2]
                + scr[slot, HALO + 1:HALO + 1 + t, :] * w[2:3] + cb_ref[:, col:col + FFN_COLS])

    n_blk = D_FF // FFN_COLS
    for cblk in range(n_blk):
        c0 = cblk * FFN_COLS
        slot = cblk % FFN_SLOTS
        if cblk + 1 < n_blk:
            for rows in up_halves:
                up((cblk + 1) % FFN_SLOTS, cblk + 1, rows)
        act_scr[:, c0:c0 + FFN_COLS] = (
            _silu(conv(ag_scr, slot, D_FF + c0)) * conv(av_scr, slot, c0)).astype(BF16)

    down = [_dot(act_scr[rows, :], wdn_ref[...]) for rows in tile_halves]
    for rows, ffn in zip(tile_halves, down):
        x2 = x1_scr[rows, :] + g2_ref[0] * ffn
        o_ref[0, rows, :] = x2 * lax.rsqrt(jnp.mean(x2 * x2, axis=-1, keepdims=True) + EPS) * nf_ref[...]


def _ffn_call(x, y_hy, y_ret, mod3, norm2, norm_f, w_oh, w_or, w_up, conv_w, conv_b, w_dn):
    t = TOK_TILE
    r = t // HALO
    last = SEQ // HALO - 1

    def main(width):
        return pl.BlockSpec((1, t, width), lambda b, i: (b, i, 0))

    def prev(width):
        return pl.BlockSpec((1, HALO, width), lambda b, i: (b, jnp.maximum(i * r - 1, 0), 0))

    def nxt(width):
        return pl.BlockSpec((1, HALO, width), lambda b, i: (b, jnp.minimum((i + 1) * r, last), 0))

    def modrow(k):
        return pl.BlockSpec((1, 1, D_MODEL), lambda b, i: (b, 0, k))

    return pl.pallas_call(
        _ffn_kernel,
        grid=(BATCH, SEQ // t),
        in_specs=[main(D_MODEL), prev(D_MODEL), nxt(D_MODEL),
                  main(HY_WIDTH), prev(HY_WIDTH), nxt(HY_WIDTH),
                  main(RET_WIDTH), prev(RET_WIDTH), nxt(RET_WIDTH),
                  modrow(2), modrow(3), modrow(4), modrow(5),
                  _resident((1, D_MODEL)), _resident((1, D_MODEL)),
                  _resident((HY_WIDTH, D_MODEL)), _resident((RET_WIDTH, D_MODEL)),
                  _resident((D_MODEL, 2 * D_FF)),
                  _resident((3, 2 * D_FF)), _resident((1, 2 * D_FF)),
                  _resident((D_FF, D_MODEL))],
        out_specs=pl.BlockSpec((1, t, D_MODEL), lambda b, i: (b, i, 0)),
        out_shape=jax.ShapeDtypeStruct((BATCH, SEQ, D_MODEL), F32),
        scratch_shapes=[pltpu.VMEM((t + 2 * HALO, D_MODEL), BF16),
                        pltpu.VMEM((t, D_MODEL), F32),
                        pltpu.VMEM((FFN_SLOTS, t + 2 * HALO, FFN_COLS), F32),
                        pltpu.VMEM((FFN_SLOTS, t + 2 * HALO, FFN_COLS), F32),
                        pltpu.VMEM((t, D_FF), BF16)],
        compiler_params=pltpu.CompilerParams(vmem_limit_bytes=VMEM_LIMIT),
        name="ffn",
    )(x, x, x, y_hy, y_hy, y_hy, y_ret, y_ret, y_ret, mod3, mod3, mod3, mod3,
      norm2, norm_f, w_oh, w_or, w_up, conv_w, conv_b, w_dn)


def kernel(x, c, ctx, c_ctx, w_mod, b_mod, norm1, w_in, hy_conv_w, hy_conv_b, hy_w1, hy_b1, hy_f1,
           hy_w2, hy_b2, hy_f2, hy_w3, hy_bias, ret_logit_f, ret_logit_b, w_out, norm2,
           ffn_w_up, ffn_conv_w, ffn_conv_b, ffn_w_down, norm_f):
    layer = 0
    rope = tuple(jnp.asarray(a) for a in _rope_tables())
    zt, t_row, absdelta = (jnp.asarray(a) for a in _filter_features())
    fwd_np, inv_np = _dft_matrices()
    fwd_u = jnp.asarray(fwd_np[:, :CONV_BLOCK]).astype(BF16)
    inv = jnp.asarray(inv_np).astype(BF16)

    w_in_l = w_in[layer]
    w_hy = w_in_l[:, :HY_COLS].astype(BF16)
    w_q = w_in_l[:, Q_OFF:K_OFF].astype(BF16)
    w_kt = w_in_l[:, K_OFF:V_OFF].T.astype(BF16)
    w_v = w_in_l[:, V_OFF:G_OFF].astype(BF16)
    w_g = w_in_l[:, G_OFF:].astype(BF16)
    w_oh = w_out[layer][:HY_WIDTH].astype(BF16)
    w_or = w_out[layer][HY_WIDTH:].astype(BF16)
    w_up = ffn_w_up[layer].astype(BF16)
    w_dn = ffn_w_down[layer].astype(BF16)
    row = lambda a: a.reshape(1, -1)
    col = lambda a: a.reshape(-1, 1)
    w1p = jnp.pad(hy_w1[layer], ((0, HY_FILTER_WIDTH - hy_w1.shape[1]), (0, 0)))
    lg = jnp.stack([jax.nn.log_sigmoid(ret_logit_f[layer].astype(F32)),
                    jax.nn.log_sigmoid(ret_logit_b[layer].astype(F32))])

    cc = jnp.concatenate([c, c_ctx[None, :], jnp.zeros((MOD_ROWS - BATCH - 1, D_MODEL), F32)], axis=0)
    mod = _mod_call(cc, w_mod[layer], row(b_mod[layer]))
    mod3 = mod.reshape(MOD_ROWS, 1, 6 * D_MODEL)
    norm1_r = row(norm1[layer])

    s_ctx = _ctx_call(lg, ctx, mod3, norm1_r, w_kt, w_v)
    zhy, q, kt, v, g = _inproj_call(x, mod3, norm1_r, w_hy, w_q, w_v, w_g, w_kt, rope)

    k_two_sided = _filter_mlp_call(zt, t_row, w1p.T, col(hy_b1[layer]), col(hy_f1[layer]), hy_w2[layer].T,
                                   col(hy_b2[layer]), col(hy_f2[layer]), hy_w3[layer].T, absdelta)
    g_all = _filter_dft_call(k_two_sided, fwd_u)
    conv_w, conv_b = hy_conv_w[layer], row(hy_conv_b[layer])
    y1 = _hyena_call(zhy, 0, zhy, 1, conv_w, conv_b, hy_bias[layer][0:1], g_all, 0, fwd_u, inv, True)
    y_hy = _hyena_call(y1, 0, zhy, 2, conv_w, conv_b, hy_bias[layer][1:2], g_all, 1, fwd_u, inv, False)

    y_ret = _ret_call(lg, q, kt, v, g, s_ctx)

    return _ffn_call(x, y_hy, y_ret, mod3, row(norm2[layer]), row(norm_f), w_oh, w_or, w_up,
                     ffn_conv_w[layer], row(ffn_conv_b[layer]), w_dn)
```

```python
import functools
import math

import numpy as np
import jax
import jax.numpy as jnp
from jax import lax
from jax.experimental import pallas as pl
from jax.experimental.pallas import tpu as pltpu

F32 = jnp.float32
BF16 = jnp.bfloat16

D_MODEL = 1024
BATCH = 8
SEQ = 2048
CTX_LEN = 256
GRID_W = 64
HY_WIDTH = 512
HY_ORDER = 2
HY_EMB_BANDS = 16
HY_FILTER_WIDTH = 64
HY_FAST_DECAY = 0.3
HY_SLOW_DECAY = 1.5
HY_TARGET = 1e-2
RET_WIDTH = 512
RET_HEADS = 4
RET_QK_DIM = 64
RET_V_DIM = 128
ROPE_BASE = 10000.0
D_FF = 2816
EPS = 1e-6
HY_COLS = (HY_ORDER + 1) * HY_WIDTH
Q_OFF = HY_COLS
K_OFF = Q_OFF + RET_HEADS * RET_QK_DIM
V_OFF = K_OFF + RET_HEADS * RET_QK_DIM
G_OFF = V_OFF + RET_WIDTH
K_SCALE = RET_QK_DIM ** -0.5

MOD_ROWS = 16
TOK_TILE = 512
IN_TILE = 1024
IN_PIECE = 256
SUBLANES = 8
LANES = 128
HALO = 16
CONV_BLOCK = 512
N_CONV_BLOCKS = SEQ // CONV_BLOCK
FREQ_ROWS = 16
FREQ_LANES = 128
DFT_ROWS = 256
CTX_PER_STEP = 4
RET_CHUNK = 256
RET_HEADS_PER_STEP = 4
FFN_COLS = 256
VMEM_LIMIT = 56 * 1024 * 1024
HYENA_VMEM_LIMIT = 62 * 1024 * 1024

_NT = (((1,), (1,)), ((), ()))


def _dot(a, b):
    return jnp.dot(a, b, preferred_element_type=F32)


def _dot_hi(a, b):
    return jnp.dot(a, b, preferred_element_type=F32, precision=lax.Precision.HIGHEST)


def _silu(x):
    return x * (1.0 / (1.0 + jnp.exp(-x)))


def _norm_mod(x, gain, shift, scale):
    y = x * lax.rsqrt(jnp.mean(x * x, axis=-1, keepdims=True) + EPS)
    return (y * gain) * (1.0 + scale) + shift


def _resident(shape):
    nd = len(shape)
    return pl.BlockSpec(shape, lambda *_: (0,) * nd, pipeline_mode=pl.Buffered(1))


def _resident_cols(rows, col0, width):
    assert col0 % width == 0
    return pl.BlockSpec((rows, width), lambda *_: (0, col0 // width), pipeline_mode=pl.Buffered(1))


def _resident_rows(row0, height, cols):
    assert row0 % height == 0
    return pl.BlockSpec((height, cols), lambda *_: (row0 // height, 0), pipeline_mode=pl.Buffered(1))


@functools.lru_cache(maxsize=None)
def _rope_tables():
    pos = np.arange(SEQ)
    row = (pos // GRID_W).astype(np.float64)
    col = (pos % GRID_W).astype(np.float64)
    quarter = RET_QK_DIM // 4
    inv_freq = ROPE_BASE ** (-np.arange(quarter, dtype=np.float64) / quarter)
    ang = np.concatenate([row[:, None] * inv_freq, col[:, None] * inv_freq], axis=-1)
    cos, sin = np.cos(ang), np.sin(ang)
    cos_h = np.concatenate([cos, cos], axis=-1)
    sin_h = np.concatenate([-sin, sin], axis=-1)
    cos_t = np.tile(cos_h, (1, RET_HEADS))
    sin_t = np.tile(sin_h, (1, RET_HEADS))
    return (cos_t.astype(np.float32), sin_t.astype(np.float32),
            np.ascontiguousarray((cos_t * K_SCALE).T).astype(np.float32),
            np.ascontiguousarray((sin_t * K_SCALE).T).astype(np.float32))


@functools.lru_cache(maxsize=None)
def _filter_features():
    lag = np.abs(np.arange(2 * SEQ) - SEQ).astype(np.float64)
    t = lag / (SEQ - 1)
    bands = np.linspace(1e-4, HY_EMB_BANDS - 1, HY_EMB_BANDS)
    ang = 2.0 * math.pi * lag[:, None] * bands[None, :] / SEQ
    z = np.concatenate([t[:, None], np.cos(ang), -np.sin(ang)], axis=-1)
    zp = np.zeros((2 * SEQ, HY_FILTER_WIDTH), np.float64)
    zp[:, :z.shape[1]] = z
    max_decay = math.log(HY_TARGET) / HY_FAST_DECAY
    min_decay = math.log(HY_TARGET) / HY_SLOW_DECAY
    absdelta = np.abs(np.linspace(min_decay, max_decay, HY_WIDTH))[:, None]
    return (np.ascontiguousarray(zp.T).astype(np.float32), t[None, :].astype(np.float32),
            absdelta.astype(np.float32))


@functools.lru_cache(maxsize=None)
def _dft_matrices():
    p = CONV_BLOCK
    n = 2 * p
    f = np.arange(p, dtype=np.float64)[:, None] + 0.5
    t = np.arange(n, dtype=np.float64)[None, :]
    theta = 2.0 * math.pi * f * t / n
    fwd = np.concatenate([np.cos(theta), -np.sin(theta)], axis=0)
    th_out = theta[:, p:].T
    inv = np.concatenate([np.cos(th_out), -np.sin(th_out)], axis=1) / p
    return fwd.astype(np.float32), inv.astype(np.float32)


def _mod_kernel(c_ref, w_ref, b_ref, o_ref):
    s = _silu(c_ref[...]).astype(BF16)
    o_ref[...] = _dot(s, w_ref[...].astype(BF16)) + b_ref[...]


def _mod_call(cc, w_mod, b_mod):
    ncol = 6 * D_MODEL
    blk = ncol // 4
    return pl.pallas_call(
        _mod_kernel,
        grid=(ncol // blk,),
        in_specs=[pl.BlockSpec((MOD_ROWS, D_MODEL), lambda j: (0, 0)),
                  pl.BlockSpec((D_MODEL, blk), lambda j: (0, j)),
                  pl.BlockSpec((1, blk), lambda j: (0, j))],
        out_specs=pl.BlockSpec((MOD_ROWS, blk), lambda j: (0, j)),
        out_shape=jax.ShapeDtypeStruct((MOD_ROWS, ncol), F32),
        compiler_params=pltpu.CompilerParams(vmem_limit_bytes=VMEM_LIMIT),
        name="mod",
    )(cc, w_mod, b_mod)


def _ctx_kernel(lg_ref, x_ref, sh_ref, sc_ref, n1_ref, wkt_ref, wv_ref, s_ref):
    pos = lax.broadcasted_iota(jnp.int32, (1, CTX_LEN), 1).astype(F32)
    w_f = [jnp.exp(lg_ref[0, hh] * (CTX_LEN - 1.0 - pos)) for hh in range(RET_HEADS)]
    w_b = [jnp.exp(lg_ref[1, hh] * pos) for hh in range(RET_HEADS)]
    for bb in range(CTX_PER_STEP):
        h = _norm_mod(x_ref[bb], n1_ref[...], sh_ref[0], sc_ref[0]).astype(BF16)
        kt = lax.dot_general(wkt_ref[...], h, _NT, preferred_element_type=F32) * K_SCALE
        v = _dot(h, wv_ref[...])
        for hh in range(RET_HEADS):
            kth = kt[hh * RET_QK_DIM:(hh + 1) * RET_QK_DIM, :]
            vh = v[:, hh * RET_V_DIM:(hh + 1) * RET_V_DIM].astype(BF16)
            s_ref[bb, hh, 0:RET_QK_DIM, :] = _dot((kth * w_f[hh]).astype(BF16), vh)
            s_ref[bb, hh, RET_QK_DIM:2 * RET_QK_DIM, :] = _dot((kth * w_b[hh]).astype(BF16), vh)


def _ctx_call(lg, ctx, mod3, norm1, w_kt, w_v):
    return pl.pallas_call(
        _ctx_kernel,
        grid=(BATCH // CTX_PER_STEP,),
        in_specs=[pl.BlockSpec(memory_space=pltpu.SMEM),
                  pl.BlockSpec((CTX_PER_STEP, CTX_LEN, D_MODEL), lambda b: (b, 0, 0)),
                  pl.BlockSpec((1, 1, D_MODEL), lambda b: (BATCH, 0, 0)),
                  pl.BlockSpec((1, 1, D_MODEL), lambda b: (BATCH, 0, 1)),
                  _resident((1, D_MODEL)),
                  _resident((RET_HEADS * RET_QK_DIM, D_MODEL)),
                  _resident_cols(D_MODEL, V_OFF, RET_WIDTH)],
        out_specs=pl.BlockSpec((CTX_PER_STEP, RET_HEADS, 2 * RET_QK_DIM, RET_V_DIM), lambda b: (b, 0, 0, 0)),
        out_shape=jax.ShapeDtypeStruct((BATCH, RET_HEADS, 2 * RET_QK_DIM, RET_V_DIM), F32),
        compiler_params=pltpu.CompilerParams(vmem_limit_bytes=VMEM_LIMIT),
        name="ctx",
    )(lg, ctx, mod3, mod3, norm1, w_kt, w_v)


def _swap_halves(x, axis):
    n = x.shape[axis]
    half = RET_QK_DIM // 2
    idx = lax.broadcasted_iota(jnp.int32, x.shape, axis)
    first = (idx & (RET_QK_DIM - 1)) < half
    return jnp.where(first, pltpu.roll(x, n - half, axis), pltpu.roll(x, half, axis))


def _inproj_kernel(x_ref, sh_ref, sc_ref, n1_ref, why_ref, wq_ref, wv_ref, wg_ref, wkt_ref,
                   cq_ref, sq_ref, ck_ref, sk_ref,
                   zhy_ref, q_ref, kt_ref, v_ref, g_ref):
    pieces = [slice(r, r + IN_PIECE) for r in range(0, IN_TILE, IN_PIECE)]
    hbs = [_norm_mod(x_ref[0, rows, :], n1_ref[...], sh_ref[0], sc_ref[0]).astype(BF16) for rows in pieces]
    for rows, hb in zip(pieces, hbs):
        zhy_ref[0, rows, :] = _dot(hb, why_ref[...]).astype(BF16)
        v_ref[0, rows, :] = _dot(hb, wv_ref[...]).astype(BF16)
        g_ref[0, rows, :] = _dot(hb, wg_ref[...]).astype(BF16)
        q = _dot(hb, wq_ref[...])
        q = q * cq_ref[rows, :] + _swap_halves(q, 1) * sq_ref[rows, :]
        for hh in range(RET_HEADS):
            q_ref[0, hh, rows, :] = q[:, hh * RET_QK_DIM:(hh + 1) * RET_QK_DIM].astype(BF16)
        kt = lax.dot_general(wkt_ref[...], hb, _NT, preferred_element_type=F32)
        kt = kt * ck_ref[:, rows] + _swap_halves(kt, 0) * sk_ref[:, rows]
        kt_ref[0, :, rows] = kt.astype(BF16)


def _inproj_call(x, mod3, norm1, w_in, w_kt, rope):
    cq, sq, ck, sk = rope
    t = IN_TILE
    qk = RET_HEADS * RET_QK_DIM
    return pl.pallas_call(
        _inproj_kernel,
        grid=(BATCH, SEQ // t),
        in_specs=[pl.BlockSpec((1, t, D_MODEL), lambda b, i: (b, i, 0)),
                  pl.BlockSpec((1, 1, D_MODEL), lambda b, i: (b, 0, 0)),
                  pl.BlockSpec((1, 1, D_MODEL), lambda b, i: (b, 0, 1)),
                  _resident((1, D_MODEL)),
                  _resident_cols(D_MODEL, 0, HY_COLS),
                  _resident_cols(D_MODEL, Q_OFF, qk),
                  _resident_cols(D_MODEL, V_OFF, RET_WIDTH),
                  _resident_cols(D_MODEL, G_OFF, RET_WIDTH),
                  _resident((qk, D_MODEL)),
                  pl.BlockSpec((t, qk), lambda b, i: (i, 0)),
                  pl.BlockSpec((t, qk), lambda b, i: (i, 0)),
                  pl.BlockSpec((qk, t), lambda b, i: (0, i)),
                  pl.BlockSpec((qk, t), lambda b, i: (0, i))],
        out_specs=[pl.BlockSpec((1, t, HY_COLS), lambda b, i: (b, i, 0)),
                   pl.BlockSpec((1, RET_HEADS, t, RET_QK_DIM), lambda b, i: (b, 0, i, 0)),
                   pl.BlockSpec((1, qk, t), lambda b, i: (b, 0, i)),
                   pl.BlockSpec((1, t, RET_WIDTH), lambda b, i: (b, i, 0)),
                   pl.BlockSpec((1, t, RET_WIDTH), lambda b, i: (b, i, 0))],
        out_shape=[jax.ShapeDtypeStruct((BATCH, SEQ, HY_COLS), BF16),
                   jax.ShapeDtypeStruct((BATCH, RET_HEADS, SEQ, RET_QK_DIM), BF16),
                   jax.ShapeDtypeStruct((BATCH, qk, SEQ), BF16),
                   jax.ShapeDtypeStruct((BATCH, SEQ, RET_WIDTH), BF16),
                   jax.ShapeDtypeStruct((BATCH, SEQ, RET_WIDTH), BF16)],
        compiler_params=pltpu.CompilerParams(vmem_limit_bytes=VMEM_LIMIT),
        name="inproj",
    )(x, mod3, mod3, norm1, w_in, w_in, w_in, w_in, w_kt, cq, sq, ck, sk)


def _filter_mlp_kernel(zt_ref, t_ref, w1t_ref, b1_ref, f1_ref, w2t_ref, b2_ref, f2_ref, w3t_ref, adel_ref,
                       kt_ref):
    hid = jnp.sin(f1_ref[...] * (_dot_hi(w1t_ref[...], zt_ref[...]) + b1_ref[...]))
    hid = jnp.sin(f2_ref[...] * (_dot_hi(w2t_ref[...], hid) + b2_ref[...])).astype(BF16)
    c = HY_WIDTH
    for half in range(2):
        lags = slice(half * SEQ, (half + 1) * SEQ)
        window = jnp.exp(-adel_ref[...] * t_ref[:, lags])
        direction = 1 - half
        for o in range(HY_ORDER):
            r0 = (direction * HY_ORDER + o) * c
            w3 = w3t_ref[r0:r0 + c, :].astype(BF16)
            kt_ref[o, :, lags] = (_dot(w3, hid[:, lags]) * window).astype(BF16)


def _filter_mlp_call(zt, t_row, w1t, b1, f1, w2t, b2, f2, w3t, absdelta):
    fw = HY_FILTER_WIDTH
    n_out = 2 * HY_ORDER * HY_WIDTH
    return pl.pallas_call(
        _filter_mlp_kernel,
        grid=(1,),
        in_specs=[_resident((fw, 2 * SEQ)), _resident((1, 2 * SEQ)),
                  _resident((fw, fw)), _resident((fw, 1)), _resident((fw, 1)),
                  _resident((fw, fw)), _resident((fw, 1)), _resident((fw, 1)),
                  _resident((n_out, fw)), _resident((HY_WIDTH, 1))],
        out_specs=pl.BlockSpec((HY_ORDER, HY_WIDTH, 2 * SEQ), lambda i: (0, 0, 0)),
        out_shape=jax.ShapeDtypeStruct((HY_ORDER, HY_WIDTH, 2 * SEQ), BF16),
        compiler_params=pltpu.CompilerParams(vmem_limit_bytes=VMEM_LIMIT),
        name="filter_mlp",
    )(zt, t_row, w1t, b1, f1, w2t, b2, f2, w3t, absdelta)


_FILTER_PLANES = (
    {0: 1},
    {-1: 1, 0: -1},
    {1: 1, 0: -1},
    {-2: 1, 0: -1},
    {-3: 1, -1: -1, -2: -1, 0: 1},
    {-1: 1, 1: -1, -2: -1, 0: 1},
    {2: 1, 0: -1},
    {1: 1, -1: -1, 2: -1, 0: 1},
    {3: 1, 1: -1, 2: -1, 0: 1},
)
DFT_BLOCKS_PER_STEP = 2
PLANES_PER_STEP = 3


def _filter_dft_kernel(kt_ref, fwd_ref, h_ref, t_scr):
    p = CONV_BLOCK
    n_dft_steps = 2 * N_CONV_BLOCKS // DFT_BLOCKS_PER_STEP
    step = pl.program_id(1)

    for s in range(n_dft_steps):
        @pl.when(step == s)
        def _(s=s):
            for k in range(DFT_BLOCKS_PER_STEP):
                t_scr[s * DFT_BLOCKS_PER_STEP + k] = lax.dot_general(
                    fwd_ref[...], kt_ref[0, :, k * p:(k + 1) * p], _NT, preferred_element_type=F32)
            for k in range(PLANES_PER_STEP):
                h_ref[0, k] = t_scr[s * DFT_BLOCKS_PER_STEP]

    row = lax.broadcasted_iota(jnp.int32, (p, 1), 0)
    sign = (1 - 2 * (row & 1)).astype(F32)
    re, im = slice(0, p), slice(p, 2 * p)

    def plane(k, coefs):
        def comb(shift, rows):
            acc = None
            for d, c in coefs.items():
                term = t_scr[d + N_CONV_BLOCKS - 1 + shift, rows, :]
                if acc is None:
                    acc = term
                else:
                    acc = acc + term if c > 0 else acc - term
            return acc

        h_ref[0, k, re, :] = comb(0, re) + sign * comb(1, im)
        h_ref[0, k, im, :] = comb(0, im) - sign * comb(1, re)

    for s in range(len(_FILTER_PLANES) // PLANES_PER_STEP):
        @pl.when(step == n_dft_steps + s)
        def _(s=s):
            for k in range(PLANES_PER_STEP):
                plane(k, _FILTER_PLANES[s * PLANES_PER_STEP + k])


def _filter_dft_call(kt, fwd_u):
    p = CONV_BLOCK
    n_blk = 2 * N_CONV_BLOCKS
    n_planes = len(_FILTER_PLANES)
    n_dft_steps = n_blk // DFT_BLOCKS_PER_STEP
    return pl.pallas_call(
        _filter_dft_kernel,
        grid=(HY_ORDER, n_dft_steps + n_planes // PLANES_PER_STEP),
        in_specs=[pl.BlockSpec((1, HY_WIDTH, DFT_BLOCKS_PER_STEP * p),
                               lambda o, s: (o, 0, jnp.minimum(s, n_dft_steps - 1))),
                  _resident((2 * p, p))],
        out_specs=pl.BlockSpec((1, PLANES_PER_STEP, 2 * p, HY_WIDTH),
                               lambda o, s: (o, jnp.maximum(s - n_dft_steps, 0), 0, 0)),
        out_shape=jax.ShapeDtypeStruct((HY_ORDER, n_planes, 2 * p, HY_WIDTH), F32),
        scratch_shapes=[pltpu.VMEM((n_blk, 2 * p, HY_WIDTH), F32)],
        compiler_params=pltpu.CompilerParams(vmem_limit_bytes=VMEM_LIMIT),
        name="filter_dft",
    )(kt, fwd_u)


def _conv3_rows(ref, j, n_blocks, rows, w, b):
    main = ref[0, j * rows:(j + 1) * rows, :].astype(F32)
    cols = main.shape[1]
    zeros = jnp.zeros((HALO, cols), F32)
    prev = ref[0, j * rows - HALO:j * rows, :].astype(F32) if j > 0 else zeros
    nxt = ref[0, (j + 1) * rows:(j + 1) * rows + HALO, :].astype(F32) if j < n_blocks - 1 else zeros
    ext = jnp.concatenate([prev, main, nxt], axis=0)
    n = rows + 2 * HALO
    before = pltpu.roll(ext, 1, 0)[HALO:HALO + rows]
    after = pltpu.roll(ext, n - 1, 0)[HALO:HALO + rows]
    return before * w[0:1] + main * w[1:2] + after * w[2:3] + b


def _hyena_kernel(u_ref, zg_ref, cwu_ref, cbu_ref, cwg_ref, cbg_ref, skip_ref, h_ref, fwd_ref, inv_ref,
                  o_ref, uf_scr, ub_scr, y_scr, *, conv_u):
    p = CONV_BLOCK
    nb = N_CONV_BLOCKS

    def prepare(j):
        if conv_u:
            ub_scr[j * p:(j + 1) * p, :] = _conv3_rows(u_ref, j, nb, p, cwu_ref[...], cbu_ref[...]).astype(BF16)

    def u_block(j):
        return ub_scr[j * p:(j + 1) * p, :] if conv_u else u_ref[0, j * p:(j + 1) * p, :]

    prepare(0)
    for j in range(nb):
        if j + 1 < nb:
            prepare(j + 1)
        for r in range(2 * p // DFT_ROWS):
            rows = slice(r * DFT_ROWS, (r + 1) * DFT_ROWS)
            uf_scr[j, rows, :] = _dot(fwd_ref[rows, :], u_block(j))

    def cadd(a, b):
        return a[0] + b[0], a[1] + b[1]

    def cmul(m, x):
        return m[0] * x[0] - m[1] * x[1], m[0] * x[1] + m[1] * x[0]

    def toeplitz2(k0, x0, x1, re, im, ln):
        m0, mu, ml = ((h_ref[0, k0 + t, re, ln], h_ref[0, k0 + t, im, ln]) for t in range(3))
        p1 = cmul(m0, cadd(x0, x1))
        return cadd(p1, cmul(mu, x1)), cadd(p1, cmul(ml, x0))

    assert nb == 4
    for r in range(p // FREQ_ROWS):
        re = slice(r * FREQ_ROWS, (r + 1) * FREQ_ROWS)
        im = slice(p + r * FREQ_ROWS, p + (r + 1) * FREQ_ROWS)
        for cb in range(HY_WIDTH // FREQ_LANES):
            ln = slice(cb * FREQ_LANES, (cb + 1) * FREQ_LANES)
            u = [(uf_scr[j, re, ln], uf_scr[j, im, ln]) for j in range(nb)]
            d0, d1 = toeplitz2(0, cadd(u[0], u[2]), cadd(u[1], u[3]), re, im, ln)
            b0, b1 = toeplitz2(3, u[2], u[3], re, im, ln)
            c0, c1 = toeplitz2(6, u[0], u[1], re, im, ln)
            for i, yi in enumerate((cadd(d0, b0), cadd(d1, b1), cadd(d0, c0), cadd(d1, c1))):
                y_scr[i, re, ln] = yi[0].astype(BF16)
                y_scr[i, im, ln] = yi[1].astype(BF16)

    for i in range(nb):
        gate = _conv3_rows(zg_ref, i, nb, p, cwg_ref[...], cbg_ref[...])
        y = _dot(inv_ref[...], y_scr[i])
        o_ref[0, i * p:(i + 1) * p, :] = (
            gate * (y + u_block(i).astype(F32) * skip_ref[...])).astype(BF16)


def _hyena_call(u, u_col, zhy, gate_col, conv_w, conv_b, skip, g_all, order, fwd_u, inv, conv_u):
    p = CONV_BLOCK
    c = HY_WIDTH
    ucol = u_col if conv_u else 0
    cwu = conv_w[:, ucol * c:(ucol + 1) * c]
    cbu = conv_b[:, ucol * c:(ucol + 1) * c]
    cwg = conv_w[:, gate_col * c:(gate_col + 1) * c]
    cbg = conv_b[:, gate_col * c:(gate_col + 1) * c]
    return pl.pallas_call(
        functools.partial(_hyena_kernel, conv_u=conv_u),
        grid=(BATCH,),
        in_specs=[pl.BlockSpec((1, SEQ, c), lambda b: (b, 0, u_col)),
                  pl.BlockSpec((1, SEQ, c), lambda b: (b, 0, gate_col)),
                  _resident((3, c)), _resident((1, c)), _resident((3, c)), _resident((1, c)),
                  _resident((1, c)),
                  pl.BlockSpec((1, len(_FILTER_PLANES), 2 * p, c), lambda b: (order, 0, 0, 0),
                               pipeline_mode=pl.Buffered(1)),
                  _resident((2 * p, p)),
                  _resident((p, 2 * p))],
        out_specs=pl.BlockSpec((1, SEQ, c), lambda b: (b, 0, 0)),
        out_shape=jax.ShapeDtypeStruct((BATCH, SEQ, c), BF16),
        scratch_shapes=[pltpu.VMEM((N_CONV_BLOCKS, 2 * p, c), F32),
                        pltpu.VMEM((SEQ, c) if conv_u else (SUBLANES * 2, LANES), BF16),
                        pltpu.VMEM((N_CONV_BLOCKS, 2 * p, c), BF16)],
        compiler_params=pltpu.CompilerParams(vmem_limit_bytes=HYENA_VMEM_LIMIT),
        name="hyena%d" % order,
    )(u, zhy, cwu, cbu, cwg, cbg, skip, g_all, fwd_u, inv)


def _ret_kernel(lg_ref, q_ref, kt_ref, v_ref, g_ref, s_ref, o_ref, b_scr, decay_scr, qw_scr):
    c = RET_CHUNK
    nc = SEQ // c
    dk = RET_QK_DIM
    dv = RET_V_DIM
    heads = range(RET_HEADS_PER_STEP)
    lg_f = [lg_ref[0, pl.program_id(0) * RET_HEADS_PER_STEP + hh] for hh in heads]
    lg_b = [lg_ref[1, pl.program_id(0) * RET_HEADS_PER_STEP + hh] for hh in heads]

    @pl.when(pl.program_id(1) == 0)
    def _():
        ii = lax.broadcasted_iota(jnp.int32, (c, c), 0).astype(F32)
        jj = lax.broadcasted_iota(jnp.int32, (c, c), 1).astype(F32)
        dif = ii - jj
        pos_q = lax.broadcasted_iota(jnp.int32, (c, dk), 0).astype(F32)
        for hh in heads:
            decay_scr[hh] = jnp.where(dif >= 0.0, jnp.exp(lg_f[hh] * jnp.maximum(dif, 0.0)),
                                      jnp.exp(lg_b[hh] * jnp.maximum(-dif, 0.0)))
            qw_scr[hh, 0] = jnp.exp(lg_f[hh] * (pos_q + 1.0))
            qw_scr[hh, 1] = jnp.exp(lg_b[hh] * (c - pos_q))

    pos_r = lax.broadcasted_iota(jnp.int32, (1, c), 1).astype(F32)
    ones = jnp.ones((1, dv), F32)
    kw_f = [jnp.exp(lg_f[hh] * (c - 1.0 - pos_r)) for hh in heads]
    kw_b = [jnp.exp(lg_b[hh] * pos_r) for hh in heads]
    dec_f = [jnp.exp(lg_f[hh] * float(c) * ones) for hh in heads]
    dec_b = [jnp.exp(lg_b[hh] * float(c) * ones) for hh in heads]

    def kt_chunk(hh, n):
        return kt_ref[0, hh * dk:(hh + 1) * dk, n * c:(n + 1) * c]

    def v_chunk(hh, n):
        return v_ref[0, n * c:(n + 1) * c, hh * dv:(hh + 1) * dv]

    state = [s_ref[0, hh, dk:2 * dk, :] for hh in heads]
    for hh in heads:
        b_scr[hh, nc - 1] = state[hh]
    for n in range(nc - 1, 0, -1):
        for hh in heads:
            ktn = (kt_chunk(hh, n).astype(F32) * kw_b[hh]).astype(BF16)
            state[hh] = state[hh] * dec_b[hh] + _dot(ktn, v_chunk(hh, n))
            b_scr[hh, n - 1] = state[hh]

    state = [s_ref[0, hh, 0:dk, :] for hh in heads]
    for n in range(nc):
        for hh in heads:
            qn = q_ref[0, hh, n * c:(n + 1) * c, :]
            ktn = kt_chunk(hh, n)
            vn = v_chunk(hh, n)
            scores = (_dot(qn, ktn) * decay_scr[hh]).astype(BF16)
            qf = qn.astype(F32)
            o = _dot(scores, vn)
            o = o + _dot((qf * qw_scr[hh, 0]).astype(BF16), state[hh].astype(BF16))
            o = o + _dot((qf * qw_scr[hh, 1]).astype(BF16), b_scr[hh, n].astype(BF16))
            state[hh] = state[hh] * dec_f[hh] + _dot((ktn.astype(F32) * kw_f[hh]).astype(BF16), vn)
            o = o * lax.rsqrt(jnp.mean(o * o, axis=-1, keepdims=True) + EPS)
            gate = g_ref[0, n * c:(n + 1) * c, hh * dv:(hh + 1) * dv].astype(F32)
            o_ref[0, n * c:(n + 1) * c, hh * dv:(hh + 1) * dv] = (_silu(gate) * o).astype(BF16)


def _ret_call(lg, q, kt, v, g, s):
    dk, dv = RET_QK_DIM, RET_V_DIM
    hp = RET_HEADS_PER_STEP
    return pl.pallas_call(
        _ret_kernel,
        grid=(RET_HEADS // hp, BATCH),
        in_specs=[pl.BlockSpec(memory_space=pltpu.SMEM),
                  pl.BlockSpec((1, hp, SEQ, dk), lambda h, b: (b, h, 0, 0)),
                  pl.BlockSpec((1, hp * dk, SEQ), lambda h, b: (b, h, 0)),
                  pl.BlockSpec((1, SEQ, hp * dv), lambda h, b: (b, 0, h)),
                  pl.BlockSpec((1, SEQ, hp * dv), lambda h, b: (b, 0, h)),
                  pl.BlockSpec((1, hp, 2 * dk, dv), lambda h, b: (b, h, 0, 0))],
        out_specs=pl.BlockSpec((1, SEQ, hp * dv), lambda h, b: (b, 0, h)),
        out_shape=jax.ShapeDtypeStruct((BATCH, SEQ, RET_WIDTH), BF16),
        scratch_shapes=[pltpu.VMEM((hp, SEQ // RET_CHUNK, dk, dv), F32),
                        pltpu.VMEM((hp, RET_CHUNK, RET_CHUNK), F32),
                        pltpu.VMEM((hp, 2, RET_CHUNK, dk), F32)],
        compiler_params=pltpu.CompilerParams(vmem_limit_bytes=VMEM_LIMIT),
        name="ret",
    )(lg, q, kt, v, g, s)


def _ffn_kernel(x_ref, xp_ref, xn_ref, yh_ref, yhp_ref, yhn_ref, yr_ref, yrp_ref, yrn_ref,
                g1_ref, sh_ref, sc_ref, g2_ref, n2_ref, nf_ref,
                woh_ref, wor_ref, wup_ref, cw_ref, cb_ref, wdn_ref,
                o_ref, hb_scr, x1_scr, av_scr, ag_scr, act_scr):
    t = TOK_TILE
    i = pl.program_id(1)
    nt = pl.num_programs(1)
    th = t // 2
    tile_halves = (slice(0, th), slice(th, t))

    def mixed(xr, yh, yr, rows=slice(None)):
        return xr[0, rows, :] + g1_ref[0] * (_dot(yh[0, rows, :], woh_ref[...]) + _dot(yr[0, rows, :], wor_ref[...]))

    def hidden(x1):
        return _norm_mod(x1, n2_ref[...], sh_ref[0], sc_ref[0])

    half = (t + 2 * HALO) // 2
    up_halves = (slice(0, half), slice(half, 2 * half))

    def up(slot, cblk, rows):
        c0 = cblk * FFN_COLS
        hb = hb_scr[rows, :]
        av_scr[slot, rows, :] = _dot(hb, wup_ref[:, c0:c0 + FFN_COLS])
        ag_scr[slot, rows, :] = _dot(hb, wup_ref[:, D_FF + c0:D_FF + c0 + FFN_COLS])

    for rows in tile_halves:
        x1_scr[rows, :] = mixed(x_ref, yh_ref, yr_ref, rows)
    hp = hidden(mixed(xp_ref, yhp_ref, yrp_ref))
    hn = hidden(mixed(xn_ref, yhn_ref, yrn_ref))
    hb_scr[0:HALO, :] = jnp.where(i > 0, hp, 0.0).astype(BF16)
    hb_scr[HALO + t:2 * HALO + t, :] = jnp.where(i < nt - 1, hn, 0.0).astype(BF16)
    hb_scr[HALO:HALO + th, :] = hidden(x1_scr[tile_halves[0], :]).astype(BF16)
    up(0, 0, up_halves[0])
    hb_scr[HALO + th:HALO + t, :] = hidden(x1_scr[tile_halves[1], :]).astype(BF16)
    up(0, 0, up_halves[1])

    def conv(scr, slot, col):
        w = cw_ref[:, col:col + FFN_COLS]
        return (scr[slot, HALO - 1:HALO - 1 + t, :] * w[0:1] + scr[slot, HALO:HALO + t, :] * w[1:2]
                + scr[slot, HALO + 1:HALO + 1 + t, :] * w[2:3] + cb_ref[:, col:col + FFN_COLS])

    n_blk = D_FF // FFN_COLS
    for cblk in range(n_blk):
        c0 = cblk * FFN_COLS
        slot = cblk % 2
        if cblk + 1 < n_blk:
            for rows in up_halves:
                up(1 - slot, cblk + 1, rows)
        act_scr[:, c0:c0 + FFN_COLS] = (
            _silu(conv(ag_scr, slot, D_FF + c0)) * conv(av_scr, slot, c0)).astype(BF16)

    down = [_dot(act_scr[rows, :], wdn_ref[...]) for rows in tile_halves]
    for rows, ffn in zip(tile_halves, down):
        x2 = x1_scr[rows, :] + g2_ref[0] * ffn
        o_ref[0, rows, :] = x2 * lax.rsqrt(jnp.mean(x2 * x2, axis=-1, keepdims=True) + EPS) * nf_ref[...]


def _ffn_call(x, y_hy, y_ret, mod3, norm2, norm_f, w_oh, w_or, w_up, conv_w, conv_b, w_dn):
    t = TOK_TILE
    r = t // HALO
    last = SEQ // HALO - 1

    def main(width):
        return pl.BlockSpec((1, t, width), lambda b, i: (b, i, 0))

    def prev(width):
        return pl.BlockSpec((1, HALO, width), lambda b, i: (b, jnp.maximum(i * r - 1, 0), 0))

    def nxt(width):
        return pl.BlockSpec((1, HALO, width), lambda b, i: (b, jnp.minimum((i + 1) * r, last), 0))

    def modrow(k):
        return pl.BlockSpec((1, 1, D_MODEL), lambda b, i: (b, 0, k))

    return pl.pallas_call(
        _ffn_kernel,
        grid=(BATCH, SEQ // t),
        in_specs=[main(D_MODEL), prev(D_MODEL), nxt(D_MODEL),
                  main(HY_WIDTH), prev(HY_WIDTH), nxt(HY_WIDTH),
                  main(RET_WIDTH), prev(RET_WIDTH), nxt(RET_WIDTH),
                  modrow(2), modrow(3), modrow(4), modrow(5),
                  _resident((1, D_MODEL)), _resident((1, D_MODEL)),
                  _resident_rows(0, HY_WIDTH, D_MODEL), _resident_rows(HY_WIDTH, RET_WIDTH, D_MODEL),
                  _resident((D_MODEL, 2 * D_FF)),
                  _resident((3, 2 * D_FF)), _resident((1, 2 * D_FF)),
                  _resident((D_FF, D_MODEL))],
        out_specs=pl.BlockSpec((1, t, D_MODEL), lambda b, i: (b, i, 0)),
        out_shape=jax.ShapeDtypeStruct((BATCH, SEQ, D_MODEL), F32),
        scratch_shapes=[pltpu.VMEM((t + 2 * HALO, D_MODEL), BF16),
                        pltpu.VMEM((t, D_MODEL), F32),
                        pltpu.VMEM((2, t + 2 * HALO, FFN_COLS), F32),
                        pltpu.VMEM((2, t + 2 * HALO, FFN_COLS), F32),
                        pltpu.VMEM((t, D_FF), BF16)],
        compiler_params=pltpu.CompilerParams(vmem_limit_bytes=VMEM_LIMIT),
        name="ffn",
    )(x, x, x, y_hy, y_hy, y_hy, y_ret, y_ret, y_ret, mod3, mod3, mod3, mod3,
      norm2, norm_f, w_oh, w_or, w_up, conv_w, conv_b, w_dn)


def kernel(x, c, ctx, c_ctx, w_mod, b_mod, norm1, w_in, hy_conv_w, hy_conv_b, hy_w1, hy_b1, hy_f1,
           hy_w2, hy_b2, hy_f2, hy_w3, hy_bias, ret_logit_f, ret_logit_b, w_out, norm2,
           ffn_w_up, ffn_conv_w, ffn_conv_b, ffn_w_down, norm_f):
    layer = 0
    rope = tuple(jnp.asarray(a) for a in _rope_tables())
    zt, t_row, absdelta = (jnp.asarray(a) for a in _filter_features())
    fwd_np, inv_np = _dft_matrices()
    fwd_u = jnp.asarray(fwd_np[:, :CONV_BLOCK]).astype(BF16)
    inv = jnp.asarray(inv_np).astype(BF16)

    w_in_b = w_in[layer].astype(BF16)
    w_kt = w_in_b[:, K_OFF:V_OFF].T
    w_out_b = w_out[layer].astype(BF16)
    w_up = ffn_w_up[layer].astype(BF16)
    w_dn = ffn_w_down[layer].astype(BF16)
    row = lambda a: a.reshape(1, -1)
    col = lambda a: a.reshape(-1, 1)
    w1p = jnp.pad(hy_w1[layer], ((0, HY_FILTER_WIDTH - hy_w1.shape[1]), (0, 0)))
    lg = jnp.stack([jax.nn.log_sigmoid(ret_logit_f[layer].astype(F32)),
                    jax.nn.log_sigmoid(ret_logit_b[layer].astype(F32))])

    cc = jnp.concatenate([c, c_ctx[None, :], jnp.zeros((MOD_ROWS - BATCH - 1, D_MODEL), F32)], axis=0)
    mod = _mod_call(cc, w_mod[layer], row(b_mod[layer]))
    mod3 = mod.reshape(MOD_ROWS, 1, 6 * D_MODEL)
    norm1_r = row(norm1[layer])

    s_ctx = _ctx_call(lg, ctx, mod3, norm1_r, w_kt, w_in_b)
    zhy, q, kt, v, g = _inproj_call(x, mod3, norm1_r, w_in_b, w_kt, rope)

    k_two_sided = _filter_mlp_call(zt, t_row, w1p.T, col(hy_b1[layer]), col(hy_f1[layer]), hy_w2[layer].T,
                                   col(hy_b2[layer]), col(hy_f2[layer]), hy_w3[layer].T, absdelta)
    g_all = _filter_dft_call(k_two_sided, fwd_u)
    conv_w, conv_b = hy_conv_w[layer], row(hy_conv_b[layer])
    y1 = _hyena_call(zhy, 0, zhy, 1, conv_w, conv_b, hy_bias[layer][0:1], g_all, 0, fwd_u, inv, True)
    y_hy = _hyena_call(y1, 0, zhy, 2, conv_w, conv_b, hy_bias[layer][1:2], g_all, 1, fwd_u, inv, False)

    y_ret = _ret_call(lg, q, kt, v, g, s_ctx)

    return _ffn_call(x, y_hy, y_ret, mod3, row(norm2[layer]), row(norm_f), w_out_b, w_out_b, w_up,
                     ffn_conv_w[layer], row(ffn_conv_b[layer]), w_dn)
```

```python
import functools
import math

import numpy as np
import jax
import jax.numpy as jnp
from jax import lax
from jax.experimental import pallas as pl
from jax.experimental.pallas import tpu as pltpu

F32 = jnp.float32
BF16 = jnp.bfloat16

D_MODEL = 1024
BATCH = 8
SEQ = 2048
CTX_LEN = 256
GRID_W = 64
HY_WIDTH = 512
HY_ORDER = 2
HY_EMB_BANDS = 16
HY_FILTER_WIDTH = 64
HY_FAST_DECAY = 0.3
HY_SLOW_DECAY = 1.5
HY_TARGET = 1e-2
RET_WIDTH = 512
RET_HEADS = 4
RET_QK_DIM = 64
RET_V_DIM = 128
ROPE_BASE = 10000.0
D_FF = 2816
EPS = 1e-6
HY_COLS = (HY_ORDER + 1) * HY_WIDTH
Q_OFF = HY_COLS
K_OFF = Q_OFF + RET_HEADS * RET_QK_DIM
V_OFF = K_OFF + RET_HEADS * RET_QK_DIM
G_OFF = V_OFF + RET_WIDTH
K_SCALE = RET_QK_DIM ** -0.5

MOD_ROWS = 16
TOK_TILE = 512
IN_TILE = 1024
IN_PIECE = 256
SUBLANES = 8
LANES = 128
HALO = 16
CONV_BLOCK = 512
N_CONV_BLOCKS = SEQ // CONV_BLOCK
FREQ_ROWS = 16
FREQ_LANES = 128
DFT_ROWS = 256
CTX_PER_STEP = 4
RET_CHUNK = 256
RET_HEADS_PER_STEP = 4
FFN_COLS = 256
VMEM_LIMIT = 56 * 1024 * 1024
HYENA_VMEM_LIMIT = 62 * 1024 * 1024

_NT = (((1,), (1,)), ((), ()))


def _dot(a, b):
    return jnp.dot(a, b, preferred_element_type=F32)


def _dot_hi(a, b):
    return jnp.dot(a, b, preferred_element_type=F32, precision=lax.Precision.HIGHEST)


def _silu(x):
    return x * (1.0 / (1.0 + jnp.exp(-x)))


def _norm_mod(x, gain, shift, scale):
    y = x * lax.rsqrt(jnp.mean(x * x, axis=-1, keepdims=True) + EPS)
    return (y * gain) * (1.0 + scale) + shift


def _resident(shape):
    nd = len(shape)
    return pl.BlockSpec(shape, lambda *_: (0,) * nd, pipeline_mode=pl.Buffered(1))


def _resident_cols(rows, col0, width):
    assert col0 % width == 0
    return pl.BlockSpec((rows, width), lambda *_: (0, col0 // width), pipeline_mode=pl.Buffered(1))


def _resident_rows(row0, height, cols):
    assert row0 % height == 0
    return pl.BlockSpec((height, cols), lambda *_: (row0 // height, 0), pipeline_mode=pl.Buffered(1))


@functools.lru_cache(maxsize=None)
def _rope_tables():
    pos = np.arange(SEQ)
    row = (pos // GRID_W).astype(np.float64)
    col = (pos % GRID_W).astype(np.float64)
    quarter = RET_QK_DIM // 4
    inv_freq = ROPE_BASE ** (-np.arange(quarter, dtype=np.float64) / quarter)
    ang = np.concatenate([row[:, None] * inv_freq, col[:, None] * inv_freq], axis=-1)
    cos, sin = np.cos(ang), np.sin(ang)
    cos_h = np.concatenate([cos, cos], axis=-1)
    sin_h = np.concatenate([-sin, sin], axis=-1)
    cos_t = np.tile(cos_h, (1, RET_HEADS))
    sin_t = np.tile(sin_h, (1, RET_HEADS))
    return (cos_t.astype(np.float32), sin_t.astype(np.float32),
            np.ascontiguousarray((cos_t * K_SCALE).T).astype(np.float32),
            np.ascontiguousarray((sin_t * K_SCALE).T).astype(np.float32))


@functools.lru_cache(maxsize=None)
def _filter_features():
    lag = np.abs(np.arange(2 * SEQ) - SEQ).astype(np.float64)
    t = lag / (SEQ - 1)
    bands = np.linspace(1e-4, HY_EMB_BANDS - 1, HY_EMB_BANDS)
    ang = 2.0 * math.pi * lag[:, None] * bands[None, :] / SEQ
    z = np.concatenate([t[:, None], np.cos(ang), -np.sin(ang)], axis=-1)
    zp = np.zeros((2 * SEQ, HY_FILTER_WIDTH), np.float64)
    zp[:, :z.shape[1]] = z
    max_decay = math.log(HY_TARGET) / HY_FAST_DECAY
    min_decay = math.log(HY_TARGET) / HY_SLOW_DECAY
    absdelta = np.abs(np.linspace(min_decay, max_decay, HY_WIDTH))[:, None]
    return (np.ascontiguousarray(zp.T).astype(np.float32), t[None, :].astype(np.float32),
            absdelta.astype(np.float32))


@functools.lru_cache(maxsize=None)
def _dft_matrices():
    p = CONV_BLOCK
    n = 2 * p
    f = np.arange(p, dtype=np.float64)[:, None] + 0.5
    t = np.arange(n, dtype=np.float64)[None, :]
    theta = 2.0 * math.pi * f * t / n
    fwd = np.concatenate([np.cos(theta), -np.sin(theta)], axis=0)
    th_out = theta[:, p:].T
    inv = np.concatenate([np.cos(th_out), -np.sin(th_out)], axis=1) / p
    return fwd.astype(np.float32), inv.astype(np.float32)


def _mod_kernel(c_ref, w_ref, b_ref, o_ref):
    s = _silu(c_ref[...]).astype(BF16)
    o_ref[...] = _dot(s, w_ref[...].astype(BF16)) + b_ref[...]


def _mod_call(cc, w_mod, b_mod):
    ncol = 6 * D_MODEL
    blk = ncol // 4
    return pl.pallas_call(
        _mod_kernel,
        grid=(ncol // blk,),
        in_specs=[pl.BlockSpec((MOD_ROWS, D_MODEL), lambda j: (0, 0)),
                  pl.BlockSpec((D_MODEL, blk), lambda j: (0, j)),
                  pl.BlockSpec((1, blk), lambda j: (0, j))],
        out_specs=pl.BlockSpec((MOD_ROWS, blk), lambda j: (0, j)),
        out_shape=jax.ShapeDtypeStruct((MOD_ROWS, ncol), F32),
        compiler_params=pltpu.CompilerParams(vmem_limit_bytes=VMEM_LIMIT),
        name="mod",
    )(cc, w_mod, b_mod)


def _ctx_kernel(lg_ref, x_ref, sh_ref, sc_ref, n1_ref, wkt_ref, wv_ref, s_ref):
    pos = lax.broadcasted_iota(jnp.int32, (1, CTX_LEN), 1).astype(F32)
    w_f = [jnp.exp(lg_ref[0, hh] * (CTX_LEN - 1.0 - pos)) for hh in range(RET_HEADS)]
    w_b = [jnp.exp(lg_ref[1, hh] * pos) for hh in range(RET_HEADS)]
    for bb in range(CTX_PER_STEP):
        h = _norm_mod(x_ref[bb], n1_ref[...], sh_ref[0], sc_ref[0]).astype(BF16)
        kt = lax.dot_general(wkt_ref[...], h, _NT, preferred_element_type=F32) * K_SCALE
        v = _dot(h, wv_ref[...])
        for hh in range(RET_HEADS):
            kth = kt[hh * RET_QK_DIM:(hh + 1) * RET_QK_DIM, :]
            vh = v[:, hh * RET_V_DIM:(hh + 1) * RET_V_DIM].astype(BF16)
            s_ref[bb, hh, 0:RET_QK_DIM, :] = _dot((kth * w_f[hh]).astype(BF16), vh)
            s_ref[bb, hh, RET_QK_DIM:2 * RET_QK_DIM, :] = _dot((kth * w_b[hh]).astype(BF16), vh)


def _ctx_call(lg, ctx, mod3, norm1, w_kt, w_v):
    return pl.pallas_call(
        _ctx_kernel,
        grid=(BATCH // CTX_PER_STEP,),
        in_specs=[pl.BlockSpec(memory_space=pltpu.SMEM),
                  pl.BlockSpec((CTX_PER_STEP, CTX_LEN, D_MODEL), lambda b: (b, 0, 0)),
                  pl.BlockSpec((1, 1, D_MODEL), lambda b: (BATCH, 0, 0)),
                  pl.BlockSpec((1, 1, D_MODEL), lambda b: (BATCH, 0, 1)),
                  _resident((1, D_MODEL)),
                  _resident((RET_HEADS * RET_QK_DIM, D_MODEL)),
                  _resident_cols(D_MODEL, V_OFF, RET_WIDTH)],
        out_specs=pl.BlockSpec((CTX_PER_STEP, RET_HEADS, 2 * RET_QK_DIM, RET_V_DIM), lambda b: (b, 0, 0, 0)),
        out_shape=jax.ShapeDtypeStruct((BATCH, RET_HEADS, 2 * RET_QK_DIM, RET_V_DIM), F32),
        compiler_params=pltpu.CompilerParams(vmem_limit_bytes=VMEM_LIMIT),
        name="ctx",
    )(lg, ctx, mod3, mod3, norm1, w_kt, w_v)


def _swap_halves(x, axis):
    n = x.shape[axis]
    half = RET_QK_DIM // 2
    idx = lax.broadcasted_iota(jnp.int32, x.shape, axis)
    first = (idx & (RET_QK_DIM - 1)) < half
    return jnp.where(first, pltpu.roll(x, n - half, axis), pltpu.roll(x, half, axis))


def _inproj_kernel(x_ref, sh_ref, sc_ref, n1_ref, why_ref, wq_ref, wv_ref, wg_ref, wkt_ref,
                   cq_ref, sq_ref, ck_ref, sk_ref,
                   zhy_ref, q_ref, kt_ref, v_ref, g_ref):
    pieces = [slice(r, r + IN_PIECE) for r in range(0, IN_TILE, IN_PIECE)]
    hbs = [_norm_mod(x_ref[0, rows, :], n1_ref[...], sh_ref[0], sc_ref[0]).astype(BF16) for rows in pieces]
    for rows, hb in zip(pieces, hbs):
        zhy_ref[0, rows, :] = _dot(hb, why_ref[...]).astype(BF16)
        v_ref[0, rows, :] = _dot(hb, wv_ref[...]).astype(BF16)
        g_ref[0, rows, :] = _dot(hb, wg_ref[...]).astype(BF16)
        q = _dot(hb, wq_ref[...])
        q = q * cq_ref[rows, :] + _swap_halves(q, 1) * sq_ref[rows, :]
        for hh in range(RET_HEADS):
            q_ref[0, hh, rows, :] = q[:, hh * RET_QK_DIM:(hh + 1) * RET_QK_DIM].astype(BF16)
        kt = lax.dot_general(wkt_ref[...], hb, _NT, preferred_element_type=F32)
        kt = kt * ck_ref[:, rows] + _swap_halves(kt, 0) * sk_ref[:, rows]
        kt_ref[0, :, rows] = kt.astype(BF16)


def _inproj_call(x, mod3, norm1, w_in, w_kt, rope):
    cq, sq, ck, sk = rope
    t = IN_TILE
    qk = RET_HEADS * RET_QK_DIM
    return pl.pallas_call(
        _inproj_kernel,
        grid=(BATCH, SEQ // t),
        in_specs=[pl.BlockSpec((1, t, D_MODEL), lambda b, i: (b, i, 0)),
                  pl.BlockSpec((1, 1, D_MODEL), lambda b, i: (b, 0, 0)),
                  pl.BlockSpec((1, 1, D_MODEL), lambda b, i: (b, 0, 1)),
                  _resident((1, D_MODEL)),
                  _resident_cols(D_MODEL, 0, HY_COLS),
                  _resident_cols(D_MODEL, Q_OFF, qk),
                  _resident_cols(D_MODEL, V_OFF, RET_WIDTH),
                  _resident_cols(D_MODEL, G_OFF, RET_WIDTH),
                  _resident((qk, D_MODEL)),
                  pl.BlockSpec((t, qk), lambda b, i: (i, 0)),
                  pl.BlockSpec((t, qk), lambda b, i: (i, 0)),
                  pl.BlockSpec((qk, t), lambda b, i: (0, i)),
                  pl.BlockSpec((qk, t), lambda b, i: (0, i))],
        out_specs=[pl.BlockSpec((1, t, HY_COLS), lambda b, i: (b, i, 0)),
                   pl.BlockSpec((1, RET_HEADS, t, RET_QK_DIM), lambda b, i: (b, 0, i, 0)),
                   pl.BlockSpec((1, qk, t), lambda b, i: (b, 0, i)),
                   pl.BlockSpec((1, t, RET_WIDTH), lambda b, i: (b, i, 0)),
                   pl.BlockSpec((1, t, RET_WIDTH), lambda b, i: (b, i, 0))],
        out_shape=[jax.ShapeDtypeStruct((BATCH, SEQ, HY_COLS), BF16),
                   jax.ShapeDtypeStruct((BATCH, RET_HEADS, SEQ, RET_QK_DIM), BF16),
                   jax.ShapeDtypeStruct((BATCH, qk, SEQ), BF16),
                   jax.ShapeDtypeStruct((BATCH, SEQ, RET_WIDTH), BF16),
                   jax.ShapeDtypeStruct((BATCH, SEQ, RET_WIDTH), BF16)],
        compiler_params=pltpu.CompilerParams(vmem_limit_bytes=VMEM_LIMIT),
        name="inproj",
    )(x, mod3, mod3, norm1, w_in, w_in, w_in, w_in, w_kt, cq, sq, ck, sk)


def _filter_mlp_kernel(zt_ref, t_ref, w1t_ref, b1_ref, f1_ref, w2t_ref, b2_ref, f2_ref, w3t_ref, adel_ref,
                       kt_ref):
    hid = jnp.sin(f1_ref[...] * (_dot_hi(w1t_ref[...], zt_ref[...]) + b1_ref[...]))
    hid = jnp.sin(f2_ref[...] * (_dot_hi(w2t_ref[...], hid) + b2_ref[...])).astype(BF16)
    c = HY_WIDTH
    for half in range(2):
        lags = slice(half * SEQ, (half + 1) * SEQ)
        window = jnp.exp(-adel_ref[...] * t_ref[:, lags])
        direction = 1 - half
        for o in range(HY_ORDER):
            r0 = (direction * HY_ORDER + o) * c
            w3 = w3t_ref[r0:r0 + c, :].astype(BF16)
            kt_ref[o, :, lags] = (_dot(w3, hid[:, lags]) * window).astype(BF16)


def _filter_mlp_call(zt, t_row, w1t, b1, f1, w2t, b2, f2, w3t, absdelta):
    fw = HY_FILTER_WIDTH
    n_out = 2 * HY_ORDER * HY_WIDTH
    return pl.pallas_call(
        _filter_mlp_kernel,
        grid=(1,),
        in_specs=[_resident((fw, 2 * SEQ)), _resident((1, 2 * SEQ)),
                  _resident((fw, fw)), _resident((fw, 1)), _resident((fw, 1)),
                  _resident((fw, fw)), _resident((fw, 1)), _resident((fw, 1)),
                  _resident((n_out, fw)), _resident((HY_WIDTH, 1))],
        out_specs=pl.BlockSpec((HY_ORDER, HY_WIDTH, 2 * SEQ), lambda i: (0, 0, 0)),
        out_shape=jax.ShapeDtypeStruct((HY_ORDER, HY_WIDTH, 2 * SEQ), BF16),
        compiler_params=pltpu.CompilerParams(vmem_limit_bytes=VMEM_LIMIT),
        name="filter_mlp",
    )(zt, t_row, w1t, b1, f1, w2t, b2, f2, w3t, absdelta)


_FILTER_PLANES = (
    {0: 1},
    {-1: 1, 0: -1},
    {1: 1, 0: -1},
    {-2: 1, 0: -1},
    {-3: 1, -1: -1, -2: -1, 0: 1},
    {-1: 1, 1: -1, -2: -1, 0: 1},
    {2: 1, 0: -1},
    {1: 1, -1: -1, 2: -1, 0: 1},
    {3: 1, 1: -1, 2: -1, 0: 1},
)
DFT_BLOCKS_PER_STEP = 2
PLANES_PER_STEP = 3


def _filter_dft_kernel(kt_ref, fwd_ref, h_ref, t_scr):
    p = CONV_BLOCK
    n_dft_steps = 2 * N_CONV_BLOCKS // DFT_BLOCKS_PER_STEP
    step = pl.program_id(1)

    for s in range(n_dft_steps):
        @pl.when(step == s)
        def _(s=s):
            for k in range(DFT_BLOCKS_PER_STEP):
                t_scr[s * DFT_BLOCKS_PER_STEP + k] = lax.dot_general(
                    fwd_ref[...], kt_ref[0, :, k * p:(k + 1) * p], _NT, preferred_element_type=F32)
            for k in range(PLANES_PER_STEP):
                h_ref[0, k] = t_scr[s * DFT_BLOCKS_PER_STEP]

    row = lax.broadcasted_iota(jnp.int32, (p, 1), 0)
    sign = (1 - 2 * (row & 1)).astype(F32)
    re, im = slice(0, p), slice(p, 2 * p)

    def plane(k, coefs):
        def comb(shift, rows):
            acc = None
            for d, c in coefs.items():
                term = t_scr[d + N_CONV_BLOCKS - 1 + shift, rows, :]
                if acc is None:
                    acc = term
                else:
                    acc = acc + term if c > 0 else acc - term
            return acc

        h_ref[0, k, re, :] = comb(0, re) + sign * comb(1, im)
        h_ref[0, k, im, :] = comb(0, im) - sign * comb(1, re)

    for s in range(len(_FILTER_PLANES) // PLANES_PER_STEP):
        @pl.when(step == n_dft_steps + s)
        def _(s=s):
            for k in range(PLANES_PER_STEP):
                plane(k, _FILTER_PLANES[s * PLANES_PER_STEP + k])


def _filter_dft_call(kt, fwd_u):
    p = CONV_BLOCK
    n_blk = 2 * N_CONV_BLOCKS
    n_planes = len(_FILTER_PLANES)
    n_dft_steps = n_blk // DFT_BLOCKS_PER_STEP
    return pl.pallas_call(
        _filter_dft_kernel,
        grid=(HY_ORDER, n_dft_steps + n_planes // PLANES_PER_STEP),
        in_specs=[pl.BlockSpec((1, HY_WIDTH, DFT_BLOCKS_PER_STEP * p),
                               lambda o, s: (o, 0, jnp.minimum(s, n_dft_steps - 1))),
                  _resident((2 * p, p))],
        out_specs=pl.BlockSpec((1, PLANES_PER_STEP, 2 * p, HY_WIDTH),
                               lambda o, s: (o, jnp.maximum(s - n_dft_steps, 0), 0, 0)),
        out_shape=jax.ShapeDtypeStruct((HY_ORDER, n_planes, 2 * p, HY_WIDTH), F32),
        scratch_shapes=[pltpu.VMEM((n_blk, 2 * p, HY_WIDTH), F32)],
        compiler_params=pltpu.CompilerParams(vmem_limit_bytes=VMEM_LIMIT),
        name="filter_dft",
    )(kt, fwd_u)


def _conv3_rows(ref, j, n_blocks, rows, w, b):
    main = ref[0, j * rows:(j + 1) * rows, :].astype(F32)
    cols = main.shape[1]
    zeros = jnp.zeros((HALO, cols), F32)
    prev = ref[0, j * rows - HALO:j * rows, :].astype(F32) if j > 0 else zeros
    nxt = ref[0, (j + 1) * rows:(j + 1) * rows + HALO, :].astype(F32) if j < n_blocks - 1 else zeros
    ext = jnp.concatenate([prev, main, nxt], axis=0)
    n = rows + 2 * HALO
    before = pltpu.roll(ext, 1, 0)[HALO:HALO + rows]
    after = pltpu.roll(ext, n - 1, 0)[HALO:HALO + rows]
    return before * w[0:1] + main * w[1:2] + after * w[2:3] + b


def _hyena_kernel(u_ref, zg_ref, cwu_ref, cbu_ref, cwg_ref, cbg_ref, skip_ref, h_ref, fwd_ref, inv_ref,
                  o_ref, uf_scr, ub_scr, y_scr, *, conv_u):
    p = CONV_BLOCK
    nb = N_CONV_BLOCKS

    def prepare(j):
        if conv_u:
            ub_scr[j * p:(j + 1) * p, :] = _conv3_rows(u_ref, j, nb, p, cwu_ref[...], cbu_ref[...]).astype(BF16)

    def u_block(j):
        return ub_scr[j * p:(j + 1) * p, :] if conv_u else u_ref[0, j * p:(j + 1) * p, :]

    prepare(0)
    for j in range(nb):
        if j + 1 < nb:
            prepare(j + 1)
        for r in range(2 * p // DFT_ROWS):
            rows = slice(r * DFT_ROWS, (r + 1) * DFT_ROWS)
            uf_scr[j, rows, :] = _dot(fwd_ref[rows, :], u_block(j))

    def cadd(a, b):
        return a[0] + b[0], a[1] + b[1]

    def cmul(m, x):
        return m[0] * x[0] - m[1] * x[1], m[0] * x[1] + m[1] * x[0]

    def toeplitz2(k0, x0, x1, re, im, ln):
        m0, mu, ml = ((h_ref[0, k0 + t, re, ln], h_ref[0, k0 + t, im, ln]) for t in range(3))
        p1 = cmul(m0, cadd(x0, x1))
        return cadd(p1, cmul(mu, x1)), cadd(p1, cmul(ml, x0))

    assert nb == 4
    for r in range(p // FREQ_ROWS):
        re = slice(r * FREQ_ROWS, (r + 1) * FREQ_ROWS)
        im = slice(p + r * FREQ_ROWS, p + (r + 1) * FREQ_ROWS)
        for cb in range(HY_WIDTH // FREQ_LANES):
            ln = slice(cb * FREQ_LANES, (cb + 1) * FREQ_LANES)
            u = [(uf_scr[j, re, ln], uf_scr[j, im, ln]) for j in range(nb)]
            d0, d1 = toeplitz2(0, cadd(u[0], u[2]), cadd(u[1], u[3]), re, im, ln)
            b0, b1 = toeplitz2(3, u[2], u[3], re, im, ln)
            c0, c1 = toeplitz2(6, u[0], u[1], re, im, ln)
            for i, yi in enumerate((cadd(d0, b0), cadd(d1, b1), cadd(d0, c0), cadd(d1, c1))):
                y_scr[i, re, ln] = yi[0].astype(BF16)
                y_scr[i, im, ln] = yi[1].astype(BF16)

    for i in range(nb):
        gate = _conv3_rows(zg_ref, i, nb, p, cwg_ref[...], cbg_ref[...])
        y = _dot(inv_ref[...], y_scr[i])
        o_ref[0, i * p:(i + 1) * p, :] = (
            gate * (y + u_block(i).astype(F32) * skip_ref[...])).astype(BF16)


def _hyena_call(u, u_col, zhy, gate_col, conv_w, conv_b, skip, g_all, order, fwd_u, inv, conv_u):
    p = CONV_BLOCK
    c = HY_WIDTH
    ucol = u_col if conv_u else 0
    cwu = conv_w[:, ucol * c:(ucol + 1) * c]
    cbu = conv_b[:, ucol * c:(ucol + 1) * c]
    cwg = conv_w[:, gate_col * c:(gate_col + 1) * c]
    cbg = conv_b[:, gate_col * c:(gate_col + 1) * c]
    return pl.pallas_call(
        functools.partial(_hyena_kernel, conv_u=conv_u),
        grid=(BATCH,),
        in_specs=[pl.BlockSpec((1, SEQ, c), lambda b: (b, 0, u_col)),
                  pl.BlockSpec((1, SEQ, c), lambda b: (b, 0, gate_col)),
                  _resident((3, c)), _resident((1, c)), _resident((3, c)), _resident((1, c)),
                  _resident((1, c)),
                  pl.BlockSpec((1, len(_FILTER_PLANES), 2 * p, c), lambda b: (order, 0, 0, 0),
                               pipeline_mode=pl.Buffered(1)),
                  _resident((2 * p, p)),
                  _resident((p, 2 * p))],
        out_specs=pl.BlockSpec((1, SEQ, c), lambda b: (b, 0, 0)),
        out_shape=jax.ShapeDtypeStruct((BATCH, SEQ, c), BF16),
        scratch_shapes=[pltpu.VMEM((N_CONV_BLOCKS, 2 * p, c), F32),
                        pltpu.VMEM((SEQ, c) if conv_u else (SUBLANES * 2, LANES), BF16),
                        pltpu.VMEM((N_CONV_BLOCKS, 2 * p, c), BF16)],
        compiler_params=pltpu.CompilerParams(vmem_limit_bytes=HYENA_VMEM_LIMIT),
        name="hyena%d" % order,
    )(u, zhy, cwu, cbu, cwg, cbg, skip, g_all, fwd_u, inv)


def _ret_kernel(lg_ref, q_ref, kt_ref, v_ref, g_ref, s_ref, wup_ref, wdn_ref,
                o_ref, wup_b_ref, wdn_b_ref, b_scr, decay_scr, qw_scr):
    wup_b_ref[...] = wup_ref[...].astype(BF16)
    wdn_b_ref[...] = wdn_ref[...].astype(BF16)
    c = RET_CHUNK
    nc = SEQ // c
    dk = RET_QK_DIM
    dv = RET_V_DIM
    heads = range(RET_HEADS_PER_STEP)
    lg_f = [lg_ref[0, pl.program_id(0) * RET_HEADS_PER_STEP + hh] for hh in heads]
    lg_b = [lg_ref[1, pl.program_id(0) * RET_HEADS_PER_STEP + hh] for hh in heads]

    @pl.when(pl.program_id(1) == 0)
    def _():
        ii = lax.broadcasted_iota(jnp.int32, (c, c), 0).astype(F32)
        jj = lax.broadcasted_iota(jnp.int32, (c, c), 1).astype(F32)
        dif = ii - jj
        pos_q = lax.broadcasted_iota(jnp.int32, (c, dk), 0).astype(F32)
        for hh in heads:
            decay_scr[hh] = jnp.where(dif >= 0.0, jnp.exp(lg_f[hh] * jnp.maximum(dif, 0.0)),
                                      jnp.exp(lg_b[hh] * jnp.maximum(-dif, 0.0)))
            qw_scr[hh, 0] = jnp.exp(lg_f[hh] * (pos_q + 1.0))
            qw_scr[hh, 1] = jnp.exp(lg_b[hh] * (c - pos_q))

    pos_r = lax.broadcasted_iota(jnp.int32, (1, c), 1).astype(F32)
    ones = jnp.ones((1, dv), F32)
    kw_f = [jnp.exp(lg_f[hh] * (c - 1.0 - pos_r)) for hh in heads]
    kw_b = [jnp.exp(lg_b[hh] * pos_r) for hh in heads]
    dec_f = [jnp.exp(lg_f[hh] * float(c) * ones) for hh in heads]
    dec_b = [jnp.exp(lg_b[hh] * float(c) * ones) for hh in heads]

    def kt_chunk(hh, n):
        return kt_ref[0, hh * dk:(hh + 1) * dk, n * c:(n + 1) * c]

    def v_chunk(hh, n):
        return v_ref[0, n * c:(n + 1) * c, hh * dv:(hh + 1) * dv]

    state = [s_ref[0, hh, dk:2 * dk, :] for hh in heads]
    for hh in heads:
        b_scr[hh, nc - 1] = state[hh]
    for n in range(nc - 1, 0, -1):
        for hh in heads:
            ktn = (kt_chunk(hh, n).astype(F32) * kw_b[hh]).astype(BF16)
            state[hh] = state[hh] * dec_b[hh] + _dot(ktn, v_chunk(hh, n))
            b_scr[hh, n - 1] = state[hh]

    state = [s_ref[0, hh, 0:dk, :] for hh in heads]
    for n in range(nc):
        for hh in heads:
            qn = q_ref[0, hh, n * c:(n + 1) * c, :]
            ktn = kt_chunk(hh, n)
            vn = v_chunk(hh, n)
            scores = (_dot(qn, ktn) * decay_scr[hh]).astype(BF16)
            qf = qn.astype(F32)
            o = _dot(scores, vn)
            o = o + _dot((qf * qw_scr[hh, 0]).astype(BF16), state[hh].astype(BF16))
            o = o + _dot((qf * qw_scr[hh, 1]).astype(BF16), b_scr[hh, n].astype(BF16))
            state[hh] = state[hh] * dec_f[hh] + _dot((ktn.astype(F32) * kw_f[hh]).astype(BF16), vn)
            o = o * lax.rsqrt(jnp.mean(o * o, axis=-1, keepdims=True) + EPS)
            gate = g_ref[0, n * c:(n + 1) * c, hh * dv:(hh + 1) * dv].astype(F32)
            o_ref[0, n * c:(n + 1) * c, hh * dv:(hh + 1) * dv] = (_silu(gate) * o).astype(BF16)


def _ret_call(lg, q, kt, v, g, s, w_up, w_dn):
    dk, dv = RET_QK_DIM, RET_V_DIM
    hp = RET_HEADS_PER_STEP
    assert RET_HEADS == hp
    up_rows, dn_rows = w_up.shape[0] // BATCH, w_dn.shape[0] // BATCH
    up_slab = pl.BlockSpec((up_rows, w_up.shape[1]), lambda h, b: (b, 0))
    dn_slab = pl.BlockSpec((dn_rows, w_dn.shape[1]), lambda h, b: (b, 0))
    return pl.pallas_call(
        _ret_kernel,
        grid=(RET_HEADS // hp, BATCH),
        in_specs=[pl.BlockSpec(memory_space=pltpu.SMEM),
                  pl.BlockSpec((1, hp, SEQ, dk), lambda h, b: (b, h, 0, 0)),
                  pl.BlockSpec((1, hp * dk, SEQ), lambda h, b: (b, h, 0)),
                  pl.BlockSpec((1, SEQ, hp * dv), lambda h, b: (b, 0, h)),
                  pl.BlockSpec((1, SEQ, hp * dv), lambda h, b: (b, 0, h)),
                  pl.BlockSpec((1, hp, 2 * dk, dv), lambda h, b: (b, h, 0, 0)),
                  up_slab, dn_slab],
        out_specs=[pl.BlockSpec((1, SEQ, hp * dv), lambda h, b: (b, 0, h)), up_slab, dn_slab],
        out_shape=[jax.ShapeDtypeStruct((BATCH, SEQ, RET_WIDTH), BF16),
                   jax.ShapeDtypeStruct(w_up.shape, BF16),
                   jax.ShapeDtypeStruct(w_dn.shape, BF16)],
        scratch_shapes=[pltpu.VMEM((hp, SEQ // RET_CHUNK, dk, dv), F32),
                        pltpu.VMEM((hp, RET_CHUNK, RET_CHUNK), F32),
                        pltpu.VMEM((hp, 2, RET_CHUNK, dk), F32)],
        compiler_params=pltpu.CompilerParams(vmem_limit_bytes=VMEM_LIMIT),
        name="ret",
    )(lg, q, kt, v, g, s, w_up, w_dn)


def _ffn_kernel(x_ref, xp_ref, xn_ref, yh_ref, yhp_ref, yhn_ref, yr_ref, yrp_ref, yrn_ref,
                g1_ref, sh_ref, sc_ref, g2_ref, n2_ref, nf_ref,
                woh_ref, wor_ref, wup_ref, cw_ref, cb_ref, wdn_ref,
                o_ref, hb_scr, x1_scr, av_scr, ag_scr, act_scr):
    t = TOK_TILE
    i = pl.program_id(1)
    nt = pl.num_programs(1)
    th = t // 2
    tile_halves = (slice(0, th), slice(th, t))

    def mixed(xr, yh, yr, rows=slice(None)):
        return xr[0, rows, :] + g1_ref[0] * (_dot(yh[0, rows, :], woh_ref[...]) + _dot(yr[0, rows, :], wor_ref[...]))

    def hidden(x1):
        return _norm_mod(x1, n2_ref[...], sh_ref[0], sc_ref[0])

    half = (t + 2 * HALO) // 2
    up_halves = (slice(0, half), slice(half, 2 * half))

    def up(slot, cblk, rows):
        c0 = cblk * FFN_COLS
        hb = hb_scr[rows, :]
        av_scr[slot, rows, :] = _dot(hb, wup_ref[:, c0:c0 + FFN_COLS])
        ag_scr[slot, rows, :] = _dot(hb, wup_ref[:, D_FF + c0:D_FF + c0 + FFN_COLS])

    for rows in tile_halves:
        x1_scr[rows, :] = mixed(x_ref, yh_ref, yr_ref, rows)
    hp = hidden(mixed(xp_ref, yhp_ref, yrp_ref))
    hn = hidden(mixed(xn_ref, yhn_ref, yrn_ref))
    hb_scr[0:HALO, :] = jnp.where(i > 0, hp, 0.0).astype(BF16)
    hb_scr[HALO + t:2 * HALO + t, :] = jnp.where(i < nt - 1, hn, 0.0).astype(BF16)
    hb_scr[HALO:HALO + th, :] = hidden(x1_scr[tile_halves[0], :]).astype(BF16)
    up(0, 0, up_halves[0])
    hb_scr[HALO + th:HALO + t, :] = hidden(x1_scr[tile_halves[1], :]).astype(BF16)
    up(0, 0, up_halves[1])

    def conv(scr, slot, col):
        w = cw_ref[:, col:col + FFN_COLS]
        return (scr[slot, HALO - 1:HALO - 1 + t, :] * w[0:1] + scr[slot, HALO:HALO + t, :] * w[1:2]
                + scr[slot, HALO + 1:HALO + 1 + t, :] * w[2:3] + cb_ref[:, col:col + FFN_COLS])

    n_blk = D_FF // FFN_COLS
    for cblk in range(n_blk):
        c0 = cblk * FFN_COLS
        slot = cblk % 2
        if cblk + 1 < n_blk:
            for rows in up_halves:
                up(1 - slot, cblk + 1, rows)
        act_scr[:, c0:c0 + FFN_COLS] = (
            _silu(conv(ag_scr, slot, D_FF + c0)) * conv(av_scr, slot, c0)).astype(BF16)

    down = [_dot(act_scr[rows, :], wdn_ref[...]) for rows in tile_halves]
    for rows, ffn in zip(tile_halves, down):
        x2 = x1_scr[rows, :] + g2_ref[0] * ffn
        o_ref[0, rows, :] = x2 * lax.rsqrt(jnp.mean(x2 * x2, axis=-1, keepdims=True) + EPS) * nf_ref[...]


def _ffn_call(x, y_hy, y_ret, mod3, norm2, norm_f, w_oh, w_or, w_up, conv_w, conv_b, w_dn):
    t = TOK_TILE
    r = t // HALO
    last = SEQ // HALO - 1

    def main(width):
        return pl.BlockSpec((1, t, width), lambda b, i: (b, i, 0))

    def prev(width):
        return pl.BlockSpec((1, HALO, width), lambda b, i: (b, jnp.maximum(i * r - 1, 0), 0))

    def nxt(width):
        return pl.BlockSpec((1, HALO, width), lambda b, i: (b, jnp.minimum((i + 1) * r, last), 0))

    def modrow(k):
        return pl.BlockSpec((1, 1, D_MODEL), lambda b, i: (b, 0, k))

    return pl.pallas_call(
        _ffn_kernel,
        grid=(BATCH, SEQ // t),
        in_specs=[main(D_MODEL), prev(D_MODEL), nxt(D_MODEL),
                  main(HY_WIDTH), prev(HY_WIDTH), nxt(HY_WIDTH),
                  main(RET_WIDTH), prev(RET_WIDTH), nxt(RET_WIDTH),
                  modrow(2), modrow(3), modrow(4), modrow(5),
                  _resident((1, D_MODEL)), _resident((1, D_MODEL)),
                  _resident_rows(0, HY_WIDTH, D_MODEL), _resident_rows(HY_WIDTH, RET_WIDTH, D_MODEL),
                  _resident((D_MODEL, 2 * D_FF)),
                  _resident((3, 2 * D_FF)), _resident((1, 2 * D_FF)),
                  _resident((D_FF, D_MODEL))],
        out_specs=pl.BlockSpec((1, t, D_MODEL), lambda b, i: (b, i, 0)),
        out_shape=jax.ShapeDtypeStruct((BATCH, SEQ, D_MODEL), F32),
        scratch_shapes=[pltpu.VMEM((t + 2 * HALO, D_MODEL), BF16),
                        pltpu.VMEM((t, D_MODEL), F32),
                        pltpu.VMEM((2, t + 2 * HALO, FFN_COLS), F32),
                        pltpu.VMEM((2, t + 2 * HALO, FFN_COLS), F32),
                        pltpu.VMEM((t, D_FF), BF16)],
        compiler_params=pltpu.CompilerParams(vmem_limit_bytes=VMEM_LIMIT),
        name="ffn",
    )(x, x, x, y_hy, y_hy, y_hy, y_ret, y_ret, y_ret, mod3, mod3, mod3, mod3,
      norm2, norm_f, w_oh, w_or, w_up, conv_w, conv_b, w_dn)


def kernel(x, c, ctx, c_ctx, w_mod, b_mod, norm1, w_in, hy_conv_w, hy_conv_b, hy_w1, hy_b1, hy_f1,
           hy_w2, hy_b2, hy_f2, hy_w3, hy_bias, ret_logit_f, ret_logit_b, w_out, norm2,
           ffn_w_up, ffn_conv_w, ffn_conv_b, ffn_w_down, norm_f):
    layer = 0
    rope = tuple(jnp.asarray(a) for a in _rope_tables())
    zt, t_row, absdelta = (jnp.asarray(a) for a in _filter_features())
    fwd_np, inv_np = _dft_matrices()
    fwd_u = jnp.asarray(fwd_np[:, :CONV_BLOCK]).astype(BF16)
    inv = jnp.asarray(inv_np).astype(BF16)

    w_in_b = w_in[layer].astype(BF16)
    w_kt = w_in_b[:, K_OFF:V_OFF].T
    w_out_b = w_out[layer].astype(BF16)
    row = lambda a: a.reshape(1, -1)
    col = lambda a: a.reshape(-1, 1)
    w1p = jnp.pad(hy_w1[layer], ((0, HY_FILTER_WIDTH - hy_w1.shape[1]), (0, 0)))
    lg = jnp.stack([jax.nn.log_sigmoid(ret_logit_f[layer].astype(F32)),
                    jax.nn.log_sigmoid(ret_logit_b[layer].astype(F32))])

    cc = jnp.concatenate([c, c_ctx[None, :], jnp.zeros((MOD_ROWS - BATCH - 1, D_MODEL), F32)], axis=0)
    mod = _mod_call(cc, w_mod[layer], row(b_mod[layer]))
    mod3 = mod.reshape(MOD_ROWS, 1, 6 * D_MODEL)
    norm1_r = row(norm1[layer])

    s_ctx = _ctx_call(lg, ctx, mod3, norm1_r, w_kt, w_in_b)
    zhy, q, kt, v, g = _inproj_call(x, mod3, norm1_r, w_in_b, w_kt, rope)

    k_two_sided = _filter_mlp_call(zt, t_row, w1p.T, col(hy_b1[layer]), col(hy_f1[layer]), hy_w2[layer].T,
                                   col(hy_b2[layer]), col(hy_f2[layer]), hy_w3[layer].T, absdelta)
    g_all = _filter_dft_call(k_two_sided, fwd_u)
    conv_w, conv_b = hy_conv_w[layer], row(hy_conv_b[layer])
    y1 = _hyena_call(zhy, 0, zhy, 1, conv_w, conv_b, hy_bias[layer][0:1], g_all, 0, fwd_u, inv, True)
    y_hy = _hyena_call(y1, 0, zhy, 2, conv_w, conv_b, hy_bias[layer][1:2], g_all, 1, fwd_u, inv, False)

    y_ret, w_up, w_dn = _ret_call(lg, q, kt, v, g, s_ctx, ffn_w_up[layer], ffn_w_down[layer])

    return _ffn_call(x, y_hy, y_ret, mod3, row(norm2[layer]), row(norm_f), w_out_b, w_out_b, w_up,
                     ffn_conv_w[layer], row(ffn_conv_b[layer]), w_dn)
```

```python
import functools
import math

import numpy as np
import jax
import jax.numpy as jnp
from jax import lax
from jax.experimental import pallas as pl
from jax.experimental.pallas import tpu as pltpu

F32 = jnp.float32
BF16 = jnp.bfloat16

D_MODEL = 1024
BATCH = 8
SEQ = 2048
CTX_LEN = 256
GRID_W = 64
HY_WIDTH = 512
HY_ORDER = 2
HY_EMB_BANDS = 16
HY_FILTER_WIDTH = 64
HY_FAST_DECAY = 0.3
HY_SLOW_DECAY = 1.5
HY_TARGET = 1e-2
RET_WIDTH = 512
RET_HEADS = 4
RET_QK_DIM = 64
RET_V_DIM = 128
ROPE_BASE = 10000.0
D_FF = 2816
EPS = 1e-6
HY_COLS = (HY_ORDER + 1) * HY_WIDTH
Q_OFF = HY_COLS
K_OFF = Q_OFF + RET_HEADS * RET_QK_DIM
V_OFF = K_OFF + RET_HEADS * RET_QK_DIM
G_OFF = V_OFF + RET_WIDTH
K_SCALE = RET_QK_DIM ** -0.5

MOD_ROWS = 16
MOD_EARLY_COLS = 2 * D_MODEL
MOD_LATE_COLS = 4 * D_MODEL
TOK_TILE = 512
IN_TILE = 1024
IN_PIECE = 256
SUBLANES = 8
LANES = 128
HALO = 16
CONV_BLOCK = 512
N_CONV_BLOCKS = SEQ // CONV_BLOCK
FREQ_ROWS = 16
FREQ_LANES = 128
DFT_ROWS = 256
CTX_PER_STEP = 4
RET_CHUNK = 256
RET_HEADS_PER_STEP = 4
FFN_COLS = 256
VMEM_LIMIT = 56 * 1024 * 1024
HYENA_VMEM_LIMIT = 62 * 1024 * 1024

_NT = (((1,), (1,)), ((), ()))


def _dot(a, b):
    return jnp.dot(a, b, preferred_element_type=F32)


def _dot_hi(a, b):
    return jnp.dot(a, b, preferred_element_type=F32, precision=lax.Precision.HIGHEST)


def _silu(x):
    return x * (1.0 / (1.0 + jnp.exp(-x)))


def _norm_mod(x, gain, shift, scale):
    y = x * lax.rsqrt(jnp.mean(x * x, axis=-1, keepdims=True) + EPS)
    return (y * gain) * (1.0 + scale) + shift


def _resident(shape):
    nd = len(shape)
    return pl.BlockSpec(shape, lambda *_: (0,) * nd, pipeline_mode=pl.Buffered(1))


def _resident_cols(rows, col0, width):
    assert col0 % width == 0
    return pl.BlockSpec((rows, width), lambda *_: (0, col0 // width), pipeline_mode=pl.Buffered(1))


def _resident_rows(row0, height, cols):
    assert row0 % height == 0
    return pl.BlockSpec((height, cols), lambda *_: (row0 // height, 0), pipeline_mode=pl.Buffered(1))


@functools.lru_cache(maxsize=None)
def _rope_tables():
    pos = np.arange(SEQ)
    row = (pos // GRID_W).astype(np.float64)
    col = (pos % GRID_W).astype(np.float64)
    quarter = RET_QK_DIM // 4
    inv_freq = ROPE_BASE ** (-np.arange(quarter, dtype=np.float64) / quarter)
    ang = np.concatenate([row[:, None] * inv_freq, col[:, None] * inv_freq], axis=-1)
    cos, sin = np.cos(ang), np.sin(ang)
    cos_h = np.concatenate([cos, cos], axis=-1)
    sin_h = np.concatenate([-sin, sin], axis=-1)
    cos_t = np.tile(cos_h, (1, RET_HEADS))
    sin_t = np.tile(sin_h, (1, RET_HEADS))
    return (cos_t.astype(np.float32), sin_t.astype(np.float32),
            np.ascontiguousarray((cos_t * K_SCALE).T).astype(np.float32),
            np.ascontiguousarray((sin_t * K_SCALE).T).astype(np.float32))


@functools.lru_cache(maxsize=None)
def _filter_features():
    lag = np.abs(np.arange(2 * SEQ) - SEQ).astype(np.float64)
    t = lag / (SEQ - 1)
    bands = np.linspace(1e-4, HY_EMB_BANDS - 1, HY_EMB_BANDS)
    ang = 2.0 * math.pi * lag[:, None] * bands[None, :] / SEQ
    z = np.concatenate([t[:, None], np.cos(ang), -np.sin(ang)], axis=-1)
    zp = np.zeros((2 * SEQ, HY_FILTER_WIDTH), np.float64)
    zp[:, :z.shape[1]] = z
    max_decay = math.log(HY_TARGET) / HY_FAST_DECAY
    min_decay = math.log(HY_TARGET) / HY_SLOW_DECAY
    absdelta = np.abs(np.linspace(min_decay, max_decay, HY_WIDTH))[:, None]
    return (np.ascontiguousarray(zp.T).astype(np.float32), t[None, :].astype(np.float32),
            absdelta.astype(np.float32))


@functools.lru_cache(maxsize=None)
def _dft_matrices():
    p = CONV_BLOCK
    n = 2 * p
    f = np.arange(p, dtype=np.float64)[:, None] + 0.5
    t = np.arange(n, dtype=np.float64)[None, :]
    theta = 2.0 * math.pi * f * t / n
    fwd = np.concatenate([np.cos(theta), -np.sin(theta)], axis=0)
    th_out = theta[:, p:].T
    inv = np.concatenate([np.cos(th_out), -np.sin(th_out)], axis=1) / p
    return fwd.astype(np.float32), inv.astype(np.float32)


def _mod_kernel(c_ref, w_ref, b_ref, o_ref):
    s = _silu(c_ref[...]).astype(BF16)
    o_ref[...] = _dot(s, w_ref[...].astype(BF16)) + b_ref[...]


def _mod_call(cc, w_mod, b_mod):
    ncol = MOD_EARLY_COLS
    blk = ncol // 2
    return pl.pallas_call(
        _mod_kernel,
        grid=(ncol // blk,),
        in_specs=[pl.BlockSpec((MOD_ROWS, D_MODEL), lambda j: (0, 0)),
                  pl.BlockSpec((D_MODEL, blk), lambda j: (0, j)),
                  pl.BlockSpec((1, blk), lambda j: (0, j))],
        out_specs=pl.BlockSpec((MOD_ROWS, blk), lambda j: (0, j)),
        out_shape=jax.ShapeDtypeStruct((MOD_ROWS, ncol), F32),
        compiler_params=pltpu.CompilerParams(vmem_limit_bytes=VMEM_LIMIT),
        name="mod",
    )(cc, w_mod, b_mod)


def _ctx_kernel(lg_ref, x_ref, sh_ref, sc_ref, n1_ref, wkt_ref, wv_ref, s_ref):
    pos = lax.broadcasted_iota(jnp.int32, (1, CTX_LEN), 1).astype(F32)
    w_f = [jnp.exp(lg_ref[0, hh] * (CTX_LEN - 1.0 - pos)) for hh in range(RET_HEADS)]
    w_b = [jnp.exp(lg_ref[1, hh] * pos) for hh in range(RET_HEADS)]
    for bb in range(CTX_PER_STEP):
        h = _norm_mod(x_ref[bb], n1_ref[...], sh_ref[0], sc_ref[0]).astype(BF16)
        kt = lax.dot_general(wkt_ref[...], h, _NT, preferred_element_type=F32) * K_SCALE
        v = _dot(h, wv_ref[...])
        for hh in range(RET_HEADS):
            kth = kt[hh * RET_QK_DIM:(hh + 1) * RET_QK_DIM, :]
            vh = v[:, hh * RET_V_DIM:(hh + 1) * RET_V_DIM].astype(BF16)
            s_ref[bb, hh, 0:RET_QK_DIM, :] = _dot((kth * w_f[hh]).astype(BF16), vh)
            s_ref[bb, hh, RET_QK_DIM:2 * RET_QK_DIM, :] = _dot((kth * w_b[hh]).astype(BF16), vh)


def _ctx_call(lg, ctx, mod3, norm1, w_kt, w_v):
    return pl.pallas_call(
        _ctx_kernel,
        grid=(BATCH // CTX_PER_STEP,),
        in_specs=[pl.BlockSpec(memory_space=pltpu.SMEM),
                  pl.BlockSpec((CTX_PER_STEP, CTX_LEN, D_MODEL), lambda b: (b, 0, 0)),
                  pl.BlockSpec((1, 1, D_MODEL), lambda b: (BATCH, 0, 0)),
                  pl.BlockSpec((1, 1, D_MODEL), lambda b: (BATCH, 0, 1)),
                  _resident((1, D_MODEL)),
                  _resident((RET_HEADS * RET_QK_DIM, D_MODEL)),
                  _resident_cols(D_MODEL, V_OFF, RET_WIDTH)],
        out_specs=pl.BlockSpec((CTX_PER_STEP, RET_HEADS, 2 * RET_QK_DIM, RET_V_DIM), lambda b: (b, 0, 0, 0)),
        out_shape=jax.ShapeDtypeStruct((BATCH, RET_HEADS, 2 * RET_QK_DIM, RET_V_DIM), F32),
        compiler_params=pltpu.CompilerParams(vmem_limit_bytes=VMEM_LIMIT),
        name="ctx",
    )(lg, ctx, mod3, mod3, norm1, w_kt, w_v)


def _swap_halves(x, axis):
    n = x.shape[axis]
    half = RET_QK_DIM // 2
    idx = lax.broadcasted_iota(jnp.int32, x.shape, axis)
    first = (idx & (RET_QK_DIM - 1)) < half
    return jnp.where(first, pltpu.roll(x, n - half, axis), pltpu.roll(x, half, axis))


def _inproj_kernel(x_ref, sh_ref, sc_ref, n1_ref, why_ref, wq_ref, wv_ref, wg_ref, wkt_ref,
                   cq_ref, sq_ref, ck_ref, sk_ref,
                   zhy_ref, q_ref, kt_ref, v_ref, g_ref):
    pieces = [slice(r, r + IN_PIECE) for r in range(0, IN_TILE, IN_PIECE)]
    hbs = [_norm_mod(x_ref[0, rows, :], n1_ref[...], sh_ref[0], sc_ref[0]).astype(BF16) for rows in pieces]
    for rows, hb in zip(pieces, hbs):
        zhy_ref[0, rows, :] = _dot(hb, why_ref[...]).astype(BF16)
        v_ref[0, rows, :] = _dot(hb, wv_ref[...]).astype(BF16)
        g_ref[0, rows, :] = _dot(hb, wg_ref[...]).astype(BF16)
        q = _dot(hb, wq_ref[...])
        q = q * cq_ref[rows, :] + _swap_halves(q, 1) * sq_ref[rows, :]
        for hh in range(RET_HEADS):
            q_ref[0, hh, rows, :] = q[:, hh * RET_QK_DIM:(hh + 1) * RET_QK_DIM].astype(BF16)
        kt = lax.dot_general(wkt_ref[...], hb, _NT, preferred_element_type=F32)
        kt = kt * ck_ref[:, rows] + _swap_halves(kt, 0) * sk_ref[:, rows]
        kt_ref[0, :, rows] = kt.astype(BF16)


def _inproj_call(x, mod3, norm1, w_in, w_kt, rope):
    cq, sq, ck, sk = rope
    t = IN_TILE
    qk = RET_HEADS * RET_QK_DIM
    return pl.pallas_call(
        _inproj_kernel,
        grid=(BATCH, SEQ // t),
        in_specs=[pl.BlockSpec((1, t, D_MODEL), lambda b, i: (b, i, 0)),
                  pl.BlockSpec((1, 1, D_MODEL), lambda b, i: (b, 0, 0)),
                  pl.BlockSpec((1, 1, D_MODEL), lambda b, i: (b, 0, 1)),
                  _resident((1, D_MODEL)),
                  _resident_cols(D_MODEL, 0, HY_COLS),
                  _resident_cols(D_MODEL, Q_OFF, qk),
                  _resident_cols(D_MODEL, V_OFF, RET_WIDTH),
                  _resident_cols(D_MODEL, G_OFF, RET_WIDTH),
                  _resident((qk, D_MODEL)),
                  pl.BlockSpec((t, qk), lambda b, i: (i, 0)),
                  pl.BlockSpec((t, qk), lambda b, i: (i, 0)),
                  pl.BlockSpec((qk, t), lambda b, i: (0, i)),
                  pl.BlockSpec((qk, t), lambda b, i: (0, i))],
        out_specs=[pl.BlockSpec((1, t, HY_COLS), lambda b, i: (b, i, 0)),
                   pl.BlockSpec((1, RET_HEADS, t, RET_QK_DIM), lambda b, i: (b, 0, i, 0)),
                   pl.BlockSpec((1, qk, t), lambda b, i: (b, 0, i)),
                   pl.BlockSpec((1, t, RET_WIDTH), lambda b, i: (b, i, 0)),
                   pl.BlockSpec((1, t, RET_WIDTH), lambda b, i: (b, i, 0))],
        out_shape=[jax.ShapeDtypeStruct((BATCH, SEQ, HY_COLS), BF16),
                   jax.ShapeDtypeStruct((BATCH, RET_HEADS, SEQ, RET_QK_DIM), BF16),
                   jax.ShapeDtypeStruct((BATCH, qk, SEQ), BF16),
                   jax.ShapeDtypeStruct((BATCH, SEQ, RET_WIDTH), BF16),
                   jax.ShapeDtypeStruct((BATCH, SEQ, RET_WIDTH), BF16)],
        compiler_params=pltpu.CompilerParams(vmem_limit_bytes=VMEM_LIMIT),
        name="inproj",
    )(x, mod3, mod3, norm1, w_in, w_in, w_in, w_in, w_kt, cq, sq, ck, sk)


def _filter_mlp_kernel(zt_ref, t_ref, w1t_ref, b1_ref, f1_ref, w2t_ref, b2_ref, f2_ref, w3t_ref, adel_ref,
                       kt_ref):
    hid = jnp.sin(f1_ref[...] * (_dot_hi(w1t_ref[...], zt_ref[...]) + b1_ref[...]))
    hid = jnp.sin(f2_ref[...] * (_dot_hi(w2t_ref[...], hid) + b2_ref[...])).astype(BF16)
    c = HY_WIDTH
    for half in range(2):
        lags = slice(half * SEQ, (half + 1) * SEQ)
        window = jnp.exp(-adel_ref[...] * t_ref[:, lags])
        direction = 1 - half
        for o in range(HY_ORDER):
            r0 = (direction * HY_ORDER + o) * c
            w3 = w3t_ref[r0:r0 + c, :].astype(BF16)
            kt_ref[o, :, lags] = (_dot(w3, hid[:, lags]) * window).astype(BF16)


def _filter_mlp_call(zt, t_row, w1t, b1, f1, w2t, b2, f2, w3t, absdelta):
    fw = HY_FILTER_WIDTH
    n_out = 2 * HY_ORDER * HY_WIDTH
    return pl.pallas_call(
        _filter_mlp_kernel,
        grid=(1,),
        in_specs=[_resident((fw, 2 * SEQ)), _resident((1, 2 * SEQ)),
                  _resident((fw, fw)), _resident((fw, 1)), _resident((fw, 1)),
                  _resident((fw, fw)), _resident((fw, 1)), _resident((fw, 1)),
                  _resident((n_out, fw)), _resident((HY_WIDTH, 1))],
        out_specs=pl.BlockSpec((HY_ORDER, HY_WIDTH, 2 * SEQ), lambda i: (0, 0, 0)),
        out_shape=jax.ShapeDtypeStruct((HY_ORDER, HY_WIDTH, 2 * SEQ), BF16),
        compiler_params=pltpu.CompilerParams(vmem_limit_bytes=VMEM_LIMIT),
        name="filter_mlp",
    )(zt, t_row, w1t, b1, f1, w2t, b2, f2, w3t, absdelta)


_FILTER_PLANES = (
    {0: 1},
    {-1: 1, 0: -1},
    {1: 1, 0: -1},
    {-2: 1, 0: -1},
    {-3: 1, -1: -1, -2: -1, 0: 1},
    {-1: 1, 1: -1, -2: -1, 0: 1},
    {2: 1, 0: -1},
    {1: 1, -1: -1, 2: -1, 0: 1},
    {3: 1, 1: -1, 2: -1, 0: 1},
)
DFT_BLOCKS_PER_STEP = 2
PLANES_PER_STEP = 3


def _filter_dft_kernel(kt_ref, fwd_ref, h_ref, t_scr):
    p = CONV_BLOCK
    n_dft_steps = 2 * N_CONV_BLOCKS // DFT_BLOCKS_PER_STEP
    step = pl.program_id(1)

    for s in range(n_dft_steps):
        @pl.when(step == s)
        def _(s=s):
            for k in range(DFT_BLOCKS_PER_STEP):
                t_scr[s * DFT_BLOCKS_PER_STEP + k] = lax.dot_general(
                    fwd_ref[...], kt_ref[0, :, k * p:(k + 1) * p], _NT, preferred_element_type=F32)
            for k in range(PLANES_PER_STEP):
                h_ref[0, k] = t_scr[s * DFT_BLOCKS_PER_STEP]

    row = lax.broadcasted_iota(jnp.int32, (p, 1), 0)
    sign = (1 - 2 * (row & 1)).astype(F32)
    re, im = slice(0, p), slice(p, 2 * p)

    def plane(k, coefs):
        def comb(shift, rows):
            acc = None
            for d, c in coefs.items():
                term = t_scr[d + N_CONV_BLOCKS - 1 + shift, rows, :]
                if acc is None:
                    acc = term
                else:
                    acc = acc + term if c > 0 else acc - term
            return acc

        h_ref[0, k, re, :] = comb(0, re) + sign * comb(1, im)
        h_ref[0, k, im, :] = comb(0, im) - sign * comb(1, re)

    for s in range(len(_FILTER_PLANES) // PLANES_PER_STEP):
        @pl.when(step == n_dft_steps + s)
        def _(s=s):
            for k in range(PLANES_PER_STEP):
                plane(k, _FILTER_PLANES[s * PLANES_PER_STEP + k])


def _filter_dft_call(kt, fwd_u):
    p = CONV_BLOCK
    n_blk = 2 * N_CONV_BLOCKS
    n_planes = len(_FILTER_PLANES)
    n_dft_steps = n_blk // DFT_BLOCKS_PER_STEP
    return pl.pallas_call(
        _filter_dft_kernel,
        grid=(HY_ORDER, n_dft_steps + n_planes // PLANES_PER_STEP),
        in_specs=[pl.BlockSpec((1, HY_WIDTH, DFT_BLOCKS_PER_STEP * p),
                               lambda o, s: (o, 0, jnp.minimum(s, n_dft_steps - 1))),
                  _resident((2 * p, p))],
        out_specs=pl.BlockSpec((1, PLANES_PER_STEP, 2 * p, HY_WIDTH),
                               lambda o, s: (o, jnp.maximum(s - n_dft_steps, 0), 0, 0)),
        out_shape=jax.ShapeDtypeStruct((HY_ORDER, n_planes, 2 * p, HY_WIDTH), F32),
        scratch_shapes=[pltpu.VMEM((n_blk, 2 * p, HY_WIDTH), F32)],
        compiler_params=pltpu.CompilerParams(vmem_limit_bytes=VMEM_LIMIT),
        name="filter_dft",
    )(kt, fwd_u)


def _conv3_rows(ref, j, n_blocks, rows, w, b):
    main = ref[0, j * rows:(j + 1) * rows, :].astype(F32)
    cols = main.shape[1]
    zeros = jnp.zeros((HALO, cols), F32)
    prev = ref[0, j * rows - HALO:j * rows, :].astype(F32) if j > 0 else zeros
    nxt = ref[0, (j + 1) * rows:(j + 1) * rows + HALO, :].astype(F32) if j < n_blocks - 1 else zeros
    ext = jnp.concatenate([prev, main, nxt], axis=0)
    n = rows + 2 * HALO
    before = pltpu.roll(ext, 1, 0)[HALO:HALO + rows]
    after = pltpu.roll(ext, n - 1, 0)[HALO:HALO + rows]
    return before * w[0:1] + main * w[1:2] + after * w[2:3] + b


def _hyena_kernel(u_ref, zg_ref, cwu_ref, cbu_ref, cwg_ref, cbg_ref, skip_ref, h_ref, fwd_ref, inv_ref,
                  o_ref, uf_scr, ub_scr, y_scr, *, conv_u):
    p = CONV_BLOCK
    nb = N_CONV_BLOCKS

    def prepare(j):
        if conv_u:
            ub_scr[j * p:(j + 1) * p, :] = _conv3_rows(u_ref, j, nb, p, cwu_ref[...], cbu_ref[...]).astype(BF16)

    def u_block(j):
        return ub_scr[j * p:(j + 1) * p, :] if conv_u else u_ref[0, j * p:(j + 1) * p, :]

    prepare(0)
    for j in range(nb):
        if j + 1 < nb:
            prepare(j + 1)
        for r in range(2 * p // DFT_ROWS):
            rows = slice(r * DFT_ROWS, (r + 1) * DFT_ROWS)
            uf_scr[j, rows, :] = _dot(fwd_ref[rows, :], u_block(j))

    def cadd(a, b):
        return a[0] + b[0], a[1] + b[1]

    def cmul(m, x):
        return m[0] * x[0] - m[1] * x[1], m[0] * x[1] + m[1] * x[0]

    def toeplitz2(k0, x0, x1, re, im, ln):
        m0, mu, ml = ((h_ref[0, k0 + t, re, ln], h_ref[0, k0 + t, im, ln]) for t in range(3))
        p1 = cmul(m0, cadd(x0, x1))
        return cadd(p1, cmul(mu, x1)), cadd(p1, cmul(ml, x0))

    assert nb == 4
    for r in range(p // FREQ_ROWS):
        re = slice(r * FREQ_ROWS, (r + 1) * FREQ_ROWS)
        im = slice(p + r * FREQ_ROWS, p + (r + 1) * FREQ_ROWS)
        for cb in range(HY_WIDTH // FREQ_LANES):
            ln = slice(cb * FREQ_LANES, (cb + 1) * FREQ_LANES)
            u = [(uf_scr[j, re, ln], uf_scr[j, im, ln]) for j in range(nb)]
            d0, d1 = toeplitz2(0, cadd(u[0], u[2]), cadd(u[1], u[3]), re, im, ln)
            b0, b1 = toeplitz2(3, u[2], u[3], re, im, ln)
            c0, c1 = toeplitz2(6, u[0], u[1], re, im, ln)
            for i, yi in enumerate((cadd(d0, b0), cadd(d1, b1), cadd(d0, c0), cadd(d1, c1))):
                y_scr[i, re, ln] = yi[0].astype(BF16)
                y_scr[i, im, ln] = yi[1].astype(BF16)

    for i in range(nb):
        gate = _conv3_rows(zg_ref, i, nb, p, cwg_ref[...], cbg_ref[...])
        y = _dot(inv_ref[...], y_scr[i])
        o_ref[0, i * p:(i + 1) * p, :] = (
            gate * (y + u_block(i).astype(F32) * skip_ref[...])).astype(BF16)


def _hyena_call(u, u_col, zhy, gate_col, conv_w, conv_b, skip, g_all, order, fwd_u, inv, conv_u):
    p = CONV_BLOCK
    c = HY_WIDTH
    ucol = u_col if conv_u else 0
    cwu = conv_w[:, ucol * c:(ucol + 1) * c]
    cbu = conv_b[:, ucol * c:(ucol + 1) * c]
    cwg = conv_w[:, gate_col * c:(gate_col + 1) * c]
    cbg = conv_b[:, gate_col * c:(gate_col + 1) * c]
    return pl.pallas_call(
        functools.partial(_hyena_kernel, conv_u=conv_u),
        grid=(BATCH,),
        in_specs=[pl.BlockSpec((1, SEQ, c), lambda b: (b, 0, u_col)),
                  pl.BlockSpec((1, SEQ, c), lambda b: (b, 0, gate_col)),
                  _resident((3, c)), _resident((1, c)), _resident((3, c)), _resident((1, c)),
                  _resident((1, c)),
                  pl.BlockSpec((1, len(_FILTER_PLANES), 2 * p, c), lambda b: (order, 0, 0, 0),
                               pipeline_mode=pl.Buffered(1)),
                  _resident((2 * p, p)),
                  _resident((p, 2 * p))],
        out_specs=pl.BlockSpec((1, SEQ, c), lambda b: (b, 0, 0)),
        out_shape=jax.ShapeDtypeStruct((BATCH, SEQ, c), BF16),
        scratch_shapes=[pltpu.VMEM((N_CONV_BLOCKS, 2 * p, c), F32),
                        pltpu.VMEM((SEQ, c) if conv_u else (SUBLANES * 2, LANES), BF16),
                        pltpu.VMEM((N_CONV_BLOCKS, 2 * p, c), BF16)],
        compiler_params=pltpu.CompilerParams(vmem_limit_bytes=HYENA_VMEM_LIMIT),
        name="hyena%d" % order,
    )(u, zhy, cwu, cbu, cwg, cbg, skip, g_all, fwd_u, inv)


def _ret_kernel(lg_ref, q_ref, kt_ref, v_ref, g_ref, s_ref, wup_ref, wdn_ref, wout_ref, cc_ref, wmod_ref, bmod_ref,
                o_ref, wup_b_ref, wdn_b_ref, wout_b_ref, modl_ref, b_scr, decay_scr, qw_scr):
    wup_b_ref[...] = wup_ref[...].astype(BF16)
    wdn_b_ref[...] = wdn_ref[...].astype(BF16)
    wout_b_ref[...] = wout_ref[...].astype(BF16)
    modl_ref[...] = _dot(_silu(cc_ref[...]).astype(BF16), wmod_ref[...].astype(BF16)) + bmod_ref[...]
    c = RET_CHUNK
    nc = SEQ // c
    dk = RET_QK_DIM
    dv = RET_V_DIM
    heads = range(RET_HEADS_PER_STEP)
    lg_f = [lg_ref[0, pl.program_id(0) * RET_HEADS_PER_STEP + hh] for hh in heads]
    lg_b = [lg_ref[1, pl.program_id(0) * RET_HEADS_PER_STEP + hh] for hh in heads]

    @pl.when(pl.program_id(1) == 0)
    def _():
        ii = lax.broadcasted_iota(jnp.int32, (c, c), 0).astype(F32)
        jj = lax.broadcasted_iota(jnp.int32, (c, c), 1).astype(F32)
        dif = ii - jj
        pos_q = lax.broadcasted_iota(jnp.int32, (c, dk), 0).astype(F32)
        for hh in heads:
            decay_scr[hh] = jnp.where(dif >= 0.0, jnp.exp(lg_f[hh] * jnp.maximum(dif, 0.0)),
                                      jnp.exp(lg_b[hh] * jnp.maximum(-dif, 0.0)))
            qw_scr[hh, 0] = jnp.exp(lg_f[hh] * (pos_q + 1.0))
            qw_scr[hh, 1] = jnp.exp(lg_b[hh] * (c - pos_q))

    pos_r = lax.broadcasted_iota(jnp.int32, (1, c), 1).astype(F32)
    ones = jnp.ones((1, dv), F32)
    kw_f = [jnp.exp(lg_f[hh] * (c - 1.0 - pos_r)) for hh in heads]
    kw_b = [jnp.exp(lg_b[hh] * pos_r) for hh in heads]
    dec_f = [jnp.exp(lg_f[hh] * float(c) * ones) for hh in heads]
    dec_b = [jnp.exp(lg_b[hh] * float(c) * ones) for hh in heads]

    def kt_chunk(hh, n):
        return kt_ref[0, hh * dk:(hh + 1) * dk, n * c:(n + 1) * c]

    def v_chunk(hh, n):
        return v_ref[0, n * c:(n + 1) * c, hh * dv:(hh + 1) * dv]

    state = [s_ref[0, hh, dk:2 * dk, :] for hh in heads]
    for hh in heads:
        b_scr[hh, nc - 1] = state[hh]
    for n in range(nc - 1, 0, -1):
        for hh in heads:
            ktn = (kt_chunk(hh, n).astype(F32) * kw_b[hh]).astype(BF16)
            state[hh] = state[hh] * dec_b[hh] + _dot(ktn, v_chunk(hh, n))
            b_scr[hh, n - 1] = state[hh]

    state = [s_ref[0, hh, 0:dk, :] for hh in heads]
    for n in range(nc):
        for hh in heads:
            qn = q_ref[0, hh, n * c:(n + 1) * c, :]
            ktn = kt_chunk(hh, n)
            vn = v_chunk(hh, n)
            scores = (_dot(qn, ktn) * decay_scr[hh]).astype(BF16)
            qf = qn.astype(F32)
            o = _dot(scores, vn)
            o = o + _dot((qf * qw_scr[hh, 0]).astype(BF16), state[hh].astype(BF16))
            o = o + _dot((qf * qw_scr[hh, 1]).astype(BF16), b_scr[hh, n].astype(BF16))
            state[hh] = state[hh] * dec_f[hh] + _dot((ktn.astype(F32) * kw_f[hh]).astype(BF16), vn)
            o = o * lax.rsqrt(jnp.mean(o * o, axis=-1, keepdims=True) + EPS)
            gate = g_ref[0, n * c:(n + 1) * c, hh * dv:(hh + 1) * dv].astype(F32)
            o_ref[0, n * c:(n + 1) * c, hh * dv:(hh + 1) * dv] = (_silu(gate) * o).astype(BF16)


def _ret_call(lg, q, kt, v, g, s, w_up, w_dn, w_out, cc, w_mod, b_mod):
    dk, dv = RET_QK_DIM, RET_V_DIM
    hp = RET_HEADS_PER_STEP
    assert RET_HEADS == hp

    def row_slab(w):
        return pl.BlockSpec((w.shape[0] // BATCH, w.shape[1]), lambda h, b: (b, 0))

    up_slab, dn_slab, out_slab = row_slab(w_up), row_slab(w_dn), row_slab(w_out)
    mcols = MOD_LATE_COLS // BATCH
    mod_first = MOD_EARLY_COLS // mcols
    mod_slab = pl.BlockSpec((MOD_ROWS, mcols), lambda h, b: (0, b))
    return pl.pallas_call(
        _ret_kernel,
        grid=(RET_HEADS // hp, BATCH),
        in_specs=[pl.BlockSpec(memory_space=pltpu.SMEM),
                  pl.BlockSpec((1, hp, SEQ, dk), lambda h, b: (b, h, 0, 0)),
                  pl.BlockSpec((1, hp * dk, SEQ), lambda h, b: (b, h, 0)),
                  pl.BlockSpec((1, SEQ, hp * dv), lambda h, b: (b, 0, h)),
                  pl.BlockSpec((1, SEQ, hp * dv), lambda h, b: (b, 0, h)),
                  pl.BlockSpec((1, hp, 2 * dk, dv), lambda h, b: (b, h, 0, 0)),
                  up_slab, dn_slab, out_slab,
                  _resident((MOD_ROWS, D_MODEL)),
                  pl.BlockSpec((D_MODEL, mcols), lambda h, b: (0, mod_first + b)),
                  pl.BlockSpec((1, mcols), lambda h, b: (0, mod_first + b))],
        out_specs=[pl.BlockSpec((1, SEQ, hp * dv), lambda h, b: (b, 0, h)), up_slab, dn_slab, out_slab, mod_slab],
        out_shape=[jax.ShapeDtypeStruct((BATCH, SEQ, RET_WIDTH), BF16),
                   jax.ShapeDtypeStruct(w_up.shape, BF16),
                   jax.ShapeDtypeStruct(w_dn.shape, BF16),
                   jax.ShapeDtypeStruct(w_out.shape, BF16),
                   jax.ShapeDtypeStruct((MOD_ROWS, MOD_LATE_COLS), F32)],
        scratch_shapes=[pltpu.VMEM((hp, SEQ // RET_CHUNK, dk, dv), F32),
                        pltpu.VMEM((hp, RET_CHUNK, RET_CHUNK), F32),
                        pltpu.VMEM((hp, 2, RET_CHUNK, dk), F32)],
        compiler_params=pltpu.CompilerParams(vmem_limit_bytes=VMEM_LIMIT),
        name="ret",
    )(lg, q, kt, v, g, s, w_up, w_dn, w_out, cc, w_mod, b_mod)


def _ffn_kernel(x_ref, xp_ref, xn_ref, yh_ref, yhp_ref, yhn_ref, yr_ref, yrp_ref, yrn_ref,
                g1_ref, sh_ref, sc_ref, g2_ref, n2_ref, nf_ref,
                woh_ref, wor_ref, wup_ref, cw_ref, cb_ref, wdn_ref,
                o_ref, hb_scr, x1_scr, av_scr, ag_scr, act_scr):
    t = TOK_TILE
    i = pl.program_id(1)
    nt = pl.num_programs(1)
    th = t // 2
    tile_halves = (slice(0, th), slice(th, t))

    def mixed(xr, yh, yr, rows=slice(None)):
        return xr[0, rows, :] + g1_ref[0] * (_dot(yh[0, rows, :], woh_ref[...]) + _dot(yr[0, rows, :], wor_ref[...]))

    def hidden(x1):
        return _norm_mod(x1, n2_ref[...], sh_ref[0], sc_ref[0])

    half = (t + 2 * HALO) // 2
    up_halves = (slice(0, half), slice(half, 2 * half))

    def up(slot, cblk, rows):
        c0 = cblk * FFN_COLS
        hb = hb_scr[rows, :]
        av_scr[slot, rows, :] = _dot(hb, wup_ref[:, c0:c0 + FFN_COLS])
        ag_scr[slot, rows, :] = _dot(hb, wup_ref[:, D_FF + c0:D_FF + c0 + FFN_COLS])

    for rows in tile_halves:
        x1_scr[rows, :] = mixed(x_ref, yh_ref, yr_ref, rows)
    hp = hidden(mixed(xp_ref, yhp_ref, yrp_ref))
    hn = hidden(mixed(xn_ref, yhn_ref, yrn_ref))
    hb_scr[0:HALO, :] = jnp.where(i > 0, hp, 0.0).astype(BF16)
    hb_scr[HALO + t:2 * HALO + t, :] = jnp.where(i < nt - 1, hn, 0.0).astype(BF16)
    hb_scr[HALO:HALO + th, :] = hidden(x1_scr[tile_halves[0], :]).astype(BF16)
    up(0, 0, up_halves[0])
    hb_scr[HALO + th:HALO + t, :] = hidden(x1_scr[tile_halves[1], :]).astype(BF16)
    up(0, 0, up_halves[1])

    def conv(scr, slot, col):
        w = cw_ref[:, col:col + FFN_COLS]
        return (scr[slot, HALO - 1:HALO - 1 + t, :] * w[0:1] + scr[slot, HALO:HALO + t, :] * w[1:2]
                + scr[slot, HALO + 1:HALO + 1 + t, :] * w[2:3] + cb_ref[:, col:col + FFN_COLS])

    n_blk = D_FF // FFN_COLS
    for cblk in range(n_blk):
        c0 = cblk * FFN_COLS
        slot = cblk % 2
        if cblk + 1 < n_blk:
            for rows in up_halves:
                up(1 - slot, cblk + 1, rows)
        act_scr[:, c0:c0 + FFN_COLS] = (
            _silu(conv(ag_scr, slot, D_FF + c0)) * conv(av_scr, slot, c0)).astype(BF16)

    down = [_dot(act_scr[rows, :], wdn_ref[...]) for rows in tile_halves]
    for rows, ffn in zip(tile_halves, down):
        x2 = x1_scr[rows, :] + g2_ref[0] * ffn
        o_ref[0, rows, :] = x2 * lax.rsqrt(jnp.mean(x2 * x2, axis=-1, keepdims=True) + EPS) * nf_ref[...]


def _ffn_call(x, y_hy, y_ret, mod3, norm2, norm_f, w_oh, w_or, w_up, conv_w, conv_b, w_dn):
    t = TOK_TILE
    r = t // HALO
    last = SEQ // HALO - 1

    def main(width):
        return pl.BlockSpec((1, t, width), lambda b, i: (b, i, 0))

    def prev(width):
        return pl.BlockSpec((1, HALO, width), lambda b, i: (b, jnp.maximum(i * r - 1, 0), 0))

    def nxt(width):
        return pl.BlockSpec((1, HALO, width), lambda b, i: (b, jnp.minimum((i + 1) * r, last), 0))

    def modrow(k):
        return pl.BlockSpec((1, 1, D_MODEL), lambda b, i: (b, 0, k))

    return pl.pallas_call(
        _ffn_kernel,
        grid=(BATCH, SEQ // t),
        in_specs=[main(D_MODEL), prev(D_MODEL), nxt(D_MODEL),
                  main(HY_WIDTH), prev(HY_WIDTH), nxt(HY_WIDTH),
                  main(RET_WIDTH), prev(RET_WIDTH), nxt(RET_WIDTH),
                  modrow(0), modrow(1), modrow(2), modrow(3),
                  _resident((1, D_MODEL)), _resident((1, D_MODEL)),
                  _resident_rows(0, HY_WIDTH, D_MODEL), _resident_rows(HY_WIDTH, RET_WIDTH, D_MODEL),
                  _resident((D_MODEL, 2 * D_FF)),
                  _resident((3, 2 * D_FF)), _resident((1, 2 * D_FF)),
                  _resident((D_FF, D_MODEL))],
        out_specs=pl.BlockSpec((1, t, D_MODEL), lambda b, i: (b, i, 0)),
        out_shape=jax.ShapeDtypeStruct((BATCH, SEQ, D_MODEL), F32),
        scratch_shapes=[pltpu.VMEM((t + 2 * HALO, D_MODEL), BF16),
                        pltpu.VMEM((t, D_MODEL), F32),
                        pltpu.VMEM((2, t + 2 * HALO, FFN_COLS), F32),
                        pltpu.VMEM((2, t + 2 * HALO, FFN_COLS), F32),
                        pltpu.VMEM((t, D_FF), BF16)],
        compiler_params=pltpu.CompilerParams(vmem_limit_bytes=VMEM_LIMIT),
        name="ffn",
    )(x, x, x, y_hy, y_hy, y_hy, y_ret, y_ret, y_ret, mod3, mod3, mod3, mod3,
      norm2, norm_f, w_oh, w_or, w_up, conv_w, conv_b, w_dn)


def kernel(x, c, ctx, c_ctx, w_mod, b_mod, norm1, w_in, hy_conv_w, hy_conv_b, hy_w1, hy_b1, hy_f1,
           hy_w2, hy_b2, hy_f2, hy_w3, hy_bias, ret_logit_f, ret_logit_b, w_out, norm2,
           ffn_w_up, ffn_conv_w, ffn_conv_b, ffn_w_down, norm_f):
    layer = 0
    rope = tuple(jnp.asarray(a) for a in _rope_tables())
    zt, t_row, absdelta = (jnp.asarray(a) for a in _filter_features())
    fwd_np, inv_np = _dft_matrices()
    fwd_u = jnp.asarray(fwd_np[:, :CONV_BLOCK]).astype(BF16)
    inv = jnp.asarray(inv_np).astype(BF16)

    w_in_b = w_in[layer].astype(BF16)
    w_kt = w_in_b[:, K_OFF:V_OFF].T
    row = lambda a: a.reshape(1, -1)
    col = lambda a: a.reshape(-1, 1)
    w1p = jnp.pad(hy_w1[layer], ((0, HY_FILTER_WIDTH - hy_w1.shape[1]), (0, 0)))
    lg = jnp.stack([jax.nn.log_sigmoid(ret_logit_f[layer].astype(F32)),
                    jax.nn.log_sigmoid(ret_logit_b[layer].astype(F32))])

    cc = jnp.concatenate([c, c_ctx[None, :], jnp.zeros((MOD_ROWS - BATCH - 1, D_MODEL), F32)], axis=0)
    mod3 = _mod_call(cc, w_mod[layer], row(b_mod[layer])).reshape(MOD_ROWS, 1, MOD_EARLY_COLS)
    norm1_r = row(norm1[layer])

    s_ctx = _ctx_call(lg, ctx, mod3, norm1_r, w_kt, w_in_b)
    zhy, q, kt, v, g = _inproj_call(x, mod3, norm1_r, w_in_b, w_kt, rope)

    k_two_sided = _filter_mlp_call(zt, t_row, w1p.T, col(hy_b1[layer]), col(hy_f1[layer]), hy_w2[layer].T,
                                   col(hy_b2[layer]), col(hy_f2[layer]), hy_w3[layer].T, absdelta)
    g_all = _filter_dft_call(k_two_sided, fwd_u)
    conv_w, conv_b = hy_conv_w[layer], row(hy_conv_b[layer])
    y1 = _hyena_call(zhy, 0, zhy, 1, conv_w, conv_b, hy_bias[layer][0:1], g_all, 0, fwd_u, inv, True)
    y_hy = _hyena_call(y1, 0, zhy, 2, conv_w, conv_b, hy_bias[layer][1:2], g_all, 1, fwd_u, inv, False)

    y_ret, w_up, w_dn, w_out_b, mod_late = _ret_call(lg, q, kt, v, g, s_ctx, ffn_w_up[layer], ffn_w_down[layer],
                                                     w_out[layer], cc, w_mod[layer], row(b_mod[layer]))
    mod_late3 = mod_late.reshape(MOD_ROWS, 1, MOD_LATE_COLS)

    return _ffn_call(x, y_hy, y_ret, mod_late3, row(norm2[layer]), row(norm_f), w_out_b, w_out_b, w_up,
                     ffn_conv_w[layer], row(ffn_conv_b[layer]), w_dn)
```

```python
import functools
import math

import numpy as np
import jax
import jax.numpy as jnp
from jax import lax
from jax.experimental import pallas as pl
from jax.experimental.pallas import tpu as pltpu

F32 = jnp.float32
BF16 = jnp.bfloat16

D_MODEL = 1024
BATCH = 8
SEQ = 2048
CTX_LEN = 256
GRID_W = 64
HY_WIDTH = 512
HY_ORDER = 2
HY_EMB_BANDS = 16
HY_FILTER_WIDTH = 64
HY_FAST_DECAY = 0.3
HY_SLOW_DECAY = 1.5
HY_TARGET = 1e-2
RET_WIDTH = 512
RET_HEADS = 4
RET_QK_DIM = 64
RET_V_DIM = 128
ROPE_BASE = 10000.0
D_FF = 2816
EPS = 1e-6
HY_COLS = (HY_ORDER + 1) * HY_WIDTH
Q_OFF = HY_COLS
K_OFF = Q_OFF + RET_HEADS * RET_QK_DIM
V_OFF = K_OFF + RET_HEADS * RET_QK_DIM
G_OFF = V_OFF + RET_WIDTH
K_SCALE = RET_QK_DIM ** -0.5

MOD_ROWS = 16
MOD_EARLY_COLS = 2 * D_MODEL
MOD_LATE_COLS = 4 * D_MODEL
TOK_TILE = 512
IN_TILE = 1024
IN_PIECE = 256
SUBLANES = 8
LANES = 128
HALO = 16
CONV_BLOCK = 512
N_CONV_BLOCKS = SEQ // CONV_BLOCK
FREQ_ROWS = 16
FREQ_LANES = 128
DFT_ROWS = 256
HY_HALF = HY_WIDTH // 2
SPECTRAL_DTYPE = F32
CTX_PER_STEP = 4
RET_CHUNK = 256
RET_HEADS_PER_STEP = 4
FFN_COLS = 256
VMEM_LIMIT = 56 * 1024 * 1024
HYENA_VMEM_LIMIT = 62 * 1024 * 1024

_NT = (((1,), (1,)), ((), ()))


def _dot(a, b):
    return jnp.dot(a, b, preferred_element_type=F32)


def _dot_hi(a, b):
    return jnp.dot(a, b, preferred_element_type=F32, precision=lax.Precision.HIGHEST)


def _silu(x):
    return x * (1.0 / (1.0 + jnp.exp(-x)))


def _norm_mod(x, gain, shift, scale):
    y = x * lax.rsqrt(jnp.mean(x * x, axis=-1, keepdims=True) + EPS)
    return (y * gain) * (1.0 + scale) + shift


def _resident(shape):
    nd = len(shape)
    return pl.BlockSpec(shape, lambda *_: (0,) * nd, pipeline_mode=pl.Buffered(1))


def _resident_cols(rows, col0, width):
    assert col0 % width == 0
    return pl.BlockSpec((rows, width), lambda *_: (0, col0 // width), pipeline_mode=pl.Buffered(1))


def _resident_rows(row0, height, cols):
    assert row0 % height == 0
    return pl.BlockSpec((height, cols), lambda *_: (row0 // height, 0), pipeline_mode=pl.Buffered(1))


@functools.lru_cache(maxsize=None)
def _rope_tables():
    pos = np.arange(SEQ)
    row = (pos // GRID_W).astype(np.float64)
    col = (pos % GRID_W).astype(np.float64)
    quarter = RET_QK_DIM // 4
    inv_freq = ROPE_BASE ** (-np.arange(quarter, dtype=np.float64) / quarter)
    ang = np.concatenate([row[:, None] * inv_freq, col[:, None] * inv_freq], axis=-1)
    cos, sin = np.cos(ang), np.sin(ang)
    cos_h = np.concatenate([cos, cos], axis=-1)
    sin_h = np.concatenate([-sin, sin], axis=-1)
    cos_t = np.tile(cos_h, (1, RET_HEADS))
    sin_t = np.tile(sin_h, (1, RET_HEADS))
    return (cos_t.astype(np.float32), sin_t.astype(np.float32),
            np.ascontiguousarray((cos_t * K_SCALE).T).astype(np.float32),
            np.ascontiguousarray((sin_t * K_SCALE).T).astype(np.float32))


@functools.lru_cache(maxsize=None)
def _filter_features():
    lag = np.abs(np.arange(2 * SEQ) - SEQ).astype(np.float64)
    t = lag / (SEQ - 1)
    bands = np.linspace(1e-4, HY_EMB_BANDS - 1, HY_EMB_BANDS)
    ang = 2.0 * math.pi * lag[:, None] * bands[None, :] / SEQ
    z = np.concatenate([t[:, None], np.cos(ang), -np.sin(ang)], axis=-1)
    zp = np.zeros((2 * SEQ, HY_FILTER_WIDTH), np.float64)
    zp[:, :z.shape[1]] = z
    max_decay = math.log(HY_TARGET) / HY_FAST_DECAY
    min_decay = math.log(HY_TARGET) / HY_SLOW_DECAY
    absdelta = np.abs(np.linspace(min_decay, max_decay, HY_WIDTH))[:, None]
    return (np.ascontiguousarray(zp.T).astype(np.float32), t[None, :].astype(np.float32),
            absdelta.astype(np.float32))


@functools.lru_cache(maxsize=None)
def _dft_matrices():
    p = CONV_BLOCK
    n = 2 * p
    f = np.arange(p, dtype=np.float64)[:, None] + 0.5
    t = np.arange(n, dtype=np.float64)[None, :]
    theta = 2.0 * math.pi * f * t / n
    fwd = np.concatenate([np.cos(theta), -np.sin(theta)], axis=0)
    th_out = theta[:, p:].T
    inv = np.concatenate([np.cos(th_out), -np.sin(th_out)], axis=1) / p
    return fwd.astype(np.float32), inv.astype(np.float32)


def _mod_kernel(c_ref, w_ref, b_ref, o_ref):
    s = _silu(c_ref[...]).astype(BF16)
    o_ref[...] = _dot(s, w_ref[...].astype(BF16)) + b_ref[...]


def _mod_call(cc, w_mod, b_mod):
    ncol = MOD_EARLY_COLS
    blk = ncol // 2
    return pl.pallas_call(
        _mod_kernel,
        grid=(ncol // blk,),
        in_specs=[pl.BlockSpec((MOD_ROWS, D_MODEL), lambda j: (0, 0)),
                  pl.BlockSpec((D_MODEL, blk), lambda j: (0, j)),
                  pl.BlockSpec((1, blk), lambda j: (0, j))],
        out_specs=pl.BlockSpec((MOD_ROWS, blk), lambda j: (0, j)),
        out_shape=jax.ShapeDtypeStruct((MOD_ROWS, ncol), F32),
        compiler_params=pltpu.CompilerParams(vmem_limit_bytes=VMEM_LIMIT),
        name="mod",
    )(cc, w_mod, b_mod)


def _ctx_kernel(lg_ref, x_ref, sh_ref, sc_ref, n1_ref, wkt_ref, wv_ref, s_ref):
    pos = lax.broadcasted_iota(jnp.int32, (1, CTX_LEN), 1).astype(F32)
    w_f = [jnp.exp(lg_ref[0, hh] * (CTX_LEN - 1.0 - pos)) for hh in range(RET_HEADS)]
    w_b = [jnp.exp(lg_ref[1, hh] * pos) for hh in range(RET_HEADS)]
    for bb in range(CTX_PER_STEP):
        h = _norm_mod(x_ref[bb], n1_ref[...], sh_ref[0], sc_ref[0]).astype(BF16)
        kt = lax.dot_general(wkt_ref[...], h, _NT, preferred_element_type=F32) * K_SCALE
        v = _dot(h, wv_ref[...])
        for hh in range(RET_HEADS):
            kth = kt[hh * RET_QK_DIM:(hh + 1) * RET_QK_DIM, :]
            vh = v[:, hh * RET_V_DIM:(hh + 1) * RET_V_DIM].astype(BF16)
            s_ref[bb, hh, 0:RET_QK_DIM, :] = _dot((kth * w_f[hh]).astype(BF16), vh)
            s_ref[bb, hh, RET_QK_DIM:2 * RET_QK_DIM, :] = _dot((kth * w_b[hh]).astype(BF16), vh)


def _ctx_call(lg, ctx, mod3, norm1, w_kt, w_v):
    return pl.pallas_call(
        _ctx_kernel,
        grid=(BATCH // CTX_PER_STEP,),
        in_specs=[pl.BlockSpec(memory_space=pltpu.SMEM),
                  pl.BlockSpec((CTX_PER_STEP, CTX_LEN, D_MODEL), lambda b: (b, 0, 0)),
                  pl.BlockSpec((1, 1, D_MODEL), lambda b: (BATCH, 0, 0)),
                  pl.BlockSpec((1, 1, D_MODEL), lambda b: (BATCH, 0, 1)),
                  _resident((1, D_MODEL)),
                  _resident((RET_HEADS * RET_QK_DIM, D_MODEL)),
                  _resident_cols(D_MODEL, V_OFF, RET_WIDTH)],
        out_specs=pl.BlockSpec((CTX_PER_STEP, RET_HEADS, 2 * RET_QK_DIM, RET_V_DIM), lambda b: (b, 0, 0, 0)),
        out_shape=jax.ShapeDtypeStruct((BATCH, RET_HEADS, 2 * RET_QK_DIM, RET_V_DIM), F32),
        compiler_params=pltpu.CompilerParams(vmem_limit_bytes=VMEM_LIMIT),
        name="ctx",
    )(lg, ctx, mod3, mod3, norm1, w_kt, w_v)


def _swap_halves(x, axis):
    n = x.shape[axis]
    half = RET_QK_DIM // 2
    idx = lax.broadcasted_iota(jnp.int32, x.shape, axis)
    first = (idx & (RET_QK_DIM - 1)) < half
    return jnp.where(first, pltpu.roll(x, n - half, axis), pltpu.roll(x, half, axis))


def _inproj_kernel(x_ref, sh_ref, sc_ref, n1_ref, why_ref, wq_ref, wv_ref, wg_ref, wkt_ref,
                   cq_ref, sq_ref, ck_ref, sk_ref,
                   zhy_ref, q_ref, kt_ref, v_ref, g_ref):
    pieces = [slice(r, r + IN_PIECE) for r in range(0, IN_TILE, IN_PIECE)]
    hbs = [_norm_mod(x_ref[0, rows, :], n1_ref[...], sh_ref[0], sc_ref[0]).astype(BF16) for rows in pieces]
    for rows, hb in zip(pieces, hbs):
        zhy_ref[0, rows, :] = _dot(hb, why_ref[...]).astype(BF16)
        v_ref[0, rows, :] = _dot(hb, wv_ref[...]).astype(BF16)
        g_ref[0, rows, :] = _dot(hb, wg_ref[...]).astype(BF16)
        q = _dot(hb, wq_ref[...])
        q = q * cq_ref[rows, :] + _swap_halves(q, 1) * sq_ref[rows, :]
        for hh in range(RET_HEADS):
            q_ref[0, hh, rows, :] = q[:, hh * RET_QK_DIM:(hh + 1) * RET_QK_DIM].astype(BF16)
        kt = lax.dot_general(wkt_ref[...], hb, _NT, preferred_element_type=F32)
        kt = kt * ck_ref[:, rows] + _swap_halves(kt, 0) * sk_ref[:, rows]
        kt_ref[0, :, rows] = kt.astype(BF16)


def _inproj_call(x, mod3, norm1, w_in, w_kt, rope):
    cq, sq, ck, sk = rope
    t = IN_TILE
    qk = RET_HEADS * RET_QK_DIM
    return pl.pallas_call(
        _inproj_kernel,
        grid=(BATCH, SEQ // t),
        in_specs=[pl.BlockSpec((1, t, D_MODEL), lambda b, i: (b, i, 0)),
                  pl.BlockSpec((1, 1, D_MODEL), lambda b, i: (b, 0, 0)),
                  pl.BlockSpec((1, 1, D_MODEL), lambda b, i: (b, 0, 1)),
                  _resident((1, D_MODEL)),
                  _resident_cols(D_MODEL, 0, HY_COLS),
                  _resident_cols(D_MODEL, Q_OFF, qk),
                  _resident_cols(D_MODEL, V_OFF, RET_WIDTH),
                  _resident_cols(D_MODEL, G_OFF, RET_WIDTH),
                  _resident((qk, D_MODEL)),
                  pl.BlockSpec((t, qk), lambda b, i: (i, 0)),
                  pl.BlockSpec((t, qk), lambda b, i: (i, 0)),
                  pl.BlockSpec((qk, t), lambda b, i: (0, i)),
                  pl.BlockSpec((qk, t), lambda b, i: (0, i))],
        out_specs=[pl.BlockSpec((1, t, HY_COLS), lambda b, i: (b, i, 0)),
                   pl.BlockSpec((1, RET_HEADS, t, RET_QK_DIM), lambda b, i: (b, 0, i, 0)),
                   pl.BlockSpec((1, qk, t), lambda b, i: (b, 0, i)),
                   pl.BlockSpec((1, t, RET_WIDTH), lambda b, i: (b, i, 0)),
                   pl.BlockSpec((1, t, RET_WIDTH), lambda b, i: (b, i, 0))],
        out_shape=[jax.ShapeDtypeStruct((BATCH, SEQ, HY_COLS), BF16),
                   jax.ShapeDtypeStruct((BATCH, RET_HEADS, SEQ, RET_QK_DIM), BF16),
                   jax.ShapeDtypeStruct((BATCH, qk, SEQ), BF16),
                   jax.ShapeDtypeStruct((BATCH, SEQ, RET_WIDTH), BF16),
                   jax.ShapeDtypeStruct((BATCH, SEQ, RET_WIDTH), BF16)],
        compiler_params=pltpu.CompilerParams(vmem_limit_bytes=VMEM_LIMIT),
        name="inproj",
    )(x, mod3, mod3, norm1, w_in, w_in, w_in, w_in, w_kt, cq, sq, ck, sk)


def _filter_mlp_kernel(zt_ref, t_ref, w1t_ref, b1_ref, f1_ref, w2t_ref, b2_ref, f2_ref, w3t_ref, adel_ref,
                       kt_ref):
    hid = jnp.sin(f1_ref[...] * (_dot_hi(w1t_ref[...], zt_ref[...]) + b1_ref[...]))
    hid = jnp.sin(f2_ref[...] * (_dot_hi(w2t_ref[...], hid) + b2_ref[...])).astype(BF16)
    c = HY_WIDTH
    for half in range(2):
        lags = slice(half * SEQ, (half + 1) * SEQ)
        window = jnp.exp(-adel_ref[...] * t_ref[:, lags])
        direction = 1 - half
        for o in range(HY_ORDER):
            r0 = (direction * HY_ORDER + o) * c
            w3 = w3t_ref[r0:r0 + c, :].astype(BF16)
            kt_ref[o, :, lags] = (_dot(w3, hid[:, lags]) * window).astype(BF16)


def _filter_mlp_call(zt, t_row, w1t, b1, f1, w2t, b2, f2, w3t, absdelta):
    fw = HY_FILTER_WIDTH
    n_out = 2 * HY_ORDER * HY_WIDTH
    return pl.pallas_call(
        _filter_mlp_kernel,
        grid=(1,),
        in_specs=[_resident((fw, 2 * SEQ)), _resident((1, 2 * SEQ)),
                  _resident((fw, fw)), _resident((fw, 1)), _resident((fw, 1)),
                  _resident((fw, fw)), _resident((fw, 1)), _resident((fw, 1)),
                  _resident((n_out, fw)), _resident((HY_WIDTH, 1))],
        out_specs=pl.BlockSpec((HY_ORDER, HY_WIDTH, 2 * SEQ), lambda i: (0, 0, 0)),
        out_shape=jax.ShapeDtypeStruct((HY_ORDER, HY_WIDTH, 2 * SEQ), BF16),
        compiler_params=pltpu.CompilerParams(vmem_limit_bytes=VMEM_LIMIT),
        name="filter_mlp",
    )(zt, t_row, w1t, b1, f1, w2t, b2, f2, w3t, absdelta)


_FILTER_PLANES = (
    {0: 1},
    {-1: 1, 0: -1},
    {1: 1, 0: -1},
    {-2: 1, 0: -1},
    {-3: 1, -1: -1, -2: -1, 0: 1},
    {-1: 1, 1: -1, -2: -1, 0: 1},
    {2: 1, 0: -1},
    {1: 1, -1: -1, 2: -1, 0: 1},
    {3: 1, 1: -1, 2: -1, 0: 1},
)
DFT_BLOCKS_PER_STEP = 2
PLANES_PER_STEP = 3


def _filter_dft_kernel(kt_ref, fwd_ref, h_ref, t_scr):
    p = CONV_BLOCK
    n_dft_steps = 2 * N_CONV_BLOCKS // DFT_BLOCKS_PER_STEP
    step = pl.program_id(1)

    for s in range(n_dft_steps):
        @pl.when(step == s)
        def _(s=s):
            for k in range(DFT_BLOCKS_PER_STEP):
                t_scr[s * DFT_BLOCKS_PER_STEP + k] = lax.dot_general(
                    fwd_ref[...], kt_ref[0, :, k * p:(k + 1) * p], _NT, preferred_element_type=F32)
            for k in range(PLANES_PER_STEP):
                h_ref[0, k] = t_scr[s * DFT_BLOCKS_PER_STEP].astype(SPECTRAL_DTYPE)

    row = lax.broadcasted_iota(jnp.int32, (p, 1), 0)
    sign = (1 - 2 * (row & 1)).astype(F32)
    re, im = slice(0, p), slice(p, 2 * p)

    def plane(k, coefs):
        def comb(shift, rows):
            acc = None
            for d, c in coefs.items():
                term = t_scr[d + N_CONV_BLOCKS - 1 + shift, rows, :]
                if acc is None:
                    acc = term
                else:
                    acc = acc + term if c > 0 else acc - term
            return acc

        h_ref[0, k, re, :] = (comb(0, re) + sign * comb(1, im)).astype(SPECTRAL_DTYPE)
        h_ref[0, k, im, :] = (comb(0, im) - sign * comb(1, re)).astype(SPECTRAL_DTYPE)

    for s in range(len(_FILTER_PLANES) // PLANES_PER_STEP):
        @pl.when(step == n_dft_steps + s)
        def _(s=s):
            for k in range(PLANES_PER_STEP):
                plane(k, _FILTER_PLANES[s * PLANES_PER_STEP + k])


def _filter_dft_call(kt, fwd_u):
    p = CONV_BLOCK
    n_blk = 2 * N_CONV_BLOCKS
    n_planes = len(_FILTER_PLANES)
    n_dft_steps = n_blk // DFT_BLOCKS_PER_STEP
    return pl.pallas_call(
        _filter_dft_kernel,
        grid=(HY_ORDER, n_dft_steps + n_planes // PLANES_PER_STEP),
        in_specs=[pl.BlockSpec((1, HY_WIDTH, DFT_BLOCKS_PER_STEP * p),
                               lambda o, s: (o, 0, jnp.minimum(s, n_dft_steps - 1))),
                  _resident((2 * p, p))],
        out_specs=pl.BlockSpec((1, PLANES_PER_STEP, 2 * p, HY_WIDTH),
                               lambda o, s: (o, jnp.maximum(s - n_dft_steps, 0), 0, 0)),
        out_shape=jax.ShapeDtypeStruct((HY_ORDER, n_planes, 2 * p, HY_WIDTH), SPECTRAL_DTYPE),
        scratch_shapes=[pltpu.VMEM((n_blk, 2 * p, HY_WIDTH), F32)],
        compiler_params=pltpu.CompilerParams(vmem_limit_bytes=VMEM_LIMIT),
        name="filter_dft",
    )(kt, fwd_u)


def _conv3_rows(ref, j, n_blocks, rows, ch, w, b):
    main = ref[0, j * rows:(j + 1) * rows, ch].astype(F32)
    cols = main.shape[1]
    zeros = jnp.zeros((HALO, cols), F32)
    prev = ref[0, j * rows - HALO:j * rows, ch].astype(F32) if j > 0 else zeros
    nxt = ref[0, (j + 1) * rows:(j + 1) * rows + HALO, ch].astype(F32) if j < n_blocks - 1 else zeros
    ext = jnp.concatenate([prev, main, nxt], axis=0)
    n = rows + 2 * HALO
    before = pltpu.roll(ext, 1, 0)[HALO:HALO + rows]
    after = pltpu.roll(ext, n - 1, 0)[HALO:HALO + rows]
    return before * w[0:1, ch] + main * w[1:2, ch] + after * w[2:3, ch] + b[:, ch]


def _hyena_kernel(u_ref, zg_ref, cwu_ref, cbu_ref, cwg_ref, cbg_ref, skip_ref, h_ref, fwd_ref, inv_ref,
                  o_ref, uf_scr, ub_scr, y_scr, *, conv_u):
    p = CONV_BLOCK
    nb = N_CONV_BLOCKS
    assert nb == 4
    halves = [slice(h * HY_HALF, (h + 1) * HY_HALF) for h in range(HY_WIDTH // HY_HALF)]

    def prepare(j, ch):
        if conv_u:
            ub_scr[j * p:(j + 1) * p, ch] = _conv3_rows(
                u_ref, j, nb, p, ch, cwu_ref[...], cbu_ref[...]).astype(BF16)

    def u_block(j, ch):
        return ub_scr[j * p:(j + 1) * p, ch] if conv_u else u_ref[0, j * p:(j + 1) * p, ch]

    def forward(j, ch):
        for r in range(2 * p // DFT_ROWS):
            rows = slice(r * DFT_ROWS, (r + 1) * DFT_ROWS)
            uf_scr[j, rows, ch] = _dot(fwd_ref[rows, :], u_block(j, ch)).astype(SPECTRAL_DTYPE)

    def cadd(a, b):
        return a[0] + b[0], a[1] + b[1]

    def cmul(m, x):
        return m[0] * x[0] - m[1] * x[1], m[0] * x[1] + m[1] * x[0]

    def toeplitz2(k0, x0, x1, re, im, ln):
        m0, mu, ml = ((h_ref[0, k0 + t, re, ln], h_ref[0, k0 + t, im, ln]) for t in range(3))
        p1 = cmul(m0, cadd(x0, x1))
        return cadd(p1, cmul(mu, x1)), cadd(p1, cmul(ml, x0))

    def spectral_tile(r, ln):
        re = slice(r * FREQ_ROWS, (r + 1) * FREQ_ROWS)
        im = slice(p + r * FREQ_ROWS, p + (r + 1) * FREQ_ROWS)
        u = [(uf_scr[j, re, ln], uf_scr[j, im, ln]) for j in range(nb)]
        d0, d1 = toeplitz2(0, cadd(u[0], u[2]), cadd(u[1], u[3]), re, im, ln)
        b0, b1 = toeplitz2(3, u[2], u[3], re, im, ln)
        c0, c1 = toeplitz2(6, u[0], u[1], re, im, ln)
        for i, yi in enumerate((cadd(d0, b0), cadd(d1, b1), cadd(d0, c0), cadd(d1, c1))):
            y_scr[i, re, ln] = yi[0].astype(BF16)
            y_scr[i, im, ln] = yi[1].astype(BF16)

    def spectral_tiles(ch):
        lanes = [slice(l, l + FREQ_LANES) for l in range(ch.start, ch.stop, FREQ_LANES)]
        return [functools.partial(spectral_tile, r, ln) for r in range(p // FREQ_ROWS) for ln in lanes]

    def inverse(i, ch):
        gate = _conv3_rows(zg_ref, i, nb, p, ch, cwg_ref[...], cbg_ref[...])
        y = _dot(inv_ref[...], y_scr[i, :, ch])
        o_ref[0, i * p:(i + 1) * p, ch] = (
            gate * (y + u_block(i, ch).astype(F32) * skip_ref[:, ch])).astype(BF16)

    def run(stages, tiles):
        per = -(-len(tiles) // len(stages)) if stages else 0
        for k, stage in enumerate(stages):
            stage()
            for tile in tiles[k * per:(k + 1) * per]:
                tile()
        if not stages:
            for tile in tiles:
                tile()

    def forward_stages(ch):
        def stage(j):
            if j == 0:
                prepare(0, ch)
            if j + 1 < nb:
                prepare(j + 1, ch)
            forward(j, ch)
        return [functools.partial(stage, j) for j in range(nb)]

    def inverse_stages(ch):
        return [functools.partial(inverse, i, ch) for i in range(nb)]

    first, second = halves
    run(forward_stages(first), [])
    run(forward_stages(second), spectral_tiles(first))
    run(inverse_stages(first), spectral_tiles(second))
    run(inverse_stages(second), [])


def _hyena_call(u, u_col, zhy, gate_col, conv_w, conv_b, skip, g_all, order, fwd_u, inv, conv_u):
    p = CONV_BLOCK
    c = HY_WIDTH
    ucol = u_col if conv_u else 0
    cwu = conv_w[:, ucol * c:(ucol + 1) * c]
    cbu = conv_b[:, ucol * c:(ucol + 1) * c]
    cwg = conv_w[:, gate_col * c:(gate_col + 1) * c]
    cbg = conv_b[:, gate_col * c:(gate_col + 1) * c]
    return pl.pallas_call(
        functools.partial(_hyena_kernel, conv_u=conv_u),
        grid=(BATCH,),
        in_specs=[pl.BlockSpec((1, SEQ, c), lambda b: (b, 0, u_col)),
                  pl.BlockSpec((1, SEQ, c), lambda b: (b, 0, gate_col)),
                  _resident((3, c)), _resident((1, c)), _resident((3, c)), _resident((1, c)),
                  _resident((1, c)),
                  pl.BlockSpec((1, len(_FILTER_PLANES), 2 * p, c), lambda b: (order, 0, 0, 0),
                               pipeline_mode=pl.Buffered(1)),
                  _resident((2 * p, p)),
                  _resident((p, 2 * p))],
        out_specs=pl.BlockSpec((1, SEQ, c), lambda b: (b, 0, 0)),
        out_shape=jax.ShapeDtypeStruct((BATCH, SEQ, c), BF16),
        scratch_shapes=[pltpu.VMEM((N_CONV_BLOCKS, 2 * p, c), SPECTRAL_DTYPE),
                        pltpu.VMEM((SEQ, c) if conv_u else (SUBLANES * 2, LANES), BF16),
                        pltpu.VMEM((N_CONV_BLOCKS, 2 * p, c), BF16)],
        compiler_params=pltpu.CompilerParams(vmem_limit_bytes=HYENA_VMEM_LIMIT),
        name="hyena%d" % order,
    )(u, zhy, cwu, cbu, cwg, cbg, skip, g_all, fwd_u, inv)


def _ret_kernel(lg_ref, q_ref, kt_ref, v_ref, g_ref, s_ref, wup_ref, wdn_ref, wout_ref, cc_ref, wmod_ref, bmod_ref,
                o_ref, wup_b_ref, wdn_b_ref, wout_b_ref, modl_ref, b_scr, decay_scr, qw_scr):
    wup_b_ref[...] = wup_ref[...].astype(BF16)
    wdn_b_ref[...] = wdn_ref[...].astype(BF16)
    wout_b_ref[...] = wout_ref[...].astype(BF16)
    modl_ref[...] = _dot(_silu(cc_ref[...]).astype(BF16), wmod_ref[...].astype(BF16)) + bmod_ref[...]
    c = RET_CHUNK
    nc = SEQ // c
    dk = RET_QK_DIM
    dv = RET_V_DIM
    heads = range(RET_HEADS_PER_STEP)
    lg_f = [lg_ref[0, pl.program_id(0) * RET_HEADS_PER_STEP + hh] for hh in heads]
    lg_b = [lg_ref[1, pl.program_id(0) * RET_HEADS_PER_STEP + hh] for hh in heads]

    @pl.when(pl.program_id(1) == 0)
    def _():
        ii = lax.broadcasted_iota(jnp.int32, (c, c), 0).astype(F32)
        jj = lax.broadcasted_iota(jnp.int32, (c, c), 1).astype(F32)
        dif = ii - jj
        pos_q = lax.broadcasted_iota(jnp.int32, (c, dk), 0).astype(F32)
        for hh in heads:
            decay_scr[hh] = jnp.where(dif >= 0.0, jnp.exp(lg_f[hh] * jnp.maximum(dif, 0.0)),
                                      jnp.exp(lg_b[hh] * jnp.maximum(-dif, 0.0)))
            qw_scr[hh, 0] = jnp.exp(lg_f[hh] * (pos_q + 1.0))
            qw_scr[hh, 1] = jnp.exp(lg_b[hh] * (c - pos_q))

    pos_r = lax.broadcasted_iota(jnp.int32, (1, c), 1).astype(F32)
    ones = jnp.ones((1, dv), F32)
    kw_f = [jnp.exp(lg_f[hh] * (c - 1.0 - pos_r)) for hh in heads]
    kw_b = [jnp.exp(lg_b[hh] * pos_r) for hh in heads]
    dec_f = [jnp.exp(lg_f[hh] * float(c) * ones) for hh in heads]
    dec_b = [jnp.exp(lg_b[hh] * float(c) * ones) for hh in heads]

    def kt_chunk(hh, n):
        return kt_ref[0, hh * dk:(hh + 1) * dk, n * c:(n + 1) * c]

    def v_chunk(hh, n):
        return v_ref[0, n * c:(n + 1) * c, hh * dv:(hh + 1) * dv]

    state = [s_ref[0, hh, dk:2 * dk, :] for hh in heads]
    for hh in heads:
        b_scr[hh, nc - 1] = state[hh]
    for n in range(nc - 1, 0, -1):
        for hh in heads:
            ktn = (kt_chunk(hh, n).astype(F32) * kw_b[hh]).astype(BF16)
            state[hh] = state[hh] * dec_b[hh] + _dot(ktn, v_chunk(hh, n))
            b_scr[hh, n - 1] = state[hh]

    state = [s_ref[0, hh, 0:dk, :] for hh in heads]
    for n in range(nc):
        for hh in heads:
            qn = q_ref[0, hh, n * c:(n + 1) * c, :]
            ktn = kt_chunk(hh, n)
            vn = v_chunk(hh, n)
            scores = (_dot(qn, ktn) * decay_scr[hh]).astype(BF16)
            qf = qn.astype(F32)
            o = _dot(scores, vn)
            o = o + _dot((qf * qw_scr[hh, 0]).astype(BF16), state[hh].astype(BF16))
            o = o + _dot((qf * qw_scr[hh, 1]).astype(BF16), b_scr[hh, n].astype(BF16))
            state[hh] = state[hh] * dec_f[hh] + _dot((ktn.astype(F32) * kw_f[hh]).astype(BF16), vn)
            o = o * lax.rsqrt(jnp.mean(o * o, axis=-1, keepdims=True) + EPS)
            gate = g_ref[0, n * c:(n + 1) * c, hh * dv:(hh + 1) * dv].astype(F32)
            o_ref[0, n * c:(n + 1) * c, hh * dv:(hh + 1) * dv] = (_silu(gate) * o).astype(BF16)


def _ret_call(lg, q, kt, v, g, s, w_up, w_dn, w_out, cc, w_mod, b_mod):
    dk, dv = RET_QK_DIM, RET_V_DIM
    hp = RET_HEADS_PER_STEP
    assert RET_HEADS == hp

    def row_slab(w):
        return pl.BlockSpec((w.shape[0] // BATCH, w.shape[1]), lambda h, b: (b, 0))

    up_slab, dn_slab, out_slab = row_slab(w_up), row_slab(w_dn), row_slab(w_out)
    mcols = MOD_LATE_COLS // BATCH
    mod_first = MOD_EARLY_COLS // mcols
    mod_slab = pl.BlockSpec((MOD_ROWS, mcols), lambda h, b: (0, b))
    return pl.pallas_call(
        _ret_kernel,
        grid=(RET_HEADS // hp, BATCH),
        in_specs=[pl.BlockSpec(memory_space=pltpu.SMEM),
                  pl.BlockSpec((1, hp, SEQ, dk), lambda h, b: (b, h, 0, 0)),
                  pl.BlockSpec((1, hp * dk, SEQ), lambda h, b: (b, h, 0)),
                  pl.BlockSpec((1, SEQ, hp * dv), lambda h, b: (b, 0, h)),
                  pl.BlockSpec((1, SEQ, hp * dv), lambda h, b: (b, 0, h)),
                  pl.BlockSpec((1, hp, 2 * dk, dv), lambda h, b: (b, h, 0, 0)),
                  up_slab, dn_slab, out_slab,
                  _resident((MOD_ROWS, D_MODEL)),
                  pl.BlockSpec((D_MODEL, mcols), lambda h, b: (0, mod_first + b)),
                  pl.BlockSpec((1, mcols), lambda h, b: (0, mod_first + b))],
        out_specs=[pl.BlockSpec((1, SEQ, hp * dv), lambda h, b: (b, 0, h)), up_slab, dn_slab, out_slab, mod_slab],
        out_shape=[jax.ShapeDtypeStruct((BATCH, SEQ, RET_WIDTH), BF16),
                   jax.ShapeDtypeStruct(w_up.shape, BF16),
                   jax.ShapeDtypeStruct(w_dn.shape, BF16),
                   jax.ShapeDtypeStruct(w_out.shape, BF16),
                   jax.ShapeDtypeStruct((MOD_ROWS, MOD_LATE_COLS), F32)],
        scratch_shapes=[pltpu.VMEM((hp, SEQ // RET_CHUNK, dk, dv), F32),
                        pltpu.VMEM((hp, RET_CHUNK, RET_CHUNK), F32),
                        pltpu.VMEM((hp, 2, RET_CHUNK, dk), F32)],
        compiler_params=pltpu.CompilerParams(vmem_limit_bytes=VMEM_LIMIT),
        name="ret",
    )(lg, q, kt, v, g, s, w_up, w_dn, w_out, cc, w_mod, b_mod)


def _ffn_kernel(x_ref, xp_ref, xn_ref, yh_ref, yhp_ref, yhn_ref, yr_ref, yrp_ref, yrn_ref,
                g1_ref, sh_ref, sc_ref, g2_ref, n2_ref, nf_ref,
                woh_ref, wor_ref, wup_ref, cw_ref, cb_ref, wdn_ref,
                o_ref, hb_scr, x1_scr, av_scr, ag_scr, act_scr):
    t = TOK_TILE
    i = pl.program_id(1)
    nt = pl.num_programs(1)
    th = t // 2
    tile_halves = (slice(0, th), slice(th, t))

    def mixed(xr, yh, yr, rows=slice(None)):
        return xr[0, rows, :] + g1_ref[0] * (_dot(yh[0, rows, :], woh_ref[...]) + _dot(yr[0, rows, :], wor_ref[...]))

    def hidden(x1):
        return _norm_mod(x1, n2_ref[...], sh_ref[0], sc_ref[0])

    half = (t + 2 * HALO) // 2
    up_halves = (slice(0, half), slice(half, 2 * half))

    def up(slot, cblk, rows):
        c0 = cblk * FFN_COLS
        hb = hb_scr[rows, :]
        av_scr[slot, rows, :] = _dot(hb, wup_ref[:, c0:c0 + FFN_COLS])
        ag_scr[slot, rows, :] = _dot(hb, wup_ref[:, D_FF + c0:D_FF + c0 + FFN_COLS])

    for rows in tile_halves:
        x1_scr[rows, :] = mixed(x_ref, yh_ref, yr_ref, rows)
    hp = hidden(mixed(xp_ref, yhp_ref, yrp_ref))
    hn = hidden(mixed(xn_ref, yhn_ref, yrn_ref))
    hb_scr[0:HALO, :] = jnp.where(i > 0, hp, 0.0).astype(BF16)
    hb_scr[HALO + t:2 * HALO + t, :] = jnp.where(i < nt - 1, hn, 0.0).astype(BF16)
    hb_scr[HALO:HALO + th, :] = hidden(x1_scr[tile_halves[0], :]).astype(BF16)
    up(0, 0, up_halves[0])
    hb_scr[HALO + th:HALO + t, :] = hidden(x1_scr[tile_halves[1], :]).astype(BF16)
    up(0, 0, up_halves[1])

    def conv(scr, slot, col):
        w = cw_ref[:, col:col + FFN_COLS]
        return (scr[slot, HALO - 1:HALO - 1 + t, :] * w[0:1] + scr[slot, HALO:HALO + t, :] * w[1:2]
                + scr[slot, HALO + 1:HALO + 1 + t, :] * w[2:3] + cb_ref[:, col:col + FFN_COLS])

    n_blk = D_FF // FFN_COLS
    for cblk in range(n_blk):
        c0 = cblk * FFN_COLS
        slot = cblk % 2
        if cblk + 1 < n_blk:
            for rows in up_halves:
                up(1 - slot, cblk + 1, rows)
        act_scr[:, c0:c0 + FFN_COLS] = (
            _silu(conv(ag_scr, slot, D_FF + c0)) * conv(av_scr, slot, c0)).astype(BF16)

    down = [_dot(act_scr[rows, :], wdn_ref[...]) for rows in tile_halves]
    for rows, ffn in zip(tile_halves, down):
        x2 = x1_scr[rows, :] + g2_ref[0] * ffn
        o_ref[0, rows, :] = x2 * lax.rsqrt(jnp.mean(x2 * x2, axis=-1, keepdims=True) + EPS) * nf_ref[...]


def _ffn_call(x, y_hy, y_ret, mod3, norm2, norm_f, w_oh, w_or, w_up, conv_w, conv_b, w_dn):
    t = TOK_TILE
    r = t // HALO
    last = SEQ // HALO - 1

    def main(width):
        return pl.BlockSpec((1, t, width), lambda b, i: (b, i, 0))

    def prev(width):
        return pl.BlockSpec((1, HALO, width), lambda b, i: (b, jnp.maximum(i * r - 1, 0), 0))

    def nxt(width):
        return pl.BlockSpec((1, HALO, width), lambda b, i: (b, jnp.minimum((i + 1) * r, last), 0))

    def modrow(k):
        return pl.BlockSpec((1, 1, D_MODEL), lambda b, i: (b, 0, k))

    return pl.pallas_call(
        _ffn_kernel,
        grid=(BATCH, SEQ // t),
        in_specs=[main(D_MODEL), prev(D_MODEL), nxt(D_MODEL),
                  main(HY_WIDTH), prev(HY_WIDTH), nxt(HY_WIDTH),
                  main(RET_WIDTH), prev(RET_WIDTH), nxt(RET_WIDTH),
                  modrow(0), modrow(1), modrow(2), modrow(3),
                  _resident((1, D_MODEL)), _resident((1, D_MODEL)),
                  _resident_rows(0, HY_WIDTH, D_MODEL), _resident_rows(HY_WIDTH, RET_WIDTH, D_MODEL),
                  _resident((D_MODEL, 2 * D_FF)),
                  _resident((3, 2 * D_FF)), _resident((1, 2 * D_FF)),
                  _resident((D_FF, D_MODEL))],
        out_specs=pl.BlockSpec((1, t, D_MODEL), lambda b, i: (b, i, 0)),
        out_shape=jax.ShapeDtypeStruct((BATCH, SEQ, D_MODEL), F32),
        scratch_shapes=[pltpu.VMEM((t + 2 * HALO, D_MODEL), BF16),
                        pltpu.VMEM((t, D_MODEL), F32),
                        pltpu.VMEM((2, t + 2 * HALO, FFN_COLS), F32),
                        pltpu.VMEM((2, t + 2 * HALO, FFN_COLS), F32),
                        pltpu.VMEM((t, D_FF), BF16)],
        compiler_params=pltpu.CompilerParams(vmem_limit_bytes=VMEM_LIMIT),
        name="ffn",
    )(x, x, x, y_hy, y_hy, y_hy, y_ret, y_ret, y_ret, mod3, mod3, mod3, mod3,
      norm2, norm_f, w_oh, w_or, w_up, conv_w, conv_b, w_dn)


def kernel(x, c, ctx, c_ctx, w_mod, b_mod, norm1, w_in, hy_conv_w, hy_conv_b, hy_w1, hy_b1, hy_f1,
           hy_w2, hy_b2, hy_f2, hy_w3, hy_bias, ret_logit_f, ret_logit_b, w_out, norm2,
           ffn_w_up, ffn_conv_w, ffn_conv_b, ffn_w_down, norm_f):
    layer = 0
    rope = tuple(jnp.asarray(a) for a in _rope_tables())
    zt, t_row, absdelta = (jnp.asarray(a) for a in _filter_features())
    fwd_np, inv_np = _dft_matrices()
    fwd_u = jnp.asarray(fwd_np[:, :CONV_BLOCK]).astype(BF16)
    inv = jnp.asarray(inv_np).astype(BF16)

    w_in_b = w_in[layer].astype(BF16)
    w_kt = w_in_b[:, K_OFF:V_OFF].T
    row = lambda a: a.reshape(1, -1)
    col = lambda a: a.reshape(-1, 1)
    w1p = jnp.pad(hy_w1[layer], ((0, HY_FILTER_WIDTH - hy_w1.shape[1]), (0, 0)))
    lg = jnp.stack([jax.nn.log_sigmoid(ret_logit_f[layer].astype(F32)),
                    jax.nn.log_sigmoid(ret_logit_b[layer].astype(F32))])

    cc = jnp.concatenate([c, c_ctx[None, :], jnp.zeros((MOD_ROWS - BATCH - 1, D_MODEL), F32)], axis=0)
    mod3 = _mod_call(cc, w_mod[layer], row(b_mod[layer])).reshape(MOD_ROWS, 1, MOD_EARLY_COLS)
    norm1_r = row(norm1[layer])

    s_ctx = _ctx_call(lg, ctx, mod3, norm1_r, w_kt, w_in_b)
    zhy, q, kt, v, g = _inproj_call(x, mod3, norm1_r, w_in_b, w_kt, rope)

    k_two_sided = _filter_mlp_call(zt, t_row, w1p.T, col(hy_b1[layer]), col(hy_f1[layer]), hy_w2[layer].T,
                                   col(hy_b2[layer]), col(hy_f2[layer]), hy_w3[layer].T, absdelta)
    g_all = _filter_dft_call(k_two_sided, fwd_u)
    conv_w, conv_b = hy_conv_w[layer], row(hy_conv_b[layer])
    y1 = _hyena_call(zhy, 0, zhy, 1, conv_w, conv_b, hy_bias[layer][0:1], g_all, 0, fwd_u, inv, True)
    y_hy = _hyena_call(y1, 0, zhy, 2, conv_w, conv_b, hy_bias[layer][1:2], g_all, 1, fwd_u, inv, False)

    y_ret, w_up, w_dn, w_out_b, mod_late = _ret_call(lg, q, kt, v, g, s_ctx, ffn_w_up[layer], ffn_w_down[layer],
                                                     w_out[layer], cc, w_mod[layer], row(b_mod[layer]))
    mod_late3 = mod_late.reshape(MOD_ROWS, 1, MOD_LATE_COLS)

    return _ffn_call(x, y_hy, y_ret, mod_late3, row(norm2[layer]), row(norm_f), w_out_b, w_out_b, w_up,
                     ffn_conv_w[layer], row(ffn_conv_b[layer]), w_dn)
```

```python
import functools
import math

import numpy as np
import jax
import jax.numpy as jnp
from jax import lax
from jax.experimental import pallas as pl
from jax.experimental.pallas import tpu as pltpu

F32 = jnp.float32
BF16 = jnp.bfloat16

D_MODEL = 1024
BATCH = 8
SEQ = 2048
CTX_LEN = 256
GRID_W = 64
HY_WIDTH = 512
HY_ORDER = 2
HY_EMB_BANDS = 16
HY_FILTER_WIDTH = 64
HY_FAST_DECAY = 0.3
HY_SLOW_DECAY = 1.5
HY_TARGET = 1e-2
RET_WIDTH = 512
RET_HEADS = 4
RET_QK_DIM = 64
RET_V_DIM = 128
ROPE_BASE = 10000.0
D_FF = 2816
EPS = 1e-6
HY_COLS = (HY_ORDER + 1) * HY_WIDTH
Q_OFF = HY_COLS
K_OFF = Q_OFF + RET_HEADS * RET_QK_DIM
V_OFF = K_OFF + RET_HEADS * RET_QK_DIM
G_OFF = V_OFF + RET_WIDTH
K_SCALE = RET_QK_DIM ** -0.5

MOD_ROWS = 16
MOD_EARLY_COLS = 2 * D_MODEL
MOD_LATE_COLS = 4 * D_MODEL
TOK_TILE = 512
IN_TILE = 1024
IN_PIECE = 256
SUBLANES = 8
LANES = 128
HALO = 16
CONV_BLOCK = 512
N_CONV_BLOCKS = SEQ // CONV_BLOCK
FREQ_ROWS = 16
FREQ_LANES = 128
DFT_ROWS = 256
CTX_PER_STEP = 4
RET_CHUNK = 256
RET_HEADS_PER_STEP = 4
FFN_COLS = 256
VMEM_LIMIT = 56 * 1024 * 1024
HYENA_VMEM_LIMIT = 62 * 1024 * 1024

_NT = (((1,), (1,)), ((), ()))


def _dot(a, b):
    return jnp.dot(a, b, preferred_element_type=F32)


def _dot_hi(a, b):
    return jnp.dot(a, b, preferred_element_type=F32, precision=lax.Precision.HIGHEST)


def _silu(x):
    return x * (1.0 / (1.0 + jnp.exp(-x)))


def _norm_mod(x, gain, shift, scale):
    y = x * lax.rsqrt(jnp.mean(x * x, axis=-1, keepdims=True) + EPS)
    return (y * gain) * (1.0 + scale) + shift


def _resident(shape):
    nd = len(shape)
    return pl.BlockSpec(shape, lambda *_: (0,) * nd, pipeline_mode=pl.Buffered(1))


def _resident_cols(rows, col0, width):
    assert col0 % width == 0
    return pl.BlockSpec((rows, width), lambda *_: (0, col0 // width), pipeline_mode=pl.Buffered(1))


def _resident_rows(row0, height, cols):
    assert row0 % height == 0
    return pl.BlockSpec((height, cols), lambda *_: (row0 // height, 0), pipeline_mode=pl.Buffered(1))


@functools.lru_cache(maxsize=None)
def _rope_tables():
    pos = np.arange(SEQ)
    row = (pos // GRID_W).astype(np.float64)
    col = (pos % GRID_W).astype(np.float64)
    quarter = RET_QK_DIM // 4
    inv_freq = ROPE_BASE ** (-np.arange(quarter, dtype=np.float64) / quarter)
    ang = np.concatenate([row[:, None] * inv_freq, col[:, None] * inv_freq], axis=-1)
    cos, sin = np.cos(ang), np.sin(ang)
    cos_h = np.concatenate([cos, cos], axis=-1)
    sin_h = np.concatenate([-sin, sin], axis=-1)
    cos_t = np.tile(cos_h, (1, RET_HEADS))
    sin_t = np.tile(sin_h, (1, RET_HEADS))
    return (cos_t.astype(np.float32), sin_t.astype(np.float32),
            np.ascontiguousarray((cos_t * K_SCALE).T).astype(np.float32),
            np.ascontiguousarray((sin_t * K_SCALE).T).astype(np.float32))


@functools.lru_cache(maxsize=None)
def _filter_features():
    lag = np.abs(np.arange(2 * SEQ) - SEQ).astype(np.float64)
    t = lag / (SEQ - 1)
    bands = np.linspace(1e-4, HY_EMB_BANDS - 1, HY_EMB_BANDS)
    ang = 2.0 * math.pi * lag[:, None] * bands[None, :] / SEQ
    z = np.concatenate([t[:, None], np.cos(ang), -np.sin(ang)], axis=-1)
    zp = np.zeros((2 * SEQ, HY_FILTER_WIDTH), np.float64)
    zp[:, :z.shape[1]] = z
    max_decay = math.log(HY_TARGET) / HY_FAST_DECAY
    min_decay = math.log(HY_TARGET) / HY_SLOW_DECAY
    absdelta = np.abs(np.linspace(min_decay, max_decay, HY_WIDTH))[:, None]
    return (np.ascontiguousarray(zp.T).astype(np.float32), t[None, :].astype(np.float32),
            absdelta.astype(np.float32))


@functools.lru_cache(maxsize=None)
def _dft_matrices():
    p = CONV_BLOCK
    n = 2 * p
    f = np.arange(p, dtype=np.float64)[:, None] + 0.5
    t = np.arange(n, dtype=np.float64)[None, :]
    theta = 2.0 * math.pi * f * t / n
    fwd = np.concatenate([np.cos(theta), -np.sin(theta)], axis=0)
    th_out = theta[:, p:].T
    inv = np.concatenate([np.cos(th_out), -np.sin(th_out)], axis=1) / p
    return fwd.astype(np.float32), inv.astype(np.float32)


def _mod_kernel(c_ref, w_ref, b_ref, o_ref):
    s = _silu(c_ref[...]).astype(BF16)
    o_ref[...] = _dot(s, w_ref[...].astype(BF16)) + b_ref[...]


def _mod_call(cc, w_mod, b_mod):
    ncol = MOD_EARLY_COLS
    blk = ncol // 2
    return pl.pallas_call(
        _mod_kernel,
        grid=(ncol // blk,),
        in_specs=[pl.BlockSpec((MOD_ROWS, D_MODEL), lambda j: (0, 0)),
                  pl.BlockSpec((D_MODEL, blk), lambda j: (0, j)),
                  pl.BlockSpec((1, blk), lambda j: (0, j))],
        out_specs=pl.BlockSpec((MOD_ROWS, blk), lambda j: (0, j)),
        out_shape=jax.ShapeDtypeStruct((MOD_ROWS, ncol), F32),
        compiler_params=pltpu.CompilerParams(vmem_limit_bytes=VMEM_LIMIT),
        name="mod",
    )(cc, w_mod, b_mod)


def _ctx_kernel(lg_ref, x_ref, sh_ref, sc_ref, n1_ref, wkt_ref, wv_ref, s_ref):
    pos = lax.broadcasted_iota(jnp.int32, (1, CTX_LEN), 1).astype(F32)
    w_f = [jnp.exp(lg_ref[0, hh] * (CTX_LEN - 1.0 - pos)) for hh in range(RET_HEADS)]
    w_b = [jnp.exp(lg_ref[1, hh] * pos) for hh in range(RET_HEADS)]
    for bb in range(CTX_PER_STEP):
        h = _norm_mod(x_ref[bb], n1_ref[...], sh_ref[0], sc_ref[0]).astype(BF16)
        kt = lax.dot_general(wkt_ref[...], h, _NT, preferred_element_type=F32) * K_SCALE
        v = _dot(h, wv_ref[...])
        for hh in range(RET_HEADS):
            kth = kt[hh * RET_QK_DIM:(hh + 1) * RET_QK_DIM, :]
            vh = v[:, hh * RET_V_DIM:(hh + 1) * RET_V_DIM].astype(BF16)
            s_ref[bb, hh, 0:RET_QK_DIM, :] = _dot((kth * w_f[hh]).astype(BF16), vh)
            s_ref[bb, hh, RET_QK_DIM:2 * RET_QK_DIM, :] = _dot((kth * w_b[hh]).astype(BF16), vh)


def _ctx_call(lg, ctx, mod3, norm1, w_kt, w_v):
    return pl.pallas_call(
        _ctx_kernel,
        grid=(BATCH // CTX_PER_STEP,),
        in_specs=[pl.BlockSpec(memory_space=pltpu.SMEM),
                  pl.BlockSpec((CTX_PER_STEP, CTX_LEN, D_MODEL), lambda b: (b, 0, 0)),
                  pl.BlockSpec((1, 1, D_MODEL), lambda b: (BATCH, 0, 0)),
                  pl.BlockSpec((1, 1, D_MODEL), lambda b: (BATCH, 0, 1)),
                  _resident((1, D_MODEL)),
                  _resident((RET_HEADS * RET_QK_DIM, D_MODEL)),
                  _resident_cols(D_MODEL, V_OFF, RET_WIDTH)],
        out_specs=pl.BlockSpec((CTX_PER_STEP, RET_HEADS, 2 * RET_QK_DIM, RET_V_DIM), lambda b: (b, 0, 0, 0)),
        out_shape=jax.ShapeDtypeStruct((BATCH, RET_HEADS, 2 * RET_QK_DIM, RET_V_DIM), F32),
        compiler_params=pltpu.CompilerParams(vmem_limit_bytes=VMEM_LIMIT),
        name="ctx",
    )(lg, ctx, mod3, mod3, norm1, w_kt, w_v)


def _swap_halves(x, axis):
    n = x.shape[axis]
    half = RET_QK_DIM // 2
    idx = lax.broadcasted_iota(jnp.int32, x.shape, axis)
    first = (idx & (RET_QK_DIM - 1)) < half
    return jnp.where(first, pltpu.roll(x, n - half, axis), pltpu.roll(x, half, axis))


def _inproj_kernel(x_ref, sh_ref, sc_ref, n1_ref, why_ref, wq_ref, wv_ref, wg_ref, wkt_ref,
                   cq_ref, sq_ref, ck_ref, sk_ref,
                   zhy_ref, q_ref, kt_ref, v_ref, g_ref):
    pieces = [slice(r, r + IN_PIECE) for r in range(0, IN_TILE, IN_PIECE)]
    hbs = [_norm_mod(x_ref[0, rows, :], n1_ref[...], sh_ref[0], sc_ref[0]).astype(BF16) for rows in pieces]
    for rows, hb in zip(pieces, hbs):
        zhy_ref[0, rows, :] = _dot(hb, why_ref[...]).astype(BF16)
        v_ref[0, rows, :] = _dot(hb, wv_ref[...]).astype(BF16)
        g_ref[0, rows, :] = _dot(hb, wg_ref[...]).astype(BF16)
        q = _dot(hb, wq_ref[...])
        q = q * cq_ref[rows, :] + _swap_halves(q, 1) * sq_ref[rows, :]
        for hh in range(RET_HEADS):
            q_ref[0, hh, rows, :] = q[:, hh * RET_QK_DIM:(hh + 1) * RET_QK_DIM].astype(BF16)
        kt = lax.dot_general(wkt_ref[...], hb, _NT, preferred_element_type=F32)
        kt = kt * ck_ref[:, rows] + _swap_halves(kt, 0) * sk_ref[:, rows]
        kt_ref[0, :, rows] = kt.astype(BF16)


def _inproj_call(x, mod3, norm1, w_in, w_kt, rope):
    cq, sq, ck, sk = rope
    t = IN_TILE
    qk = RET_HEADS * RET_QK_DIM
    return pl.pallas_call(
        _inproj_kernel,
        grid=(BATCH, SEQ // t),
        in_specs=[pl.BlockSpec((1, t, D_MODEL), lambda b, i: (b, i, 0)),
                  pl.BlockSpec((1, 1, D_MODEL), lambda b, i: (b, 0, 0)),
                  pl.BlockSpec((1, 1, D_MODEL), lambda b, i: (b, 0, 1)),
                  _resident((1, D_MODEL)),
                  _resident_cols(D_MODEL, 0, HY_COLS),
                  _resident_cols(D_MODEL, Q_OFF, qk),
                  _resident_cols(D_MODEL, V_OFF, RET_WIDTH),
                  _resident_cols(D_MODEL, G_OFF, RET_WIDTH),
                  _resident((qk, D_MODEL)),
                  pl.BlockSpec((t, qk), lambda b, i: (i, 0)),
                  pl.BlockSpec((t, qk), lambda b, i: (i, 0)),
                  pl.BlockSpec((qk, t), lambda b, i: (0, i)),
                  pl.BlockSpec((qk, t), lambda b, i: (0, i))],
        out_specs=[pl.BlockSpec((1, t, HY_COLS), lambda b, i: (b, i, 0)),
                   pl.BlockSpec((1, RET_HEADS, t, RET_QK_DIM), lambda b, i: (b, 0, i, 0)),
                   pl.BlockSpec((1, qk, t), lambda b, i: (b, 0, i)),
                   pl.BlockSpec((1, t, RET_WIDTH), lambda b, i: (b, i, 0)),
                   pl.BlockSpec((1, t, RET_WIDTH), lambda b, i: (b, i, 0))],
        out_shape=[jax.ShapeDtypeStruct((BATCH, SEQ, HY_COLS), BF16),
                   jax.ShapeDtypeStruct((BATCH, RET_HEADS, SEQ, RET_QK_DIM), BF16),
                   jax.ShapeDtypeStruct((BATCH, qk, SEQ), BF16),
                   jax.ShapeDtypeStruct((BATCH, SEQ, RET_WIDTH), BF16),
                   jax.ShapeDtypeStruct((BATCH, SEQ, RET_WIDTH), BF16)],
        compiler_params=pltpu.CompilerParams(vmem_limit_bytes=VMEM_LIMIT),
        name="inproj",
    )(x, mod3, mod3, norm1, w_in, w_in, w_in, w_in, w_kt, cq, sq, ck, sk)


def _filter_mlp_kernel(zt_ref, t_ref, w1t_ref, b1_ref, f1_ref, w2t_ref, b2_ref, f2_ref, w3t_ref, adel_ref,
                       kt_ref):
    hid = jnp.sin(f1_ref[...] * (_dot_hi(w1t_ref[...], zt_ref[...]) + b1_ref[...]))
    hid = jnp.sin(f2_ref[...] * (_dot_hi(w2t_ref[...], hid) + b2_ref[...])).astype(BF16)
    c = HY_WIDTH
    for half in range(2):
        lags = slice(half * SEQ, (half + 1) * SEQ)
        window = jnp.exp(-adel_ref[...] * t_ref[:, lags])
        direction = 1 - half
        for o in range(HY_ORDER):
            r0 = (direction * HY_ORDER + o) * c
            w3 = w3t_ref[r0:r0 + c, :].astype(BF16)
            kt_ref[o, :, lags] = (_dot(w3, hid[:, lags]) * window).astype(BF16)


def _filter_mlp_call(zt, t_row, w1t, b1, f1, w2t, b2, f2, w3t, absdelta):
    fw = HY_FILTER_WIDTH
    n_out = 2 * HY_ORDER * HY_WIDTH
    return pl.pallas_call(
        _filter_mlp_kernel,
        grid=(1,),
        in_specs=[_resident((fw, 2 * SEQ)), _resident((1, 2 * SEQ)),
                  _resident((fw, fw)), _resident((fw, 1)), _resident((fw, 1)),
                  _resident((fw, fw)), _resident((fw, 1)), _resident((fw, 1)),
                  _resident((n_out, fw)), _resident((HY_WIDTH, 1))],
        out_specs=pl.BlockSpec((HY_ORDER, HY_WIDTH, 2 * SEQ), lambda i: (0, 0, 0)),
        out_shape=jax.ShapeDtypeStruct((HY_ORDER, HY_WIDTH, 2 * SEQ), BF16),
        compiler_params=pltpu.CompilerParams(vmem_limit_bytes=VMEM_LIMIT),
        name="filter_mlp",
    )(zt, t_row, w1t, b1, f1, w2t, b2, f2, w3t, absdelta)


_FILTER_PLANES = (
    {0: 1},
    {-1: 1, 0: -1},
    {1: 1, 0: -1},
    {-2: 1, 0: -1},
    {-3: 1, -1: -1, -2: -1, 0: 1},
    {-1: 1, 1: -1, -2: -1, 0: 1},
    {2: 1, 0: -1},
    {1: 1, -1: -1, 2: -1, 0: 1},
    {3: 1, 1: -1, 2: -1, 0: 1},
)
DFT_BLOCKS_PER_STEP = 2
PLANES_PER_STEP = 3


def _filter_dft_kernel(kt_ref, fwd_ref, h_ref, t_scr):
    p = CONV_BLOCK
    n_dft_steps = 2 * N_CONV_BLOCKS // DFT_BLOCKS_PER_STEP
    step = pl.program_id(1)

    for s in range(n_dft_steps):
        @pl.when(step == s)
        def _(s=s):
            for k in range(DFT_BLOCKS_PER_STEP):
                t_scr[s * DFT_BLOCKS_PER_STEP + k] = lax.dot_general(
                    fwd_ref[...], kt_ref[0, :, k * p:(k + 1) * p], _NT, preferred_element_type=F32)
            for k in range(PLANES_PER_STEP):
                h_ref[0, k] = t_scr[s * DFT_BLOCKS_PER_STEP]

    row = lax.broadcasted_iota(jnp.int32, (p, 1), 0)
    sign = (1 - 2 * (row & 1)).astype(F32)
    re, im = slice(0, p), slice(p, 2 * p)

    def plane(k, coefs):
        def comb(shift, rows):
            acc = None
            for d, c in coefs.items():
                term = t_scr[d + N_CONV_BLOCKS - 1 + shift, rows, :]
                if acc is None:
                    acc = term
                else:
                    acc = acc + term if c > 0 else acc - term
            return acc

        h_ref[0, k, re, :] = comb(0, re) + sign * comb(1, im)
        h_ref[0, k, im, :] = comb(0, im) - sign * comb(1, re)

    for s in range(len(_FILTER_PLANES) // PLANES_PER_STEP):
        @pl.when(step == n_dft_steps + s)
        def _(s=s):
            for k in range(PLANES_PER_STEP):
                plane(k, _FILTER_PLANES[s * PLANES_PER_STEP + k])


def _filter_dft_call(kt, fwd_u):
    p = CONV_BLOCK
    n_blk = 2 * N_CONV_BLOCKS
    n_planes = len(_FILTER_PLANES)
    n_dft_steps = n_blk // DFT_BLOCKS_PER_STEP
    return pl.pallas_call(
        _filter_dft_kernel,
        grid=(HY_ORDER, n_dft_steps + n_planes // PLANES_PER_STEP),
        in_specs=[pl.BlockSpec((1, HY_WIDTH, DFT_BLOCKS_PER_STEP * p),
                               lambda o, s: (o, 0, jnp.minimum(s, n_dft_steps - 1))),
                  _resident((2 * p, p))],
        out_specs=pl.BlockSpec((1, PLANES_PER_STEP, 2 * p, HY_WIDTH),
                               lambda o, s: (o, jnp.maximum(s - n_dft_steps, 0), 0, 0)),
        out_shape=jax.ShapeDtypeStruct((HY_ORDER, n_planes, 2 * p, HY_WIDTH), F32),
        scratch_shapes=[pltpu.VMEM((n_blk, 2 * p, HY_WIDTH), F32)],
        compiler_params=pltpu.CompilerParams(vmem_limit_bytes=VMEM_LIMIT),
        name="filter_dft",
    )(kt, fwd_u)


def _conv3_rows(ref, j, n_blocks, rows, w, b):
    main = ref[0, j * rows:(j + 1) * rows, :].astype(F32)
    cols = main.shape[1]
    zeros = jnp.zeros((HALO, cols), F32)
    prev = ref[0, j * rows - HALO:j * rows, :].astype(F32) if j > 0 else zeros
    nxt = ref[0, (j + 1) * rows:(j + 1) * rows + HALO, :].astype(F32) if j < n_blocks - 1 else zeros
    ext = jnp.concatenate([prev, main, nxt], axis=0)
    n = rows + 2 * HALO
    before = pltpu.roll(ext, 1, 0)[HALO:HALO + rows]
    after = pltpu.roll(ext, n - 1, 0)[HALO:HALO + rows]
    return before * w[0:1] + main * w[1:2] + after * w[2:3] + b


def _hyena_kernel(u_ref, zg_ref, cwu_ref, cbu_ref, cwg_ref, cbg_ref, skip_ref, h_hbm, fwd_ref, inv_ref,
                  o_ref, uf_scr, ub_scr, y_scr, h_ref, h_sem, *, conv_u, order):
    p = CONV_BLOCK
    nb = N_CONV_BLOCKS
    first_step = pl.program_id(0) == 0
    planes_copy = pltpu.make_async_copy(h_hbm.at[order], h_ref, h_sem.at[0])

    @pl.when(first_step)
    def _():
        planes_copy.start()

    def prepare(j):
        if conv_u:
            ub_scr[j * p:(j + 1) * p, :] = _conv3_rows(u_ref, j, nb, p, cwu_ref[...], cbu_ref[...]).astype(BF16)

    def u_block(j):
        return ub_scr[j * p:(j + 1) * p, :] if conv_u else u_ref[0, j * p:(j + 1) * p, :]

    prepare(0)
    for j in range(nb):
        if j + 1 < nb:
            prepare(j + 1)
        for r in range(2 * p // DFT_ROWS):
            rows = slice(r * DFT_ROWS, (r + 1) * DFT_ROWS)
            uf_scr[j, rows, :] = _dot(fwd_ref[rows, :], u_block(j))

    @pl.when(first_step)
    def _():
        planes_copy.wait()

    def cadd(a, b):
        return a[0] + b[0], a[1] + b[1]

    def cmul(m, x):
        return m[0] * x[0] - m[1] * x[1], m[0] * x[1] + m[1] * x[0]

    def toeplitz2(k0, x0, x1, re, im, ln):
        m0, mu, ml = ((h_ref[k0 + t, re, ln], h_ref[k0 + t, im, ln]) for t in range(3))
        p1 = cmul(m0, cadd(x0, x1))
        return cadd(p1, cmul(mu, x1)), cadd(p1, cmul(ml, x0))

    assert nb == 4
    for r in range(p // FREQ_ROWS):
        re = slice(r * FREQ_ROWS, (r + 1) * FREQ_ROWS)
        im = slice(p + r * FREQ_ROWS, p + (r + 1) * FREQ_ROWS)
        for cb in range(HY_WIDTH // FREQ_LANES):
            ln = slice(cb * FREQ_LANES, (cb + 1) * FREQ_LANES)
            u = [(uf_scr[j, re, ln], uf_scr[j, im, ln]) for j in range(nb)]
            d0, d1 = toeplitz2(0, cadd(u[0], u[2]), cadd(u[1], u[3]), re, im, ln)
            b0, b1 = toeplitz2(3, u[2], u[3], re, im, ln)
            c0, c1 = toeplitz2(6, u[0], u[1], re, im, ln)
            for i, yi in enumerate((cadd(d0, b0), cadd(d1, b1), cadd(d0, c0), cadd(d1, c1))):
                y_scr[i, re, ln] = yi[0].astype(BF16)
                y_scr[i, im, ln] = yi[1].astype(BF16)

    for i in range(nb):
        gate = _conv3_rows(zg_ref, i, nb, p, cwg_ref[...], cbg_ref[...])
        y = _dot(inv_ref[...], y_scr[i])
        o_ref[0, i * p:(i + 1) * p, :] = (
            gate * (y + u_block(i).astype(F32) * skip_ref[...])).astype(BF16)


def _hyena_call(u, u_col, zhy, gate_col, conv_w, conv_b, skip, g_all, order, fwd_u, inv, conv_u):
    p = CONV_BLOCK
    c = HY_WIDTH
    ucol = u_col if conv_u else 0
    cwu = conv_w[:, ucol * c:(ucol + 1) * c]
    cbu = conv_b[:, ucol * c:(ucol + 1) * c]
    cwg = conv_w[:, gate_col * c:(gate_col + 1) * c]
    cbg = conv_b[:, gate_col * c:(gate_col + 1) * c]
    return pl.pallas_call(
        functools.partial(_hyena_kernel, conv_u=conv_u, order=order),
        grid=(BATCH,),
        in_specs=[pl.BlockSpec((1, SEQ, c), lambda b: (b, 0, u_col)),
                  pl.BlockSpec((1, SEQ, c), lambda b: (b, 0, gate_col)),
                  _resident((3, c)), _resident((1, c)), _resident((3, c)), _resident((1, c)),
                  _resident((1, c)),
                  pl.BlockSpec(memory_space=pl.ANY),
                  _resident((2 * p, p)),
                  _resident((p, 2 * p))],
        out_specs=pl.BlockSpec((1, SEQ, c), lambda b: (b, 0, 0)),
        out_shape=jax.ShapeDtypeStruct((BATCH, SEQ, c), BF16),
        scratch_shapes=[pltpu.VMEM((N_CONV_BLOCKS, 2 * p, c), F32),
                        pltpu.VMEM((SEQ, c) if conv_u else (SUBLANES * 2, LANES), BF16),
                        pltpu.VMEM((N_CONV_BLOCKS, 2 * p, c), BF16),
                        pltpu.VMEM((len(_FILTER_PLANES), 2 * p, c), F32),
                        pltpu.SemaphoreType.DMA((1,))],
        compiler_params=pltpu.CompilerParams(vmem_limit_bytes=HYENA_VMEM_LIMIT,
                                             dimension_semantics=("arbitrary",)),
        name="hyena%d" % order,
    )(u, zhy, cwu, cbu, cwg, cbg, skip, g_all, fwd_u, inv)


def _ret_kernel(lg_ref, q_ref, kt_ref, v_ref, g_ref, s_ref, wup_ref, wdn_ref, wout_ref, cc_ref, wmod_ref, bmod_ref,
                o_ref, wup_b_ref, wdn_b_ref, wout_b_ref, modl_ref, b_scr, decay_scr, qw_scr):
    wup_b_ref[...] = wup_ref[...].astype(BF16)
    wdn_b_ref[...] = wdn_ref[...].astype(BF16)
    wout_b_ref[...] = wout_ref[...].astype(BF16)
    modl_ref[...] = _dot(_silu(cc_ref[...]).astype(BF16), wmod_ref[...].astype(BF16)) + bmod_ref[...]
    c = RET_CHUNK
    nc = SEQ // c
    dk = RET_QK_DIM
    dv = RET_V_DIM
    heads = range(RET_HEADS_PER_STEP)
    lg_f = [lg_ref[0, pl.program_id(0) * RET_HEADS_PER_STEP + hh] for hh in heads]
    lg_b = [lg_ref[1, pl.program_id(0) * RET_HEADS_PER_STEP + hh] for hh in heads]

    @pl.when(pl.program_id(1) == 0)
    def _():
        ii = lax.broadcasted_iota(jnp.int32, (c, c), 0).astype(F32)
        jj = lax.broadcasted_iota(jnp.int32, (c, c), 1).astype(F32)
        dif = ii - jj
        pos_q = lax.broadcasted_iota(jnp.int32, (c, dk), 0).astype(F32)
        for hh in heads:
            decay_scr[hh] = jnp.where(dif >= 0.0, jnp.exp(lg_f[hh] * jnp.maximum(dif, 0.0)),
                                      jnp.exp(lg_b[hh] * jnp.maximum(-dif, 0.0)))
            qw_scr[hh, 0] = jnp.exp(lg_f[hh] * (pos_q + 1.0))
            qw_scr[hh, 1] = jnp.exp(lg_b[hh] * (c - pos_q))

    pos_r = lax.broadcasted_iota(jnp.int32, (1, c), 1).astype(F32)
    ones = jnp.ones((1, dv), F32)
    kw_f = [jnp.exp(lg_f[hh] * (c - 1.0 - pos_r)) for hh in heads]
    kw_b = [jnp.exp(lg_b[hh] * pos_r) for hh in heads]
    dec_f = [jnp.exp(lg_f[hh] * float(c) * ones) for hh in heads]
    dec_b = [jnp.exp(lg_b[hh] * float(c) * ones) for hh in heads]

    def kt_chunk(hh, n):
        return kt_ref[0, hh * dk:(hh + 1) * dk, n * c:(n + 1) * c]

    def v_chunk(hh, n):
        return v_ref[0, n * c:(n + 1) * c, hh * dv:(hh + 1) * dv]

    state = [s_ref[0, hh, dk:2 * dk, :] for hh in heads]
    for hh in heads:
        b_scr[hh, nc - 1] = state[hh]
    for n in range(nc - 1, 0, -1):
        for hh in heads:
            ktn = (kt_chunk(hh, n).astype(F32) * kw_b[hh]).astype(BF16)
            state[hh] = state[hh] * dec_b[hh] + _dot(ktn, v_chunk(hh, n))
            b_scr[hh, n - 1] = state[hh]

    state = [s_ref[0, hh, 0:dk, :] for hh in heads]
    for n in range(nc):
        for hh in heads:
            qn = q_ref[0, hh, n * c:(n + 1) * c, :]
            ktn = kt_chunk(hh, n)
            vn = v_chunk(hh, n)
            scores = (_dot(qn, ktn) * decay_scr[hh]).astype(BF16)
            qf = qn.astype(F32)
            o = _dot(scores, vn)
            o = o + _dot((qf * qw_scr[hh, 0]).astype(BF16), state[hh].astype(BF16))
            o = o + _dot((qf * qw_scr[hh, 1]).astype(BF16), b_scr[hh, n].astype(BF16))
            state[hh] = state[hh] * dec_f[hh] + _dot((ktn.astype(F32) * kw_f[hh]).astype(BF16), vn)
            o = o * lax.rsqrt(jnp.mean(o * o, axis=-1, keepdims=True) + EPS)
            gate = g_ref[0, n * c:(n + 1) * c, hh * dv:(hh + 1) * dv].astype(F32)
            o_ref[0, n * c:(n + 1) * c, hh * dv:(hh + 1) * dv] = (_silu(gate) * o).astype(BF16)


def _ret_call(lg, q, kt, v, g, s, w_up, w_dn, w_out, cc, w_mod, b_mod):
    dk, dv = RET_QK_DIM, RET_V_DIM
    hp = RET_HEADS_PER_STEP
    assert RET_HEADS == hp

    def row_slab(w):
        return pl.BlockSpec((w.shape[0] // BATCH, w.shape[1]), lambda h, b: (b, 0))

    up_slab, dn_slab, out_slab = row_slab(w_up), row_slab(w_dn), row_slab(w_out)
    mcols = MOD_LATE_COLS // BATCH
    mod_first = MOD_EARLY_COLS // mcols
    mod_slab = pl.BlockSpec((MOD_ROWS, mcols), lambda h, b: (0, b))
    return pl.pallas_call(
        _ret_kernel,
        grid=(RET_HEADS // hp, BATCH),
        in_specs=[pl.BlockSpec(memory_space=pltpu.SMEM),
                  pl.BlockSpec((1, hp, SEQ, dk), lambda h, b: (b, h, 0, 0)),
                  pl.BlockSpec((1, hp * dk, SEQ), lambda h, b: (b, h, 0)),
                  pl.BlockSpec((1, SEQ, hp * dv), lambda h, b: (b, 0, h)),
                  pl.BlockSpec((1, SEQ, hp * dv), lambda h, b: (b, 0, h)),
                  pl.BlockSpec((1, hp, 2 * dk, dv), lambda h, b: (b, h, 0, 0)),
                  up_slab, dn_slab, out_slab,
                  _resident((MOD_ROWS, D_MODEL)),
                  pl.BlockSpec((D_MODEL, mcols), lambda h, b: (0, mod_first + b)),
                  pl.BlockSpec((1, mcols), lambda h, b: (0, mod_first + b))],
        out_specs=[pl.BlockSpec((1, SEQ, hp * dv), lambda h, b: (b, 0, h)), up_slab, dn_slab, out_slab, mod_slab],
        out_shape=[jax.ShapeDtypeStruct((BATCH, SEQ, RET_WIDTH), BF16),
                   jax.ShapeDtypeStruct(w_up.shape, BF16),
                   jax.ShapeDtypeStruct(w_dn.shape, BF16),
                   jax.ShapeDtypeStruct(w_out.shape, BF16),
                   jax.ShapeDtypeStruct((MOD_ROWS, MOD_LATE_COLS), F32)],
        scratch_shapes=[pltpu.VMEM((hp, SEQ // RET_CHUNK, dk, dv), F32),
                        pltpu.VMEM((hp, RET_CHUNK, RET_CHUNK), F32),
                        pltpu.VMEM((hp, 2, RET_CHUNK, dk), F32)],
        compiler_params=pltpu.CompilerParams(vmem_limit_bytes=VMEM_LIMIT),
        name="ret",
    )(lg, q, kt, v, g, s, w_up, w_dn, w_out, cc, w_mod, b_mod)


def _ffn_kernel(x_ref, xp_ref, xn_ref, yh_ref, yhp_ref, yhn_ref, yr_ref, yrp_ref, yrn_ref,
                g1_ref, sh_ref, sc_ref, g2_ref, n2_ref, nf_ref,
                woh_ref, wor_ref, wup_ref, cw_ref, cb_ref, wdn_ref,
                o_ref, hb_scr, x1_scr, av_scr, ag_scr, act_scr):
    t = TOK_TILE
    i = pl.program_id(1)
    nt = pl.num_programs(1)
    th = t // 2
    tile_halves = (slice(0, th), slice(th, t))

    def mixed(xr, yh, yr, rows=slice(None)):
        return xr[0, rows, :] + g1_ref[0] * (_dot(yh[0, rows, :], woh_ref[...]) + _dot(yr[0, rows, :], wor_ref[...]))

    def hidden(x1):
        return _norm_mod(x1, n2_ref[...], sh_ref[0], sc_ref[0])

    half = (t + 2 * HALO) // 2
    up_halves = (slice(0, half), slice(half, 2 * half))

    def up(slot, cblk, rows):
        c0 = cblk * FFN_COLS
        hb = hb_scr[rows, :]
        av_scr[slot, rows, :] = _dot(hb, wup_ref[:, c0:c0 + FFN_COLS])
        ag_scr[slot, rows, :] = _dot(hb, wup_ref[:, D_FF + c0:D_FF + c0 + FFN_COLS])

    for rows in tile_halves:
        x1_scr[rows, :] = mixed(x_ref, yh_ref, yr_ref, rows)
    hp = hidden(mixed(xp_ref, yhp_ref, yrp_ref))
    hn = hidden(mixed(xn_ref, yhn_ref, yrn_ref))
    hb_scr[0:HALO, :] = jnp.where(i > 0, hp, 0.0).astype(BF16)
    hb_scr[HALO + t:2 * HALO + t, :] = jnp.where(i < nt - 1, hn, 0.0).astype(BF16)
    hb_scr[HALO:HALO + th, :] = hidden(x1_scr[tile_halves[0], :]).astype(BF16)
    up(0, 0, up_halves[0])
    hb_scr[HALO + th:HALO + t, :] = hidden(x1_scr[tile_halves[1], :]).astype(BF16)
    up(0, 0, up_halves[1])

    def conv(scr, slot, col):
        w = cw_ref[:, col:col + FFN_COLS]
        return (scr[slot, HALO - 1:HALO - 1 + t, :] * w[0:1] + scr[slot, HALO:HALO + t, :] * w[1:2]
                + scr[slot, HALO + 1:HALO + 1 + t, :] * w[2:3] + cb_ref[:, col:col + FFN_COLS])

    n_blk = D_FF // FFN_COLS
    for cblk in range(n_blk):
        c0 = cblk * FFN_COLS
        slot = cblk % 2
        if cblk + 1 < n_blk:
            for rows in up_halves:
                up(1 - slot, cblk + 1, rows)
        act_scr[:, c0:c0 + FFN_COLS] = (
            _silu(conv(ag_scr, slot, D_FF + c0)) * conv(av_scr, slot, c0)).astype(BF16)

    down = [_dot(act_scr[rows, :], wdn_ref[...]) for rows in tile_halves]
    for rows, ffn in zip(tile_halves, down):
        x2 = x1_scr[rows, :] + g2_ref[0] * ffn
        o_ref[0, rows, :] = x2 * lax.rsqrt(jnp.mean(x2 * x2, axis=-1, keepdims=True) + EPS) * nf_ref[...]


def _ffn_call(x, y_hy, y_ret, mod3, norm2, norm_f, w_oh, w_or, w_up, conv_w, conv_b, w_dn):
    t = TOK_TILE
    r = t // HALO
    last = SEQ // HALO - 1

    def main(width):
        return pl.BlockSpec((1, t, width), lambda b, i: (b, i, 0))

    def prev(width):
        return pl.BlockSpec((1, HALO, width), lambda b, i: (b, jnp.maximum(i * r - 1, 0), 0))

    def nxt(width):
        return pl.BlockSpec((1, HALO, width), lambda b, i: (b, jnp.minimum((i + 1) * r, last), 0))

    def modrow(k):
        return pl.BlockSpec((1, 1, D_MODEL), lambda b, i: (b, 0, k))

    return pl.pallas_call(
        _ffn_kernel,
        grid=(BATCH, SEQ // t),
        in_specs=[main(D_MODEL), prev(D_MODEL), nxt(D_MODEL),
                  main(HY_WIDTH), prev(HY_WIDTH), nxt(HY_WIDTH),
                  main(RET_WIDTH), prev(RET_WIDTH), nxt(RET_WIDTH),
                  modrow(0), modrow(1), modrow(2), modrow(3),
                  _resident((1, D_MODEL)), _resident((1, D_MODEL)),
                  _resident_rows(0, HY_WIDTH, D_MODEL), _resident_rows(HY_WIDTH, RET_WIDTH, D_MODEL),
                  _resident((D_MODEL, 2 * D_FF)),
                  _resident((3, 2 * D_FF)), _resident((1, 2 * D_FF)),
                  _resident((D_FF, D_MODEL))],
        out_specs=pl.BlockSpec((1, t, D_MODEL), lambda b, i: (b, i, 0)),
        out_shape=jax.ShapeDtypeStruct((BATCH, SEQ, D_MODEL), F32),
        scratch_shapes=[pltpu.VMEM((t + 2 * HALO, D_MODEL), BF16),
                        pltpu.VMEM((t, D_MODEL), F32),
                        pltpu.VMEM((2, t + 2 * HALO, FFN_COLS), F32),
                        pltpu.VMEM((2, t + 2 * HALO, FFN_COLS), F32),
                        pltpu.VMEM((t, D_FF), BF16)],
        compiler_params=pltpu.CompilerParams(vmem_limit_bytes=VMEM_LIMIT),
        name="ffn",
    )(x, x, x, y_hy, y_hy, y_hy, y_ret, y_ret, y_ret, mod3, mod3, mod3, mod3,
      norm2, norm_f, w_oh, w_or, w_up, conv_w, conv_b, w_dn)


def kernel(x, c, ctx, c_ctx, w_mod, b_mod, norm1, w_in, hy_conv_w, hy_conv_b, hy_w1, hy_b1, hy_f1,
           hy_w2, hy_b2, hy_f2, hy_w3, hy_bias, ret_logit_f, ret_logit_b, w_out, norm2,
           ffn_w_up, ffn_conv_w, ffn_conv_b, ffn_w_down, norm_f):
    layer = 0
    rope = tuple(jnp.asarray(a) for a in _rope_tables())
    zt, t_row, absdelta = (jnp.asarray(a) for a in _filter_features())
    fwd_np, inv_np = _dft_matrices()
    fwd_u = jnp.asarray(fwd_np[:, :CONV_BLOCK]).astype(BF16)
    inv = jnp.asarray(inv_np).astype(BF16)

    w_in_b = w_in[layer].astype(BF16)
    w_kt = w_in_b[:, K_OFF:V_OFF].T
    row = lambda a: a.reshape(1, -1)
    col = lambda a: a.reshape(-1, 1)
    w1p = jnp.pad(hy_w1[layer], ((0, HY_FILTER_WIDTH - hy_w1.shape[1]), (0, 0)))
    lg = jnp.stack([jax.nn.log_sigmoid(ret_logit_f[layer].astype(F32)),
                    jax.nn.log_sigmoid(ret_logit_b[layer].astype(F32))])

    cc = jnp.concatenate([c, c_ctx[None, :], jnp.zeros((MOD_ROWS - BATCH - 1, D_MODEL), F32)], axis=0)
    mod3 = _mod_call(cc, w_mod[layer], row(b_mod[layer])).reshape(MOD_ROWS, 1, MOD_EARLY_COLS)
    norm1_r = row(norm1[layer])

    s_ctx = _ctx_call(lg, ctx, mod3, norm1_r, w_kt, w_in_b)
    zhy, q, kt, v, g = _inproj_call(x, mod3, norm1_r, w_in_b, w_kt, rope)

    k_two_sided = _filter_mlp_call(zt, t_row, w1p.T, col(hy_b1[layer]), col(hy_f1[layer]), hy_w2[layer].T,
                                   col(hy_b2[layer]), col(hy_f2[layer]), hy_w3[layer].T, absdelta)
    g_all = _filter_dft_call(k_two_sided, fwd_u)
    conv_w, conv_b = hy_conv_w[layer], row(hy_conv_b[layer])
    y1 = _hyena_call(zhy, 0, zhy, 1, conv_w, conv_b, hy_bias[layer][0:1], g_all, 0, fwd_u, inv, True)
    y_hy = _hyena_call(y1, 0, zhy, 2, conv_w, conv_b, hy_bias[layer][1:2], g_all, 1, fwd_u, inv, False)

    y_ret, w_up, w_dn, w_out_b, mod_late = _ret_call(lg, q, kt, v, g, s_ctx, ffn_w_up[layer], ffn_w_down[layer],
                                                     w_out[layer], cc, w_mod[layer], row(b_mod[layer]))
    mod_late3 = mod_late.reshape(MOD_ROWS, 1, MOD_LATE_COLS)

    return _ffn_call(x, y_hy, y_ret, mod_late3, row(norm2[layer]), row(norm_f), w_out_b, w_out_b, w_up,
                     ffn_conv_w[layer], row(ffn_conv_b[layer]), w_dn)
```

```python
import functools
import math

import numpy as np
import jax
import jax.numpy as jnp
from jax import lax
from jax.experimental import pallas as pl
from jax.experimental.pallas import tpu as pltpu

F32 = jnp.float32
BF16 = jnp.bfloat16

D_MODEL = 1024
BATCH = 8
SEQ = 2048
CTX_LEN = 256
GRID_W = 64
HY_WIDTH = 512
HY_ORDER = 2
HY_EMB_BANDS = 16
HY_FILTER_WIDTH = 64
HY_FAST_DECAY = 0.3
HY_SLOW_DECAY = 1.5
HY_TARGET = 1e-2
RET_WIDTH = 512
RET_HEADS = 4
RET_QK_DIM = 64
RET_V_DIM = 128
ROPE_BASE = 10000.0
D_FF = 2816
EPS = 1e-6
HY_COLS = (HY_ORDER + 1) * HY_WIDTH
Q_OFF = HY_COLS
K_OFF = Q_OFF + RET_HEADS * RET_QK_DIM
V_OFF = K_OFF + RET_HEADS * RET_QK_DIM
G_OFF = V_OFF + RET_WIDTH
K_SCALE = RET_QK_DIM ** -0.5

MOD_ROWS = 16
MOD_EARLY_COLS = 2 * D_MODEL
MOD_LATE_COLS = 4 * D_MODEL
TOK_TILE = 512
IN_TILE = 1024
IN_PIECE = 256
SUBLANES = 8
LANES = 128
HALO = 16
CONV_BLOCK = 512
N_CONV_BLOCKS = SEQ // CONV_BLOCK
FREQ_ROWS = 16
FREQ_LANES = 128
DFT_ROWS = 256
CTX_PER_STEP = 4
RET_CHUNK = 256
RET_HEADS_PER_STEP = 4
FFN_COLS = 256
VMEM_LIMIT = 56 * 1024 * 1024
HYENA_VMEM_LIMIT = 62 * 1024 * 1024

_NT = (((1,), (1,)), ((), ()))


def _dot(a, b):
    return jnp.dot(a, b, preferred_element_type=F32)


def _dot_hi(a, b):
    return jnp.dot(a, b, preferred_element_type=F32, precision=lax.Precision.HIGHEST)


def _silu(x):
    return x * (1.0 / (1.0 + jnp.exp(-x)))


def _norm_mod(x, gain, shift, scale):
    y = x * lax.rsqrt(jnp.mean(x * x, axis=-1, keepdims=True) + EPS)
    return (y * gain) * (1.0 + scale) + shift


def _resident(shape):
    nd = len(shape)
    return pl.BlockSpec(shape, lambda *_: (0,) * nd, pipeline_mode=pl.Buffered(1))


def _resident_cols(rows, col0, width):
    assert col0 % width == 0
    return pl.BlockSpec((rows, width), lambda *_: (0, col0 // width), pipeline_mode=pl.Buffered(1))


def _resident_rows(row0, height, cols):
    assert row0 % height == 0
    return pl.BlockSpec((height, cols), lambda *_: (row0 // height, 0), pipeline_mode=pl.Buffered(1))


@functools.lru_cache(maxsize=None)
def _rope_tables():
    pos = np.arange(SEQ)
    row = (pos // GRID_W).astype(np.float64)
    col = (pos % GRID_W).astype(np.float64)
    quarter = RET_QK_DIM // 4
    inv_freq = ROPE_BASE ** (-np.arange(quarter, dtype=np.float64) / quarter)
    ang = np.concatenate([row[:, None] * inv_freq, col[:, None] * inv_freq], axis=-1)
    cos, sin = np.cos(ang), np.sin(ang)
    cos_h = np.concatenate([cos, cos], axis=-1)
    sin_h = np.concatenate([-sin, sin], axis=-1)
    cos_t = np.tile(cos_h, (1, RET_HEADS))
    sin_t = np.tile(sin_h, (1, RET_HEADS))
    return (cos_t.astype(np.float32), sin_t.astype(np.float32),
            np.ascontiguousarray((cos_t * K_SCALE).T).astype(np.float32),
            np.ascontiguousarray((sin_t * K_SCALE).T).astype(np.float32))


@functools.lru_cache(maxsize=None)
def _filter_features():
    lag = np.abs(np.arange(2 * SEQ) - SEQ).astype(np.float64)
    t = lag / (SEQ - 1)
    bands = np.linspace(1e-4, HY_EMB_BANDS - 1, HY_EMB_BANDS)
    ang = 2.0 * math.pi * lag[:, None] * bands[None, :] / SEQ
    z = np.concatenate([t[:, None], np.cos(ang), -np.sin(ang)], axis=-1)
    zp = np.zeros((2 * SEQ, HY_FILTER_WIDTH), np.float64)
    zp[:, :z.shape[1]] = z
    max_decay = math.log(HY_TARGET) / HY_FAST_DECAY
    min_decay = math.log(HY_TARGET) / HY_SLOW_DECAY
    absdelta = np.abs(np.linspace(min_decay, max_decay, HY_WIDTH))[:, None]
    return (np.ascontiguousarray(zp.T).astype(np.float32), t[None, :].astype(np.float32),
            absdelta.astype(np.float32))


@functools.lru_cache(maxsize=None)
def _dft_matrices():
    p = CONV_BLOCK
    n = 2 * p
    f = np.arange(p, dtype=np.float64)[:, None] + 0.5
    t = np.arange(n, dtype=np.float64)[None, :]
    theta = 2.0 * math.pi * f * t / n
    fwd = np.concatenate([np.cos(theta), -np.sin(theta)], axis=0)
    th_out = theta[:, p:].T
    inv = np.concatenate([np.cos(th_out), -np.sin(th_out)], axis=1) / p
    return fwd.astype(np.float32), inv.astype(np.float32)


def _mod_kernel(c_ref, w_ref, b_ref, o_ref):
    s = _silu(c_ref[...]).astype(BF16)
    o_ref[...] = _dot(s, w_ref[...].astype(BF16)) + b_ref[...]


def _mod_call(cc, w_mod, b_mod):
    ncol = MOD_EARLY_COLS
    blk = ncol // 2
    return pl.pallas_call(
        _mod_kernel,
        grid=(ncol // blk,),
        in_specs=[pl.BlockSpec((MOD_ROWS, D_MODEL), lambda j: (0, 0)),
                  pl.BlockSpec((D_MODEL, blk), lambda j: (0, j)),
                  pl.BlockSpec((1, blk), lambda j: (0, j))],
        out_specs=pl.BlockSpec((MOD_ROWS, blk), lambda j: (0, j)),
        out_shape=jax.ShapeDtypeStruct((MOD_ROWS, ncol), F32),
        compiler_params=pltpu.CompilerParams(vmem_limit_bytes=VMEM_LIMIT),
        name="mod",
    )(cc, w_mod, b_mod)


def _ctx_kernel(lg_ref, x_ref, sh_ref, sc_ref, n1_ref, wkt_ref, wv_ref, s_ref):
    pos = lax.broadcasted_iota(jnp.int32, (1, CTX_LEN), 1).astype(F32)
    w_f = [jnp.exp(lg_ref[0, hh] * (CTX_LEN - 1.0 - pos)) for hh in range(RET_HEADS)]
    w_b = [jnp.exp(lg_ref[1, hh] * pos) for hh in range(RET_HEADS)]
    for bb in range(CTX_PER_STEP):
        h = _norm_mod(x_ref[bb], n1_ref[...], sh_ref[0], sc_ref[0]).astype(BF16)
        kt = lax.dot_general(wkt_ref[...], h, _NT, preferred_element_type=F32) * K_SCALE
        v = _dot(h, wv_ref[...])
        for hh in range(RET_HEADS):
            kth = kt[hh * RET_QK_DIM:(hh + 1) * RET_QK_DIM, :]
            vh = v[:, hh * RET_V_DIM:(hh + 1) * RET_V_DIM].astype(BF16)
            s_ref[bb, hh, 0:RET_QK_DIM, :] = _dot((kth * w_f[hh]).astype(BF16), vh)
            s_ref[bb, hh, RET_QK_DIM:2 * RET_QK_DIM, :] = _dot((kth * w_b[hh]).astype(BF16), vh)


def _ctx_call(lg, ctx, mod3, norm1, w_kt, w_v):
    return pl.pallas_call(
        _ctx_kernel,
        grid=(BATCH // CTX_PER_STEP,),
        in_specs=[pl.BlockSpec(memory_space=pltpu.SMEM),
                  pl.BlockSpec((CTX_PER_STEP, CTX_LEN, D_MODEL), lambda b: (b, 0, 0)),
                  pl.BlockSpec((1, 1, D_MODEL), lambda b: (BATCH, 0, 0)),
                  pl.BlockSpec((1, 1, D_MODEL), lambda b: (BATCH, 0, 1)),
                  _resident((1, D_MODEL)),
                  _resident((RET_HEADS * RET_QK_DIM, D_MODEL)),
                  _resident_cols(D_MODEL, V_OFF, RET_WIDTH)],
        out_specs=pl.BlockSpec((CTX_PER_STEP, RET_HEADS, 2 * RET_QK_DIM, RET_V_DIM), lambda b: (b, 0, 0, 0)),
        out_shape=jax.ShapeDtypeStruct((BATCH, RET_HEADS, 2 * RET_QK_DIM, RET_V_DIM), F32),
        compiler_params=pltpu.CompilerParams(vmem_limit_bytes=VMEM_LIMIT),
        name="ctx",
    )(lg, ctx, mod3, mod3, norm1, w_kt, w_v)


def _swap_halves(x, axis):
    n = x.shape[axis]
    half = RET_QK_DIM // 2
    idx = lax.broadcasted_iota(jnp.int32, x.shape, axis)
    first = (idx & (RET_QK_DIM - 1)) < half
    return jnp.where(first, pltpu.roll(x, n - half, axis), pltpu.roll(x, half, axis))


def _inproj_kernel(x_ref, sh_ref, sc_ref, n1_ref, why_ref, wq_ref, wv_ref, wg_ref, wkt_ref,
                   cq_ref, sq_ref, ck_ref, sk_ref, wup_ref, wdn_ref, wout_ref,
                   zhy_ref, q_ref, kt_ref, v_ref, g_ref, wup_b_ref, wdn_b_ref, wout_b_ref):
    wup_b_ref[...] = wup_ref[...].astype(BF16)
    wdn_b_ref[...] = wdn_ref[...].astype(BF16)
    wout_b_ref[...] = wout_ref[...].astype(BF16)
    pieces = [slice(r, r + IN_PIECE) for r in range(0, IN_TILE, IN_PIECE)]
    hbs = [_norm_mod(x_ref[0, rows, :], n1_ref[...], sh_ref[0], sc_ref[0]).astype(BF16) for rows in pieces]
    for rows, hb in zip(pieces, hbs):
        zhy_ref[0, rows, :] = _dot(hb, why_ref[...]).astype(BF16)
        v_ref[0, rows, :] = _dot(hb, wv_ref[...]).astype(BF16)
        g_ref[0, rows, :] = _dot(hb, wg_ref[...]).astype(BF16)
        q = _dot(hb, wq_ref[...])
        q = q * cq_ref[rows, :] + _swap_halves(q, 1) * sq_ref[rows, :]
        for hh in range(RET_HEADS):
            q_ref[0, hh, rows, :] = q[:, hh * RET_QK_DIM:(hh + 1) * RET_QK_DIM].astype(BF16)
        kt = lax.dot_general(wkt_ref[...], hb, _NT, preferred_element_type=F32)
        kt = kt * ck_ref[:, rows] + _swap_halves(kt, 0) * sk_ref[:, rows]
        kt_ref[0, :, rows] = kt.astype(BF16)


def _inproj_call(x, mod3, norm1, w_in, w_kt, rope, w_up, w_dn, w_out):
    cq, sq, ck, sk = rope
    t = IN_TILE
    nt = SEQ // t
    qk = RET_HEADS * RET_QK_DIM

    def row_slab(w):
        return pl.BlockSpec((w.shape[0] // (BATCH * nt), w.shape[1]), lambda b, i: (b * nt + i, 0))

    slabs = [row_slab(w_up), row_slab(w_dn), row_slab(w_out)]
    return pl.pallas_call(
        _inproj_kernel,
        grid=(BATCH, SEQ // t),
        in_specs=[pl.BlockSpec((1, t, D_MODEL), lambda b, i: (b, i, 0)),
                  pl.BlockSpec((1, 1, D_MODEL), lambda b, i: (b, 0, 0)),
                  pl.BlockSpec((1, 1, D_MODEL), lambda b, i: (b, 0, 1)),
                  _resident((1, D_MODEL)),
                  _resident_cols(D_MODEL, 0, HY_COLS),
                  _resident_cols(D_MODEL, Q_OFF, qk),
                  _resident_cols(D_MODEL, V_OFF, RET_WIDTH),
                  _resident_cols(D_MODEL, G_OFF, RET_WIDTH),
                  _resident((qk, D_MODEL)),
                  pl.BlockSpec((t, qk), lambda b, i: (i, 0)),
                  pl.BlockSpec((t, qk), lambda b, i: (i, 0)),
                  pl.BlockSpec((qk, t), lambda b, i: (0, i)),
                  pl.BlockSpec((qk, t), lambda b, i: (0, i))] + slabs,
        out_specs=[pl.BlockSpec((1, t, HY_COLS), lambda b, i: (b, i, 0)),
                   pl.BlockSpec((1, RET_HEADS, t, RET_QK_DIM), lambda b, i: (b, 0, i, 0)),
                   pl.BlockSpec((1, qk, t), lambda b, i: (b, 0, i)),
                   pl.BlockSpec((1, t, RET_WIDTH), lambda b, i: (b, i, 0)),
                   pl.BlockSpec((1, t, RET_WIDTH), lambda b, i: (b, i, 0))] + slabs,
        out_shape=[jax.ShapeDtypeStruct((BATCH, SEQ, HY_COLS), BF16),
                   jax.ShapeDtypeStruct((BATCH, RET_HEADS, SEQ, RET_QK_DIM), BF16),
                   jax.ShapeDtypeStruct((BATCH, qk, SEQ), BF16),
                   jax.ShapeDtypeStruct((BATCH, SEQ, RET_WIDTH), BF16),
                   jax.ShapeDtypeStruct((BATCH, SEQ, RET_WIDTH), BF16),
                   jax.ShapeDtypeStruct(w_up.shape, BF16),
                   jax.ShapeDtypeStruct(w_dn.shape, BF16),
                   jax.ShapeDtypeStruct(w_out.shape, BF16)],
        compiler_params=pltpu.CompilerParams(vmem_limit_bytes=VMEM_LIMIT),
        name="inproj",
    )(x, mod3, mod3, norm1, w_in, w_in, w_in, w_in, w_kt, cq, sq, ck, sk, w_up, w_dn, w_out)


def _filter_mlp_kernel(zt_ref, t_ref, w1t_ref, b1_ref, f1_ref, w2t_ref, b2_ref, f2_ref, w3t_ref, adel_ref,
                       kt_ref):
    hid = jnp.sin(f1_ref[...] * (_dot_hi(w1t_ref[...], zt_ref[...]) + b1_ref[...]))
    hid = jnp.sin(f2_ref[...] * (_dot_hi(w2t_ref[...], hid) + b2_ref[...])).astype(BF16)
    c = HY_WIDTH
    for half in range(2):
        lags = slice(half * SEQ, (half + 1) * SEQ)
        window = jnp.exp(-adel_ref[...] * t_ref[:, lags])
        direction = 1 - half
        for o in range(HY_ORDER):
            r0 = (direction * HY_ORDER + o) * c
            w3 = w3t_ref[r0:r0 + c, :].astype(BF16)
            kt_ref[o, :, lags] = (_dot(w3, hid[:, lags]) * window).astype(BF16)


def _filter_mlp_call(zt, t_row, w1t, b1, f1, w2t, b2, f2, w3t, absdelta):
    fw = HY_FILTER_WIDTH
    n_out = 2 * HY_ORDER * HY_WIDTH
    return pl.pallas_call(
        _filter_mlp_kernel,
        grid=(1,),
        in_specs=[_resident((fw, 2 * SEQ)), _resident((1, 2 * SEQ)),
                  _resident((fw, fw)), _resident((fw, 1)), _resident((fw, 1)),
                  _resident((fw, fw)), _resident((fw, 1)), _resident((fw, 1)),
                  _resident((n_out, fw)), _resident((HY_WIDTH, 1))],
        out_specs=pl.BlockSpec((HY_ORDER, HY_WIDTH, 2 * SEQ), lambda i: (0, 0, 0)),
        out_shape=jax.ShapeDtypeStruct((HY_ORDER, HY_WIDTH, 2 * SEQ), BF16),
        compiler_params=pltpu.CompilerParams(vmem_limit_bytes=VMEM_LIMIT),
        name="filter_mlp",
    )(zt, t_row, w1t, b1, f1, w2t, b2, f2, w3t, absdelta)


_FILTER_PLANES = (
    {0: 1},
    {-1: 1, 0: -1},
    {1: 1, 0: -1},
    {-2: 1, 0: -1},
    {-3: 1, -1: -1, -2: -1, 0: 1},
    {-1: 1, 1: -1, -2: -1, 0: 1},
    {2: 1, 0: -1},
    {1: 1, -1: -1, 2: -1, 0: 1},
    {3: 1, 1: -1, 2: -1, 0: 1},
)
DFT_BLOCKS_PER_STEP = 2
PLANES_PER_STEP = 3


def _filter_dft_kernel(kt_ref, fwd_ref, h_ref, t_scr):
    p = CONV_BLOCK
    n_dft_steps = 2 * N_CONV_BLOCKS // DFT_BLOCKS_PER_STEP
    step = pl.program_id(1)

    for s in range(n_dft_steps):
        @pl.when(step == s)
        def _(s=s):
            for k in range(DFT_BLOCKS_PER_STEP):
                t_scr[s * DFT_BLOCKS_PER_STEP + k] = lax.dot_general(
                    fwd_ref[...], kt_ref[0, :, k * p:(k + 1) * p], _NT, preferred_element_type=F32)
            for k in range(PLANES_PER_STEP):
                h_ref[0, k] = t_scr[s * DFT_BLOCKS_PER_STEP]

    row = lax.broadcasted_iota(jnp.int32, (p, 1), 0)
    sign = (1 - 2 * (row & 1)).astype(F32)
    re, im = slice(0, p), slice(p, 2 * p)

    def plane(k, coefs):
        def comb(shift, rows):
            acc = None
            for d, c in coefs.items():
                term = t_scr[d + N_CONV_BLOCKS - 1 + shift, rows, :]
                if acc is None:
                    acc = term
                else:
                    acc = acc + term if c > 0 else acc - term
            return acc

        h_ref[0, k, re, :] = comb(0, re) + sign * comb(1, im)
        h_ref[0, k, im, :] = comb(0, im) - sign * comb(1, re)

    for s in range(len(_FILTER_PLANES) // PLANES_PER_STEP):
        @pl.when(step == n_dft_steps + s)
        def _(s=s):
            for k in range(PLANES_PER_STEP):
                plane(k, _FILTER_PLANES[s * PLANES_PER_STEP + k])


def _filter_dft_call(kt, fwd_u):
    p = CONV_BLOCK
    n_blk = 2 * N_CONV_BLOCKS
    n_planes = len(_FILTER_PLANES)
    n_dft_steps = n_blk // DFT_BLOCKS_PER_STEP
    return pl.pallas_call(
        _filter_dft_kernel,
        grid=(HY_ORDER, n_dft_steps + n_planes // PLANES_PER_STEP),
        in_specs=[pl.BlockSpec((1, HY_WIDTH, DFT_BLOCKS_PER_STEP * p),
                               lambda o, s: (o, 0, jnp.minimum(s, n_dft_steps - 1))),
                  _resident((2 * p, p))],
        out_specs=pl.BlockSpec((1, PLANES_PER_STEP, 2 * p, HY_WIDTH),
                               lambda o, s: (o, jnp.maximum(s - n_dft_steps, 0), 0, 0)),
        out_shape=jax.ShapeDtypeStruct((HY_ORDER, n_planes, 2 * p, HY_WIDTH), F32),
        scratch_shapes=[pltpu.VMEM((n_blk, 2 * p, HY_WIDTH), F32)],
        compiler_params=pltpu.CompilerParams(vmem_limit_bytes=VMEM_LIMIT),
        name="filter_dft",
    )(kt, fwd_u)


def _conv3_rows(ref, j, n_blocks, rows, w, b):
    main = ref[0, j * rows:(j + 1) * rows, :].astype(F32)
    cols = main.shape[1]
    zeros = jnp.zeros((HALO, cols), F32)
    prev = ref[0, j * rows - HALO:j * rows, :].astype(F32) if j > 0 else zeros
    nxt = ref[0, (j + 1) * rows:(j + 1) * rows + HALO, :].astype(F32) if j < n_blocks - 1 else zeros
    ext = jnp.concatenate([prev, main, nxt], axis=0)
    n = rows + 2 * HALO
    before = pltpu.roll(ext, 1, 0)[HALO:HALO + rows]
    after = pltpu.roll(ext, n - 1, 0)[HALO:HALO + rows]
    return before * w[0:1] + main * w[1:2] + after * w[2:3] + b


def _hyena_kernel(u_ref, zg_ref, cwu_ref, cbu_ref, cwg_ref, cbg_ref, skip_ref, h_ref, fwd_ref, inv_ref,
                  o_ref, uf_scr, ub_scr, y_scr, *, conv_u):
    p = CONV_BLOCK
    nb = N_CONV_BLOCKS

    def prepare(j):
        if conv_u:
            ub_scr[j * p:(j + 1) * p, :] = _conv3_rows(u_ref, j, nb, p, cwu_ref[...], cbu_ref[...]).astype(BF16)

    def u_block(j):
        return ub_scr[j * p:(j + 1) * p, :] if conv_u else u_ref[0, j * p:(j + 1) * p, :]

    prepare(0)
    for j in range(nb):
        if j + 1 < nb:
            prepare(j + 1)
        for r in range(2 * p // DFT_ROWS):
            rows = slice(r * DFT_ROWS, (r + 1) * DFT_ROWS)
            uf_scr[j, rows, :] = _dot(fwd_ref[rows, :], u_block(j))

    def cadd(a, b):
        return a[0] + b[0], a[1] + b[1]

    def cmul(m, x):
        return m[0] * x[0] - m[1] * x[1], m[0] * x[1] + m[1] * x[0]

    def toeplitz2(k0, x0, x1, re, im, ln):
        m0, mu, ml = ((h_ref[0, k0 + t, re, ln], h_ref[0, k0 + t, im, ln]) for t in range(3))
        p1 = cmul(m0, cadd(x0, x1))
        return cadd(p1, cmul(mu, x1)), cadd(p1, cmul(ml, x0))

    assert nb == 4
    for r in range(p // FREQ_ROWS):
        re = slice(r * FREQ_ROWS, (r + 1) * FREQ_ROWS)
        im = slice(p + r * FREQ_ROWS, p + (r + 1) * FREQ_ROWS)
        for cb in range(HY_WIDTH // FREQ_LANES):
            ln = slice(cb * FREQ_LANES, (cb + 1) * FREQ_LANES)
            u = [(uf_scr[j, re, ln], uf_scr[j, im, ln]) for j in range(nb)]
            d0, d1 = toeplitz2(0, cadd(u[0], u[2]), cadd(u[1], u[3]), re, im, ln)
            b0, b1 = toeplitz2(3, u[2], u[3], re, im, ln)
            c0, c1 = toeplitz2(6, u[0], u[1], re, im, ln)
            for i, yi in enumerate((cadd(d0, b0), cadd(d1, b1), cadd(d0, c0), cadd(d1, c1))):
                y_scr[i, re, ln] = yi[0].astype(BF16)
                y_scr[i, im, ln] = yi[1].astype(BF16)

    for i in range(nb):
        gate = _conv3_rows(zg_ref, i, nb, p, cwg_ref[...], cbg_ref[...])
        y = _dot(inv_ref[...], y_scr[i])
        o_ref[0, i * p:(i + 1) * p, :] = (
            gate * (y + u_block(i).astype(F32) * skip_ref[...])).astype(BF16)


def _hyena_call(u, u_col, zhy, gate_col, conv_w, conv_b, skip, g_all, order, fwd_u, inv, conv_u):
    p = CONV_BLOCK
    c = HY_WIDTH
    ucol = u_col if conv_u else 0
    cwu = conv_w[:, ucol * c:(ucol + 1) * c]
    cbu = conv_b[:, ucol * c:(ucol + 1) * c]
    cwg = conv_w[:, gate_col * c:(gate_col + 1) * c]
    cbg = conv_b[:, gate_col * c:(gate_col + 1) * c]
    return pl.pallas_call(
        functools.partial(_hyena_kernel, conv_u=conv_u),
        grid=(BATCH,),
        in_specs=[pl.BlockSpec((1, SEQ, c), lambda b: (b, 0, u_col)),
                  pl.BlockSpec((1, SEQ, c), lambda b: (b, 0, gate_col)),
                  _resident((3, c)), _resident((1, c)), _resident((3, c)), _resident((1, c)),
                  _resident((1, c)),
                  pl.BlockSpec((1, len(_FILTER_PLANES), 2 * p, c), lambda b: (order, 0, 0, 0),
                               pipeline_mode=pl.Buffered(1)),
                  _resident((2 * p, p)),
                  _resident((p, 2 * p))],
        out_specs=pl.BlockSpec((1, SEQ, c), lambda b: (b, 0, 0)),
        out_shape=jax.ShapeDtypeStruct((BATCH, SEQ, c), BF16),
        scratch_shapes=[pltpu.VMEM((N_CONV_BLOCKS, 2 * p, c), F32),
                        pltpu.VMEM((SEQ, c) if conv_u else (SUBLANES * 2, LANES), BF16),
                        pltpu.VMEM((N_CONV_BLOCKS, 2 * p, c), BF16)],
        compiler_params=pltpu.CompilerParams(vmem_limit_bytes=HYENA_VMEM_LIMIT),
        name="hyena%d" % order,
    )(u, zhy, cwu, cbu, cwg, cbg, skip, g_all, fwd_u, inv)


def _ret_kernel(lg_ref, q_ref, kt_ref, v_ref, g_ref, s_ref, cc_ref, wmod_ref, bmod_ref,
                o_ref, modl_ref, b_scr, decay_scr, qw_scr):
    modl_ref[...] = _dot(_silu(cc_ref[...]).astype(BF16), wmod_ref[...].astype(BF16)) + bmod_ref[...]
    c = RET_CHUNK
    nc = SEQ // c
    dk = RET_QK_DIM
    dv = RET_V_DIM
    heads = range(RET_HEADS_PER_STEP)
    lg_f = [lg_ref[0, pl.program_id(0) * RET_HEADS_PER_STEP + hh] for hh in heads]
    lg_b = [lg_ref[1, pl.program_id(0) * RET_HEADS_PER_STEP + hh] for hh in heads]

    @pl.when(pl.program_id(1) == 0)
    def _():
        ii = lax.broadcasted_iota(jnp.int32, (c, c), 0).astype(F32)
        jj = lax.broadcasted_iota(jnp.int32, (c, c), 1).astype(F32)
        dif = ii - jj
        pos_q = lax.broadcasted_iota(jnp.int32, (c, dk), 0).astype(F32)
        for hh in heads:
            decay_scr[hh] = jnp.where(dif >= 0.0, jnp.exp(lg_f[hh] * jnp.maximum(dif, 0.0)),
                                      jnp.exp(lg_b[hh] * jnp.maximum(-dif, 0.0)))
            qw_scr[hh, 0] = jnp.exp(lg_f[hh] * (pos_q + 1.0))
            qw_scr[hh, 1] = jnp.exp(lg_b[hh] * (c - pos_q))

    pos_r = lax.broadcasted_iota(jnp.int32, (1, c), 1).astype(F32)
    ones = jnp.ones((1, dv), F32)
    kw_f = [jnp.exp(lg_f[hh] * (c - 1.0 - pos_r)) for hh in heads]
    kw_b = [jnp.exp(lg_b[hh] * pos_r) for hh in heads]
    dec_f = [jnp.exp(lg_f[hh] * float(c) * ones) for hh in heads]
    dec_b = [jnp.exp(lg_b[hh] * float(c) * ones) for hh in heads]

    def kt_chunk(hh, n):
        return kt_ref[0, hh * dk:(hh + 1) * dk, n * c:(n + 1) * c]

    def v_chunk(hh, n):
        return v_ref[0, n * c:(n + 1) * c, hh * dv:(hh + 1) * dv]

    state = [s_ref[0, hh, dk:2 * dk, :] for hh in heads]
    for hh in heads:
        b_scr[hh, nc - 1] = state[hh]
    for n in range(nc - 1, 0, -1):
        for hh in heads:
            ktn = (kt_chunk(hh, n).astype(F32) * kw_b[hh]).astype(BF16)
            state[hh] = state[hh] * dec_b[hh] + _dot(ktn, v_chunk(hh, n))
            b_scr[hh, n - 1] = state[hh]

    state = [s_ref[0, hh, 0:dk, :] for hh in heads]
    for n in range(nc):
        for hh in heads:
            qn = q_ref[0, hh, n * c:(n + 1) * c, :]
            ktn = kt_chunk(hh, n)
            vn = v_chunk(hh, n)
            scores = (_dot(qn, ktn) * decay_scr[hh]).astype(BF16)
            qf = qn.astype(F32)
            o = _dot(scores, vn)
            o = o + _dot((qf * qw_scr[hh, 0]).astype(BF16), state[hh].astype(BF16))
            o = o + _dot((qf * qw_scr[hh, 1]).astype(BF16), b_scr[hh, n].astype(BF16))
            state[hh] = state[hh] * dec_f[hh] + _dot((ktn.astype(F32) * kw_f[hh]).astype(BF16), vn)
            o = o * lax.rsqrt(jnp.mean(o * o, axis=-1, keepdims=True) + EPS)
            gate = g_ref[0, n * c:(n + 1) * c, hh * dv:(hh + 1) * dv].astype(F32)
            o_ref[0, n * c:(n + 1) * c, hh * dv:(hh + 1) * dv] = (_silu(gate) * o).astype(BF16)


def _ret_call(lg, q, kt, v, g, s, cc, w_mod, b_mod):
    dk, dv = RET_QK_DIM, RET_V_DIM
    hp = RET_HEADS_PER_STEP
    assert RET_HEADS == hp
    mcols = MOD_LATE_COLS // BATCH
    mod_first = MOD_EARLY_COLS // mcols
    mod_slab = pl.BlockSpec((MOD_ROWS, mcols), lambda h, b: (0, b))
    return pl.pallas_call(
        _ret_kernel,
        grid=(RET_HEADS // hp, BATCH),
        in_specs=[pl.BlockSpec(memory_space=pltpu.SMEM),
                  pl.BlockSpec((1, hp, SEQ, dk), lambda h, b: (b, h, 0, 0)),
                  pl.BlockSpec((1, hp * dk, SEQ), lambda h, b: (b, h, 0)),
                  pl.BlockSpec((1, SEQ, hp * dv), lambda h, b: (b, 0, h)),
                  pl.BlockSpec((1, SEQ, hp * dv), lambda h, b: (b, 0, h)),
                  pl.BlockSpec((1, hp, 2 * dk, dv), lambda h, b: (b, h, 0, 0)),
                  _resident((MOD_ROWS, D_MODEL)),
                  pl.BlockSpec((D_MODEL, mcols), lambda h, b: (0, mod_first + b)),
                  pl.BlockSpec((1, mcols), lambda h, b: (0, mod_first + b))],
        out_specs=[pl.BlockSpec((1, SEQ, hp * dv), lambda h, b: (b, 0, h)), mod_slab],
        out_shape=[jax.ShapeDtypeStruct((BATCH, SEQ, RET_WIDTH), BF16),
                   jax.ShapeDtypeStruct((MOD_ROWS, MOD_LATE_COLS), F32)],
        scratch_shapes=[pltpu.VMEM((hp, SEQ // RET_CHUNK, dk, dv), F32),
                        pltpu.VMEM((hp, RET_CHUNK, RET_CHUNK), F32),
                        pltpu.VMEM((hp, 2, RET_CHUNK, dk), F32)],
        compiler_params=pltpu.CompilerParams(vmem_limit_bytes=VMEM_LIMIT),
        name="ret",
    )(lg, q, kt, v, g, s, cc, w_mod, b_mod)


def _ffn_kernel(x_ref, xp_ref, xn_ref, yh_ref, yhp_ref, yhn_ref, yr_ref, yrp_ref, yrn_ref,
                g1_ref, sh_ref, sc_ref, g2_ref, n2_ref, nf_ref,
                woh_ref, wor_ref, wup_ref, cw_ref, cb_ref, wdn_ref,
                o_ref, hb_scr, x1_scr, av_scr, ag_scr, act_scr):
    t = TOK_TILE
    i = pl.program_id(1)
    nt = pl.num_programs(1)
    th = t // 2
    tile_halves = (slice(0, th), slice(th, t))

    def mixed(xr, yh, yr, rows=slice(None)):
        return xr[0, rows, :] + g1_ref[0] * (_dot(yh[0, rows, :], woh_ref[...]) + _dot(yr[0, rows, :], wor_ref[...]))

    def hidden(x1):
        return _norm_mod(x1, n2_ref[...], sh_ref[0], sc_ref[0])

    half = (t + 2 * HALO) // 2
    up_halves = (slice(0, half), slice(half, 2 * half))

    def up(slot, cblk, rows):
        c0 = cblk * FFN_COLS
        hb = hb_scr[rows, :]
        av_scr[slot, rows, :] = _dot(hb, wup_ref[:, c0:c0 + FFN_COLS])
        ag_scr[slot, rows, :] = _dot(hb, wup_ref[:, D_FF + c0:D_FF + c0 + FFN_COLS])

    for rows in tile_halves:
        x1_scr[rows, :] = mixed(x_ref, yh_ref, yr_ref, rows)
    hp = hidden(mixed(xp_ref, yhp_ref, yrp_ref))
    hn = hidden(mixed(xn_ref, yhn_ref, yrn_ref))
    hb_scr[0:HALO, :] = jnp.where(i > 0, hp, 0.0).astype(BF16)
    hb_scr[HALO + t:2 * HALO + t, :] = jnp.where(i < nt - 1, hn, 0.0).astype(BF16)
    hb_scr[HALO:HALO + th, :] = hidden(x1_scr[tile_halves[0], :]).astype(BF16)
    up(0, 0, up_halves[0])
    hb_scr[HALO + th:HALO + t, :] = hidden(x1_scr[tile_halves[1], :]).astype(BF16)
    up(0, 0, up_halves[1])

    def conv(scr, slot, col):
        w = cw_ref[:, col:col + FFN_COLS]
        return (scr[slot, HALO - 1:HALO - 1 + t, :] * w[0:1] + scr[slot, HALO:HALO + t, :] * w[1:2]
                + scr[slot, HALO + 1:HALO + 1 + t, :] * w[2:3] + cb_ref[:, col:col + FFN_COLS])

    n_blk = D_FF // FFN_COLS
    for cblk in range(n_blk):
        c0 = cblk * FFN_COLS
        slot = cblk % 2
        if cblk + 1 < n_blk:
            for rows in up_halves:
                up(1 - slot, cblk + 1, rows)
        act_scr[:, c0:c0 + FFN_COLS] = (
            _silu(conv(ag_scr, slot, D_FF + c0)) * conv(av_scr, slot, c0)).astype(BF16)

    down = [_dot(act_scr[rows, :], wdn_ref[...]) for rows in tile_halves]
    for rows, ffn in zip(tile_halves, down):
        x2 = x1_scr[rows, :] + g2_ref[0] * ffn
        o_ref[0, rows, :] = x2 * lax.rsqrt(jnp.mean(x2 * x2, axis=-1, keepdims=True) + EPS) * nf_ref[...]


def _ffn_call(x, y_hy, y_ret, mod3, norm2, norm_f, w_oh, w_or, w_up, conv_w, conv_b, w_dn):
    t = TOK_TILE
    r = t // HALO
    last = SEQ // HALO - 1

    def main(width):
        return pl.BlockSpec((1, t, width), lambda b, i: (b, i, 0))

    def prev(width):
        return pl.BlockSpec((1, HALO, width), lambda b, i: (b, jnp.maximum(i * r - 1, 0), 0))

    def nxt(width):
        return pl.BlockSpec((1, HALO, width), lambda b, i: (b, jnp.minimum((i + 1) * r, last), 0))

    def modrow(k):
        return pl.BlockSpec((1, 1, D_MODEL), lambda b, i: (b, 0, k))

    return pl.pallas_call(
        _ffn_kernel,
        grid=(BATCH, SEQ // t),
        in_specs=[main(D_MODEL), prev(D_MODEL), nxt(D_MODEL),
                  main(HY_WIDTH), prev(HY_WIDTH), nxt(HY_WIDTH),
                  main(RET_WIDTH), prev(RET_WIDTH), nxt(RET_WIDTH),
                  modrow(0), modrow(1), modrow(2), modrow(3),
                  _resident((1, D_MODEL)), _resident((1, D_MODEL)),
                  _resident_rows(0, HY_WIDTH, D_MODEL), _resident_rows(HY_WIDTH, RET_WIDTH, D_MODEL),
                  _resident((D_MODEL, 2 * D_FF)),
                  _resident((3, 2 * D_FF)), _resident((1, 2 * D_FF)),
                  _resident((D_FF, D_MODEL))],
        out_specs=pl.BlockSpec((1, t, D_MODEL), lambda b, i: (b, i, 0)),
        out_shape=jax.ShapeDtypeStruct((BATCH, SEQ, D_MODEL), F32),
        scratch_shapes=[pltpu.VMEM((t + 2 * HALO, D_MODEL), BF16),
                        pltpu.VMEM((t, D_MODEL), F32),
                        pltpu.VMEM((2, t + 2 * HALO, FFN_COLS), F32),
                        pltpu.VMEM((2, t + 2 * HALO, FFN_COLS), F32),
                        pltpu.VMEM((t, D_FF), BF16)],
        compiler_params=pltpu.CompilerParams(vmem_limit_bytes=VMEM_LIMIT),
        name="ffn",
    )(x, x, x, y_hy, y_hy, y_hy, y_ret, y_ret, y_ret, mod3, mod3, mod3, mod3,
      norm2, norm_f, w_oh, w_or, w_up, conv_w, conv_b, w_dn)


def kernel(x, c, ctx, c_ctx, w_mod, b_mod, norm1, w_in, hy_conv_w, hy_conv_b, hy_w1, hy_b1, hy_f1,
           hy_w2, hy_b2, hy_f2, hy_w3, hy_bias, ret_logit_f, ret_logit_b, w_out, norm2,
           ffn_w_up, ffn_conv_w, ffn_conv_b, ffn_w_down, norm_f):
    layer = 0
    rope = tuple(jnp.asarray(a) for a in _rope_tables())
    zt, t_row, absdelta = (jnp.asarray(a) for a in _filter_features())
    fwd_np, inv_np = _dft_matrices()
    fwd_u = jnp.asarray(fwd_np[:, :CONV_BLOCK]).astype(BF16)
    inv = jnp.asarray(inv_np).astype(BF16)

    w_in_b = w_in[layer].astype(BF16)
    w_kt = w_in_b[:, K_OFF:V_OFF].T
    row = lambda a: a.reshape(1, -1)
    col = lambda a: a.reshape(-1, 1)
    w1p = jnp.pad(hy_w1[layer], ((0, HY_FILTER_WIDTH - hy_w1.shape[1]), (0, 0)))
    lg = jnp.stack([jax.nn.log_sigmoid(ret_logit_f[layer].astype(F32)),
                    jax.nn.log_sigmoid(ret_logit_b[layer].astype(F32))])

    cc = jnp.concatenate([c, c_ctx[None, :], jnp.zeros((MOD_ROWS - BATCH - 1, D_MODEL), F32)], axis=0)
    mod3 = _mod_call(cc, w_mod[layer], row(b_mod[layer])).reshape(MOD_ROWS, 1, MOD_EARLY_COLS)
    norm1_r = row(norm1[layer])

    s_ctx = _ctx_call(lg, ctx, mod3, norm1_r, w_kt, w_in_b)
    zhy, q, kt, v, g, w_up, w_dn, w_out_b = _inproj_call(
        x, mod3, norm1_r, w_in_b, w_kt, rope, ffn_w_up[layer], ffn_w_down[layer], w_out[layer])

    k_two_sided = _filter_mlp_call(zt, t_row, w1p.T, col(hy_b1[layer]), col(hy_f1[layer]), hy_w2[layer].T,
                                   col(hy_b2[layer]), col(hy_f2[layer]), hy_w3[layer].T, absdelta)
    g_all = _filter_dft_call(k_two_sided, fwd_u)
    conv_w, conv_b = hy_conv_w[layer], row(hy_conv_b[layer])
    y1 = _hyena_call(zhy, 0, zhy, 1, conv_w, conv_b, hy_bias[layer][0:1], g_all, 0, fwd_u, inv, True)
    y_hy = _hyena_call(y1, 0, zhy, 2, conv_w, conv_b, hy_bias[layer][1:2], g_all, 1, fwd_u, inv, False)

    y_ret, mod_late = _ret_call(lg, q, kt, v, g, s_ctx, cc, w_mod[layer], row(b_mod[layer]))
    mod_late3 = mod_late.reshape(MOD_ROWS, 1, MOD_LATE_COLS)

    return _ffn_call(x, y_hy, y_ret, mod_late3, row(norm2[layer]), row(norm_f), w_out_b, w_out_b, w_up,
                     ffn_conv_w[layer], row(ffn_conv_b[layer]), w_dn)
```

```python
import functools
import math

import numpy as np
import jax
import jax.numpy as jnp
from jax import lax
from jax.experimental import pallas as pl
from jax.experimental.pallas import tpu as pltpu

F32 = jnp.float32
BF16 = jnp.bfloat16

D_MODEL = 1024
BATCH = 8
SEQ = 2048
CTX_LEN = 256
GRID_W = 64
HY_WIDTH = 512
HY_ORDER = 2
HY_EMB_BANDS = 16
HY_FILTER_WIDTH = 64
HY_FAST_DECAY = 0.3
HY_SLOW_DECAY = 1.5
HY_TARGET = 1e-2
RET_WIDTH = 512
RET_HEADS = 4
RET_QK_DIM = 64
RET_V_DIM = 128
ROPE_BASE = 10000.0
D_FF = 2816
EPS = 1e-6
HY_COLS = (HY_ORDER + 1) * HY_WIDTH
Q_OFF = HY_COLS
K_OFF = Q_OFF + RET_HEADS * RET_QK_DIM
V_OFF = K_OFF + RET_HEADS * RET_QK_DIM
G_OFF = V_OFF + RET_WIDTH
K_SCALE = RET_QK_DIM ** -0.5

MOD_ROWS = 16
MOD_EARLY_COLS = 2 * D_MODEL
MOD_LATE_COLS = 4 * D_MODEL
TOK_TILE = 512
IN_TILE = 1024
IN_PIECE = 256
SUBLANES = 8
LANES = 128
HALO = 16
CONV_BLOCK = 512
N_CONV_BLOCKS = SEQ // CONV_BLOCK
FREQ_ROWS = 16
FREQ_LANES = 128
DFT_ROWS = 256
CTX_PER_STEP = 4
RET_CHUNK = 256
RET_HEADS_PER_STEP = 4
FFN_COLS = 256
VMEM_LIMIT = 56 * 1024 * 1024
HYENA_VMEM_LIMIT = 62 * 1024 * 1024

_NT = (((1,), (1,)), ((), ()))


def _dot(a, b):
    return jnp.dot(a, b, preferred_element_type=F32)


def _dot_hi(a, b):
    return jnp.dot(a, b, preferred_element_type=F32, precision=lax.Precision.HIGHEST)


def _silu(x):
    return x * (1.0 / (1.0 + jnp.exp(-x)))


def _norm_mod(x, gain, shift, scale):
    y = x * lax.rsqrt(jnp.mean(x * x, axis=-1, keepdims=True) + EPS)
    return (y * gain) * (1.0 + scale) + shift


def _resident(shape):
    nd = len(shape)
    return pl.BlockSpec(shape, lambda *_: (0,) * nd, pipeline_mode=pl.Buffered(1))


def _resident_cols(rows, col0, width):
    assert col0 % width == 0
    return pl.BlockSpec((rows, width), lambda *_: (0, col0 // width), pipeline_mode=pl.Buffered(1))


def _resident_rows(row0, height, cols):
    assert row0 % height == 0
    return pl.BlockSpec((height, cols), lambda *_: (row0 // height, 0), pipeline_mode=pl.Buffered(1))


@functools.lru_cache(maxsize=None)
def _rope_tables():
    pos = np.arange(SEQ)
    row = (pos // GRID_W).astype(np.float64)
    col = (pos % GRID_W).astype(np.float64)
    quarter = RET_QK_DIM // 4
    inv_freq = ROPE_BASE ** (-np.arange(quarter, dtype=np.float64) / quarter)
    ang = np.concatenate([row[:, None] * inv_freq, col[:, None] * inv_freq], axis=-1)
    cos, sin = np.cos(ang), np.sin(ang)
    cos_h = np.concatenate([cos, cos], axis=-1)
    sin_h = np.concatenate([-sin, sin], axis=-1)
    cos_t = np.tile(cos_h, (1, RET_HEADS))
    sin_t = np.tile(sin_h, (1, RET_HEADS))
    return (cos_t.astype(np.float32), sin_t.astype(np.float32),
            np.ascontiguousarray((cos_t * K_SCALE).T).astype(np.float32),
            np.ascontiguousarray((sin_t * K_SCALE).T).astype(np.float32))


@functools.lru_cache(maxsize=None)
def _filter_features():
    lag = np.abs(np.arange(2 * SEQ) - SEQ).astype(np.float64)
    t = lag / (SEQ - 1)
    bands = np.linspace(1e-4, HY_EMB_BANDS - 1, HY_EMB_BANDS)
    ang = 2.0 * math.pi * lag[:, None] * bands[None, :] / SEQ
    z = np.concatenate([t[:, None], np.cos(ang), -np.sin(ang)], axis=-1)
    zp = np.zeros((2 * SEQ, HY_FILTER_WIDTH), np.float64)
    zp[:, :z.shape[1]] = z
    max_decay = math.log(HY_TARGET) / HY_FAST_DECAY
    min_decay = math.log(HY_TARGET) / HY_SLOW_DECAY
    absdelta = np.abs(np.linspace(min_decay, max_decay, HY_WIDTH))[:, None]
    return (np.ascontiguousarray(zp.T).astype(np.float32), t[None, :].astype(np.float32),
            absdelta.astype(np.float32))


@functools.lru_cache(maxsize=None)
def _dft_matrices():
    p = CONV_BLOCK
    n = 2 * p
    f = np.arange(p, dtype=np.float64)[:, None] + 0.5
    t = np.arange(n, dtype=np.float64)[None, :]
    theta = 2.0 * math.pi * f * t / n
    fwd = np.concatenate([np.cos(theta), -np.sin(theta)], axis=0)
    th_out = theta[:, p:].T
    inv = np.concatenate([np.cos(th_out), -np.sin(th_out)], axis=1) / p
    return fwd.astype(np.float32), inv.astype(np.float32)


def _mod_kernel(c_ref, w_ref, b_ref, o_ref):
    s = _silu(c_ref[...]).astype(BF16)
    o_ref[...] = _dot(s, w_ref[...].astype(BF16)) + b_ref[...]


def _mod_call(cc, w_mod, b_mod):
    ncol = MOD_EARLY_COLS
    blk = ncol // 2
    return pl.pallas_call(
        _mod_kernel,
        grid=(ncol // blk,),
        in_specs=[pl.BlockSpec((MOD_ROWS, D_MODEL), lambda j: (0, 0)),
                  pl.BlockSpec((D_MODEL, blk), lambda j: (0, j)),
                  pl.BlockSpec((1, blk), lambda j: (0, j))],
        out_specs=pl.BlockSpec((MOD_ROWS, blk), lambda j: (0, j)),
        out_shape=jax.ShapeDtypeStruct((MOD_ROWS, ncol), F32),
        compiler_params=pltpu.CompilerParams(vmem_limit_bytes=VMEM_LIMIT),
        name="mod",
    )(cc, w_mod, b_mod)


def _ctx_kernel(lg_ref, x_ref, sh_ref, sc_ref, n1_ref, wkt_ref, wv_ref, s_ref):
    pos = lax.broadcasted_iota(jnp.int32, (1, CTX_LEN), 1).astype(F32)
    w_f = [jnp.exp(lg_ref[0, hh] * (CTX_LEN - 1.0 - pos)) for hh in range(RET_HEADS)]
    w_b = [jnp.exp(lg_ref[1, hh] * pos) for hh in range(RET_HEADS)]
    for bb in range(CTX_PER_STEP):
        h = _norm_mod(x_ref[bb], n1_ref[...], sh_ref[0], sc_ref[0]).astype(BF16)
        kt = lax.dot_general(wkt_ref[...], h, _NT, preferred_element_type=F32) * K_SCALE
        v = _dot(h, wv_ref[...])
        for hh in range(RET_HEADS):
            kth = kt[hh * RET_QK_DIM:(hh + 1) * RET_QK_DIM, :]
            vh = v[:, hh * RET_V_DIM:(hh + 1) * RET_V_DIM].astype(BF16)
            s_ref[bb, hh, 0:RET_QK_DIM, :] = _dot((kth * w_f[hh]).astype(BF16), vh)
            s_ref[bb, hh, RET_QK_DIM:2 * RET_QK_DIM, :] = _dot((kth * w_b[hh]).astype(BF16), vh)


def _ctx_call(lg, ctx, mod3, norm1, w_kt, w_v):
    return pl.pallas_call(
        _ctx_kernel,
        grid=(BATCH // CTX_PER_STEP,),
        in_specs=[pl.BlockSpec(memory_space=pltpu.SMEM),
                  pl.BlockSpec((CTX_PER_STEP, CTX_LEN, D_MODEL), lambda b: (b, 0, 0)),
                  pl.BlockSpec((1, 1, D_MODEL), lambda b: (BATCH, 0, 0)),
                  pl.BlockSpec((1, 1, D_MODEL), lambda b: (BATCH, 0, 1)),
                  _resident((1, D_MODEL)),
                  _resident((RET_HEADS * RET_QK_DIM, D_MODEL)),
                  _resident_cols(D_MODEL, V_OFF, RET_WIDTH)],
        out_specs=pl.BlockSpec((CTX_PER_STEP, RET_HEADS, 2 * RET_QK_DIM, RET_V_DIM), lambda b: (b, 0, 0, 0)),
        out_shape=jax.ShapeDtypeStruct((BATCH, RET_HEADS, 2 * RET_QK_DIM, RET_V_DIM), F32),
        compiler_params=pltpu.CompilerParams(vmem_limit_bytes=VMEM_LIMIT),
        name="ctx",
    )(lg, ctx, mod3, mod3, norm1, w_kt, w_v)


def _swap_halves(x, axis):
    n = x.shape[axis]
    half = RET_QK_DIM // 2
    idx = lax.broadcasted_iota(jnp.int32, x.shape, axis)
    first = (idx & (RET_QK_DIM - 1)) < half
    return jnp.where(first, pltpu.roll(x, n - half, axis), pltpu.roll(x, half, axis))


def _inproj_kernel(x_ref, sh_ref, sc_ref, n1_ref, why_ref, wq_ref, wv_ref, wg_ref, wkt_ref,
                   cq_ref, sq_ref, ck_ref, sk_ref, wup_ref, wdn_ref, wout_ref,
                   zhy_ref, q_ref, kt_ref, v_ref, g_ref, wup_b_ref, wdn_b_ref, wout_b_ref):
    wup_b_ref[...] = wup_ref[...].astype(BF16)
    wdn_b_ref[...] = wdn_ref[...].astype(BF16)
    wout_b_ref[...] = wout_ref[...].astype(BF16)
    pieces = [slice(r, r + IN_PIECE) for r in range(0, IN_TILE, IN_PIECE)]
    hbs = [_norm_mod(x_ref[0, rows, :], n1_ref[...], sh_ref[0], sc_ref[0]).astype(BF16) for rows in pieces]
    for rows, hb in zip(pieces, hbs):
        zhy_ref[0, rows, :] = _dot(hb, why_ref[...]).astype(BF16)
        v_ref[0, rows, :] = _dot(hb, wv_ref[...]).astype(BF16)
        g_ref[0, rows, :] = _silu(_dot(hb, wg_ref[...])).astype(BF16)
        q = _dot(hb, wq_ref[...])
        q = q * cq_ref[rows, :] + _swap_halves(q, 1) * sq_ref[rows, :]
        for hh in range(RET_HEADS):
            q_ref[0, hh, rows, :] = q[:, hh * RET_QK_DIM:(hh + 1) * RET_QK_DIM].astype(BF16)
        kt = lax.dot_general(wkt_ref[...], hb, _NT, preferred_element_type=F32)
        kt = kt * ck_ref[:, rows] + _swap_halves(kt, 0) * sk_ref[:, rows]
        kt_ref[0, :, rows] = kt.astype(BF16)


def _inproj_call(x, mod3, norm1, w_in, w_kt, rope, w_up, w_dn, w_out):
    cq, sq, ck, sk = rope
    t = IN_TILE
    nt = SEQ // t
    qk = RET_HEADS * RET_QK_DIM

    def row_slab(w):
        return pl.BlockSpec((w.shape[0] // (BATCH * nt), w.shape[1]), lambda b, i: (b * nt + i, 0))

    slabs = [row_slab(w_up), row_slab(w_dn), row_slab(w_out)]
    return pl.pallas_call(
        _inproj_kernel,
        grid=(BATCH, SEQ // t),
        in_specs=[pl.BlockSpec((1, t, D_MODEL), lambda b, i: (b, i, 0)),
                  pl.BlockSpec((1, 1, D_MODEL), lambda b, i: (b, 0, 0)),
                  pl.BlockSpec((1, 1, D_MODEL), lambda b, i: (b, 0, 1)),
                  _resident((1, D_MODEL)),
                  _resident_cols(D_MODEL, 0, HY_COLS),
                  _resident_cols(D_MODEL, Q_OFF, qk),
                  _resident_cols(D_MODEL, V_OFF, RET_WIDTH),
                  _resident_cols(D_MODEL, G_OFF, RET_WIDTH),
                  _resident((qk, D_MODEL)),
                  pl.BlockSpec((t, qk), lambda b, i: (i, 0)),
                  pl.BlockSpec((t, qk), lambda b, i: (i, 0)),
                  pl.BlockSpec((qk, t), lambda b, i: (0, i)),
                  pl.BlockSpec((qk, t), lambda b, i: (0, i))] + slabs,
        out_specs=[pl.BlockSpec((1, t, HY_COLS), lambda b, i: (b, i, 0)),
                   pl.BlockSpec((1, RET_HEADS, t, RET_QK_DIM), lambda b, i: (b, 0, i, 0)),
                   pl.BlockSpec((1, qk, t), lambda b, i: (b, 0, i)),
                   pl.BlockSpec((1, t, RET_WIDTH), lambda b, i: (b, i, 0)),
                   pl.BlockSpec((1, t, RET_WIDTH), lambda b, i: (b, i, 0))] + slabs,
        out_shape=[jax.ShapeDtypeStruct((BATCH, SEQ, HY_COLS), BF16),
                   jax.ShapeDtypeStruct((BATCH, RET_HEADS, SEQ, RET_QK_DIM), BF16),
                   jax.ShapeDtypeStruct((BATCH, qk, SEQ), BF16),
                   jax.ShapeDtypeStruct((BATCH, SEQ, RET_WIDTH), BF16),
                   jax.ShapeDtypeStruct((BATCH, SEQ, RET_WIDTH), BF16),
                   jax.ShapeDtypeStruct(w_up.shape, BF16),
                   jax.ShapeDtypeStruct(w_dn.shape, BF16),
                   jax.ShapeDtypeStruct(w_out.shape, BF16)],
        compiler_params=pltpu.CompilerParams(vmem_limit_bytes=VMEM_LIMIT),
        name="inproj",
    )(x, mod3, mod3, norm1, w_in, w_in, w_in, w_in, w_kt, cq, sq, ck, sk, w_up, w_dn, w_out)


def _filter_mlp_kernel(zt_ref, t_ref, w1t_ref, b1_ref, f1_ref, w2t_ref, b2_ref, f2_ref, w3t_ref, adel_ref,
                       kt_ref):
    hid = jnp.sin(f1_ref[...] * (_dot_hi(w1t_ref[...], zt_ref[...]) + b1_ref[...]))
    hid = jnp.sin(f2_ref[...] * (_dot_hi(w2t_ref[...], hid) + b2_ref[...])).astype(BF16)
    c = HY_WIDTH
    for half in range(2):
        lags = slice(half * SEQ, (half + 1) * SEQ)
        window = jnp.exp(-adel_ref[...] * t_ref[:, lags])
        direction = 1 - half
        for o in range(HY_ORDER):
            r0 = (direction * HY_ORDER + o) * c
            w3 = w3t_ref[r0:r0 + c, :].astype(BF16)
            kt_ref[o, :, lags] = (_dot(w3, hid[:, lags]) * window).astype(BF16)


def _filter_mlp_call(zt, t_row, w1t, b1, f1, w2t, b2, f2, w3t, absdelta):
    fw = HY_FILTER_WIDTH
    n_out = 2 * HY_ORDER * HY_WIDTH
    return pl.pallas_call(
        _filter_mlp_kernel,
        grid=(1,),
        in_specs=[_resident((fw, 2 * SEQ)), _resident((1, 2 * SEQ)),
                  _resident((fw, fw)), _resident((fw, 1)), _resident((fw, 1)),
                  _resident((fw, fw)), _resident((fw, 1)), _resident((fw, 1)),
                  _resident((n_out, fw)), _resident((HY_WIDTH, 1))],
        out_specs=pl.BlockSpec((HY_ORDER, HY_WIDTH, 2 * SEQ), lambda i: (0, 0, 0)),
        out_shape=jax.ShapeDtypeStruct((HY_ORDER, HY_WIDTH, 2 * SEQ), BF16),
        compiler_params=pltpu.CompilerParams(vmem_limit_bytes=VMEM_LIMIT),
        name="filter_mlp",
    )(zt, t_row, w1t, b1, f1, w2t, b2, f2, w3t, absdelta)


_FILTER_PLANES = (
    {0: 1},
    {-1: 1, 0: -1},
    {1: 1, 0: -1},
    {-2: 1, 0: -1},
    {-3: 1, -1: -1, -2: -1, 0: 1},
    {-1: 1, 1: -1, -2: -1, 0: 1},
    {2: 1, 0: -1},
    {1: 1, -1: -1, 2: -1, 0: 1},
    {3: 1, 1: -1, 2: -1, 0: 1},
)
DFT_BLOCKS_PER_STEP = 2
PLANES_PER_STEP = 3


def _filter_dft_kernel(kt_ref, fwd_ref, h_ref, t_scr):
    p = CONV_BLOCK
    n_dft_steps = 2 * N_CONV_BLOCKS // DFT_BLOCKS_PER_STEP
    step = pl.program_id(1)

    for s in range(n_dft_steps):
        @pl.when(step == s)
        def _(s=s):
            for k in range(DFT_BLOCKS_PER_STEP):
                t_scr[s * DFT_BLOCKS_PER_STEP + k] = lax.dot_general(
                    fwd_ref[...], kt_ref[0, :, k * p:(k + 1) * p], _NT, preferred_element_type=F32)
            for k in range(PLANES_PER_STEP):
                h_ref[0, k] = t_scr[s * DFT_BLOCKS_PER_STEP]

    row = lax.broadcasted_iota(jnp.int32, (p, 1), 0)
    sign = (1 - 2 * (row & 1)).astype(F32)
    re, im = slice(0, p), slice(p, 2 * p)

    def plane(k, coefs):
        def comb(shift, rows):
            acc = None
            for d, c in coefs.items():
                term = t_scr[d + N_CONV_BLOCKS - 1 + shift, rows, :]
                if acc is None:
                    acc = term
                else:
                    acc = acc + term if c > 0 else acc - term
            return acc

        h_ref[0, k, re, :] = comb(0, re) + sign * comb(1, im)
        h_ref[0, k, im, :] = comb(0, im) - sign * comb(1, re)

    for s in range(len(_FILTER_PLANES) // PLANES_PER_STEP):
        @pl.when(step == n_dft_steps + s)
        def _(s=s):
            for k in range(PLANES_PER_STEP):
                plane(k, _FILTER_PLANES[s * PLANES_PER_STEP + k])


def _filter_dft_call(kt, fwd_u):
    p = CONV_BLOCK
    n_blk = 2 * N_CONV_BLOCKS
    n_planes = len(_FILTER_PLANES)
    n_dft_steps = n_blk // DFT_BLOCKS_PER_STEP
    return pl.pallas_call(
        _filter_dft_kernel,
        grid=(HY_ORDER, n_dft_steps + n_planes // PLANES_PER_STEP),
        in_specs=[pl.BlockSpec((1, HY_WIDTH, DFT_BLOCKS_PER_STEP * p),
                               lambda o, s: (o, 0, jnp.minimum(s, n_dft_steps - 1))),
                  _resident((2 * p, p))],
        out_specs=pl.BlockSpec((1, PLANES_PER_STEP, 2 * p, HY_WIDTH),
                               lambda o, s: (o, jnp.maximum(s - n_dft_steps, 0), 0, 0)),
        out_shape=jax.ShapeDtypeStruct((HY_ORDER, n_planes, 2 * p, HY_WIDTH), F32),
        scratch_shapes=[pltpu.VMEM((n_blk, 2 * p, HY_WIDTH), F32)],
        compiler_params=pltpu.CompilerParams(vmem_limit_bytes=VMEM_LIMIT),
        name="filter_dft",
    )(kt, fwd_u)


def _conv3_rows(ref, j, n_blocks, rows, w, b):
    main = ref[0, j * rows:(j + 1) * rows, :].astype(F32)
    cols = main.shape[1]
    zeros = jnp.zeros((HALO, cols), F32)
    prev = ref[0, j * rows - HALO:j * rows, :].astype(F32) if j > 0 else zeros
    nxt = ref[0, (j + 1) * rows:(j + 1) * rows + HALO, :].astype(F32) if j < n_blocks - 1 else zeros
    ext = jnp.concatenate([prev, main, nxt], axis=0)
    n = rows + 2 * HALO
    before = pltpu.roll(ext, 1, 0)[HALO:HALO + rows]
    after = pltpu.roll(ext, n - 1, 0)[HALO:HALO + rows]
    return before * w[0:1] + main * w[1:2] + after * w[2:3] + b


def _hyena_kernel(u_ref, zg_ref, cwu_ref, cbu_ref, cwg_ref, cbg_ref, skip_ref, h_ref, fwd_ref, inv_ref,
                  o_ref, uf_scr, ub_scr, y_scr, *, conv_u):
    p = CONV_BLOCK
    nb = N_CONV_BLOCKS

    def prepare(j):
        if conv_u:
            ub_scr[j * p:(j + 1) * p, :] = _conv3_rows(u_ref, j, nb, p, cwu_ref[...], cbu_ref[...]).astype(BF16)

    def u_block(j):
        return ub_scr[j * p:(j + 1) * p, :] if conv_u else u_ref[0, j * p:(j + 1) * p, :]

    prepare(0)
    for j in range(nb):
        if j + 1 < nb:
            prepare(j + 1)
        for r in range(2 * p // DFT_ROWS):
            rows = slice(r * DFT_ROWS, (r + 1) * DFT_ROWS)
            uf_scr[j, rows, :] = _dot(fwd_ref[rows, :], u_block(j))

    def cadd(a, b):
        return a[0] + b[0], a[1] + b[1]

    def cmul(m, x):
        return m[0] * x[0] - m[1] * x[1], m[0] * x[1] + m[1] * x[0]

    def toeplitz2(k0, x0, x1, re, im, ln):
        m0, mu, ml = ((h_ref[0, k0 + t, re, ln], h_ref[0, k0 + t, im, ln]) for t in range(3))
        p1 = cmul(m0, cadd(x0, x1))
        return cadd(p1, cmul(mu, x1)), cadd(p1, cmul(ml, x0))

    assert nb == 4
    for r in range(p // FREQ_ROWS):
        re = slice(r * FREQ_ROWS, (r + 1) * FREQ_ROWS)
        im = slice(p + r * FREQ_ROWS, p + (r + 1) * FREQ_ROWS)
        for cb in range(HY_WIDTH // FREQ_LANES):
            ln = slice(cb * FREQ_LANES, (cb + 1) * FREQ_LANES)
            u = [(uf_scr[j, re, ln], uf_scr[j, im, ln]) for j in range(nb)]
            d0, d1 = toeplitz2(0, cadd(u[0], u[2]), cadd(u[1], u[3]), re, im, ln)
            b0, b1 = toeplitz2(3, u[2], u[3], re, im, ln)
            c0, c1 = toeplitz2(6, u[0], u[1], re, im, ln)
            for i, yi in enumerate((cadd(d0, b0), cadd(d1, b1), cadd(d0, c0), cadd(d1, c1))):
                y_scr[i, re, ln] = yi[0].astype(BF16)
                y_scr[i, im, ln] = yi[1].astype(BF16)

    for i in range(nb):
        gate = _conv3_rows(zg_ref, i, nb, p, cwg_ref[...], cbg_ref[...])
        y = _dot(inv_ref[...], y_scr[i])
        o_ref[0, i * p:(i + 1) * p, :] = (
            gate * (y + u_block(i).astype(F32) * skip_ref[...])).astype(BF16)


def _hyena_call(u, u_col, zhy, gate_col, conv_w, conv_b, skip, g_all, order, fwd_u, inv, conv_u):
    p = CONV_BLOCK
    c = HY_WIDTH
    ucol = u_col if conv_u else 0
    cwu = conv_w[:, ucol * c:(ucol + 1) * c]
    cbu = conv_b[:, ucol * c:(ucol + 1) * c]
    cwg = conv_w[:, gate_col * c:(gate_col + 1) * c]
    cbg = conv_b[:, gate_col * c:(gate_col + 1) * c]
    return pl.pallas_call(
        functools.partial(_hyena_kernel, conv_u=conv_u),
        grid=(BATCH,),
        in_specs=[pl.BlockSpec((1, SEQ, c), lambda b: (b, 0, u_col)),
                  pl.BlockSpec((1, SEQ, c), lambda b: (b, 0, gate_col)),
                  _resident((3, c)), _resident((1, c)), _resident((3, c)), _resident((1, c)),
                  _resident((1, c)),
                  pl.BlockSpec((1, len(_FILTER_PLANES), 2 * p, c), lambda b: (order, 0, 0, 0),
                               pipeline_mode=pl.Buffered(1)),
                  _resident((2 * p, p)),
                  _resident((p, 2 * p))],
        out_specs=pl.BlockSpec((1, SEQ, c), lambda b: (b, 0, 0)),
        out_shape=jax.ShapeDtypeStruct((BATCH, SEQ, c), BF16),
        scratch_shapes=[pltpu.VMEM((N_CONV_BLOCKS, 2 * p, c), F32),
                        pltpu.VMEM((SEQ, c) if conv_u else (SUBLANES * 2, LANES), BF16),
                        pltpu.VMEM((N_CONV_BLOCKS, 2 * p, c), BF16)],
        compiler_params=pltpu.CompilerParams(vmem_limit_bytes=HYENA_VMEM_LIMIT),
        name="hyena%d" % order,
    )(u, zhy, cwu, cbu, cwg, cbg, skip, g_all, fwd_u, inv)


def _ret_kernel(lg_ref, q_ref, kt_ref, v_ref, g_ref, s_ref, cc_ref, wmod_ref, bmod_ref,
                o_ref, modl_ref, b_scr, decay_scr, qw_scr):
    modl_ref[...] = _dot(_silu(cc_ref[...]).astype(BF16), wmod_ref[...].astype(BF16)) + bmod_ref[...]
    c = RET_CHUNK
    nc = SEQ // c
    dk = RET_QK_DIM
    dv = RET_V_DIM
    heads = range(RET_HEADS_PER_STEP)
    lg_f = [lg_ref[0, pl.program_id(0) * RET_HEADS_PER_STEP + hh] for hh in heads]
    lg_b = [lg_ref[1, pl.program_id(0) * RET_HEADS_PER_STEP + hh] for hh in heads]

    @pl.when(pl.program_id(1) == 0)
    def _():
        ii = lax.broadcasted_iota(jnp.int32, (c, c), 0).astype(F32)
        jj = lax.broadcasted_iota(jnp.int32, (c, c), 1).astype(F32)
        dif = ii - jj
        pos_q = lax.broadcasted_iota(jnp.int32, (c, dk), 0).astype(F32)
        for hh in heads:
            decay_scr[hh] = jnp.where(dif >= 0.0, jnp.exp(lg_f[hh] * jnp.maximum(dif, 0.0)),
                                      jnp.exp(lg_b[hh] * jnp.maximum(-dif, 0.0)))
            qw_scr[hh, 0] = jnp.exp(lg_f[hh] * (pos_q + 1.0))
            qw_scr[hh, 1] = jnp.exp(lg_b[hh] * (c - pos_q))

    pos_r = lax.broadcasted_iota(jnp.int32, (1, c), 1).astype(F32)
    ones = jnp.ones((1, dv), F32)
    kw_f = [jnp.exp(lg_f[hh] * (c - 1.0 - pos_r)) for hh in heads]
    kw_b = [jnp.exp(lg_b[hh] * pos_r) for hh in heads]
    dec_f = [jnp.exp(lg_f[hh] * float(c) * ones) for hh in heads]
    dec_b = [jnp.exp(lg_b[hh] * float(c) * ones) for hh in heads]

    def kt_chunk(hh, n):
        return kt_ref[0, hh * dk:(hh + 1) * dk, n * c:(n + 1) * c]

    def v_chunk(hh, n):
        return v_ref[0, n * c:(n + 1) * c, hh * dv:(hh + 1) * dv]

    state = [s_ref[0, hh, dk:2 * dk, :] for hh in heads]
    for hh in heads:
        b_scr[hh, nc - 1] = state[hh]
    for n in range(nc - 1, 0, -1):
        for hh in heads:
            ktn = (kt_chunk(hh, n).astype(F32) * kw_b[hh]).astype(BF16)
            state[hh] = state[hh] * dec_b[hh] + _dot(ktn, v_chunk(hh, n))
            b_scr[hh, n - 1] = state[hh]

    state = [s_ref[0, hh, 0:dk, :] for hh in heads]
    for n in range(nc):
        for hh in heads:
            qn = q_ref[0, hh, n * c:(n + 1) * c, :]
            ktn = kt_chunk(hh, n)
            vn = v_chunk(hh, n)
            scores = (_dot(qn, ktn) * decay_scr[hh]).astype(BF16)
            qf = qn.astype(F32)
            o = _dot(scores, vn)
            o = o + _dot((qf * qw_scr[hh, 0]).astype(BF16), state[hh].astype(BF16))
            o = o + _dot((qf * qw_scr[hh, 1]).astype(BF16), b_scr[hh, n].astype(BF16))
            state[hh] = state[hh] * dec_f[hh] + _dot((ktn.astype(F32) * kw_f[hh]).astype(BF16), vn)
            o = o * lax.rsqrt(jnp.mean(o * o, axis=-1, keepdims=True) + EPS)
            gate = g_ref[0, n * c:(n + 1) * c, hh * dv:(hh + 1) * dv].astype(F32)
            o_ref[0, n * c:(n + 1) * c, hh * dv:(hh + 1) * dv] = (gate * o).astype(BF16)


def _ret_call(lg, q, kt, v, g, s, cc, w_mod, b_mod):
    dk, dv = RET_QK_DIM, RET_V_DIM
    hp = RET_HEADS_PER_STEP
    assert RET_HEADS == hp
    mcols = MOD_LATE_COLS // BATCH
    mod_first = MOD_EARLY_COLS // mcols
    mod_slab = pl.BlockSpec((MOD_ROWS, mcols), lambda h, b: (0, b))
    return pl.pallas_call(
        _ret_kernel,
        grid=(RET_HEADS // hp, BATCH),
        in_specs=[pl.BlockSpec(memory_space=pltpu.SMEM),
                  pl.BlockSpec((1, hp, SEQ, dk), lambda h, b: (b, h, 0, 0)),
                  pl.BlockSpec((1, hp * dk, SEQ), lambda h, b: (b, h, 0)),
                  pl.BlockSpec((1, SEQ, hp * dv), lambda h, b: (b, 0, h)),
                  pl.BlockSpec((1, SEQ, hp * dv), lambda h, b: (b, 0, h)),
                  pl.BlockSpec((1, hp, 2 * dk, dv), lambda h, b: (b, h, 0, 0)),
                  _resident((MOD_ROWS, D_MODEL)),
                  pl.BlockSpec((D_MODEL, mcols), lambda h, b: (0, mod_first + b)),
                  pl.BlockSpec((1, mcols), lambda h, b: (0, mod_first + b))],
        out_specs=[pl.BlockSpec((1, SEQ, hp * dv), lambda h, b: (b, 0, h)), mod_slab],
        out_shape=[jax.ShapeDtypeStruct((BATCH, SEQ, RET_WIDTH), BF16),
                   jax.ShapeDtypeStruct((MOD_ROWS, MOD_LATE_COLS), F32)],
        scratch_shapes=[pltpu.VMEM((hp, SEQ // RET_CHUNK, dk, dv), F32),
                        pltpu.VMEM((hp, RET_CHUNK, RET_CHUNK), F32),
                        pltpu.VMEM((hp, 2, RET_CHUNK, dk), F32)],
        compiler_params=pltpu.CompilerParams(vmem_limit_bytes=VMEM_LIMIT),
        name="ret",
    )(lg, q, kt, v, g, s, cc, w_mod, b_mod)


def _ffn_kernel(x_ref, xp_ref, xn_ref, yh_ref, yhp_ref, yhn_ref, yr_ref, yrp_ref, yrn_ref,
                g1_ref, sh_ref, sc_ref, g2_ref, n2_ref, nf_ref,
                woh_ref, wor_ref, wup_ref, cw_ref, cb_ref, wdn_ref,
                o_ref, hb_scr, x1_scr, av_scr, ag_scr, act_scr):
    t = TOK_TILE
    i = pl.program_id(1)
    nt = pl.num_programs(1)
    th = t // 2
    tile_halves = (slice(0, th), slice(th, t))

    def mixed(xr, yh, yr, rows=slice(None)):
        return xr[0, rows, :] + g1_ref[0] * (_dot(yh[0, rows, :], woh_ref[...]) + _dot(yr[0, rows, :], wor_ref[...]))

    def hidden(x1):
        return _norm_mod(x1, n2_ref[...], sh_ref[0], sc_ref[0])

    half = (t + 2 * HALO) // 2
    up_halves = (slice(0, half), slice(half, 2 * half))

    def up(slot, cblk, rows):
        c0 = cblk * FFN_COLS
        hb = hb_scr[rows, :]
        av_scr[slot, rows, :] = _dot(hb, wup_ref[:, c0:c0 + FFN_COLS])
        ag_scr[slot, rows, :] = _dot(hb, wup_ref[:, D_FF + c0:D_FF + c0 + FFN_COLS])

    for rows in tile_halves:
        x1_scr[rows, :] = mixed(x_ref, yh_ref, yr_ref, rows)
    hp = hidden(mixed(xp_ref, yhp_ref, yrp_ref))
    hn = hidden(mixed(xn_ref, yhn_ref, yrn_ref))
    hb_scr[0:HALO, :] = jnp.where(i > 0, hp, 0.0).astype(BF16)
    hb_scr[HALO + t:2 * HALO + t, :] = jnp.where(i < nt - 1, hn, 0.0).astype(BF16)
    hb_scr[HALO:HALO + th, :] = hidden(x1_scr[tile_halves[0], :]).astype(BF16)
    up(0, 0, up_halves[0])
    hb_scr[HALO + th:HALO + t, :] = hidden(x1_scr[tile_halves[1], :]).astype(BF16)
    up(0, 0, up_halves[1])

    def conv(scr, slot, col):
        w = cw_ref[:, col:col + FFN_COLS]
        return (scr[slot, HALO - 1:HALO - 1 + t, :] * w[0:1] + scr[slot, HALO:HALO + t, :] * w[1:2]
                + scr[slot, HALO + 1:HALO + 1 + t, :] * w[2:3] + cb_ref[:, col:col + FFN_COLS])

    n_blk = D_FF // FFN_COLS
    for cblk in range(n_blk):
        c0 = cblk * FFN_COLS
        slot = cblk % 2
        if cblk + 1 < n_blk:
            for rows in up_halves:
                up(1 - slot, cblk + 1, rows)
        act_scr[:, c0:c0 + FFN_COLS] = (
            _silu(conv(ag_scr, slot, D_FF + c0)) * conv(av_scr, slot, c0)).astype(BF16)

    down = [_dot(act_scr[rows, :], wdn_ref[...]) for rows in tile_halves]
    for rows, ffn in zip(tile_halves, down):
        x2 = x1_scr[rows, :] + g2_ref[0] * ffn
        o_ref[0, rows, :] = x2 * lax.rsqrt(jnp.mean(x2 * x2, axis=-1, keepdims=True) + EPS) * nf_ref[...]


def _ffn_call(x, y_hy, y_ret, mod3, norm2, norm_f, w_oh, w_or, w_up, conv_w, conv_b, w_dn):
    t = TOK_TILE
    r = t // HALO
    last = SEQ // HALO - 1

    def main(width):
        return pl.BlockSpec((1, t, width), lambda b, i: (b, i, 0))

    def prev(width):
        return pl.BlockSpec((1, HALO, width), lambda b, i: (b, jnp.maximum(i * r - 1, 0), 0))

    def nxt(width):
        return pl.BlockSpec((1, HALO, width), lambda b, i: (b, jnp.minimum((i + 1) * r, last), 0))

    def modrow(k):
        return pl.BlockSpec((1, 1, D_MODEL), lambda b, i: (b, 0, k))

    return pl.pallas_call(
        _ffn_kernel,
        grid=(BATCH, SEQ // t),
        in_specs=[main(D_MODEL), prev(D_MODEL), nxt(D_MODEL),
                  main(HY_WIDTH), prev(HY_WIDTH), nxt(HY_WIDTH),
                  main(RET_WIDTH), prev(RET_WIDTH), nxt(RET_WIDTH),
                  modrow(0), modrow(1), modrow(2), modrow(3),
                  _resident((1, D_MODEL)), _resident((1, D_MODEL)),
                  _resident_rows(0, HY_WIDTH, D_MODEL), _resident_rows(HY_WIDTH, RET_WIDTH, D_MODEL),
                  _resident((D_MODEL, 2 * D_FF)),
                  _resident((3, 2 * D_FF)), _resident((1, 2 * D_FF)),
                  _resident((D_FF, D_MODEL))],
        out_specs=pl.BlockSpec((1, t, D_MODEL), lambda b, i: (b, i, 0)),
        out_shape=jax.ShapeDtypeStruct((BATCH, SEQ, D_MODEL), F32),
        scratch_shapes=[pltpu.VMEM((t + 2 * HALO, D_MODEL), BF16),
                        pltpu.VMEM((t, D_MODEL), F32),
                        pltpu.VMEM((2, t + 2 * HALO, FFN_COLS), F32),
                        pltpu.VMEM((2, t + 2 * HALO, FFN_COLS), F32),
                        pltpu.VMEM((t, D_FF), BF16)],
        compiler_params=pltpu.CompilerParams(vmem_limit_bytes=VMEM_LIMIT),
        name="ffn",
    )(x, x, x, y_hy, y_hy, y_hy, y_ret, y_ret, y_ret, mod3, mod3, mod3, mod3,
      norm2, norm_f, w_oh, w_or, w_up, conv_w, conv_b, w_dn)


def kernel(x, c, ctx, c_ctx, w_mod, b_mod, norm1, w_in, hy_conv_w, hy_conv_b, hy_w1, hy_b1, hy_f1,
           hy_w2, hy_b2, hy_f2, hy_w3, hy_bias, ret_logit_f, ret_logit_b, w_out, norm2,
           ffn_w_up, ffn_conv_w, ffn_conv_b, ffn_w_down, norm_f):
    layer = 0
    rope = tuple(jnp.asarray(a) for a in _rope_tables())
    zt, t_row, absdelta = (jnp.asarray(a) for a in _filter_features())
    fwd_np, inv_np = _dft_matrices()
    fwd_u = jnp.asarray(fwd_np[:, :CONV_BLOCK]).astype(BF16)
    inv = jnp.asarray(inv_np).astype(BF16)

    w_in_b = w_in[layer].astype(BF16)
    w_kt = w_in_b[:, K_OFF:V_OFF].T
    row = lambda a: a.reshape(1, -1)
    col = lambda a: a.reshape(-1, 1)
    w1p = jnp.pad(hy_w1[layer], ((0, HY_FILTER_WIDTH - hy_w1.shape[1]), (0, 0)))
    lg = jnp.stack([jax.nn.log_sigmoid(ret_logit_f[layer].astype(F32)),
                    jax.nn.log_sigmoid(ret_logit_b[layer].astype(F32))])

    cc = jnp.concatenate([c, c_ctx[None, :], jnp.zeros((MOD_ROWS - BATCH - 1, D_MODEL), F32)], axis=0)
    mod3 = _mod_call(cc, w_mod[layer], row(b_mod[layer])).reshape(MOD_ROWS, 1, MOD_EARLY_COLS)
    norm1_r = row(norm1[layer])

    s_ctx = _ctx_call(lg, ctx, mod3, norm1_r, w_kt, w_in_b)
    zhy, q, kt, v, g, w_up, w_dn, w_out_b = _inproj_call(
        x, mod3, norm1_r, w_in_b, w_kt, rope, ffn_w_up[layer], ffn_w_down[layer], w_out[layer])

    k_two_sided = _filter_mlp_call(zt, t_row, w1p.T, col(hy_b1[layer]), col(hy_f1[layer]), hy_w2[layer].T,
                                   col(hy_b2[layer]), col(hy_f2[layer]), hy_w3[layer].T, absdelta)
    g_all = _filter_dft_call(k_two_sided, fwd_u)
    conv_w, conv_b = hy_conv_w[layer], row(hy_conv_b[layer])
    y1 = _hyena_call(zhy, 0, zhy, 1, conv_w, conv_b, hy_bias[layer][0:1], g_all, 0, fwd_u, inv, True)
    y_hy = _hyena_call(y1, 0, zhy, 2, conv_w, conv_b, hy_bias[layer][1:2], g_all, 1, fwd_u, inv, False)

    y_ret, mod_late = _ret_call(lg, q, kt, v, g, s_ctx, cc, w_mod[layer], row(b_mod[layer]))
    mod_late3 = mod_late.reshape(MOD_ROWS, 1, MOD_LATE_COLS)

    return _ffn_call(x, y_hy, y_ret, mod_late3, row(norm2[layer]), row(norm_f), w_out_b, w_out_b, w_up,
                     ffn_conv_w[layer], row(ffn_conv_b[layer]), w_dn)
```

```python
import functools
import math

import numpy as np
import jax
import jax.numpy as jnp
from jax import lax
from jax.experimental import pallas as pl
from jax.experimental.pallas import tpu as pltpu

F32 = jnp.float32
BF16 = jnp.bfloat16

D_MODEL = 1024
BATCH = 8
SEQ = 2048
CTX_LEN = 256
GRID_W = 64
HY_WIDTH = 512
HY_ORDER = 2
HY_EMB_BANDS = 16
HY_FILTER_WIDTH = 64
HY_FAST_DECAY = 0.3
HY_SLOW_DECAY = 1.5
HY_TARGET = 1e-2
RET_WIDTH = 512
RET_HEADS = 4
RET_QK_DIM = 64
RET_V_DIM = 128
ROPE_BASE = 10000.0
D_FF = 2816
EPS = 1e-6
HY_COLS = (HY_ORDER + 1) * HY_WIDTH
Q_OFF = HY_COLS
K_OFF = Q_OFF + RET_HEADS * RET_QK_DIM
V_OFF = K_OFF + RET_HEADS * RET_QK_DIM
G_OFF = V_OFF + RET_WIDTH
K_SCALE = RET_QK_DIM ** -0.5

MOD_ROWS = 16
MOD_EARLY_COLS = 2 * D_MODEL
MOD_LATE_COLS = 4 * D_MODEL
TOK_TILE = 512
IN_TILE = 1024
IN_PIECE = 256
SUBLANES = 8
LANES = 128
HALO = 16
CONV_BLOCK = 512
N_CONV_BLOCKS = SEQ // CONV_BLOCK
FREQ_ROWS = 16
FREQ_LANES = 128
DFT_ROWS = 256
CTX_PER_STEP = 4
RET_CHUNK = 256
RET_HEADS_PER_STEP = 4
FFN_COLS = 256
VMEM_LIMIT = 56 * 1024 * 1024
HYENA_VMEM_LIMIT = 62 * 1024 * 1024

_NT = (((1,), (1,)), ((), ()))


def _dot(a, b):
    return jnp.dot(a, b, preferred_element_type=F32)


def _dot_hi(a, b):
    return jnp.dot(a, b, preferred_element_type=F32, precision=lax.Precision.HIGHEST)


def _silu(x):
    return x * (1.0 / (1.0 + jnp.exp(-x)))


def _norm_mod(x, gain, shift, scale):
    y = x * lax.rsqrt(jnp.mean(x * x, axis=-1, keepdims=True) + EPS)
    return (y * gain) * (1.0 + scale) + shift


def _resident(shape):
    nd = len(shape)
    return pl.BlockSpec(shape, lambda *_: (0,) * nd, pipeline_mode=pl.Buffered(1))


def _resident_cols(rows, col0, width):
    assert col0 % width == 0
    return pl.BlockSpec((rows, width), lambda *_: (0, col0 // width), pipeline_mode=pl.Buffered(1))


def _resident_rows(row0, height, cols):
    assert row0 % height == 0
    return pl.BlockSpec((height, cols), lambda *_: (row0 // height, 0), pipeline_mode=pl.Buffered(1))


@functools.lru_cache(maxsize=None)
def _rope_tables():
    pos = np.arange(SEQ)
    row = (pos // GRID_W).astype(np.float64)
    col = (pos % GRID_W).astype(np.float64)
    quarter = RET_QK_DIM // 4
    inv_freq = ROPE_BASE ** (-np.arange(quarter, dtype=np.float64) / quarter)
    ang = np.concatenate([row[:, None] * inv_freq, col[:, None] * inv_freq], axis=-1)
    cos, sin = np.cos(ang), np.sin(ang)
    cos_h = np.concatenate([cos, cos], axis=-1)
    sin_h = np.concatenate([-sin, sin], axis=-1)
    cos_t = np.tile(cos_h, (1, RET_HEADS))
    sin_t = np.tile(sin_h, (1, RET_HEADS))
    return (cos_t.astype(np.float32), sin_t.astype(np.float32),
            np.ascontiguousarray((cos_t * K_SCALE).T).astype(np.float32),
            np.ascontiguousarray((sin_t * K_SCALE).T).astype(np.float32))


@functools.lru_cache(maxsize=None)
def _filter_features():
    lag = np.abs(np.arange(2 * SEQ) - SEQ).astype(np.float64)
    t = lag / (SEQ - 1)
    bands = np.linspace(1e-4, HY_EMB_BANDS - 1, HY_EMB_BANDS)
    ang = 2.0 * math.pi * lag[:, None] * bands[None, :] / SEQ
    z = np.concatenate([t[:, None], np.cos(ang), -np.sin(ang)], axis=-1)
    zp = np.zeros((2 * SEQ, HY_FILTER_WIDTH), np.float64)
    zp[:, :z.shape[1]] = z
    max_decay = math.log(HY_TARGET) / HY_FAST_DECAY
    min_decay = math.log(HY_TARGET) / HY_SLOW_DECAY
    absdelta = np.abs(np.linspace(min_decay, max_decay, HY_WIDTH))[:, None]
    return (np.ascontiguousarray(zp.T).astype(np.float32), t[None, :].astype(np.float32),
            absdelta.astype(np.float32))


@functools.lru_cache(maxsize=None)
def _dft_matrices():
    p = CONV_BLOCK
    n = 2 * p
    f = np.arange(p, dtype=np.float64)[:, None] + 0.5
    t = np.arange(n, dtype=np.float64)[None, :]
    theta = 2.0 * math.pi * f * t / n
    fwd = np.concatenate([np.cos(theta), -np.sin(theta)], axis=0)
    th_out = theta[:, p:].T
    inv = np.concatenate([np.cos(th_out), -np.sin(th_out)], axis=1) / p
    return fwd.astype(np.float32), inv.astype(np.float32)


def _mod_kernel(c_ref, w_ref, b_ref, o_ref):
    s = _silu(c_ref[...]).astype(BF16)
    o_ref[...] = _dot(s, w_ref[...].astype(BF16)) + b_ref[...]


def _mod_call(cc, w_mod, b_mod):
    ncol = MOD_EARLY_COLS
    blk = ncol // 2
    return pl.pallas_call(
        _mod_kernel,
        grid=(ncol // blk,),
        in_specs=[pl.BlockSpec((MOD_ROWS, D_MODEL), lambda j: (0, 0)),
                  pl.BlockSpec((D_MODEL, blk), lambda j: (0, j)),
                  pl.BlockSpec((1, blk), lambda j: (0, j))],
        out_specs=pl.BlockSpec((MOD_ROWS, blk), lambda j: (0, j)),
        out_shape=jax.ShapeDtypeStruct((MOD_ROWS, ncol), F32),
        compiler_params=pltpu.CompilerParams(vmem_limit_bytes=VMEM_LIMIT),
        name="mod",
    )(cc, w_mod, b_mod)


def _ctx_kernel(lg_ref, x_ref, sh_ref, sc_ref, n1_ref, wkt_ref, wv_ref, s_ref):
    pos = lax.broadcasted_iota(jnp.int32, (1, CTX_LEN), 1).astype(F32)
    w_f = [jnp.exp(lg_ref[0, hh] * (CTX_LEN - 1.0 - pos)) for hh in range(RET_HEADS)]
    w_b = [jnp.exp(lg_ref[1, hh] * pos) for hh in range(RET_HEADS)]
    for bb in range(CTX_PER_STEP):
        h = _norm_mod(x_ref[bb], n1_ref[...], sh_ref[0], sc_ref[0]).astype(BF16)
        kt = lax.dot_general(wkt_ref[...], h, _NT, preferred_element_type=F32) * K_SCALE
        v = _dot(h, wv_ref[...])
        for hh in range(RET_HEADS):
            kth = kt[hh * RET_QK_DIM:(hh + 1) * RET_QK_DIM, :]
            vh = v[:, hh * RET_V_DIM:(hh + 1) * RET_V_DIM].astype(BF16)
            s_ref[bb, hh, 0:RET_QK_DIM, :] = _dot((kth * w_f[hh]).astype(BF16), vh)
            s_ref[bb, hh, RET_QK_DIM:2 * RET_QK_DIM, :] = _dot((kth * w_b[hh]).astype(BF16), vh)


def _ctx_call(lg, ctx, mod3, norm1, w_kt, w_v):
    return pl.pallas_call(
        _ctx_kernel,
        grid=(BATCH // CTX_PER_STEP,),
        in_specs=[pl.BlockSpec(memory_space=pltpu.SMEM),
                  pl.BlockSpec((CTX_PER_STEP, CTX_LEN, D_MODEL), lambda b: (b, 0, 0)),
                  pl.BlockSpec((1, 1, D_MODEL), lambda b: (BATCH, 0, 0)),
                  pl.BlockSpec((1, 1, D_MODEL), lambda b: (BATCH, 0, 1)),
                  _resident((1, D_MODEL)),
                  _resident((RET_HEADS * RET_QK_DIM, D_MODEL)),
                  _resident_cols(D_MODEL, V_OFF, RET_WIDTH)],
        out_specs=pl.BlockSpec((CTX_PER_STEP, RET_HEADS, 2 * RET_QK_DIM, RET_V_DIM), lambda b: (b, 0, 0, 0)),
        out_shape=jax.ShapeDtypeStruct((BATCH, RET_HEADS, 2 * RET_QK_DIM, RET_V_DIM), F32),
        compiler_params=pltpu.CompilerParams(vmem_limit_bytes=VMEM_LIMIT),
        name="ctx",
    )(lg, ctx, mod3, mod3, norm1, w_kt, w_v)


def _swap_halves(x, axis):
    n = x.shape[axis]
    half = RET_QK_DIM // 2
    idx = lax.broadcasted_iota(jnp.int32, x.shape, axis)
    first = (idx & (RET_QK_DIM - 1)) < half
    return jnp.where(first, pltpu.roll(x, n - half, axis), pltpu.roll(x, half, axis))


def _inproj_kernel(x_ref, sh_ref, sc_ref, n1_ref, why_ref, wq_ref, wv_ref, wg_ref, wkt_ref,
                   cq_ref, sq_ref, ck_ref, sk_ref,
                   zhy_ref, q_ref, kt_ref, v_ref, g_ref):
    pieces = [slice(r, r + IN_PIECE) for r in range(0, IN_TILE, IN_PIECE)]
    hbs = [_norm_mod(x_ref[0, rows, :], n1_ref[...], sh_ref[0], sc_ref[0]).astype(BF16) for rows in pieces]
    for rows, hb in zip(pieces, hbs):
        zhy_ref[0, rows, :] = _dot(hb, why_ref[...]).astype(BF16)
        v_ref[0, rows, :] = _dot(hb, wv_ref[...]).astype(BF16)
        g_ref[0, rows, :] = _dot(hb, wg_ref[...]).astype(BF16)
        q = _dot(hb, wq_ref[...])
        q = q * cq_ref[rows, :] + _swap_halves(q, 1) * sq_ref[rows, :]
        for hh in range(RET_HEADS):
            q_ref[0, hh, rows, :] = q[:, hh * RET_QK_DIM:(hh + 1) * RET_QK_DIM].astype(BF16)
        kt = lax.dot_general(wkt_ref[...], hb, _NT, preferred_element_type=F32)
        kt = kt * ck_ref[:, rows] + _swap_halves(kt, 0) * sk_ref[:, rows]
        kt_ref[0, :, rows] = kt.astype(BF16)


def _inproj_call(x, mod3, norm1, w_in, w_kt, rope):
    cq, sq, ck, sk = rope
    t = IN_TILE
    qk = RET_HEADS * RET_QK_DIM
    return pl.pallas_call(
        _inproj_kernel,
        grid=(BATCH, SEQ // t),
        in_specs=[pl.BlockSpec((1, t, D_MODEL), lambda b, i: (b, i, 0)),
                  pl.BlockSpec((1, 1, D_MODEL), lambda b, i: (b, 0, 0)),
                  pl.BlockSpec((1, 1, D_MODEL), lambda b, i: (b, 0, 1)),
                  _resident((1, D_MODEL)),
                  _resident_cols(D_MODEL, 0, HY_COLS),
                  _resident_cols(D_MODEL, Q_OFF, qk),
                  _resident_cols(D_MODEL, V_OFF, RET_WIDTH),
                  _resident_cols(D_MODEL, G_OFF, RET_WIDTH),
                  _resident((qk, D_MODEL)),
                  pl.BlockSpec((t, qk), lambda b, i: (i, 0)),
                  pl.BlockSpec((t, qk), lambda b, i: (i, 0)),
                  pl.BlockSpec((qk, t), lambda b, i: (0, i)),
                  pl.BlockSpec((qk, t), lambda b, i: (0, i))],
        out_specs=[pl.BlockSpec((1, t, HY_COLS), lambda b, i: (b, i, 0)),
                   pl.BlockSpec((1, RET_HEADS, t, RET_QK_DIM), lambda b, i: (b, 0, i, 0)),
                   pl.BlockSpec((1, qk, t), lambda b, i: (b, 0, i)),
                   pl.BlockSpec((1, t, RET_WIDTH), lambda b, i: (b, i, 0)),
                   pl.BlockSpec((1, t, RET_WIDTH), lambda b, i: (b, i, 0))],
        out_shape=[jax.ShapeDtypeStruct((BATCH, SEQ, HY_COLS), BF16),
                   jax.ShapeDtypeStruct((BATCH, RET_HEADS, SEQ, RET_QK_DIM), BF16),
                   jax.ShapeDtypeStruct((BATCH, qk, SEQ), BF16),
                   jax.ShapeDtypeStruct((BATCH, SEQ, RET_WIDTH), BF16),
                   jax.ShapeDtypeStruct((BATCH, SEQ, RET_WIDTH), BF16)],
        compiler_params=pltpu.CompilerParams(vmem_limit_bytes=VMEM_LIMIT),
        name="inproj",
    )(x, mod3, mod3, norm1, w_in, w_in, w_in, w_in, w_kt, cq, sq, ck, sk)


def _filter_mlp_kernel(zt_ref, t_ref, w1t_ref, b1_ref, f1_ref, w2t_ref, b2_ref, f2_ref, w3t_ref, adel_ref,
                       kt_ref):
    hid = jnp.sin(f1_ref[...] * (_dot_hi(w1t_ref[...], zt_ref[...]) + b1_ref[...]))
    hid = jnp.sin(f2_ref[...] * (_dot_hi(w2t_ref[...], hid) + b2_ref[...])).astype(BF16)
    c = HY_WIDTH
    for half in range(2):
        lags = slice(half * SEQ, (half + 1) * SEQ)
        window = jnp.exp(-adel_ref[...] * t_ref[:, lags])
        direction = 1 - half
        for o in range(HY_ORDER):
            r0 = (direction * HY_ORDER + o) * c
            w3 = w3t_ref[r0:r0 + c, :].astype(BF16)
            kt_ref[o, :, lags] = (_dot(w3, hid[:, lags]) * window).astype(BF16)


def _filter_mlp_call(zt, t_row, w1t, b1, f1, w2t, b2, f2, w3t, absdelta):
    fw = HY_FILTER_WIDTH
    n_out = 2 * HY_ORDER * HY_WIDTH
    return pl.pallas_call(
        _filter_mlp_kernel,
        grid=(1,),
        in_specs=[_resident((fw, 2 * SEQ)), _resident((1, 2 * SEQ)),
                  _resident((fw, fw)), _resident((fw, 1)), _resident((fw, 1)),
                  _resident((fw, fw)), _resident((fw, 1)), _resident((fw, 1)),
                  _resident((n_out, fw)), _resident((HY_WIDTH, 1))],
        out_specs=pl.BlockSpec((HY_ORDER, HY_WIDTH, 2 * SEQ), lambda i: (0, 0, 0)),
        out_shape=jax.ShapeDtypeStruct((HY_ORDER, HY_WIDTH, 2 * SEQ), BF16),
        compiler_params=pltpu.CompilerParams(vmem_limit_bytes=VMEM_LIMIT),
        name="filter_mlp",
    )(zt, t_row, w1t, b1, f1, w2t, b2, f2, w3t, absdelta)


_FILTER_PLANES = (
    {0: 1},
    {-1: 1, 0: -1},
    {1: 1, 0: -1},
    {-2: 1, 0: -1},
    {-3: 1, -1: -1, -2: -1, 0: 1},
    {-1: 1, 1: -1, -2: -1, 0: 1},
    {2: 1, 0: -1},
    {1: 1, -1: -1, 2: -1, 0: 1},
    {3: 1, 1: -1, 2: -1, 0: 1},
)
DFT_BLOCKS_PER_STEP = 4
PLANES_PER_STEP = 3


def _filter_dft_kernel(kt_ref, fwd_ref, h_ref, t_scr):
    p = CONV_BLOCK
    n_dft_steps = 2 * N_CONV_BLOCKS // DFT_BLOCKS_PER_STEP
    step = pl.program_id(1)

    for s in range(n_dft_steps):
        @pl.when(step == s)
        def _(s=s):
            for k in range(DFT_BLOCKS_PER_STEP):
                t_scr[s * DFT_BLOCKS_PER_STEP + k] = lax.dot_general(
                    fwd_ref[...], kt_ref[0, :, k * p:(k + 1) * p], _NT, preferred_element_type=F32)
            if s == 0:
                for k in range(PLANES_PER_STEP):
                    h_ref[0, k] = t_scr[0]

    row = lax.broadcasted_iota(jnp.int32, (p, 1), 0)
    sign = (1 - 2 * (row & 1)).astype(F32)
    re, im = slice(0, p), slice(p, 2 * p)

    def plane(k, coefs):
        def comb(shift, rows):
            acc = None
            for d, c in coefs.items():
                term = t_scr[d + N_CONV_BLOCKS - 1 + shift, rows, :]
                if acc is None:
                    acc = term
                else:
                    acc = acc + term if c > 0 else acc - term
            return acc

        h_ref[0, k, re, :] = comb(0, re) + sign * comb(1, im)
        h_ref[0, k, im, :] = comb(0, im) - sign * comb(1, re)

    for s in range(len(_FILTER_PLANES) // PLANES_PER_STEP):
        @pl.when(step == n_dft_steps + s)
        def _(s=s):
            for k in range(PLANES_PER_STEP):
                plane(k, _FILTER_PLANES[s * PLANES_PER_STEP + k])


def _filter_dft_call(kt, fwd_u):
    p = CONV_BLOCK
    n_blk = 2 * N_CONV_BLOCKS
    n_planes = len(_FILTER_PLANES)
    n_dft_steps = n_blk // DFT_BLOCKS_PER_STEP
    return pl.pallas_call(
        _filter_dft_kernel,
        grid=(HY_ORDER, n_dft_steps + n_planes // PLANES_PER_STEP),
        in_specs=[pl.BlockSpec((1, HY_WIDTH, DFT_BLOCKS_PER_STEP * p),
                               lambda o, s: (o, 0, jnp.minimum(s, n_dft_steps - 1))),
                  _resident((2 * p, p))],
        out_specs=pl.BlockSpec((1, PLANES_PER_STEP, 2 * p, HY_WIDTH),
                               lambda o, s: (o, jnp.maximum(s - n_dft_steps, 0), 0, 0)),
        out_shape=jax.ShapeDtypeStruct((HY_ORDER, n_planes, 2 * p, HY_WIDTH), F32),
        scratch_shapes=[pltpu.VMEM((n_blk, 2 * p, HY_WIDTH), F32)],
        compiler_params=pltpu.CompilerParams(vmem_limit_bytes=VMEM_LIMIT),
        name="filter_dft",
    )(kt, fwd_u)


def _conv3_rows(ref, j, n_blocks, rows, w, b):
    main = ref[0, j * rows:(j + 1) * rows, :].astype(F32)
    cols = main.shape[1]
    zeros = jnp.zeros((HALO, cols), F32)
    prev = ref[0, j * rows - HALO:j * rows, :].astype(F32) if j > 0 else zeros
    nxt = ref[0, (j + 1) * rows:(j + 1) * rows + HALO, :].astype(F32) if j < n_blocks - 1 else zeros
    ext = jnp.concatenate([prev, main, nxt], axis=0)
    n = rows + 2 * HALO
    before = pltpu.roll(ext, 1, 0)[HALO:HALO + rows]
    after = pltpu.roll(ext, n - 1, 0)[HALO:HALO + rows]
    return before * w[0:1] + main * w[1:2] + after * w[2:3] + b


def _hyena_kernel(u_ref, zg_ref, cwu_ref, cbu_ref, cwg_ref, cbg_ref, skip_ref, h_ref, fwd_ref, inv_ref,
                  o_ref, uf_scr, ub_scr, y_scr, *, conv_u):
    p = CONV_BLOCK
    nb = N_CONV_BLOCKS

    def prepare(j):
        if conv_u:
            ub_scr[j * p:(j + 1) * p, :] = _conv3_rows(u_ref, j, nb, p, cwu_ref[...], cbu_ref[...]).astype(BF16)

    def u_block(j):
        return ub_scr[j * p:(j + 1) * p, :] if conv_u else u_ref[0, j * p:(j + 1) * p, :]

    prepare(0)
    for j in range(nb):
        if j + 1 < nb:
            prepare(j + 1)
        for r in range(2 * p // DFT_ROWS):
            rows = slice(r * DFT_ROWS, (r + 1) * DFT_ROWS)
            uf_scr[j, rows, :] = _dot(fwd_ref[rows, :], u_block(j))

    def cadd(a, b):
        return a[0] + b[0], a[1] + b[1]

    def cmul(m, x):
        return m[0] * x[0] - m[1] * x[1], m[0] * x[1] + m[1] * x[0]

    def toeplitz2(k0, x0, x1, re, im, ln):
        m0, mu, ml = ((h_ref[0, k0 + t, re, ln], h_ref[0, k0 + t, im, ln]) for t in range(3))
        p1 = cmul(m0, cadd(x0, x1))
        return cadd(p1, cmul(mu, x1)), cadd(p1, cmul(ml, x0))

    assert nb == 4
    for r in range(p // FREQ_ROWS):
        re = slice(r * FREQ_ROWS, (r + 1) * FREQ_ROWS)
        im = slice(p + r * FREQ_ROWS, p + (r + 1) * FREQ_ROWS)
        for cb in range(HY_WIDTH // FREQ_LANES):
            ln = slice(cb * FREQ_LANES, (cb + 1) * FREQ_LANES)
            u = [(uf_scr[j, re, ln], uf_scr[j, im, ln]) for j in range(nb)]
            d0, d1 = toeplitz2(0, cadd(u[0], u[2]), cadd(u[1], u[3]), re, im, ln)
            b0, b1 = toeplitz2(3, u[2], u[3], re, im, ln)
            c0, c1 = toeplitz2(6, u[0], u[1], re, im, ln)
            for i, yi in enumerate((cadd(d0, b0), cadd(d1, b1), cadd(d0, c0), cadd(d1, c1))):
                y_scr[i, re, ln] = yi[0].astype(BF16)
                y_scr[i, im, ln] = yi[1].astype(BF16)

    for i in range(nb):
        gate = _conv3_rows(zg_ref, i, nb, p, cwg_ref[...], cbg_ref[...])
        y = _dot(inv_ref[...], y_scr[i])
        o_ref[0, i * p:(i + 1) * p, :] = (
            gate * (y + u_block(i).astype(F32) * skip_ref[...])).astype(BF16)


def _hyena_call(u, u_col, zhy, gate_col, conv_w, conv_b, skip, g_all, order, fwd_u, inv, conv_u):
    p = CONV_BLOCK
    c = HY_WIDTH
    ucol = u_col if conv_u else 0
    cwu = conv_w[:, ucol * c:(ucol + 1) * c]
    cbu = conv_b[:, ucol * c:(ucol + 1) * c]
    cwg = conv_w[:, gate_col * c:(gate_col + 1) * c]
    cbg = conv_b[:, gate_col * c:(gate_col + 1) * c]
    return pl.pallas_call(
        functools.partial(_hyena_kernel, conv_u=conv_u),
        grid=(BATCH,),
        in_specs=[pl.BlockSpec((1, SEQ, c), lambda b: (b, 0, u_col)),
                  pl.BlockSpec((1, SEQ, c), lambda b: (b, 0, gate_col)),
                  _resident((3, c)), _resident((1, c)), _resident((3, c)), _resident((1, c)),
                  _resident((1, c)),
                  pl.BlockSpec((1, len(_FILTER_PLANES), 2 * p, c), lambda b: (order, 0, 0, 0),
                               pipeline_mode=pl.Buffered(1)),
                  _resident((2 * p, p)),
                  _resident((p, 2 * p))],
        out_specs=pl.BlockSpec((1, SEQ, c), lambda b: (b, 0, 0)),
        out_shape=jax.ShapeDtypeStruct((BATCH, SEQ, c), BF16),
        scratch_shapes=[pltpu.VMEM((N_CONV_BLOCKS, 2 * p, c), F32),
                        pltpu.VMEM((SEQ, c) if conv_u else (SUBLANES * 2, LANES), BF16),
                        pltpu.VMEM((N_CONV_BLOCKS, 2 * p, c), BF16)],
        compiler_params=pltpu.CompilerParams(vmem_limit_bytes=HYENA_VMEM_LIMIT),
        name="hyena%d" % order,
    )(u, zhy, cwu, cbu, cwg, cbg, skip, g_all, fwd_u, inv)


def _ret_kernel(lg_ref, q_ref, kt_ref, v_ref, g_ref, s_ref, wup_ref, wdn_ref, wout_ref, cc_ref, wmod_ref, bmod_ref,
                o_ref, wup_b_ref, wdn_b_ref, wout_b_ref, modl_ref, b_scr, decay_scr, qw_scr):
    wup_b_ref[...] = wup_ref[...].astype(BF16)
    wdn_b_ref[...] = wdn_ref[...].astype(BF16)
    wout_b_ref[...] = wout_ref[...].astype(BF16)
    modl_ref[...] = _dot(_silu(cc_ref[...]).astype(BF16), wmod_ref[...].astype(BF16)) + bmod_ref[...]
    c = RET_CHUNK
    nc = SEQ // c
    dk = RET_QK_DIM
    dv = RET_V_DIM
    heads = range(RET_HEADS_PER_STEP)
    lg_f = [lg_ref[0, pl.program_id(0) * RET_HEADS_PER_STEP + hh] for hh in heads]
    lg_b = [lg_ref[1, pl.program_id(0) * RET_HEADS_PER_STEP + hh] for hh in heads]

    @pl.when(pl.program_id(1) == 0)
    def _():
        ii = lax.broadcasted_iota(jnp.int32, (c, c), 0).astype(F32)
        jj = lax.broadcasted_iota(jnp.int32, (c, c), 1).astype(F32)
        dif = ii - jj
        pos_q = lax.broadcasted_iota(jnp.int32, (c, dk), 0).astype(F32)
        for hh in heads:
            decay_scr[hh] = jnp.where(dif >= 0.0, jnp.exp(lg_f[hh] * jnp.maximum(dif, 0.0)),
                                      jnp.exp(lg_b[hh] * jnp.maximum(-dif, 0.0)))
            qw_scr[hh, 0] = jnp.exp(lg_f[hh] * (pos_q + 1.0))
            qw_scr[hh, 1] = jnp.exp(lg_b[hh] * (c - pos_q))

    pos_r = lax.broadcasted_iota(jnp.int32, (1, c), 1).astype(F32)
    ones = jnp.ones((1, dv), F32)
    kw_f = [jnp.exp(lg_f[hh] * (c - 1.0 - pos_r)) for hh in heads]
    kw_b = [jnp.exp(lg_b[hh] * pos_r) for hh in heads]
    dec_f = [jnp.exp(lg_f[hh] * float(c) * ones) for hh in heads]
    dec_b = [jnp.exp(lg_b[hh] * float(c) * ones) for hh in heads]

    def kt_chunk(hh, n):
        return kt_ref[0, hh * dk:(hh + 1) * dk, n * c:(n + 1) * c]

    def v_chunk(hh, n):
        return v_ref[0, n * c:(n + 1) * c, hh * dv:(hh + 1) * dv]

    state = [s_ref[0, hh, dk:2 * dk, :] for hh in heads]
    for hh in heads:
        b_scr[hh, nc - 1] = state[hh]
    for n in range(nc - 1, 0, -1):
        for hh in heads:
            ktn = (kt_chunk(hh, n).astype(F32) * kw_b[hh]).astype(BF16)
            state[hh] = state[hh] * dec_b[hh] + _dot(ktn, v_chunk(hh, n))
            b_scr[hh, n - 1] = state[hh]

    state = [s_ref[0, hh, 0:dk, :] for hh in heads]
    for n in range(nc):
        for hh in heads:
            qn = q_ref[0, hh, n * c:(n + 1) * c, :]
            ktn = kt_chunk(hh, n)
            vn = v_chunk(hh, n)
            scores = (_dot(qn, ktn) * decay_scr[hh]).astype(BF16)
            qf = qn.astype(F32)
            o = _dot(scores, vn)
            o = o + _dot((qf * qw_scr[hh, 0]).astype(BF16), state[hh].astype(BF16))
            o = o + _dot((qf * qw_scr[hh, 1]).astype(BF16), b_scr[hh, n].astype(BF16))
            state[hh] = state[hh] * dec_f[hh] + _dot((ktn.astype(F32) * kw_f[hh]).astype(BF16), vn)
            o = o * lax.rsqrt(jnp.mean(o * o, axis=-1, keepdims=True) + EPS)
            gate = g_ref[0, n * c:(n + 1) * c, hh * dv:(hh + 1) * dv].astype(F32)
            o_ref[0, n * c:(n + 1) * c, hh * dv:(hh + 1) * dv] = (_silu(gate) * o).astype(BF16)


def _ret_call(lg, q, kt, v, g, s, w_up, w_dn, w_out, cc, w_mod, b_mod):
    dk, dv = RET_QK_DIM, RET_V_DIM
    hp = RET_HEADS_PER_STEP
    assert RET_HEADS == hp

    def row_slab(w):
        return pl.BlockSpec((w.shape[0] // BATCH, w.shape[1]), lambda h, b: (b, 0))

    up_slab, dn_slab, out_slab = row_slab(w_up), row_slab(w_dn), row_slab(w_out)
    mcols = MOD_LATE_COLS // BATCH
    mod_first = MOD_EARLY_COLS // mcols
    mod_slab = pl.BlockSpec((MOD_ROWS, mcols), lambda h, b: (0, b))
    return pl.pallas_call(
        _ret_kernel,
        grid=(RET_HEADS // hp, BATCH),
        in_specs=[pl.BlockSpec(memory_space=pltpu.SMEM),
                  pl.BlockSpec((1, hp, SEQ, dk), lambda h, b: (b, h, 0, 0)),
                  pl.BlockSpec((1, hp * dk, SEQ), lambda h, b: (b, h, 0)),
                  pl.BlockSpec((1, SEQ, hp * dv), lambda h, b: (b, 0, h)),
                  pl.BlockSpec((1, SEQ, hp * dv), lambda h, b: (b, 0, h)),
                  pl.BlockSpec((1, hp, 2 * dk, dv), lambda h, b: (b, h, 0, 0)),
                  up_slab, dn_slab, out_slab,
                  _resident((MOD_ROWS, D_MODEL)),
                  pl.BlockSpec((D_MODEL, mcols), lambda h, b: (0, mod_first + b)),
                  pl.BlockSpec((1, mcols), lambda h, b: (0, mod_first + b))],
        out_specs=[pl.BlockSpec((1, SEQ, hp * dv), lambda h, b: (b, 0, h)), up_slab, dn_slab, out_slab, mod_slab],
        out_shape=[jax.ShapeDtypeStruct((BATCH, SEQ, RET_WIDTH), BF16),
                   jax.ShapeDtypeStruct(w_up.shape, BF16),
                   jax.ShapeDtypeStruct(w_dn.shape, BF16),
                   jax.ShapeDtypeStruct(w_out.shape, BF16),
                   jax.ShapeDtypeStruct((MOD_ROWS, MOD_LATE_COLS), F32)],
        scratch_shapes=[pltpu.VMEM((hp, SEQ // RET_CHUNK, dk, dv), F32),
                        pltpu.VMEM((hp, RET_CHUNK, RET_CHUNK), F32),
                        pltpu.VMEM((hp, 2, RET_CHUNK, dk), F32)],
        compiler_params=pltpu.CompilerParams(vmem_limit_bytes=VMEM_LIMIT),
        name="ret",
    )(lg, q, kt, v, g, s, w_up, w_dn, w_out, cc, w_mod, b_mod)


def _ffn_kernel(x_ref, xp_ref, xn_ref, yh_ref, yhp_ref, yhn_ref, yr_ref, yrp_ref, yrn_ref,
                g1_ref, sh_ref, sc_ref, g2_ref, n2_ref, nf_ref,
                woh_ref, wor_ref, wup_ref, cw_ref, cb_ref, wdn_ref,
                o_ref, hb_scr, x1_scr, av_scr, ag_scr, act_scr):
    t = TOK_TILE
    i = pl.program_id(1)
    nt = pl.num_programs(1)
    th = t // 2
    tile_halves = (slice(0, th), slice(th, t))

    def mixed(xr, yh, yr, rows=slice(None)):
        return xr[0, rows, :] + g1_ref[0] * (_dot(yh[0, rows, :], woh_ref[...]) + _dot(yr[0, rows, :], wor_ref[...]))

    def hidden(x1):
        return _norm_mod(x1, n2_ref[...], sh_ref[0], sc_ref[0])

    half = (t + 2 * HALO) // 2
    up_halves = (slice(0, half), slice(half, 2 * half))

    def up(slot, cblk, rows):
        c0 = cblk * FFN_COLS
        hb = hb_scr[rows, :]
        av_scr[slot, rows, :] = _dot(hb, wup_ref[:, c0:c0 + FFN_COLS])
        ag_scr[slot, rows, :] = _dot(hb, wup_ref[:, D_FF + c0:D_FF + c0 + FFN_COLS])

    for rows in tile_halves:
        x1_scr[rows, :] = mixed(x_ref, yh_ref, yr_ref, rows)
    hp = hidden(mixed(xp_ref, yhp_ref, yrp_ref))
    hn = hidden(mixed(xn_ref, yhn_ref, yrn_ref))
    hb_scr[0:HALO, :] = jnp.where(i > 0, hp, 0.0).astype(BF16)
    hb_scr[HALO + t:2 * HALO + t, :] = jnp.where(i < nt - 1, hn, 0.0).astype(BF16)
    hb_scr[HALO:HALO + th, :] = hidden(x1_scr[tile_halves[0], :]).astype(BF16)
    up(0, 0, up_halves[0])
    hb_scr[HALO + th:HALO + t, :] = hidden(x1_scr[tile_halves[1], :]).astype(BF16)
    up(0, 0, up_halves[1])

    def conv(scr, slot, col):
        w = cw_ref[:, col:col + FFN_COLS]
        return (scr[slot, HALO - 1:HALO - 1 + t, :] * w[0:1] + scr[slot, HALO:HALO + t, :] * w[1:2]
                + scr[slot, HALO + 1:HALO + 1 + t, :] * w[2:3] + cb_ref[:, col:col + FFN_COLS])

    n_blk = D_FF // FFN_COLS
    for cblk in range(n_blk):
        c0 = cblk * FFN_COLS
        slot = cblk % 2
        if cblk + 1 < n_blk:
            for rows in up_halves:
                up(1 - slot, cblk + 1, rows)
        act_scr[:, c0:c0 + FFN_COLS] = (
            _silu(conv(ag_scr, slot, D_FF + c0)) * conv(av_scr, slot, c0)).astype(BF16)

    down = [_dot(act_scr[rows, :], wdn_ref[...]) for rows in tile_halves]
    for rows, ffn in zip(tile_halves, down):
        x2 = x1_scr[rows, :] + g2_ref[0] * ffn
        o_ref[0, rows, :] = x2 * lax.rsqrt(jnp.mean(x2 * x2, axis=-1, keepdims=True) + EPS) * nf_ref[...]


def _ffn_call(x, y_hy, y_ret, mod3, norm2, norm_f, w_oh, w_or, w_up, conv_w, conv_b, w_dn):
    t = TOK_TILE
    r = t // HALO
    last = SEQ // HALO - 1

    def main(width):
        return pl.BlockSpec((1, t, width), lambda b, i: (b, i, 0))

    def prev(width):
        return pl.BlockSpec((1, HALO, width), lambda b, i: (b, jnp.maximum(i * r - 1, 0), 0))

    def nxt(width):
        return pl.BlockSpec((1, HALO, width), lambda b, i: (b, jnp.minimum((i + 1) * r, last), 0))

    def modrow(k):
        return pl.BlockSpec((1, 1, D_MODEL), lambda b, i: (b, 0, k))

    return pl.pallas_call(
        _ffn_kernel,
        grid=(BATCH, SEQ // t),
        in_specs=[main(D_MODEL), prev(D_MODEL), nxt(D_MODEL),
                  main(HY_WIDTH), prev(HY_WIDTH), nxt(HY_WIDTH),
                  main(RET_WIDTH), prev(RET_WIDTH), nxt(RET_WIDTH),
                  modrow(0), modrow(1), modrow(2), modrow(3),
                  _resident((1, D_MODEL)), _resident((1, D_MODEL)),
                  _resident_rows(0, HY_WIDTH, D_MODEL), _resident_rows(HY_WIDTH, RET_WIDTH, D_MODEL),
                  _resident((D_MODEL, 2 * D_FF)),
                  _resident((3, 2 * D_FF)), _resident((1, 2 * D_FF)),
                  _resident((D_FF, D_MODEL))],
        out_specs=pl.BlockSpec((1, t, D_MODEL), lambda b, i: (b, i, 0)),
        out_shape=jax.ShapeDtypeStruct((BATCH, SEQ, D_MODEL), F32),
        scratch_shapes=[pltpu.VMEM((t + 2 * HALO, D_MODEL), BF16),
                        pltpu.VMEM((t, D_MODEL), F32),
                        pltpu.VMEM((2, t + 2 * HALO, FFN_COLS), F32),
                        pltpu.VMEM((2, t + 2 * HALO, FFN_COLS), F32),
                        pltpu.VMEM((t, D_FF), BF16)],
        compiler_params=pltpu.CompilerParams(vmem_limit_bytes=VMEM_LIMIT),
        name="ffn",
    )(x, x, x, y_hy, y_hy, y_hy, y_ret, y_ret, y_ret, mod3, mod3, mod3, mod3,
      norm2, norm_f, w_oh, w_or, w_up, conv_w, conv_b, w_dn)


def kernel(x, c, ctx, c_ctx, w_mod, b_mod, norm1, w_in, hy_conv_w, hy_conv_b, hy_w1, hy_b1, hy_f1,
           hy_w2, hy_b2, hy_f2, hy_w3, hy_bias, ret_logit_f, ret_logit_b, w_out, norm2,
           ffn_w_up, ffn_conv_w, ffn_conv_b, ffn_w_down, norm_f):
    layer = 0
    rope = tuple(jnp.asarray(a) for a in _rope_tables())
    zt, t_row, absdelta = (jnp.asarray(a) for a in _filter_features())
    fwd_np, inv_np = _dft_matrices()
    fwd_u = jnp.asarray(fwd_np[:, :CONV_BLOCK]).astype(BF16)
    inv = jnp.asarray(inv_np).astype(BF16)

    w_in_b = w_in[layer].astype(BF16)
    w_kt = w_in_b[:, K_OFF:V_OFF].T
    row = lambda a: a.reshape(1, -1)
    col = lambda a: a.reshape(-1, 1)
    w1p = jnp.pad(hy_w1[layer], ((0, HY_FILTER_WIDTH - hy_w1.shape[1]), (0, 0)))
    lg = jnp.stack([jax.nn.log_sigmoid(ret_logit_f[layer].astype(F32)),
                    jax.nn.log_sigmoid(ret_logit_b[layer].astype(F32))])

    cc = jnp.concatenate([c, c_ctx[None, :], jnp.zeros((MOD_ROWS - BATCH - 1, D_MODEL), F32)], axis=0)
    mod3 = _mod_call(cc, w_mod[layer], row(b_mod[layer])).reshape(MOD_ROWS, 1, MOD_EARLY_COLS)
    norm1_r = row(norm1[layer])

    s_ctx = _ctx_call(lg, ctx, mod3, norm1_r, w_kt, w_in_b)
    zhy, q, kt, v, g = _inproj_call(x, mod3, norm1_r, w_in_b, w_kt, rope)

    k_two_sided = _filter_mlp_call(zt, t_row, w1p.T, col(hy_b1[layer]), col(hy_f1[layer]), hy_w2[layer].T,
                                   col(hy_b2[layer]), col(hy_f2[layer]), hy_w3[layer].T, absdelta)
    g_all = _filter_dft_call(k_two_sided, fwd_u)
    conv_w, conv_b = hy_conv_w[layer], row(hy_conv_b[layer])
    y1 = _hyena_call(zhy, 0, zhy, 1, conv_w, conv_b, hy_bias[layer][0:1], g_all, 0, fwd_u, inv, True)
    y_hy = _hyena_call(y1, 0, zhy, 2, conv_w, conv_b, hy_bias[layer][1:2], g_all, 1, fwd_u, inv, False)

    y_ret, w_up, w_dn, w_out_b, mod_late = _ret_call(lg, q, kt, v, g, s_ctx, ffn_w_up[layer], ffn_w_down[layer],
                                                     w_out[layer], cc, w_mod[layer], row(b_mod[layer]))
    mod_late3 = mod_late.reshape(MOD_ROWS, 1, MOD_LATE_COLS)

    return _ffn_call(x, y_hy, y_ret, mod_late3, row(norm2[layer]), row(norm_f), w_out_b, w_out_b, w_up,
                     ffn_conv_w[layer], row(ffn_conv_b[layer]), w_dn)
```

```python
import functools
import math

import numpy as np
import jax
import jax.numpy as jnp
from jax import lax
from jax.experimental import pallas as pl
from jax.experimental.pallas import tpu as pltpu

F32 = jnp.float32
BF16 = jnp.bfloat16

D_MODEL = 1024
BATCH = 8
SEQ = 2048
CTX_LEN = 256
GRID_W = 64
HY_WIDTH = 512
HY_ORDER = 2
HY_EMB_BANDS = 16
HY_FILTER_WIDTH = 64
HY_FAST_DECAY = 0.3
HY_SLOW_DECAY = 1.5
HY_TARGET = 1e-2
RET_WIDTH = 512
RET_HEADS = 4
RET_QK_DIM = 64
RET_V_DIM = 128
ROPE_BASE = 10000.0
D_FF = 2816
EPS = 1e-6
HY_COLS = (HY_ORDER + 1) * HY_WIDTH
Q_OFF = HY_COLS
K_OFF = Q_OFF + RET_HEADS * RET_QK_DIM
V_OFF = K_OFF + RET_HEADS * RET_QK_DIM
G_OFF = V_OFF + RET_WIDTH
K_SCALE = RET_QK_DIM ** -0.5

MOD_ROWS = 16
MOD_EARLY_COLS = 2 * D_MODEL
MOD_LATE_COLS = 4 * D_MODEL
TOK_TILE = 512
IN_TILE = 1024
IN_PIECE = 256
SUBLANES = 8
LANES = 128
HALO = 16
CONV_BLOCK = 512
N_CONV_BLOCKS = SEQ // CONV_BLOCK
FREQ_ROWS = 16
FREQ_LANES = 128
DFT_ROWS = 512
CTX_PER_STEP = 4
RET_CHUNK = 256
RET_HEADS_PER_STEP = 4
FFN_COLS = 256
VMEM_LIMIT = 56 * 1024 * 1024
HYENA_VMEM_LIMIT = 62 * 1024 * 1024

_NT = (((1,), (1,)), ((), ()))


def _dot(a, b):
    return jnp.dot(a, b, preferred_element_type=F32)


def _dot_hi(a, b):
    return jnp.dot(a, b, preferred_element_type=F32, precision=lax.Precision.HIGHEST)


def _silu(x):
    return x * (1.0 / (1.0 + jnp.exp(-x)))


def _norm_mod(x, gain, shift, scale):
    y = x * lax.rsqrt(jnp.mean(x * x, axis=-1, keepdims=True) + EPS)
    return (y * gain) * (1.0 + scale) + shift


def _resident(shape):
    nd = len(shape)
    return pl.BlockSpec(shape, lambda *_: (0,) * nd, pipeline_mode=pl.Buffered(1))


def _resident_cols(rows, col0, width):
    assert col0 % width == 0
    return pl.BlockSpec((rows, width), lambda *_: (0, col0 // width), pipeline_mode=pl.Buffered(1))


def _resident_rows(row0, height, cols):
    assert row0 % height == 0
    return pl.BlockSpec((height, cols), lambda *_: (row0 // height, 0), pipeline_mode=pl.Buffered(1))


@functools.lru_cache(maxsize=None)
def _rope_tables():
    pos = np.arange(SEQ)
    row = (pos // GRID_W).astype(np.float64)
    col = (pos % GRID_W).astype(np.float64)
    quarter = RET_QK_DIM // 4
    inv_freq = ROPE_BASE ** (-np.arange(quarter, dtype=np.float64) / quarter)
    ang = np.concatenate([row[:, None] * inv_freq, col[:, None] * inv_freq], axis=-1)
    cos, sin = np.cos(ang), np.sin(ang)
    cos_h = np.concatenate([cos, cos], axis=-1)
    sin_h = np.concatenate([-sin, sin], axis=-1)
    cos_t = np.tile(cos_h, (1, RET_HEADS))
    sin_t = np.tile(sin_h, (1, RET_HEADS))
    return (cos_t.astype(np.float32), sin_t.astype(np.float32),
            np.ascontiguousarray((cos_t * K_SCALE).T).astype(np.float32),
            np.ascontiguousarray((sin_t * K_SCALE).T).astype(np.float32))


@functools.lru_cache(maxsize=None)
def _filter_features():
    lag = np.abs(np.arange(2 * SEQ) - SEQ).astype(np.float64)
    t = lag / (SEQ - 1)
    bands = np.linspace(1e-4, HY_EMB_BANDS - 1, HY_EMB_BANDS)
    ang = 2.0 * math.pi * lag[:, None] * bands[None, :] / SEQ
    z = np.concatenate([t[:, None], np.cos(ang), -np.sin(ang)], axis=-1)
    zp = np.zeros((2 * SEQ, HY_FILTER_WIDTH), np.float64)
    zp[:, :z.shape[1]] = z
    max_decay = math.log(HY_TARGET) / HY_FAST_DECAY
    min_decay = math.log(HY_TARGET) / HY_SLOW_DECAY
    absdelta = np.abs(np.linspace(min_decay, max_decay, HY_WIDTH))[:, None]
    return (np.ascontiguousarray(zp.T).astype(np.float32), t[None, :].astype(np.float32),
            absdelta.astype(np.float32))


@functools.lru_cache(maxsize=None)
def _dft_matrices():
    p = CONV_BLOCK
    n = 2 * p
    f = np.arange(p, dtype=np.float64)[:, None] + 0.5
    t = np.arange(n, dtype=np.float64)[None, :]
    theta = 2.0 * math.pi * f * t / n
    fwd = np.concatenate([np.cos(theta), -np.sin(theta)], axis=0)
    th_out = theta[:, p:].T
    inv = np.concatenate([np.cos(th_out), -np.sin(th_out)], axis=1) / p
    return fwd.astype(np.float32), inv.astype(np.float32)


def _mod_kernel(c_ref, w_ref, b_ref, o_ref):
    s = _silu(c_ref[...]).astype(BF16)
    o_ref[...] = _dot(s, w_ref[...].astype(BF16)) + b_ref[...]


def _mod_call(cc, w_mod, b_mod):
    ncol = MOD_EARLY_COLS
    blk = ncol // 2
    return pl.pallas_call(
        _mod_kernel,
        grid=(ncol // blk,),
        in_specs=[pl.BlockSpec((MOD_ROWS, D_MODEL), lambda j: (0, 0)),
                  pl.BlockSpec((D_MODEL, blk), lambda j: (0, j)),
                  pl.BlockSpec((1, blk), lambda j: (0, j))],
        out_specs=pl.BlockSpec((MOD_ROWS, blk), lambda j: (0, j)),
        out_shape=jax.ShapeDtypeStruct((MOD_ROWS, ncol), F32),
        compiler_params=pltpu.CompilerParams(vmem_limit_bytes=VMEM_LIMIT),
        name="mod",
    )(cc, w_mod, b_mod)


def _ctx_kernel(lg_ref, x_ref, sh_ref, sc_ref, n1_ref, wkt_ref, wv_ref, s_ref):
    pos = lax.broadcasted_iota(jnp.int32, (1, CTX_LEN), 1).astype(F32)
    w_f = [jnp.exp(lg_ref[0, hh] * (CTX_LEN - 1.0 - pos)) for hh in range(RET_HEADS)]
    w_b = [jnp.exp(lg_ref[1, hh] * pos) for hh in range(RET_HEADS)]
    wkt, wv = wkt_ref[...].astype(BF16), wv_ref[...].astype(BF16)
    for bb in range(CTX_PER_STEP):
        h = _norm_mod(x_ref[bb], n1_ref[...], sh_ref[0], sc_ref[0]).astype(BF16)
        kt = lax.dot_general(wkt, h, _NT, preferred_element_type=F32) * K_SCALE
        v = _dot(h, wv)
        for hh in range(RET_HEADS):
            kth = kt[hh * RET_QK_DIM:(hh + 1) * RET_QK_DIM, :]
            vh = v[:, hh * RET_V_DIM:(hh + 1) * RET_V_DIM].astype(BF16)
            s_ref[bb, hh, 0:RET_QK_DIM, :] = _dot((kth * w_f[hh]).astype(BF16), vh)
            s_ref[bb, hh, RET_QK_DIM:2 * RET_QK_DIM, :] = _dot((kth * w_b[hh]).astype(BF16), vh)


def _ctx_call(lg, ctx, mod3, norm1, w_kt, w_v):
    return pl.pallas_call(
        _ctx_kernel,
        grid=(BATCH // CTX_PER_STEP,),
        in_specs=[pl.BlockSpec(memory_space=pltpu.SMEM),
                  pl.BlockSpec((CTX_PER_STEP, CTX_LEN, D_MODEL), lambda b: (b, 0, 0)),
                  pl.BlockSpec((1, 1, D_MODEL), lambda b: (BATCH, 0, 0)),
                  pl.BlockSpec((1, 1, D_MODEL), lambda b: (BATCH, 0, 1)),
                  _resident((1, D_MODEL)),
                  _resident((RET_HEADS * RET_QK_DIM, D_MODEL)),
                  _resident_cols(D_MODEL, V_OFF, RET_WIDTH)],
        out_specs=pl.BlockSpec((CTX_PER_STEP, RET_HEADS, 2 * RET_QK_DIM, RET_V_DIM), lambda b: (b, 0, 0, 0)),
        out_shape=jax.ShapeDtypeStruct((BATCH, RET_HEADS, 2 * RET_QK_DIM, RET_V_DIM), F32),
        compiler_params=pltpu.CompilerParams(vmem_limit_bytes=VMEM_LIMIT),
        name="ctx",
    )(lg, ctx, mod3, mod3, norm1, w_kt, w_v)


def _swap_halves(x, axis):
    n = x.shape[axis]
    half = RET_QK_DIM // 2
    idx = lax.broadcasted_iota(jnp.int32, x.shape, axis)
    first = (idx & (RET_QK_DIM - 1)) < half
    return jnp.where(first, pltpu.roll(x, n - half, axis), pltpu.roll(x, half, axis))


def _inproj_kernel(x_ref, sh_ref, sc_ref, n1_ref, why_ref, wq_ref, wv_ref, wg_ref, wkt_ref,
                   cq_ref, sq_ref, ck_ref, sk_ref,
                   zhy_ref, q_ref, kt_ref, v_ref, g_ref):
    pieces = [slice(r, r + IN_PIECE) for r in range(0, IN_TILE, IN_PIECE)]
    why, wq, wv, wg, wkt = (r[...].astype(BF16) for r in (why_ref, wq_ref, wv_ref, wg_ref, wkt_ref))
    hbs = [_norm_mod(x_ref[0, rows, :], n1_ref[...], sh_ref[0], sc_ref[0]).astype(BF16) for rows in pieces]
    for rows, hb in zip(pieces, hbs):
        zhy_ref[0, rows, :] = _dot(hb, why).astype(BF16)
        v_ref[0, rows, :] = _dot(hb, wv).astype(BF16)
        g_ref[0, rows, :] = _dot(hb, wg).astype(BF16)
        q = _dot(hb, wq)
        q = q * cq_ref[rows, :] + _swap_halves(q, 1) * sq_ref[rows, :]
        for hh in range(RET_HEADS):
            q_ref[0, hh, rows, :] = q[:, hh * RET_QK_DIM:(hh + 1) * RET_QK_DIM].astype(BF16)
        kt = lax.dot_general(wkt, hb, _NT, preferred_element_type=F32)
        kt = kt * ck_ref[:, rows] + _swap_halves(kt, 0) * sk_ref[:, rows]
        kt_ref[0, :, rows] = kt.astype(BF16)


def _inproj_call(x, mod3, norm1, w_in, w_kt, rope):
    cq, sq, ck, sk = rope
    t = IN_TILE
    qk = RET_HEADS * RET_QK_DIM
    return pl.pallas_call(
        _inproj_kernel,
        grid=(BATCH, SEQ // t),
        in_specs=[pl.BlockSpec((1, t, D_MODEL), lambda b, i: (b, i, 0)),
                  pl.BlockSpec((1, 1, D_MODEL), lambda b, i: (b, 0, 0)),
                  pl.BlockSpec((1, 1, D_MODEL), lambda b, i: (b, 0, 1)),
                  _resident((1, D_MODEL)),
                  _resident_cols(D_MODEL, 0, HY_COLS),
                  _resident_cols(D_MODEL, Q_OFF, qk),
                  _resident_cols(D_MODEL, V_OFF, RET_WIDTH),
                  _resident_cols(D_MODEL, G_OFF, RET_WIDTH),
                  _resident((qk, D_MODEL)),
                  pl.BlockSpec((t, qk), lambda b, i: (i, 0)),
                  pl.BlockSpec((t, qk), lambda b, i: (i, 0)),
                  pl.BlockSpec((qk, t), lambda b, i: (0, i)),
                  pl.BlockSpec((qk, t), lambda b, i: (0, i))],
        out_specs=[pl.BlockSpec((1, t, HY_COLS), lambda b, i: (b, i, 0)),
                   pl.BlockSpec((1, RET_HEADS, t, RET_QK_DIM), lambda b, i: (b, 0, i, 0)),
                   pl.BlockSpec((1, qk, t), lambda b, i: (b, 0, i)),
                   pl.BlockSpec((1, t, RET_WIDTH), lambda b, i: (b, i, 0)),
                   pl.BlockSpec((1, t, RET_WIDTH), lambda b, i: (b, i, 0))],
        out_shape=[jax.ShapeDtypeStruct((BATCH, SEQ, HY_COLS), BF16),
                   jax.ShapeDtypeStruct((BATCH, RET_HEADS, SEQ, RET_QK_DIM), BF16),
                   jax.ShapeDtypeStruct((BATCH, qk, SEQ), BF16),
                   jax.ShapeDtypeStruct((BATCH, SEQ, RET_WIDTH), BF16),
                   jax.ShapeDtypeStruct((BATCH, SEQ, RET_WIDTH), BF16)],
        compiler_params=pltpu.CompilerParams(vmem_limit_bytes=VMEM_LIMIT),
        name="inproj",
    )(x, mod3, mod3, norm1, w_in, w_in, w_in, w_in, w_kt, cq, sq, ck, sk)


def _filter_mlp_kernel(zt_ref, t_ref, w1t_ref, b1_ref, f1_ref, w2t_ref, b2_ref, f2_ref, w3t_ref, adel_ref,
                       kt_ref):
    hid = jnp.sin(f1_ref[...] * (_dot_hi(w1t_ref[...], zt_ref[...]) + b1_ref[...]))
    hid = jnp.sin(f2_ref[...] * (_dot_hi(w2t_ref[...], hid) + b2_ref[...])).astype(BF16)
    c = HY_WIDTH
    for half in range(2):
        lags = slice(half * SEQ, (half + 1) * SEQ)
        window = jnp.exp(-adel_ref[...] * t_ref[:, lags])
        direction = 1 - half
        for o in range(HY_ORDER):
            r0 = (direction * HY_ORDER + o) * c
            w3 = w3t_ref[r0:r0 + c, :].astype(BF16)
            kt_ref[o, :, lags] = (_dot(w3, hid[:, lags]) * window).astype(BF16)


def _filter_mlp_call(zt, t_row, w1t, b1, f1, w2t, b2, f2, w3t, absdelta):
    fw = HY_FILTER_WIDTH
    n_out = 2 * HY_ORDER * HY_WIDTH
    return pl.pallas_call(
        _filter_mlp_kernel,
        grid=(1,),
        in_specs=[_resident((fw, 2 * SEQ)), _resident((1, 2 * SEQ)),
                  _resident((fw, fw)), _resident((fw, 1)), _resident((fw, 1)),
                  _resident((fw, fw)), _resident((fw, 1)), _resident((fw, 1)),
                  _resident((n_out, fw)), _resident((HY_WIDTH, 1))],
        out_specs=pl.BlockSpec((HY_ORDER, HY_WIDTH, 2 * SEQ), lambda i: (0, 0, 0)),
        out_shape=jax.ShapeDtypeStruct((HY_ORDER, HY_WIDTH, 2 * SEQ), BF16),
        compiler_params=pltpu.CompilerParams(vmem_limit_bytes=VMEM_LIMIT),
        name="filter_mlp",
    )(zt, t_row, w1t, b1, f1, w2t, b2, f2, w3t, absdelta)


_FILTER_PLANES = (
    {0: 1},
    {-1: 1, 0: -1},
    {1: 1, 0: -1},
    {-2: 1, 0: -1},
    {-3: 1, -1: -1, -2: -1, 0: 1},
    {-1: 1, 1: -1, -2: -1, 0: 1},
    {2: 1, 0: -1},
    {1: 1, -1: -1, 2: -1, 0: 1},
    {3: 1, 1: -1, 2: -1, 0: 1},
)
DFT_BLOCKS_PER_STEP = 4
PLANES_PER_STEP = 3


def _filter_dft_kernel(kt_ref, fwd_ref, h_ref, t_scr):
    p = CONV_BLOCK
    n_dft_steps = 2 * N_CONV_BLOCKS // DFT_BLOCKS_PER_STEP
    step = pl.program_id(1)

    for s in range(n_dft_steps):
        @pl.when(step == s)
        def _(s=s):
            for k in range(DFT_BLOCKS_PER_STEP):
                t_scr[s * DFT_BLOCKS_PER_STEP + k] = lax.dot_general(
                    fwd_ref[...], kt_ref[0, :, k * p:(k + 1) * p], _NT, preferred_element_type=F32)
            if s == 0:
                for k in range(PLANES_PER_STEP):
                    h_ref[0, k] = t_scr[0]

    row = lax.broadcasted_iota(jnp.int32, (p, 1), 0)
    sign = (1 - 2 * (row & 1)).astype(F32)
    re, im = slice(0, p), slice(p, 2 * p)

    def plane(k, coefs):
        def comb(shift, rows):
            acc = None
            for d, c in coefs.items():
                term = t_scr[d + N_CONV_BLOCKS - 1 + shift, rows, :]
                if acc is None:
                    acc = term
                else:
                    acc = acc + term if c > 0 else acc - term
            return acc

        h_ref[0, k, re, :] = comb(0, re) + sign * comb(1, im)
        h_ref[0, k, im, :] = comb(0, im) - sign * comb(1, re)

    for s in range(len(_FILTER_PLANES) // PLANES_PER_STEP):
        @pl.when(step == n_dft_steps + s)
        def _(s=s):
            for k in range(PLANES_PER_STEP):
                plane(k, _FILTER_PLANES[s * PLANES_PER_STEP + k])


def _filter_dft_call(kt, fwd_u):
    p = CONV_BLOCK
    n_blk = 2 * N_CONV_BLOCKS
    n_planes = len(_FILTER_PLANES)
    n_dft_steps = n_blk // DFT_BLOCKS_PER_STEP
    return pl.pallas_call(
        _filter_dft_kernel,
        grid=(HY_ORDER, n_dft_steps + n_planes // PLANES_PER_STEP),
        in_specs=[pl.BlockSpec((1, HY_WIDTH, DFT_BLOCKS_PER_STEP * p),
                               lambda o, s: (o, 0, jnp.minimum(s, n_dft_steps - 1))),
                  _resident((2 * p, p))],
        out_specs=pl.BlockSpec((1, PLANES_PER_STEP, 2 * p, HY_WIDTH),
                               lambda o, s: (o, jnp.maximum(s - n_dft_steps, 0), 0, 0)),
        out_shape=jax.ShapeDtypeStruct((HY_ORDER, n_planes, 2 * p, HY_WIDTH), F32),
        scratch_shapes=[pltpu.VMEM((n_blk, 2 * p, HY_WIDTH), F32)],
        compiler_params=pltpu.CompilerParams(vmem_limit_bytes=VMEM_LIMIT),
        name="filter_dft",
    )(kt, fwd_u)


def _conv3_rows(ref, j, n_blocks, rows, w, b):
    main = ref[0, j * rows:(j + 1) * rows, :].astype(F32)
    cols = main.shape[1]
    zeros = jnp.zeros((HALO, cols), F32)
    prev = ref[0, j * rows - HALO:j * rows, :].astype(F32) if j > 0 else zeros
    nxt = ref[0, (j + 1) * rows:(j + 1) * rows + HALO, :].astype(F32) if j < n_blocks - 1 else zeros
    ext = jnp.concatenate([prev, main, nxt], axis=0)
    n = rows + 2 * HALO
    before = pltpu.roll(ext, 1, 0)[HALO:HALO + rows]
    after = pltpu.roll(ext, n - 1, 0)[HALO:HALO + rows]
    return before * w[0:1] + main * w[1:2] + after * w[2:3] + b


def _hyena_kernel(u_ref, zg_ref, cwu_ref, cbu_ref, cwg_ref, cbg_ref, skip_ref, h_ref, fwd_ref, inv_ref,
                  o_ref, uf_scr, ub_scr, y_scr, *, conv_u):
    p = CONV_BLOCK
    nb = N_CONV_BLOCKS

    def prepare(j):
        if conv_u:
            ub_scr[j * p:(j + 1) * p, :] = _conv3_rows(u_ref, j, nb, p, cwu_ref[...], cbu_ref[...]).astype(BF16)

    def u_block(j):
        return ub_scr[j * p:(j + 1) * p, :] if conv_u else u_ref[0, j * p:(j + 1) * p, :]

    prepare(0)
    for j in range(nb):
        if j + 1 < nb:
            prepare(j + 1)
        for r in range(2 * p // DFT_ROWS):
            rows = slice(r * DFT_ROWS, (r + 1) * DFT_ROWS)
            uf_scr[j, rows, :] = _dot(fwd_ref[rows, :], u_block(j))

    def cadd(a, b):
        return a[0] + b[0], a[1] + b[1]

    def cmul(m, x):
        return m[0] * x[0] - m[1] * x[1], m[0] * x[1] + m[1] * x[0]

    def toeplitz2(k0, x0, x1, re, im, ln):
        m0, mu, ml = ((h_ref[0, k0 + t, re, ln], h_ref[0, k0 + t, im, ln]) for t in range(3))
        p1 = cmul(m0, cadd(x0, x1))
        return cadd(p1, cmul(mu, x1)), cadd(p1, cmul(ml, x0))

    assert nb == 4
    for r in range(p // FREQ_ROWS):
        re = slice(r * FREQ_ROWS, (r + 1) * FREQ_ROWS)
        im = slice(p + r * FREQ_ROWS, p + (r + 1) * FREQ_ROWS)
        for cb in range(HY_WIDTH // FREQ_LANES):
            ln = slice(cb * FREQ_LANES, (cb + 1) * FREQ_LANES)
            u = [(uf_scr[j, re, ln], uf_scr[j, im, ln]) for j in range(nb)]
            d0, d1 = toeplitz2(0, cadd(u[0], u[2]), cadd(u[1], u[3]), re, im, ln)
            b0, b1 = toeplitz2(3, u[2], u[3], re, im, ln)
            c0, c1 = toeplitz2(6, u[0], u[1], re, im, ln)
            for i, yi in enumerate((cadd(d0, b0), cadd(d1, b1), cadd(d0, c0), cadd(d1, c1))):
                y_scr[i, re, ln] = yi[0].astype(BF16)
                y_scr[i, im, ln] = yi[1].astype(BF16)

    for i in range(nb):
        gate = _conv3_rows(zg_ref, i, nb, p, cwg_ref[...], cbg_ref[...])
        y = _dot(inv_ref[...], y_scr[i])
        o_ref[0, i * p:(i + 1) * p, :] = (
            gate * (y + u_block(i).astype(F32) * skip_ref[...])).astype(BF16)


def _hyena_call(u, u_col, zhy, gate_col, conv_w, conv_b, skip, g_all, order, fwd_u, inv, conv_u):
    p = CONV_BLOCK
    c = HY_WIDTH
    ucol = u_col if conv_u else 0
    cwu = conv_w[:, ucol * c:(ucol + 1) * c]
    cbu = conv_b[:, ucol * c:(ucol + 1) * c]
    cwg = conv_w[:, gate_col * c:(gate_col + 1) * c]
    cbg = conv_b[:, gate_col * c:(gate_col + 1) * c]
    return pl.pallas_call(
        functools.partial(_hyena_kernel, conv_u=conv_u),
        grid=(BATCH,),
        in_specs=[pl.BlockSpec((1, SEQ, c), lambda b: (b, 0, u_col)),
                  pl.BlockSpec((1, SEQ, c), lambda b: (b, 0, gate_col)),
                  _resident((3, c)), _resident((1, c)), _resident((3, c)), _resident((1, c)),
                  _resident((1, c)),
                  pl.BlockSpec((1, len(_FILTER_PLANES), 2 * p, c), lambda b: (order, 0, 0, 0),
                               pipeline_mode=pl.Buffered(1)),
                  _resident((2 * p, p)),
                  _resident((p, 2 * p))],
        out_specs=pl.BlockSpec((1, SEQ, c), lambda b: (b, 0, 0)),
        out_shape=jax.ShapeDtypeStruct((BATCH, SEQ, c), BF16),
        scratch_shapes=[pltpu.VMEM((N_CONV_BLOCKS, 2 * p, c), F32),
                        pltpu.VMEM((SEQ, c) if conv_u else (SUBLANES * 2, LANES), BF16),
                        pltpu.VMEM((N_CONV_BLOCKS, 2 * p, c), BF16)],
        compiler_params=pltpu.CompilerParams(vmem_limit_bytes=HYENA_VMEM_LIMIT),
        name="hyena%d" % order,
    )(u, zhy, cwu, cbu, cwg, cbg, skip, g_all, fwd_u, inv)


def _ret_kernel(lg_ref, q_ref, kt_ref, v_ref, g_ref, s_ref, wup_ref, wdn_ref, wout_ref, cc_ref, wmod_ref, bmod_ref,
                o_ref, wup_b_ref, wdn_b_ref, wout_b_ref, modl_ref, b_scr, decay_scr, qw_scr):
    wup_b_ref[...] = wup_ref[...].astype(BF16)
    wdn_b_ref[...] = wdn_ref[...].astype(BF16)
    wout_b_ref[...] = wout_ref[...].astype(BF16)
    modl_ref[...] = _dot(_silu(cc_ref[...]).astype(BF16), wmod_ref[...].astype(BF16)) + bmod_ref[...]
    c = RET_CHUNK
    nc = SEQ // c
    dk = RET_QK_DIM
    dv = RET_V_DIM
    heads = range(RET_HEADS_PER_STEP)
    lg_f = [lg_ref[0, pl.program_id(0) * RET_HEADS_PER_STEP + hh] for hh in heads]
    lg_b = [lg_ref[1, pl.program_id(0) * RET_HEADS_PER_STEP + hh] for hh in heads]

    @pl.when(pl.program_id(1) == 0)
    def _():
        ii = lax.broadcasted_iota(jnp.int32, (c, c), 0).astype(F32)
        jj = lax.broadcasted_iota(jnp.int32, (c, c), 1).astype(F32)
        dif = ii - jj
        pos_q = lax.broadcasted_iota(jnp.int32, (c, dk), 0).astype(F32)
        for hh in heads:
            decay_scr[hh] = jnp.where(dif >= 0.0, jnp.exp(lg_f[hh] * jnp.maximum(dif, 0.0)),
                                      jnp.exp(lg_b[hh] * jnp.maximum(-dif, 0.0)))
            qw_scr[hh, 0] = jnp.exp(lg_f[hh] * (pos_q + 1.0))
            qw_scr[hh, 1] = jnp.exp(lg_b[hh] * (c - pos_q))

    pos_r = lax.broadcasted_iota(jnp.int32, (1, c), 1).astype(F32)
    ones = jnp.ones((1, dv), F32)
    kw_f = [jnp.exp(lg_f[hh] * (c - 1.0 - pos_r)) for hh in heads]
    kw_b = [jnp.exp(lg_b[hh] * pos_r) for hh in heads]
    dec_f = [jnp.exp(lg_f[hh] * float(c) * ones) for hh in heads]
    dec_b = [jnp.exp(lg_b[hh] * float(c) * ones) for hh in heads]

    def kt_chunk(hh, n):
        return kt_ref[0, hh * dk:(hh + 1) * dk, n * c:(n + 1) * c]

    def v_chunk(hh, n):
        return v_ref[0, n * c:(n + 1) * c, hh * dv:(hh + 1) * dv]

    state = [s_ref[0, hh, dk:2 * dk, :] for hh in heads]
    for hh in heads:
        b_scr[hh, nc - 1] = state[hh]
    for n in range(nc - 1, 0, -1):
        for hh in heads:
            ktn = (kt_chunk(hh, n).astype(F32) * kw_b[hh]).astype(BF16)
            state[hh] = state[hh] * dec_b[hh] + _dot(ktn, v_chunk(hh, n))
            b_scr[hh, n - 1] = state[hh]

    state = [s_ref[0, hh, 0:dk, :] for hh in heads]
    for n in range(nc):
        for hh in heads:
            qn = q_ref[0, hh, n * c:(n + 1) * c, :]
            ktn = kt_chunk(hh, n)
            vn = v_chunk(hh, n)
            scores = (_dot(qn, ktn) * decay_scr[hh]).astype(BF16)
            qf = qn.astype(F32)
            o = _dot(scores, vn)
            o = o + _dot((qf * qw_scr[hh, 0]).astype(BF16), state[hh].astype(BF16))
            o = o + _dot((qf * qw_scr[hh, 1]).astype(BF16), b_scr[hh, n].astype(BF16))
            state[hh] = state[hh] * dec_f[hh] + _dot((ktn.astype(F32) * kw_f[hh]).astype(BF16), vn)
            o = o * lax.rsqrt(jnp.mean(o * o, axis=-1, keepdims=True) + EPS)
            gate = g_ref[0, n * c:(n + 1) * c, hh * dv:(hh + 1) * dv].astype(F32)
            o_ref[0, n * c:(n + 1) * c, hh * dv:(hh + 1) * dv] = (_silu(gate) * o).astype(BF16)


def _ret_call(lg, q, kt, v, g, s, w_up, w_dn, w_out, cc, w_mod, b_mod):
    dk, dv = RET_QK_DIM, RET_V_DIM
    hp = RET_HEADS_PER_STEP
    assert RET_HEADS == hp

    def row_slab(w):
        return pl.BlockSpec((w.shape[0] // BATCH, w.shape[1]), lambda h, b: (b, 0))

    up_slab, dn_slab, out_slab = row_slab(w_up), row_slab(w_dn), row_slab(w_out)
    mcols = MOD_LATE_COLS // BATCH
    mod_first = MOD_EARLY_COLS // mcols
    mod_slab = pl.BlockSpec((MOD_ROWS, mcols), lambda h, b: (0, b))
    return pl.pallas_call(
        _ret_kernel,
        grid=(RET_HEADS // hp, BATCH),
        in_specs=[pl.BlockSpec(memory_space=pltpu.SMEM),
                  pl.BlockSpec((1, hp, SEQ, dk), lambda h, b: (b, h, 0, 0)),
                  pl.BlockSpec((1, hp * dk, SEQ), lambda h, b: (b, h, 0)),
                  pl.BlockSpec((1, SEQ, hp * dv), lambda h, b: (b, 0, h)),
                  pl.BlockSpec((1, SEQ, hp * dv), lambda h, b: (b, 0, h)),
                  pl.BlockSpec((1, hp, 2 * dk, dv), lambda h, b: (b, h, 0, 0)),
                  up_slab, dn_slab, out_slab,
                  _resident((MOD_ROWS, D_MODEL)),
                  pl.BlockSpec((D_MODEL, mcols), lambda h, b: (0, mod_first + b)),
                  pl.BlockSpec((1, mcols), lambda h, b: (0, mod_first + b))],
        out_specs=[pl.BlockSpec((1, SEQ, hp * dv), lambda h, b: (b, 0, h)), up_slab, dn_slab, out_slab, mod_slab],
        out_shape=[jax.ShapeDtypeStruct((BATCH, SEQ, RET_WIDTH), BF16),
                   jax.ShapeDtypeStruct(w_up.shape, BF16),
                   jax.ShapeDtypeStruct(w_dn.shape, BF16),
                   jax.ShapeDtypeStruct(w_out.shape, BF16),
                   jax.ShapeDtypeStruct((MOD_ROWS, MOD_LATE_COLS), F32)],
        scratch_shapes=[pltpu.VMEM((hp, SEQ // RET_CHUNK, dk, dv), F32),
                        pltpu.VMEM((hp, RET_CHUNK, RET_CHUNK), F32),
                        pltpu.VMEM((hp, 2, RET_CHUNK, dk), F32)],
        compiler_params=pltpu.CompilerParams(vmem_limit_bytes=VMEM_LIMIT),
        name="ret",
    )(lg, q, kt, v, g, s, w_up, w_dn, w_out, cc, w_mod, b_mod)


def _ffn_kernel(x_ref, xp_ref, xn_ref, yh_ref, yhp_ref, yhn_ref, yr_ref, yrp_ref, yrn_ref,
                g1_ref, sh_ref, sc_ref, g2_ref, n2_ref, nf_ref,
                woh_ref, wor_ref, wup_ref, cw_ref, cb_ref, wdn_ref,
                o_ref, hb_scr, x1_scr, av_scr, ag_scr, act_scr):
    t = TOK_TILE
    i = pl.program_id(1)
    nt = pl.num_programs(1)
    th = t // 2
    tile_halves = (slice(0, th), slice(th, t))

    def mixed(xr, yh, yr, rows=slice(None)):
        return xr[0, rows, :] + g1_ref[0] * (_dot(yh[0, rows, :], woh_ref[...]) + _dot(yr[0, rows, :], wor_ref[...]))

    def hidden(x1):
        return _norm_mod(x1, n2_ref[...], sh_ref[0], sc_ref[0])

    half = (t + 2 * HALO) // 2
    up_halves = (slice(0, half), slice(half, 2 * half))

    def up(slot, cblk, rows):
        c0 = cblk * FFN_COLS
        hb = hb_scr[rows, :]
        av_scr[slot, rows, :] = _dot(hb, wup_ref[:, c0:c0 + FFN_COLS])
        ag_scr[slot, rows, :] = _dot(hb, wup_ref[:, D_FF + c0:D_FF + c0 + FFN_COLS])

    for rows in tile_halves:
        x1_scr[rows, :] = mixed(x_ref, yh_ref, yr_ref, rows)
    hp = hidden(mixed(xp_ref, yhp_ref, yrp_ref))
    hn = hidden(mixed(xn_ref, yhn_ref, yrn_ref))
    hb_scr[0:HALO, :] = jnp.where(i > 0, hp, 0.0).astype(BF16)
    hb_scr[HALO + t:2 * HALO + t, :] = jnp.where(i < nt - 1, hn, 0.0).astype(BF16)
    hb_scr[HALO:HALO + th, :] = hidden(x1_scr[tile_halves[0], :]).astype(BF16)
    up(0, 0, up_halves[0])
    hb_scr[HALO + th:HALO + t, :] = hidden(x1_scr[tile_halves[1], :]).astype(BF16)
    up(0, 0, up_halves[1])

    def conv(scr, slot, col):
        w = cw_ref[:, col:col + FFN_COLS]
        return (scr[slot, HALO - 1:HALO - 1 + t, :] * w[0:1] + scr[slot, HALO:HALO + t, :] * w[1:2]
                + scr[slot, HALO + 1:HALO + 1 + t, :] * w[2:3] + cb_ref[:, col:col + FFN_COLS])

    n_blk = D_FF // FFN_COLS
    for cblk in range(n_blk):
        c0 = cblk * FFN_COLS
        slot = cblk % 2
        if cblk + 1 < n_blk:
            for rows in up_halves:
                up(1 - slot, cblk + 1, rows)
        act_scr[:, c0:c0 + FFN_COLS] = (
            _silu(conv(ag_scr, slot, D_FF + c0)) * conv(av_scr, slot, c0)).astype(BF16)

    down = [_dot(act_scr[rows, :], wdn_ref[...]) for rows in tile_halves]
    for rows, ffn in zip(tile_halves, down):
        x2 = x1_scr[rows, :] + g2_ref[0] * ffn
        o_ref[0, rows, :] = x2 * lax.rsqrt(jnp.mean(x2 * x2, axis=-1, keepdims=True) + EPS) * nf_ref[...]


def _ffn_call(x, y_hy, y_ret, mod3, norm2, norm_f, w_oh, w_or, w_up, conv_w, conv_b, w_dn):
    t = TOK_TILE
    r = t // HALO
    last = SEQ // HALO - 1

    def main(width):
        return pl.BlockSpec((1, t, width), lambda b, i: (b, i, 0))

    def prev(width):
        return pl.BlockSpec((1, HALO, width), lambda b, i: (b, jnp.maximum(i * r - 1, 0), 0))

    def nxt(width):
        return pl.BlockSpec((1, HALO, width), lambda b, i: (b, jnp.minimum((i + 1) * r, last), 0))

    def modrow(k):
        return pl.BlockSpec((1, 1, D_MODEL), lambda b, i: (b, 0, k))

    return pl.pallas_call(
        _ffn_kernel,
        grid=(BATCH, SEQ // t),
        in_specs=[main(D_MODEL), prev(D_MODEL), nxt(D_MODEL),
                  main(HY_WIDTH), prev(HY_WIDTH), nxt(HY_WIDTH),
                  main(RET_WIDTH), prev(RET_WIDTH), nxt(RET_WIDTH),
                  modrow(0), modrow(1), modrow(2), modrow(3),
                  _resident((1, D_MODEL)), _resident((1, D_MODEL)),
                  _resident_rows(0, HY_WIDTH, D_MODEL), _resident_rows(HY_WIDTH, RET_WIDTH, D_MODEL),
                  _resident((D_MODEL, 2 * D_FF)),
                  _resident((3, 2 * D_FF)), _resident((1, 2 * D_FF)),
                  _resident((D_FF, D_MODEL))],
        out_specs=pl.BlockSpec((1, t, D_MODEL), lambda b, i: (b, i, 0)),
        out_shape=jax.ShapeDtypeStruct((BATCH, SEQ, D_MODEL), F32),
        scratch_shapes=[pltpu.VMEM((t + 2 * HALO, D_MODEL), BF16),
                        pltpu.VMEM((t, D_MODEL), F32),
                        pltpu.VMEM((2, t + 2 * HALO, FFN_COLS), F32),
                        pltpu.VMEM((2, t + 2 * HALO, FFN_COLS), F32),
                        pltpu.VMEM((t, D_FF), BF16)],
        compiler_params=pltpu.CompilerParams(vmem_limit_bytes=VMEM_LIMIT),
        name="ffn",
    )(x, x, x, y_hy, y_hy, y_hy, y_ret, y_ret, y_ret, mod3, mod3, mod3, mod3,
      norm2, norm_f, w_oh, w_or, w_up, conv_w, conv_b, w_dn)


def kernel(x, c, ctx, c_ctx, w_mod, b_mod, norm1, w_in, hy_conv_w, hy_conv_b, hy_w1, hy_b1, hy_f1,
           hy_w2, hy_b2, hy_f2, hy_w3, hy_bias, ret_logit_f, ret_logit_b, w_out, norm2,
           ffn_w_up, ffn_conv_w, ffn_conv_b, ffn_w_down, norm_f):
    layer = 0
    rope = tuple(jnp.asarray(a) for a in _rope_tables())
    zt, t_row, absdelta = (jnp.asarray(a) for a in _filter_features())
    fwd_np, inv_np = _dft_matrices()
    fwd_u = jnp.asarray(fwd_np[:, :CONV_BLOCK]).astype(BF16)
    inv = jnp.asarray(inv_np).astype(BF16)

    w_in_b = w_in[layer]
    w_kt = w_in_b[:, K_OFF:V_OFF].T
    row = lambda a: a.reshape(1, -1)
    col = lambda a: a.reshape(-1, 1)
    w1p = jnp.pad(hy_w1[layer], ((0, HY_FILTER_WIDTH - hy_w1.shape[1]), (0, 0)))
    lg = jnp.stack([jax.nn.log_sigmoid(ret_logit_f[layer].astype(F32)),
                    jax.nn.log_sigmoid(ret_logit_b[layer].astype(F32))])

    cc = jnp.concatenate([c, c_ctx[None, :], jnp.zeros((MOD_ROWS - BATCH - 1, D_MODEL), F32)], axis=0)
    mod3 = _mod_call(cc, w_mod[layer], row(b_mod[layer])).reshape(MOD_ROWS, 1, MOD_EARLY_COLS)
    norm1_r = row(norm1[layer])

    s_ctx = _ctx_call(lg, ctx, mod3, norm1_r, w_kt, w_in_b)
    zhy, q, kt, v, g = _inproj_call(x, mod3, norm1_r, w_in_b, w_kt, rope)

    k_two_sided = _filter_mlp_call(zt, t_row, w1p.T, col(hy_b1[layer]), col(hy_f1[layer]), hy_w2[layer].T,
                                   col(hy_b2[layer]), col(hy_f2[layer]), hy_w3[layer].T, absdelta)
    g_all = _filter_dft_call(k_two_sided, fwd_u)
    conv_w, conv_b = hy_conv_w[layer], row(hy_conv_b[layer])
    y1 = _hyena_call(zhy, 0, zhy, 1, conv_w, conv_b, hy_bias[layer][0:1], g_all, 0, fwd_u, inv, True)
    y_hy = _hyena_call(y1, 0, zhy, 2, conv_w, conv_b, hy_bias[layer][1:2], g_all, 1, fwd_u, inv, False)

    y_ret, w_up, w_dn, w_out_b, mod_late = _ret_call(lg, q, kt, v, g, s_ctx, ffn_w_up[layer], ffn_w_down[layer],
                                                     w_out[layer], cc, w_mod[layer], row(b_mod[layer]))
    mod_late3 = mod_late.reshape(MOD_ROWS, 1, MOD_LATE_COLS)

    return _ffn_call(x, y_hy, y_ret, mod_late3, row(norm2[layer]), row(norm_f), w_out_b, w_out_b, w_up,
                     ffn_conv_w[layer], row(ffn_conv_b[layer]), w_dn)
```

```python
import functools
import math

import numpy as np
import jax
import jax.numpy as jnp
from jax import lax
from jax.experimental import pallas as pl
from jax.experimental.pallas import tpu as pltpu

F32 = jnp.float32
BF16 = jnp.bfloat16

D_MODEL = 1024
BATCH = 8
SEQ = 2048
CTX_LEN = 256
GRID_W = 64
HY_WIDTH = 512
HY_ORDER = 2
HY_EMB_BANDS = 16
HY_FILTER_WIDTH = 64
HY_FAST_DECAY = 0.3
HY_SLOW_DECAY = 1.5
HY_TARGET = 1e-2
RET_WIDTH = 512
RET_HEADS = 4
RET_QK_DIM = 64
RET_V_DIM = 128
ROPE_BASE = 10000.0
D_FF = 2816
EPS = 1e-6
HY_COLS = (HY_ORDER + 1) * HY_WIDTH
Q_OFF = HY_COLS
K_OFF = Q_OFF + RET_HEADS * RET_QK_DIM
V_OFF = K_OFF + RET_HEADS * RET_QK_DIM
G_OFF = V_OFF + RET_WIDTH
K_SCALE = RET_QK_DIM ** -0.5

MOD_ROWS = 16
MOD_EARLY_COLS = 2 * D_MODEL
MOD_LATE_COLS = 4 * D_MODEL
TOK_TILE = 512
IN_TILE = 1024
IN_PIECE = 256
SUBLANES = 8
LANES = 128
HALO = 16
CONV_BLOCK = 512
N_CONV_BLOCKS = SEQ // CONV_BLOCK
FREQ_ROWS = 16
FREQ_LANES = 128
DFT_ROWS = 512
CTX_PER_STEP = 4
RET_CHUNK = 256
RET_HEADS_PER_STEP = 4
FFN_COLS = 256
VMEM_LIMIT = 56 * 1024 * 1024
HYENA_VMEM_LIMIT = 62 * 1024 * 1024

_NT = (((1,), (1,)), ((), ()))


def _dot(a, b):
    return jnp.dot(a, b, preferred_element_type=F32)


def _dot_hi(a, b):
    return jnp.dot(a, b, preferred_element_type=F32, precision=lax.Precision.HIGHEST)


def _silu(x):
    return x * (1.0 / (1.0 + jnp.exp(-x)))


def _norm_mod(x, gain, shift, scale):
    y = x * lax.rsqrt(jnp.mean(x * x, axis=-1, keepdims=True) + EPS)
    return (y * gain) * (1.0 + scale) + shift


def _resident(shape):
    nd = len(shape)
    return pl.BlockSpec(shape, lambda *_: (0,) * nd, pipeline_mode=pl.Buffered(1))


def _resident_cols(rows, col0, width):
    assert col0 % width == 0
    return pl.BlockSpec((rows, width), lambda *_: (0, col0 // width), pipeline_mode=pl.Buffered(1))


def _mod_rows(row_block, k):
    return pl.BlockSpec((SUBLANES, D_MODEL), lambda *_: (row_block, k), pipeline_mode=pl.Buffered(1))


def _resident_rows(row0, height, cols):
    assert row0 % height == 0
    return pl.BlockSpec((height, cols), lambda *_: (row0 // height, 0), pipeline_mode=pl.Buffered(1))


@functools.lru_cache(maxsize=None)
def _rope_tables():
    pos = np.arange(SEQ)
    row = (pos // GRID_W).astype(np.float64)
    col = (pos % GRID_W).astype(np.float64)
    quarter = RET_QK_DIM // 4
    inv_freq = ROPE_BASE ** (-np.arange(quarter, dtype=np.float64) / quarter)
    ang = np.concatenate([row[:, None] * inv_freq, col[:, None] * inv_freq], axis=-1)
    cos, sin = np.cos(ang), np.sin(ang)
    cos_h = np.concatenate([cos, cos], axis=-1)
    sin_h = np.concatenate([-sin, sin], axis=-1)
    cos_t = np.tile(cos_h, (1, RET_HEADS))
    sin_t = np.tile(sin_h, (1, RET_HEADS))
    return (cos_t.astype(np.float32), sin_t.astype(np.float32),
            np.ascontiguousarray((cos_t * K_SCALE).T).astype(np.float32),
            np.ascontiguousarray((sin_t * K_SCALE).T).astype(np.float32))


@functools.lru_cache(maxsize=None)
def _filter_features():
    lag = np.abs(np.arange(2 * SEQ) - SEQ).astype(np.float64)
    t = lag / (SEQ - 1)
    bands = np.linspace(1e-4, HY_EMB_BANDS - 1, HY_EMB_BANDS)
    ang = 2.0 * math.pi * lag[:, None] * bands[None, :] / SEQ
    z = np.concatenate([t[:, None], np.cos(ang), -np.sin(ang)], axis=-1)
    zp = np.zeros((2 * SEQ, HY_FILTER_WIDTH), np.float64)
    zp[:, :z.shape[1]] = z
    max_decay = math.log(HY_TARGET) / HY_FAST_DECAY
    min_decay = math.log(HY_TARGET) / HY_SLOW_DECAY
    absdelta = np.abs(np.linspace(min_decay, max_decay, HY_WIDTH))[:, None]
    return (np.ascontiguousarray(zp.T).astype(np.float32), t[None, :].astype(np.float32),
            absdelta.astype(np.float32))


@functools.lru_cache(maxsize=None)
def _dft_matrices():
    p = CONV_BLOCK
    n = 2 * p
    f = np.arange(p, dtype=np.float64)[:, None] + 0.5
    t = np.arange(n, dtype=np.float64)[None, :]
    theta = 2.0 * math.pi * f * t / n
    fwd = np.concatenate([np.cos(theta), -np.sin(theta)], axis=0)
    th_out = theta[:, p:].T
    inv = np.concatenate([np.cos(th_out), -np.sin(th_out)], axis=1) / p
    return fwd.astype(np.float32), inv.astype(np.float32)


def _mod_kernel(c_ref, w_ref, b_ref, o_ref):
    s = _silu(c_ref[...]).astype(BF16)
    o_ref[...] = _dot(s, w_ref[...].astype(BF16)) + b_ref[...]


def _mod_call(cc, w_mod, b_mod):
    ncol = MOD_EARLY_COLS
    blk = ncol // 2
    return pl.pallas_call(
        _mod_kernel,
        grid=(ncol // blk,),
        in_specs=[pl.BlockSpec((MOD_ROWS, D_MODEL), lambda j: (0, 0)),
                  pl.BlockSpec((D_MODEL, blk), lambda j: (0, j)),
                  pl.BlockSpec((1, blk), lambda j: (0, j))],
        out_specs=pl.BlockSpec((MOD_ROWS, blk), lambda j: (0, j)),
        out_shape=jax.ShapeDtypeStruct((MOD_ROWS, ncol), F32),
        compiler_params=pltpu.CompilerParams(vmem_limit_bytes=VMEM_LIMIT),
        name="mod",
    )(cc, w_mod, b_mod)


def _ctx_kernel(lg_ref, x_ref, sh_ref, sc_ref, n1_ref, wkt_ref, wv_ref, s_ref):
    pos = lax.broadcasted_iota(jnp.int32, (1, CTX_LEN), 1).astype(F32)
    w_f = [jnp.exp(lg_ref[0, hh] * (CTX_LEN - 1.0 - pos)) for hh in range(RET_HEADS)]
    w_b = [jnp.exp(lg_ref[1, hh] * pos) for hh in range(RET_HEADS)]
    wkt, wv = wkt_ref[...].astype(BF16), wv_ref[...].astype(BF16)
    for bb in range(CTX_PER_STEP):
        h = _norm_mod(x_ref[bb], n1_ref[...], sh_ref[0:1, :], sc_ref[0:1, :]).astype(BF16)
        kt = lax.dot_general(wkt, h, _NT, preferred_element_type=F32) * K_SCALE
        v = _dot(h, wv)
        for hh in range(RET_HEADS):
            kth = kt[hh * RET_QK_DIM:(hh + 1) * RET_QK_DIM, :]
            vh = v[:, hh * RET_V_DIM:(hh + 1) * RET_V_DIM].astype(BF16)
            s_ref[bb, hh, 0:RET_QK_DIM, :] = _dot((kth * w_f[hh]).astype(BF16), vh)
            s_ref[bb, hh, RET_QK_DIM:2 * RET_QK_DIM, :] = _dot((kth * w_b[hh]).astype(BF16), vh)


def _ctx_call(lg, ctx, mod3, norm1, w_kt, w_v):
    return pl.pallas_call(
        _ctx_kernel,
        grid=(BATCH // CTX_PER_STEP,),
        in_specs=[pl.BlockSpec(memory_space=pltpu.SMEM),
                  pl.BlockSpec((CTX_PER_STEP, CTX_LEN, D_MODEL), lambda b: (b, 0, 0)),
                  _mod_rows(BATCH // SUBLANES, 0),
                  _mod_rows(BATCH // SUBLANES, 1),
                  _resident((1, D_MODEL)),
                  _resident((RET_HEADS * RET_QK_DIM, D_MODEL)),
                  _resident_cols(D_MODEL, V_OFF, RET_WIDTH)],
        out_specs=pl.BlockSpec((CTX_PER_STEP, RET_HEADS, 2 * RET_QK_DIM, RET_V_DIM), lambda b: (b, 0, 0, 0)),
        out_shape=jax.ShapeDtypeStruct((BATCH, RET_HEADS, 2 * RET_QK_DIM, RET_V_DIM), F32),
        compiler_params=pltpu.CompilerParams(vmem_limit_bytes=VMEM_LIMIT),
        name="ctx",
    )(lg, ctx, mod3, mod3, norm1, w_kt, w_v)


def _swap_halves(x, axis):
    n = x.shape[axis]
    half = RET_QK_DIM // 2
    idx = lax.broadcasted_iota(jnp.int32, x.shape, axis)
    first = (idx & (RET_QK_DIM - 1)) < half
    return jnp.where(first, pltpu.roll(x, n - half, axis), pltpu.roll(x, half, axis))


def _inproj_kernel(x_ref, sh_ref, sc_ref, n1_ref, why_ref, wq_ref, wv_ref, wg_ref, wkt_ref,
                   cq_ref, sq_ref, ck_ref, sk_ref,
                   zhy_ref, q_ref, kt_ref, v_ref, g_ref):
    pieces = [slice(r, r + IN_PIECE) for r in range(0, IN_TILE, IN_PIECE)]
    why, wq, wv, wg, wkt = (r[...].astype(BF16) for r in (why_ref, wq_ref, wv_ref, wg_ref, wkt_ref))
    batch_row = pl.ds(pl.program_id(0), 1)
    shift, scale = sh_ref[batch_row, :], sc_ref[batch_row, :]
    hbs = [_norm_mod(x_ref[0, rows, :], n1_ref[...], shift, scale).astype(BF16) for rows in pieces]
    for rows, hb in zip(pieces, hbs):
        zhy_ref[0, rows, :] = _dot(hb, why).astype(BF16)
        v_ref[0, rows, :] = _dot(hb, wv).astype(BF16)
        g_ref[0, rows, :] = _dot(hb, wg).astype(BF16)
        q = _dot(hb, wq)
        q = q * cq_ref[rows, :] + _swap_halves(q, 1) * sq_ref[rows, :]
        for hh in range(RET_HEADS):
            q_ref[0, hh, rows, :] = q[:, hh * RET_QK_DIM:(hh + 1) * RET_QK_DIM].astype(BF16)
        kt = lax.dot_general(wkt, hb, _NT, preferred_element_type=F32)
        kt = kt * ck_ref[:, rows] + _swap_halves(kt, 0) * sk_ref[:, rows]
        kt_ref[0, :, rows] = kt.astype(BF16)


def _inproj_call(x, mod3, norm1, w_in, w_kt, rope):
    cq, sq, ck, sk = rope
    t = IN_TILE
    qk = RET_HEADS * RET_QK_DIM
    return pl.pallas_call(
        _inproj_kernel,
        grid=(BATCH, SEQ // t),
        in_specs=[pl.BlockSpec((1, t, D_MODEL), lambda b, i: (b, i, 0)),
                  _mod_rows(0, 0),
                  _mod_rows(0, 1),
                  _resident((1, D_MODEL)),
                  _resident_cols(D_MODEL, 0, HY_COLS),
                  _resident_cols(D_MODEL, Q_OFF, qk),
                  _resident_cols(D_MODEL, V_OFF, RET_WIDTH),
                  _resident_cols(D_MODEL, G_OFF, RET_WIDTH),
                  _resident((qk, D_MODEL)),
                  pl.BlockSpec((t, qk), lambda b, i: (i, 0)),
                  pl.BlockSpec((t, qk), lambda b, i: (i, 0)),
                  pl.BlockSpec((qk, t), lambda b, i: (0, i)),
                  pl.BlockSpec((qk, t), lambda b, i: (0, i))],
        out_specs=[pl.BlockSpec((1, t, HY_COLS), lambda b, i: (b, i, 0)),
                   pl.BlockSpec((1, RET_HEADS, t, RET_QK_DIM), lambda b, i: (b, 0, i, 0)),
                   pl.BlockSpec((1, qk, t), lambda b, i: (b, 0, i)),
                   pl.BlockSpec((1, t, RET_WIDTH), lambda b, i: (b, i, 0)),
                   pl.BlockSpec((1, t, RET_WIDTH), lambda b, i: (b, i, 0))],
        out_shape=[jax.ShapeDtypeStruct((BATCH, SEQ, HY_COLS), BF16),
                   jax.ShapeDtypeStruct((BATCH, RET_HEADS, SEQ, RET_QK_DIM), BF16),
                   jax.ShapeDtypeStruct((BATCH, qk, SEQ), BF16),
                   jax.ShapeDtypeStruct((BATCH, SEQ, RET_WIDTH), BF16),
                   jax.ShapeDtypeStruct((BATCH, SEQ, RET_WIDTH), BF16)],
        compiler_params=pltpu.CompilerParams(vmem_limit_bytes=VMEM_LIMIT),
        name="inproj",
    )(x, mod3, mod3, norm1, w_in, w_in, w_in, w_in, w_kt, cq, sq, ck, sk)


def _filter_mlp_kernel(zt_ref, t_ref, w1t_ref, b1_ref, f1_ref, w2t_ref, b2_ref, f2_ref, w3t_ref, adel_ref,
                       kt_ref):
    hid = jnp.sin(f1_ref[...] * (_dot_hi(w1t_ref[...], zt_ref[...]) + b1_ref[...]))
    hid = jnp.sin(f2_ref[...] * (_dot_hi(w2t_ref[...], hid) + b2_ref[...])).astype(BF16)
    c = HY_WIDTH
    for half in range(2):
        lags = slice(half * SEQ, (half + 1) * SEQ)
        window = jnp.exp(-adel_ref[...] * t_ref[:, lags])
        direction = 1 - half
        for o in range(HY_ORDER):
            r0 = (direction * HY_ORDER + o) * c
            w3 = w3t_ref[r0:r0 + c, :].astype(BF16)
            kt_ref[o, :, lags] = (_dot(w3, hid[:, lags]) * window).astype(BF16)


def _filter_mlp_call(zt, t_row, w1t, b1, f1, w2t, b2, f2, w3t, absdelta):
    fw = HY_FILTER_WIDTH
    n_out = 2 * HY_ORDER * HY_WIDTH
    return pl.pallas_call(
        _filter_mlp_kernel,
        grid=(1,),
        in_specs=[_resident((fw, 2 * SEQ)), _resident((1, 2 * SEQ)),
                  _resident((fw, fw)), _resident((fw, 1)), _resident((fw, 1)),
                  _resident((fw, fw)), _resident((fw, 1)), _resident((fw, 1)),
                  _resident((n_out, fw)), _resident((HY_WIDTH, 1))],
        out_specs=pl.BlockSpec((HY_ORDER, HY_WIDTH, 2 * SEQ), lambda i: (0, 0, 0)),
        out_shape=jax.ShapeDtypeStruct((HY_ORDER, HY_WIDTH, 2 * SEQ), BF16),
        compiler_params=pltpu.CompilerParams(vmem_limit_bytes=VMEM_LIMIT),
        name="filter_mlp",
    )(zt, t_row, w1t, b1, f1, w2t, b2, f2, w3t, absdelta)


_FILTER_PLANES = (
    {0: 1},
    {-1: 1, 0: -1},
    {1: 1, 0: -1},
    {-2: 1, 0: -1},
    {-3: 1, -1: -1, -2: -1, 0: 1},
    {-1: 1, 1: -1, -2: -1, 0: 1},
    {2: 1, 0: -1},
    {1: 1, -1: -1, 2: -1, 0: 1},
    {3: 1, 1: -1, 2: -1, 0: 1},
)
DFT_BLOCKS_PER_STEP = 4
PLANES_PER_STEP = 3


def _filter_dft_kernel(kt_ref, fwd_ref, h_ref, t_scr):
    p = CONV_BLOCK
    n_dft_steps = 2 * N_CONV_BLOCKS // DFT_BLOCKS_PER_STEP
    step = pl.program_id(1)

    for s in range(n_dft_steps):
        @pl.when(step == s)
        def _(s=s):
            for k in range(DFT_BLOCKS_PER_STEP):
                t_scr[s * DFT_BLOCKS_PER_STEP + k] = lax.dot_general(
                    fwd_ref[...], kt_ref[0, :, k * p:(k + 1) * p], _NT, preferred_element_type=F32)
            if s == 0:
                for k in range(PLANES_PER_STEP):
                    h_ref[0, k] = t_scr[0]

    row = lax.broadcasted_iota(jnp.int32, (p, 1), 0)
    sign = (1 - 2 * (row & 1)).astype(F32)
    re, im = slice(0, p), slice(p, 2 * p)

    def plane(k, coefs):
        def comb(shift, rows):
            acc = None
            for d, c in coefs.items():
                term = t_scr[d + N_CONV_BLOCKS - 1 + shift, rows, :]
                if acc is None:
                    acc = term
                else:
                    acc = acc + term if c > 0 else acc - term
            return acc

        h_ref[0, k, re, :] = comb(0, re) + sign * comb(1, im)
        h_ref[0, k, im, :] = comb(0, im) - sign * comb(1, re)

    for s in range(len(_FILTER_PLANES) // PLANES_PER_STEP):
        @pl.when(step == n_dft_steps + s)
        def _(s=s):
            for k in range(PLANES_PER_STEP):
                plane(k, _FILTER_PLANES[s * PLANES_PER_STEP + k])


def _filter_dft_call(kt, fwd_u):
    p = CONV_BLOCK
    n_blk = 2 * N_CONV_BLOCKS
    n_planes = len(_FILTER_PLANES)
    n_dft_steps = n_blk // DFT_BLOCKS_PER_STEP
    return pl.pallas_call(
        _filter_dft_kernel,
        grid=(HY_ORDER, n_dft_steps + n_planes // PLANES_PER_STEP),
        in_specs=[pl.BlockSpec((1, HY_WIDTH, DFT_BLOCKS_PER_STEP * p),
                               lambda o, s: (o, 0, jnp.minimum(s, n_dft_steps - 1))),
                  _resident((2 * p, p))],
        out_specs=pl.BlockSpec((1, PLANES_PER_STEP, 2 * p, HY_WIDTH),
                               lambda o, s: (o, jnp.maximum(s - n_dft_steps, 0), 0, 0)),
        out_shape=jax.ShapeDtypeStruct((HY_ORDER, n_planes, 2 * p, HY_WIDTH), F32),
        scratch_shapes=[pltpu.VMEM((n_blk, 2 * p, HY_WIDTH), F32)],
        compiler_params=pltpu.CompilerParams(vmem_limit_bytes=VMEM_LIMIT),
        name="filter_dft",
    )(kt, fwd_u)


def _conv3_rows(ref, j, n_blocks, rows, w, b):
    main = ref[0, j * rows:(j + 1) * rows, :].astype(F32)
    cols = main.shape[1]
    zeros = jnp.zeros((HALO, cols), F32)
    prev = ref[0, j * rows - HALO:j * rows, :].astype(F32) if j > 0 else zeros
    nxt = ref[0, (j + 1) * rows:(j + 1) * rows + HALO, :].astype(F32) if j < n_blocks - 1 else zeros
    ext = jnp.concatenate([prev, main, nxt], axis=0)
    n = rows + 2 * HALO
    before = pltpu.roll(ext, 1, 0)[HALO:HALO + rows]
    after = pltpu.roll(ext, n - 1, 0)[HALO:HALO + rows]
    return before * w[0:1] + main * w[1:2] + after * w[2:3] + b


def _hyena_kernel(u_ref, zg_ref, cwu_ref, cbu_ref, cwg_ref, cbg_ref, skip_ref, h_ref, fwd_ref, inv_ref,
                  o_ref, uf_scr, ub_scr, y_scr, *, conv_u):
    p = CONV_BLOCK
    nb = N_CONV_BLOCKS

    def prepare(j):
        if conv_u:
            ub_scr[j * p:(j + 1) * p, :] = _conv3_rows(u_ref, j, nb, p, cwu_ref[...], cbu_ref[...]).astype(BF16)

    def u_block(j):
        return ub_scr[j * p:(j + 1) * p, :] if conv_u else u_ref[0, j * p:(j + 1) * p, :]

    prepare(0)
    for j in range(nb):
        if j + 1 < nb:
            prepare(j + 1)
        for r in range(2 * p // DFT_ROWS):
            rows = slice(r * DFT_ROWS, (r + 1) * DFT_ROWS)
            uf_scr[j, rows, :] = _dot(fwd_ref[rows, :], u_block(j))

    def cadd(a, b):
        return a[0] + b[0], a[1] + b[1]

    def cmul(m, x):
        return m[0] * x[0] - m[1] * x[1], m[0] * x[1] + m[1] * x[0]

    def toeplitz2(k0, x0, x1, re, im, ln):
        m0, mu, ml = ((h_ref[0, k0 + t, re, ln], h_ref[0, k0 + t, im, ln]) for t in range(3))
        p1 = cmul(m0, cadd(x0, x1))
        return cadd(p1, cmul(mu, x1)), cadd(p1, cmul(ml, x0))

    assert nb == 4
    for r in range(p // FREQ_ROWS):
        re = slice(r * FREQ_ROWS, (r + 1) * FREQ_ROWS)
        im = slice(p + r * FREQ_ROWS, p + (r + 1) * FREQ_ROWS)
        for cb in range(HY_WIDTH // FREQ_LANES):
            ln = slice(cb * FREQ_LANES, (cb + 1) * FREQ_LANES)
            u = [(uf_scr[j, re, ln], uf_scr[j, im, ln]) for j in range(nb)]
            d0, d1 = toeplitz2(0, cadd(u[0], u[2]), cadd(u[1], u[3]), re, im, ln)
            b0, b1 = toeplitz2(3, u[2], u[3], re, im, ln)
            c0, c1 = toeplitz2(6, u[0], u[1], re, im, ln)
            for i, yi in enumerate((cadd(d0, b0), cadd(d1, b1), cadd(d0, c0), cadd(d1, c1))):
                y_scr[i, re, ln] = yi[0].astype(BF16)
                y_scr[i, im, ln] = yi[1].astype(BF16)

    for i in range(nb):
        gate = _conv3_rows(zg_ref, i, nb, p, cwg_ref[...], cbg_ref[...])
        y = _dot(inv_ref[...], y_scr[i])
        o_ref[0, i * p:(i + 1) * p, :] = (
            gate * (y + u_block(i).astype(F32) * skip_ref[...])).astype(BF16)


def _hyena_call(u, u_col, zhy, gate_col, conv_w, conv_b, skip, g_all, order, fwd_u, inv, conv_u):
    p = CONV_BLOCK
    c = HY_WIDTH
    ucol = u_col if conv_u else 0
    cwu = conv_w[:, ucol * c:(ucol + 1) * c]
    cbu = conv_b[:, ucol * c:(ucol + 1) * c]
    cwg = conv_w[:, gate_col * c:(gate_col + 1) * c]
    cbg = conv_b[:, gate_col * c:(gate_col + 1) * c]
    return pl.pallas_call(
        functools.partial(_hyena_kernel, conv_u=conv_u),
        grid=(BATCH,),
        in_specs=[pl.BlockSpec((1, SEQ, c), lambda b: (b, 0, u_col)),
                  pl.BlockSpec((1, SEQ, c), lambda b: (b, 0, gate_col)),
                  _resident((3, c)), _resident((1, c)), _resident((3, c)), _resident((1, c)),
                  _resident((1, c)),
                  pl.BlockSpec((1, len(_FILTER_PLANES), 2 * p, c), lambda b: (order, 0, 0, 0),
                               pipeline_mode=pl.Buffered(1)),
                  _resident((2 * p, p)),
                  _resident((p, 2 * p))],
        out_specs=pl.BlockSpec((1, SEQ, c), lambda b: (b, 0, 0)),
        out_shape=jax.ShapeDtypeStruct((BATCH, SEQ, c), BF16),
        scratch_shapes=[pltpu.VMEM((N_CONV_BLOCKS, 2 * p, c), F32),
                        pltpu.VMEM((SEQ, c) if conv_u else (SUBLANES * 2, LANES), BF16),
                        pltpu.VMEM((N_CONV_BLOCKS, 2 * p, c), BF16)],
        compiler_params=pltpu.CompilerParams(vmem_limit_bytes=HYENA_VMEM_LIMIT),
        name="hyena%d" % order,
    )(u, zhy, cwu, cbu, cwg, cbg, skip, g_all, fwd_u, inv)


def _ret_kernel(lg_ref, q_ref, kt_ref, v_ref, g_ref, s_ref, wup_ref, wdn_ref, wout_ref, cc_ref, wmod_ref, bmod_ref,
                o_ref, wup_b_ref, wdn_b_ref, wout_b_ref, modl_ref, b_scr, decay_scr, qw_scr):
    wup_b_ref[...] = wup_ref[...].astype(BF16)
    wdn_b_ref[...] = wdn_ref[...].astype(BF16)
    wout_b_ref[...] = wout_ref[...].astype(BF16)
    modl_ref[...] = _dot(_silu(cc_ref[...]).astype(BF16), wmod_ref[...].astype(BF16)) + bmod_ref[...]
    c = RET_CHUNK
    nc = SEQ // c
    dk = RET_QK_DIM
    dv = RET_V_DIM
    heads = range(RET_HEADS_PER_STEP)
    lg_f = [lg_ref[0, pl.program_id(0) * RET_HEADS_PER_STEP + hh] for hh in heads]
    lg_b = [lg_ref[1, pl.program_id(0) * RET_HEADS_PER_STEP + hh] for hh in heads]

    @pl.when(pl.program_id(1) == 0)
    def _():
        ii = lax.broadcasted_iota(jnp.int32, (c, c), 0).astype(F32)
        jj = lax.broadcasted_iota(jnp.int32, (c, c), 1).astype(F32)
        dif = ii - jj
        pos_q = lax.broadcasted_iota(jnp.int32, (c, dk), 0).astype(F32)
        for hh in heads:
            decay_scr[hh] = jnp.where(dif >= 0.0, jnp.exp(lg_f[hh] * jnp.maximum(dif, 0.0)),
                                      jnp.exp(lg_b[hh] * jnp.maximum(-dif, 0.0)))
            qw_scr[hh, 0] = jnp.exp(lg_f[hh] * (pos_q + 1.0))
            qw_scr[hh, 1] = jnp.exp(lg_b[hh] * (c - pos_q))

    pos_r = lax.broadcasted_iota(jnp.int32, (1, c), 1).astype(F32)
    ones = jnp.ones((1, dv), F32)
    kw_f = [jnp.exp(lg_f[hh] * (c - 1.0 - pos_r)) for hh in heads]
    kw_b = [jnp.exp(lg_b[hh] * pos_r) for hh in heads]
    dec_f = [jnp.exp(lg_f[hh] * float(c) * ones) for hh in heads]
    dec_b = [jnp.exp(lg_b[hh] * float(c) * ones) for hh in heads]

    def kt_chunk(hh, n):
        return kt_ref[0, hh * dk:(hh + 1) * dk, n * c:(n + 1) * c]

    def v_chunk(hh, n):
        return v_ref[0, n * c:(n + 1) * c, hh * dv:(hh + 1) * dv]

    state = [s_ref[0, hh, dk:2 * dk, :] for hh in heads]
    for hh in heads:
        b_scr[hh, nc - 1] = state[hh]
    for n in range(nc - 1, 0, -1):
        for hh in heads:
            ktn = (kt_chunk(hh, n).astype(F32) * kw_b[hh]).astype(BF16)
            state[hh] = state[hh] * dec_b[hh] + _dot(ktn, v_chunk(hh, n))
            b_scr[hh, n - 1] = state[hh]

    state = [s_ref[0, hh, 0:dk, :] for hh in heads]
    for n in range(nc):
        for hh in heads:
            qn = q_ref[0, hh, n * c:(n + 1) * c, :]
            ktn = kt_chunk(hh, n)
            vn = v_chunk(hh, n)
            scores = (_dot(qn, ktn) * decay_scr[hh]).astype(BF16)
            qf = qn.astype(F32)
            o = _dot(scores, vn)
            o = o + _dot((qf * qw_scr[hh, 0]).astype(BF16), state[hh].astype(BF16))
            o = o + _dot((qf * qw_scr[hh, 1]).astype(BF16), b_scr[hh, n].astype(BF16))
            state[hh] = state[hh] * dec_f[hh] + _dot((ktn.astype(F32) * kw_f[hh]).astype(BF16), vn)
            o = o * lax.rsqrt(jnp.mean(o * o, axis=-1, keepdims=True) + EPS)
            gate = g_ref[0, n * c:(n + 1) * c, hh * dv:(hh + 1) * dv].astype(F32)
            o_ref[0, n * c:(n + 1) * c, hh * dv:(hh + 1) * dv] = (_silu(gate) * o).astype(BF16)


def _ret_call(lg, q, kt, v, g, s, w_up, w_dn, w_out, cc, w_mod, b_mod):
    dk, dv = RET_QK_DIM, RET_V_DIM
    hp = RET_HEADS_PER_STEP
    assert RET_HEADS == hp

    def row_slab(w):
        return pl.BlockSpec((w.shape[0] // BATCH, w.shape[1]), lambda h, b: (b, 0))

    up_slab, dn_slab, out_slab = row_slab(w_up), row_slab(w_dn), row_slab(w_out)
    mcols = MOD_LATE_COLS // BATCH
    mod_first = MOD_EARLY_COLS // mcols
    mod_slab = pl.BlockSpec((MOD_ROWS, mcols), lambda h, b: (0, b))
    return pl.pallas_call(
        _ret_kernel,
        grid=(RET_HEADS // hp, BATCH),
        in_specs=[pl.BlockSpec(memory_space=pltpu.SMEM),
                  pl.BlockSpec((1, hp, SEQ, dk), lambda h, b: (b, h, 0, 0)),
                  pl.BlockSpec((1, hp * dk, SEQ), lambda h, b: (b, h, 0)),
                  pl.BlockSpec((1, SEQ, hp * dv), lambda h, b: (b, 0, h)),
                  pl.BlockSpec((1, SEQ, hp * dv), lambda h, b: (b, 0, h)),
                  pl.BlockSpec((1, hp, 2 * dk, dv), lambda h, b: (b, h, 0, 0)),
                  up_slab, dn_slab, out_slab,
                  _resident((MOD_ROWS, D_MODEL)),
                  pl.BlockSpec((D_MODEL, mcols), lambda h, b: (0, mod_first + b)),
                  pl.BlockSpec((1, mcols), lambda h, b: (0, mod_first + b))],
        out_specs=[pl.BlockSpec((1, SEQ, hp * dv), lambda h, b: (b, 0, h)), up_slab, dn_slab, out_slab, mod_slab],
        out_shape=[jax.ShapeDtypeStruct((BATCH, SEQ, RET_WIDTH), BF16),
                   jax.ShapeDtypeStruct(w_up.shape, BF16),
                   jax.ShapeDtypeStruct(w_dn.shape, BF16),
                   jax.ShapeDtypeStruct(w_out.shape, BF16),
                   jax.ShapeDtypeStruct((MOD_ROWS, MOD_LATE_COLS), F32)],
        scratch_shapes=[pltpu.VMEM((hp, SEQ // RET_CHUNK, dk, dv), F32),
                        pltpu.VMEM((hp, RET_CHUNK, RET_CHUNK), F32),
                        pltpu.VMEM((hp, 2, RET_CHUNK, dk), F32)],
        compiler_params=pltpu.CompilerParams(vmem_limit_bytes=VMEM_LIMIT),
        name="ret",
    )(lg, q, kt, v, g, s, w_up, w_dn, w_out, cc, w_mod, b_mod)


def _ffn_kernel(x_ref, xp_ref, xn_ref, yh_ref, yhp_ref, yhn_ref, yr_ref, yrp_ref, yrn_ref,
                g1_ref, sh_ref, sc_ref, g2_ref, n2_ref, nf_ref,
                woh_ref, wor_ref, wup_ref, cw_ref, cb_ref, wdn_ref,
                o_ref, hb_scr, x1_scr, av_scr, ag_scr, act_scr):
    t = TOK_TILE
    i = pl.program_id(1)
    nt = pl.num_programs(1)
    th = t // 2
    tile_halves = (slice(0, th), slice(th, t))
    batch_row = pl.ds(pl.program_id(0), 1)
    gate1, shift2, scale2, gate2 = (r[batch_row, :] for r in (g1_ref, sh_ref, sc_ref, g2_ref))

    def mixed(xr, yh, yr, rows=slice(None)):
        return xr[0, rows, :] + gate1 * (_dot(yh[0, rows, :], woh_ref[...]) + _dot(yr[0, rows, :], wor_ref[...]))

    def hidden(x1):
        return _norm_mod(x1, n2_ref[...], shift2, scale2)

    half = (t + 2 * HALO) // 2
    up_halves = (slice(0, half), slice(half, 2 * half))

    def up(slot, cblk, rows):
        c0 = cblk * FFN_COLS
        hb = hb_scr[rows, :]
        av_scr[slot, rows, :] = _dot(hb, wup_ref[:, c0:c0 + FFN_COLS])
        ag_scr[slot, rows, :] = _dot(hb, wup_ref[:, D_FF + c0:D_FF + c0 + FFN_COLS])

    for rows in tile_halves:
        x1_scr[rows, :] = mixed(x_ref, yh_ref, yr_ref, rows)
    hp = hidden(mixed(xp_ref, yhp_ref, yrp_ref))
    hn = hidden(mixed(xn_ref, yhn_ref, yrn_ref))
    hb_scr[0:HALO, :] = jnp.where(i > 0, hp, 0.0).astype(BF16)
    hb_scr[HALO + t:2 * HALO + t, :] = jnp.where(i < nt - 1, hn, 0.0).astype(BF16)
    hb_scr[HALO:HALO + th, :] = hidden(x1_scr[tile_halves[0], :]).astype(BF16)
    up(0, 0, up_halves[0])
    hb_scr[HALO + th:HALO + t, :] = hidden(x1_scr[tile_halves[1], :]).astype(BF16)
    up(0, 0, up_halves[1])

    def conv(scr, slot, col):
        w = cw_ref[:, col:col + FFN_COLS]
        return (scr[slot, HALO - 1:HALO - 1 + t, :] * w[0:1] + scr[slot, HALO:HALO + t, :] * w[1:2]
                + scr[slot, HALO + 1:HALO + 1 + t, :] * w[2:3] + cb_ref[:, col:col + FFN_COLS])

    n_blk = D_FF // FFN_COLS
    for cblk in range(n_blk):
        c0 = cblk * FFN_COLS
        slot = cblk % 2
        if cblk + 1 < n_blk:
            for rows in up_halves:
                up(1 - slot, cblk + 1, rows)
        act_scr[:, c0:c0 + FFN_COLS] = (
            _silu(conv(ag_scr, slot, D_FF + c0)) * conv(av_scr, slot, c0)).astype(BF16)

    down = [_dot(act_scr[rows, :], wdn_ref[...]) for rows in tile_halves]
    for rows, ffn in zip(tile_halves, down):
        x2 = x1_scr[rows, :] + gate2 * ffn
        o_ref[0, rows, :] = x2 * lax.rsqrt(jnp.mean(x2 * x2, axis=-1, keepdims=True) + EPS) * nf_ref[...]


def _ffn_call(x, y_hy, y_ret, mod3, norm2, norm_f, w_oh, w_or, w_up, conv_w, conv_b, w_dn):
    t = TOK_TILE
    r = t // HALO
    last = SEQ // HALO - 1

    def main(width):
        return pl.BlockSpec((1, t, width), lambda b, i: (b, i, 0))

    def prev(width):
        return pl.BlockSpec((1, HALO, width), lambda b, i: (b, jnp.maximum(i * r - 1, 0), 0))

    def nxt(width):
        return pl.BlockSpec((1, HALO, width), lambda b, i: (b, jnp.minimum((i + 1) * r, last), 0))

    def modrow(k):
        return _mod_rows(0, k)

    return pl.pallas_call(
        _ffn_kernel,
        grid=(BATCH, SEQ // t),
        in_specs=[main(D_MODEL), prev(D_MODEL), nxt(D_MODEL),
                  main(HY_WIDTH), prev(HY_WIDTH), nxt(HY_WIDTH),
                  main(RET_WIDTH), prev(RET_WIDTH), nxt(RET_WIDTH),
                  modrow(0), modrow(1), modrow(2), modrow(3),
                  _resident((1, D_MODEL)), _resident((1, D_MODEL)),
                  _resident_rows(0, HY_WIDTH, D_MODEL), _resident_rows(HY_WIDTH, RET_WIDTH, D_MODEL),
                  _resident((D_MODEL, 2 * D_FF)),
                  _resident((3, 2 * D_FF)), _resident((1, 2 * D_FF)),
                  _resident((D_FF, D_MODEL))],
        out_specs=pl.BlockSpec((1, t, D_MODEL), lambda b, i: (b, i, 0)),
        out_shape=jax.ShapeDtypeStruct((BATCH, SEQ, D_MODEL), F32),
        scratch_shapes=[pltpu.VMEM((t + 2 * HALO, D_MODEL), BF16),
                        pltpu.VMEM((t, D_MODEL), F32),
                        pltpu.VMEM((2, t + 2 * HALO, FFN_COLS), F32),
                        pltpu.VMEM((2, t + 2 * HALO, FFN_COLS), F32),
                        pltpu.VMEM((t, D_FF), BF16)],
        compiler_params=pltpu.CompilerParams(vmem_limit_bytes=VMEM_LIMIT),
        name="ffn",
    )(x, x, x, y_hy, y_hy, y_hy, y_ret, y_ret, y_ret, mod3, mod3, mod3, mod3,
      norm2, norm_f, w_oh, w_or, w_up, conv_w, conv_b, w_dn)


def kernel(x, c, ctx, c_ctx, w_mod, b_mod, norm1, w_in, hy_conv_w, hy_conv_b, hy_w1, hy_b1, hy_f1,
           hy_w2, hy_b2, hy_f2, hy_w3, hy_bias, ret_logit_f, ret_logit_b, w_out, norm2,
           ffn_w_up, ffn_conv_w, ffn_conv_b, ffn_w_down, norm_f):
    layer = 0
    rope = tuple(jnp.asarray(a) for a in _rope_tables())
    zt, t_row, absdelta = (jnp.asarray(a) for a in _filter_features())
    fwd_np, inv_np = _dft_matrices()
    fwd_u = jnp.asarray(fwd_np[:, :CONV_BLOCK]).astype(BF16)
    inv = jnp.asarray(inv_np).astype(BF16)

    w_in_b = w_in[layer]
    w_kt = w_in_b[:, K_OFF:V_OFF].T
    row = lambda a: a.reshape(1, -1)
    col = lambda a: a.reshape(-1, 1)
    w1p = jnp.pad(hy_w1[layer], ((0, HY_FILTER_WIDTH - hy_w1.shape[1]), (0, 0)))
    lg = jnp.stack([jax.nn.log_sigmoid(ret_logit_f[layer].astype(F32)),
                    jax.nn.log_sigmoid(ret_logit_b[layer].astype(F32))])

    cc = jnp.concatenate([c, c_ctx[None, :], jnp.zeros((MOD_ROWS - BATCH - 1, D_MODEL), F32)], axis=0)
    assert BATCH <= SUBLANES
    mod3 = _mod_call(cc, w_mod[layer], row(b_mod[layer]))
    norm1_r = row(norm1[layer])

    s_ctx = _ctx_call(lg, ctx, mod3, norm1_r, w_kt, w_in_b)
    zhy, q, kt, v, g = _inproj_call(x, mod3, norm1_r, w_in_b, w_kt, rope)

    k_two_sided = _filter_mlp_call(zt, t_row, w1p.T, col(hy_b1[layer]), col(hy_f1[layer]), hy_w2[layer].T,
                                   col(hy_b2[layer]), col(hy_f2[layer]), hy_w3[layer].T, absdelta)
    g_all = _filter_dft_call(k_two_sided, fwd_u)
    conv_w, conv_b = hy_conv_w[layer], row(hy_conv_b[layer])
    y1 = _hyena_call(zhy, 0, zhy, 1, conv_w, conv_b, hy_bias[layer][0:1], g_all, 0, fwd_u, inv, True)
    y_hy = _hyena_call(y1, 0, zhy, 2, conv_w, conv_b, hy_bias[layer][1:2], g_all, 1, fwd_u, inv, False)

    y_ret, w_up, w_dn, w_out_b, mod_late = _ret_call(lg, q, kt, v, g, s_ctx, ffn_w_up[layer], ffn_w_down[layer],
                                                     w_out[layer], cc, w_mod[layer], row(b_mod[layer]))

    return _ffn_call(x, y_hy, y_ret, mod_late, row(norm2[layer]), row(norm_f), w_out_b, w_out_b, w_up,
                     ffn_conv_w[layer], row(ffn_conv_b[layer]), w_dn)
```

```python
import functools
import math

import numpy as np
import jax
import jax.numpy as jnp
from jax import lax
from jax.experimental import pallas as pl
from jax.experimental.pallas import tpu as pltpu

F32 = jnp.float32
BF16 = jnp.bfloat16

D_MODEL = 1024
BATCH = 8
SEQ = 2048
CTX_LEN = 256
GRID_W = 64
HY_WIDTH = 512
HY_ORDER = 2
HY_EMB_BANDS = 16
HY_FILTER_WIDTH = 64
HY_FAST_DECAY = 0.3
HY_SLOW_DECAY = 1.5
HY_TARGET = 1e-2
RET_WIDTH = 512
RET_HEADS = 4
RET_QK_DIM = 64
RET_V_DIM = 128
ROPE_BASE = 10000.0
D_FF = 2816
EPS = 1e-6
HY_COLS = (HY_ORDER + 1) * HY_WIDTH
Q_OFF = HY_COLS
K_OFF = Q_OFF + RET_HEADS * RET_QK_DIM
V_OFF = K_OFF + RET_HEADS * RET_QK_DIM
G_OFF = V_OFF + RET_WIDTH
K_SCALE = RET_QK_DIM ** -0.5

MOD_ROWS = 16
MOD_EARLY_COLS = 2 * D_MODEL
MOD_LATE_COLS = 4 * D_MODEL
TOK_TILE = 512
IN_TILE = 1024
IN_PIECE = 256
SUBLANES = 8
LANES = 128
HALO = 16
CONV_BLOCK = 512
N_CONV_BLOCKS = SEQ // CONV_BLOCK
FREQ_ROWS = 16
FREQ_LANES = 128
DFT_ROWS = 512
CTX_PER_STEP = 4
RET_CHUNK = 256
RET_HEADS_PER_STEP = 4
FFN_COLS = 256
VMEM_LIMIT = 56 * 1024 * 1024
HYENA_VMEM_LIMIT = 62 * 1024 * 1024

_NT = (((1,), (1,)), ((), ()))


def _dot(a, b):
    return jnp.dot(a, b, preferred_element_type=F32)


def _dot_hi(a, b):
    return jnp.dot(a, b, preferred_element_type=F32, precision=lax.Precision.HIGHEST)


def _silu(x):
    return x * (1.0 / (1.0 + jnp.exp(-x)))


def _norm_mod(x, gain, shift, scale):
    y = x * lax.rsqrt(jnp.mean(x * x, axis=-1, keepdims=True) + EPS)
    return (y * gain) * (1.0 + scale) + shift


def _resident(shape):
    nd = len(shape)
    return pl.BlockSpec(shape, lambda *_: (0,) * nd, pipeline_mode=pl.Buffered(1))


def _resident_cols(rows, col0, width):
    assert col0 % width == 0
    return pl.BlockSpec((rows, width), lambda *_: (0, col0 // width), pipeline_mode=pl.Buffered(1))


def _mod_rows(row_block, k):
    return pl.BlockSpec((SUBLANES, D_MODEL), lambda *_: (row_block, k), pipeline_mode=pl.Buffered(1))


def _resident_rows(row0, height, cols):
    assert row0 % height == 0
    return pl.BlockSpec((height, cols), lambda *_: (row0 // height, 0), pipeline_mode=pl.Buffered(1))


@functools.lru_cache(maxsize=None)
def _rope_tables():
    pos = np.arange(SEQ)
    row = (pos // GRID_W).astype(np.float64)
    col = (pos % GRID_W).astype(np.float64)
    quarter = RET_QK_DIM // 4
    inv_freq = ROPE_BASE ** (-np.arange(quarter, dtype=np.float64) / quarter)
    ang = np.concatenate([row[:, None] * inv_freq, col[:, None] * inv_freq], axis=-1)
    cos, sin = np.cos(ang), np.sin(ang)
    cos_h = np.concatenate([cos, cos], axis=-1)
    sin_h = np.concatenate([-sin, sin], axis=-1)
    cos_t = np.tile(cos_h, (1, RET_HEADS))
    sin_t = np.tile(sin_h, (1, RET_HEADS))
    return (cos_t.astype(np.float32), sin_t.astype(np.float32),
            np.ascontiguousarray((cos_t * K_SCALE).T).astype(np.float32),
            np.ascontiguousarray((sin_t * K_SCALE).T).astype(np.float32))


@functools.lru_cache(maxsize=None)
def _filter_features():
    lag = np.abs(np.arange(2 * SEQ) - SEQ).astype(np.float64)
    t = lag / (SEQ - 1)
    bands = np.linspace(1e-4, HY_EMB_BANDS - 1, HY_EMB_BANDS)
    ang = 2.0 * math.pi * lag[:, None] * bands[None, :] / SEQ
    z = np.concatenate([t[:, None], np.cos(ang), -np.sin(ang)], axis=-1)
    zp = np.zeros((2 * SEQ, HY_FILTER_WIDTH), np.float64)
    zp[:, :z.shape[1]] = z
    max_decay = math.log(HY_TARGET) / HY_FAST_DECAY
    min_decay = math.log(HY_TARGET) / HY_SLOW_DECAY
    absdelta = np.abs(np.linspace(min_decay, max_decay, HY_WIDTH))[:, None]
    return (np.ascontiguousarray(zp.T).astype(np.float32), t[None, :].astype(np.float32),
            absdelta.astype(np.float32))


@functools.lru_cache(maxsize=None)
def _dft_matrices():
    p = CONV_BLOCK
    n = 2 * p
    f = np.arange(p, dtype=np.float64)[:, None] + 0.5
    t = np.arange(n, dtype=np.float64)[None, :]
    theta = 2.0 * math.pi * f * t / n
    fwd = np.concatenate([np.cos(theta), -np.sin(theta)], axis=0)
    th_out = theta[:, p:].T
    inv = np.concatenate([np.cos(th_out), -np.sin(th_out)], axis=1) / p
    return fwd.astype(np.float32), inv.astype(np.float32)


def _mod_kernel(c_ref, w_ref, b_ref, o_ref):
    s = _silu(c_ref[...]).astype(BF16)
    o_ref[...] = _dot(s, w_ref[...].astype(BF16)) + b_ref[...]


def _mod_call(cc, w_mod, b_mod):
    ncol = MOD_EARLY_COLS
    blk = ncol // 2
    return pl.pallas_call(
        _mod_kernel,
        grid=(ncol // blk,),
        in_specs=[pl.BlockSpec((MOD_ROWS, D_MODEL), lambda j: (0, 0)),
                  pl.BlockSpec((D_MODEL, blk), lambda j: (0, j)),
                  pl.BlockSpec((1, blk), lambda j: (0, j))],
        out_specs=pl.BlockSpec((MOD_ROWS, blk), lambda j: (0, j)),
        out_shape=jax.ShapeDtypeStruct((MOD_ROWS, ncol), F32),
        compiler_params=pltpu.CompilerParams(vmem_limit_bytes=VMEM_LIMIT),
        name="mod",
    )(cc, w_mod, b_mod)


def _ctx_kernel(lg_ref, x_ref, sh_ref, sc_ref, n1_ref, wkt_ref, wv_ref, s_ref):
    pos = lax.broadcasted_iota(jnp.int32, (1, CTX_LEN), 1).astype(F32)
    w_f = [jnp.exp(lg_ref[0, hh] * (CTX_LEN - 1.0 - pos)) for hh in range(RET_HEADS)]
    w_b = [jnp.exp(lg_ref[1, hh] * pos) for hh in range(RET_HEADS)]
    wkt, wv = wkt_ref[...].astype(BF16), wv_ref[...].astype(BF16)
    for bb in range(CTX_PER_STEP):
        h = _norm_mod(x_ref[bb], n1_ref[...], sh_ref[0:1, :], sc_ref[0:1, :]).astype(BF16)
        kt = lax.dot_general(wkt, h, _NT, preferred_element_type=F32) * K_SCALE
        v = _dot(h, wv)
        for hh in range(RET_HEADS):
            kth = kt[hh * RET_QK_DIM:(hh + 1) * RET_QK_DIM, :]
            vh = v[:, hh * RET_V_DIM:(hh + 1) * RET_V_DIM].astype(BF16)
            s_ref[bb, hh, 0:RET_QK_DIM, :] = _dot((kth * w_f[hh]).astype(BF16), vh)
            s_ref[bb, hh, RET_QK_DIM:2 * RET_QK_DIM, :] = _dot((kth * w_b[hh]).astype(BF16), vh)


def _ctx_call(lg, ctx, mod3, norm1, w_kt, w_v):
    return pl.pallas_call(
        _ctx_kernel,
        grid=(BATCH // CTX_PER_STEP,),
        in_specs=[pl.BlockSpec(memory_space=pltpu.SMEM),
                  pl.BlockSpec((CTX_PER_STEP, CTX_LEN, D_MODEL), lambda b: (b, 0, 0)),
                  _mod_rows(BATCH // SUBLANES, 0),
                  _mod_rows(BATCH // SUBLANES, 1),
                  _resident((1, D_MODEL)),
                  _resident((RET_HEADS * RET_QK_DIM, D_MODEL)),
                  _resident_cols(D_MODEL, V_OFF, RET_WIDTH)],
        out_specs=pl.BlockSpec((CTX_PER_STEP, RET_HEADS, 2 * RET_QK_DIM, RET_V_DIM), lambda b: (b, 0, 0, 0)),
        out_shape=jax.ShapeDtypeStruct((BATCH, RET_HEADS, 2 * RET_QK_DIM, RET_V_DIM), F32),
        compiler_params=pltpu.CompilerParams(vmem_limit_bytes=VMEM_LIMIT),
        name="ctx",
    )(lg, ctx, mod3, mod3, norm1, w_kt, w_v)


def _swap_halves(x, axis):
    n = x.shape[axis]
    half = RET_QK_DIM // 2
    idx = lax.broadcasted_iota(jnp.int32, x.shape, axis)
    first = (idx & (RET_QK_DIM - 1)) < half
    return jnp.where(first, pltpu.roll(x, n - half, axis), pltpu.roll(x, half, axis))


def _inproj_kernel(x_ref, sh_ref, sc_ref, n1_ref, why_ref, wq_ref, wv_ref, wg_ref, wkt_ref,
                   cq_ref, sq_ref, ck_ref, sk_ref,
                   zhy_ref, q_ref, kt_ref, v_ref, g_ref):
    pieces = [slice(r, r + IN_PIECE) for r in range(0, IN_TILE, IN_PIECE)]
    why, wq, wv, wg, wkt = (r[...].astype(BF16) for r in (why_ref, wq_ref, wv_ref, wg_ref, wkt_ref))
    batch_row = pl.ds(pl.program_id(0), 1)
    shift, scale = sh_ref[batch_row, :], sc_ref[batch_row, :]
    hbs = [_norm_mod(x_ref[0, rows, :], n1_ref[...], shift, scale).astype(BF16) for rows in pieces]
    for rows, hb in zip(pieces, hbs):
        zhy_ref[0, rows, :] = _dot(hb, why).astype(BF16)
        v_ref[0, rows, :] = _dot(hb, wv).astype(BF16)
        g_ref[0, rows, :] = _dot(hb, wg).astype(BF16)
        q = _dot(hb, wq)
        q = q * cq_ref[rows, :] + _swap_halves(q, 1) * sq_ref[rows, :]
        for hh in range(RET_HEADS):
            q_ref[0, hh, rows, :] = q[:, hh * RET_QK_DIM:(hh + 1) * RET_QK_DIM].astype(BF16)
        kt = lax.dot_general(wkt, hb, _NT, preferred_element_type=F32)
        kt = kt * ck_ref[:, rows] + _swap_halves(kt, 0) * sk_ref[:, rows]
        kt_ref[0, :, rows] = kt.astype(BF16)


def _inproj_call(x, mod3, norm1, w_in, w_kt, rope):
    cq, sq, ck, sk = rope
    t = IN_TILE
    qk = RET_HEADS * RET_QK_DIM
    return pl.pallas_call(
        _inproj_kernel,
        grid=(BATCH, SEQ // t),
        in_specs=[pl.BlockSpec((1, t, D_MODEL), lambda b, i: (b, i, 0)),
                  _mod_rows(0, 0),
                  _mod_rows(0, 1),
                  _resident((1, D_MODEL)),
                  _resident_cols(D_MODEL, 0, HY_COLS),
                  _resident_cols(D_MODEL, Q_OFF, qk),
                  _resident_cols(D_MODEL, V_OFF, RET_WIDTH),
                  _resident_cols(D_MODEL, G_OFF, RET_WIDTH),
                  _resident((qk, D_MODEL)),
                  pl.BlockSpec((t, qk), lambda b, i: (i, 0)),
                  pl.BlockSpec((t, qk), lambda b, i: (i, 0)),
                  pl.BlockSpec((qk, t), lambda b, i: (0, i)),
                  pl.BlockSpec((qk, t), lambda b, i: (0, i))],
        out_specs=[pl.BlockSpec((1, t, HY_COLS), lambda b, i: (b, i, 0)),
                   pl.BlockSpec((1, RET_HEADS, t, RET_QK_DIM), lambda b, i: (b, 0, i, 0)),
                   pl.BlockSpec((1, qk, t), lambda b, i: (b, 0, i)),
                   pl.BlockSpec((1, t, RET_WIDTH), lambda b, i: (b, i, 0)),
                   pl.BlockSpec((1, t, RET_WIDTH), lambda b, i: (b, i, 0))],
        out_shape=[jax.ShapeDtypeStruct((BATCH, SEQ, HY_COLS), BF16),
                   jax.ShapeDtypeStruct((BATCH, RET_HEADS, SEQ, RET_QK_DIM), BF16),
                   jax.ShapeDtypeStruct((BATCH, qk, SEQ), BF16),
                   jax.ShapeDtypeStruct((BATCH, SEQ, RET_WIDTH), BF16),
                   jax.ShapeDtypeStruct((BATCH, SEQ, RET_WIDTH), BF16)],
        compiler_params=pltpu.CompilerParams(vmem_limit_bytes=VMEM_LIMIT),
        name="inproj",
    )(x, mod3, mod3, norm1, w_in, w_in, w_in, w_in, w_kt, cq, sq, ck, sk)


def _filter_mlp_kernel(zt_ref, t_ref, w1t_ref, b1_ref, f1_ref, w2t_ref, b2_ref, f2_ref, w3t_ref, adel_ref,
                       kt_ref):
    hid = jnp.sin(f1_ref[...] * (_dot_hi(w1t_ref[...], zt_ref[...]) + b1_ref[...]))
    hid = jnp.sin(f2_ref[...] * (_dot_hi(w2t_ref[...], hid) + b2_ref[...])).astype(BF16)
    c = HY_WIDTH
    for half in range(2):
        lags = slice(half * SEQ, (half + 1) * SEQ)
        window = jnp.exp(-adel_ref[...] * t_ref[:, lags])
        direction = 1 - half
        for o in range(HY_ORDER):
            r0 = (direction * HY_ORDER + o) * c
            w3 = w3t_ref[r0:r0 + c, :].astype(BF16)
            kt_ref[o, :, lags] = (_dot(w3, hid[:, lags]) * window).astype(BF16)


def _filter_mlp_call(zt, t_row, w1t, b1, f1, w2t, b2, f2, w3t, absdelta):
    fw = HY_FILTER_WIDTH
    n_out = 2 * HY_ORDER * HY_WIDTH
    return pl.pallas_call(
        _filter_mlp_kernel,
        grid=(1,),
        in_specs=[_resident((fw, 2 * SEQ)), _resident((1, 2 * SEQ)),
                  _resident((fw, fw)), _resident((fw, 1)), _resident((fw, 1)),
                  _resident((fw, fw)), _resident((fw, 1)), _resident((fw, 1)),
                  _resident((n_out, fw)), _resident((HY_WIDTH, 1))],
        out_specs=pl.BlockSpec((HY_ORDER, HY_WIDTH, 2 * SEQ), lambda i: (0, 0, 0)),
        out_shape=jax.ShapeDtypeStruct((HY_ORDER, HY_WIDTH, 2 * SEQ), BF16),
        compiler_params=pltpu.CompilerParams(vmem_limit_bytes=VMEM_LIMIT),
        name="filter_mlp",
    )(zt, t_row, w1t, b1, f1, w2t, b2, f2, w3t, absdelta)


_FILTER_PLANES = (
    {0: 1},
    {-1: 1, 0: -1},
    {1: 1, 0: -1},
    {-2: 1, 0: -1},
    {-3: 1, -1: -1, -2: -1, 0: 1},
    {-1: 1, 1: -1, -2: -1, 0: 1},
    {2: 1, 0: -1},
    {1: 1, -1: -1, 2: -1, 0: 1},
    {3: 1, 1: -1, 2: -1, 0: 1},
)
DFT_BLOCKS_PER_STEP = 4
PLANES_PER_STEP = 3


def _filter_dft_kernel(kt_ref, fwd_ref, h_ref, t_scr):
    p = CONV_BLOCK
    n_dft_steps = 2 * N_CONV_BLOCKS // DFT_BLOCKS_PER_STEP
    step = pl.program_id(1)

    for s in range(n_dft_steps):
        @pl.when(step == s)
        def _(s=s):
            for k in range(DFT_BLOCKS_PER_STEP):
                t_scr[s * DFT_BLOCKS_PER_STEP + k] = lax.dot_general(
                    fwd_ref[...], kt_ref[0, :, k * p:(k + 1) * p], _NT, preferred_element_type=F32)
            if s == 0:
                for k in range(PLANES_PER_STEP):
                    h_ref[0, k] = t_scr[0].astype(BF16)

    row = lax.broadcasted_iota(jnp.int32, (p, 1), 0)
    sign = (1 - 2 * (row & 1)).astype(F32)
    re, im = slice(0, p), slice(p, 2 * p)

    def plane(k, coefs):
        def comb(shift, rows):
            acc = None
            for d, c in coefs.items():
                term = t_scr[d + N_CONV_BLOCKS - 1 + shift, rows, :]
                if acc is None:
                    acc = term
                else:
                    acc = acc + term if c > 0 else acc - term
            return acc

        h_ref[0, k, re, :] = (comb(0, re) + sign * comb(1, im)).astype(BF16)
        h_ref[0, k, im, :] = (comb(0, im) - sign * comb(1, re)).astype(BF16)

    for s in range(len(_FILTER_PLANES) // PLANES_PER_STEP):
        @pl.when(step == n_dft_steps + s)
        def _(s=s):
            for k in range(PLANES_PER_STEP):
                plane(k, _FILTER_PLANES[s * PLANES_PER_STEP + k])


def _filter_dft_call(kt, fwd_u):
    p = CONV_BLOCK
    n_blk = 2 * N_CONV_BLOCKS
    n_planes = len(_FILTER_PLANES)
    n_dft_steps = n_blk // DFT_BLOCKS_PER_STEP
    return pl.pallas_call(
        _filter_dft_kernel,
        grid=(HY_ORDER, n_dft_steps + n_planes // PLANES_PER_STEP),
        in_specs=[pl.BlockSpec((1, HY_WIDTH, DFT_BLOCKS_PER_STEP * p),
                               lambda o, s: (o, 0, jnp.minimum(s, n_dft_steps - 1))),
                  _resident((2 * p, p))],
        out_specs=pl.BlockSpec((1, PLANES_PER_STEP, 2 * p, HY_WIDTH),
                               lambda o, s: (o, jnp.maximum(s - n_dft_steps, 0), 0, 0)),
        out_shape=jax.ShapeDtypeStruct((HY_ORDER, n_planes, 2 * p, HY_WIDTH), BF16),
        scratch_shapes=[pltpu.VMEM((n_blk, 2 * p, HY_WIDTH), F32)],
        compiler_params=pltpu.CompilerParams(vmem_limit_bytes=VMEM_LIMIT),
        name="filter_dft",
    )(kt, fwd_u)


def _conv3_rows(ref, j, n_blocks, rows, w, b):
    main = ref[0, j * rows:(j + 1) * rows, :].astype(F32)
    cols = main.shape[1]
    zeros = jnp.zeros((HALO, cols), F32)
    prev = ref[0, j * rows - HALO:j * rows, :].astype(F32) if j > 0 else zeros
    nxt = ref[0, (j + 1) * rows:(j + 1) * rows + HALO, :].astype(F32) if j < n_blocks - 1 else zeros
    ext = jnp.concatenate([prev, main, nxt], axis=0)
    n = rows + 2 * HALO
    before = pltpu.roll(ext, 1, 0)[HALO:HALO + rows]
    after = pltpu.roll(ext, n - 1, 0)[HALO:HALO + rows]
    return before * w[0:1] + main * w[1:2] + after * w[2:3] + b


def _hyena_kernel(u_ref, zg_ref, cwu_ref, cbu_ref, cwg_ref, cbg_ref, skip_ref, h_ref, fwd_ref, inv_ref,
                  o_ref, uf_scr, ub_scr, y_scr, *, conv_u):
    p = CONV_BLOCK
    nb = N_CONV_BLOCKS

    def prepare(j):
        if conv_u:
            ub_scr[j * p:(j + 1) * p, :] = _conv3_rows(u_ref, j, nb, p, cwu_ref[...], cbu_ref[...]).astype(BF16)

    def u_block(j):
        return ub_scr[j * p:(j + 1) * p, :] if conv_u else u_ref[0, j * p:(j + 1) * p, :]

    prepare(0)
    for j in range(nb):
        if j + 1 < nb:
            prepare(j + 1)
        for r in range(2 * p // DFT_ROWS):
            rows = slice(r * DFT_ROWS, (r + 1) * DFT_ROWS)
            uf_scr[j, rows, :] = _dot(fwd_ref[rows, :], u_block(j))

    def cadd(a, b):
        return a[0] + b[0], a[1] + b[1]

    def cmul(m, x):
        return m[0] * x[0] - m[1] * x[1], m[0] * x[1] + m[1] * x[0]

    def toeplitz2(k0, x0, x1, re, im, ln):
        m0, mu, ml = ((h_ref[0, k0 + t, re, ln].astype(F32), h_ref[0, k0 + t, im, ln].astype(F32))
                      for t in range(3))
        p1 = cmul(m0, cadd(x0, x1))
        return cadd(p1, cmul(mu, x1)), cadd(p1, cmul(ml, x0))

    assert nb == 4
    for r in range(p // FREQ_ROWS):
        re = slice(r * FREQ_ROWS, (r + 1) * FREQ_ROWS)
        im = slice(p + r * FREQ_ROWS, p + (r + 1) * FREQ_ROWS)
        for cb in range(HY_WIDTH // FREQ_LANES):
            ln = slice(cb * FREQ_LANES, (cb + 1) * FREQ_LANES)
            u = [(uf_scr[j, re, ln], uf_scr[j, im, ln]) for j in range(nb)]
            d0, d1 = toeplitz2(0, cadd(u[0], u[2]), cadd(u[1], u[3]), re, im, ln)
            b0, b1 = toeplitz2(3, u[2], u[3], re, im, ln)
            c0, c1 = toeplitz2(6, u[0], u[1], re, im, ln)
            for i, yi in enumerate((cadd(d0, b0), cadd(d1, b1), cadd(d0, c0), cadd(d1, c1))):
                y_scr[i, re, ln] = yi[0].astype(BF16)
                y_scr[i, im, ln] = yi[1].astype(BF16)

    for i in range(nb):
        gate = _conv3_rows(zg_ref, i, nb, p, cwg_ref[...], cbg_ref[...])
        y = _dot(inv_ref[...], y_scr[i])
        o_ref[0, i * p:(i + 1) * p, :] = (
            gate * (y + u_block(i).astype(F32) * skip_ref[...])).astype(BF16)


def _hyena_call(u, u_col, zhy, gate_col, conv_w, conv_b, skip, g_all, order, fwd_u, inv, conv_u):
    p = CONV_BLOCK
    c = HY_WIDTH
    ucol = u_col if conv_u else 0
    cwu = conv_w[:, ucol * c:(ucol + 1) * c]
    cbu = conv_b[:, ucol * c:(ucol + 1) * c]
    cwg = conv_w[:, gate_col * c:(gate_col + 1) * c]
    cbg = conv_b[:, gate_col * c:(gate_col + 1) * c]
    return pl.pallas_call(
        functools.partial(_hyena_kernel, conv_u=conv_u),
        grid=(BATCH,),
        in_specs=[pl.BlockSpec((1, SEQ, c), lambda b: (b, 0, u_col)),
                  pl.BlockSpec((1, SEQ, c), lambda b: (b, 0, gate_col)),
                  _resident((3, c)), _resident((1, c)), _resident((3, c)), _resident((1, c)),
                  _resident((1, c)),
                  pl.BlockSpec((1, len(_FILTER_PLANES), 2 * p, c), lambda b: (order, 0, 0, 0),
                               pipeline_mode=pl.Buffered(1)),
                  _resident((2 * p, p)),
                  _resident((p, 2 * p))],
        out_specs=pl.BlockSpec((1, SEQ, c), lambda b: (b, 0, 0)),
        out_shape=jax.ShapeDtypeStruct((BATCH, SEQ, c), BF16),
        scratch_shapes=[pltpu.VMEM((N_CONV_BLOCKS, 2 * p, c), F32),
                        pltpu.VMEM((SEQ, c) if conv_u else (SUBLANES * 2, LANES), BF16),
                        pltpu.VMEM((N_CONV_BLOCKS, 2 * p, c), BF16)],
        compiler_params=pltpu.CompilerParams(vmem_limit_bytes=HYENA_VMEM_LIMIT),
        name="hyena%d" % order,
    )(u, zhy, cwu, cbu, cwg, cbg, skip, g_all, fwd_u, inv)


def _ret_kernel(lg_ref, q_ref, kt_ref, v_ref, g_ref, s_ref, wup_ref, wdn_ref, wout_ref, cc_ref, wmod_ref, bmod_ref,
                o_ref, wup_b_ref, wdn_b_ref, wout_b_ref, modl_ref, b_scr, decay_scr, qw_scr):
    wup_b_ref[...] = wup_ref[...].astype(BF16)
    wdn_b_ref[...] = wdn_ref[...].astype(BF16)
    wout_b_ref[...] = wout_ref[...].astype(BF16)
    modl_ref[...] = _dot(_silu(cc_ref[...]).astype(BF16), wmod_ref[...].astype(BF16)) + bmod_ref[...]
    c = RET_CHUNK
    nc = SEQ // c
    dk = RET_QK_DIM
    dv = RET_V_DIM
    heads = range(RET_HEADS_PER_STEP)
    lg_f = [lg_ref[0, pl.program_id(0) * RET_HEADS_PER_STEP + hh] for hh in heads]
    lg_b = [lg_ref[1, pl.program_id(0) * RET_HEADS_PER_STEP + hh] for hh in heads]

    @pl.when(pl.program_id(1) == 0)
    def _():
        ii = lax.broadcasted_iota(jnp.int32, (c, c), 0).astype(F32)
        jj = lax.broadcasted_iota(jnp.int32, (c, c), 1).astype(F32)
        dif = ii - jj
        pos_q = lax.broadcasted_iota(jnp.int32, (c, dk), 0).astype(F32)
        for hh in heads:
            decay_scr[hh] = jnp.where(dif >= 0.0, jnp.exp(lg_f[hh] * jnp.maximum(dif, 0.0)),
                                      jnp.exp(lg_b[hh] * jnp.maximum(-dif, 0.0)))
            qw_scr[hh, 0] = jnp.exp(lg_f[hh] * (pos_q + 1.0))
            qw_scr[hh, 1] = jnp.exp(lg_b[hh] * (c - pos_q))

    pos_r = lax.broadcasted_iota(jnp.int32, (1, c), 1).astype(F32)
    ones = jnp.ones((1, dv), F32)
    kw_f = [jnp.exp(lg_f[hh] * (c - 1.0 - pos_r)) for hh in heads]
    kw_b = [jnp.exp(lg_b[hh] * pos_r) for hh in heads]
    dec_f = [jnp.exp(lg_f[hh] * float(c) * ones) for hh in heads]
    dec_b = [jnp.exp(lg_b[hh] * float(c) * ones) for hh in heads]

    def kt_chunk(hh, n):
        return kt_ref[0, hh * dk:(hh + 1) * dk, n * c:(n + 1) * c]

    def v_chunk(hh, n):
        return v_ref[0, n * c:(n + 1) * c, hh * dv:(hh + 1) * dv]

    state = [s_ref[0, hh, dk:2 * dk, :] for hh in heads]
    for hh in heads:
        b_scr[hh, nc - 1] = state[hh]
    for n in range(nc - 1, 0, -1):
        for hh in heads:
            ktn = (kt_chunk(hh, n).astype(F32) * kw_b[hh]).astype(BF16)
            state[hh] = state[hh] * dec_b[hh] + _dot(ktn, v_chunk(hh, n))
            b_scr[hh, n - 1] = state[hh]

    state = [s_ref[0, hh, 0:dk, :] for hh in heads]
    for n in range(nc):
        for hh in heads:
            qn = q_ref[0, hh, n * c:(n + 1) * c, :]
            ktn = kt_chunk(hh, n)
            vn = v_chunk(hh, n)
            scores = (_dot(qn, ktn) * decay_scr[hh]).astype(BF16)
            qf = qn.astype(F32)
            o = _dot(scores, vn)
            o = o + _dot((qf * qw_scr[hh, 0]).astype(BF16), state[hh].astype(BF16))
            o = o + _dot((qf * qw_scr[hh, 1]).astype(BF16), b_scr[hh, n].astype(BF16))
            state[hh] = state[hh] * dec_f[hh] + _dot((ktn.astype(F32) * kw_f[hh]).astype(BF16), vn)
            o = o * lax.rsqrt(jnp.mean(o * o, axis=-1, keepdims=True) + EPS)
            gate = g_ref[0, n * c:(n + 1) * c, hh * dv:(hh + 1) * dv].astype(F32)
            o_ref[0, n * c:(n + 1) * c, hh * dv:(hh + 1) * dv] = (_silu(gate) * o).astype(BF16)


def _ret_call(lg, q, kt, v, g, s, w_up, w_dn, w_out, cc, w_mod, b_mod):
    dk, dv = RET_QK_DIM, RET_V_DIM
    hp = RET_HEADS_PER_STEP
    assert RET_HEADS == hp

    def row_slab(w):
        return pl.BlockSpec((w.shape[0] // BATCH, w.shape[1]), lambda h, b: (b, 0))

    up_slab, dn_slab, out_slab = row_slab(w_up), row_slab(w_dn), row_slab(w_out)
    mcols = MOD_LATE_COLS // BATCH
    mod_first = MOD_EARLY_COLS // mcols
    mod_slab = pl.BlockSpec((MOD_ROWS, mcols), lambda h, b: (0, b))
    return pl.pallas_call(
        _ret_kernel,
        grid=(RET_HEADS // hp, BATCH),
        in_specs=[pl.BlockSpec(memory_space=pltpu.SMEM),
                  pl.BlockSpec((1, hp, SEQ, dk), lambda h, b: (b, h, 0, 0)),
                  pl.BlockSpec((1, hp * dk, SEQ), lambda h, b: (b, h, 0)),
                  pl.BlockSpec((1, SEQ, hp * dv), lambda h, b: (b, 0, h)),
                  pl.BlockSpec((1, SEQ, hp * dv), lambda h, b: (b, 0, h)),
                  pl.BlockSpec((1, hp, 2 * dk, dv), lambda h, b: (b, h, 0, 0)),
                  up_slab, dn_slab, out_slab,
                  _resident((MOD_ROWS, D_MODEL)),
                  pl.BlockSpec((D_MODEL, mcols), lambda h, b: (0, mod_first + b)),
                  pl.BlockSpec((1, mcols), lambda h, b: (0, mod_first + b))],
        out_specs=[pl.BlockSpec((1, SEQ, hp * dv), lambda h, b: (b, 0, h)), up_slab, dn_slab, out_slab, mod_slab],
        out_shape=[jax.ShapeDtypeStruct((BATCH, SEQ, RET_WIDTH), BF16),
                   jax.ShapeDtypeStruct(w_up.shape, BF16),
                   jax.ShapeDtypeStruct(w_dn.shape, BF16),
                   jax.ShapeDtypeStruct(w_out.shape, BF16),
                   jax.ShapeDtypeStruct((MOD_ROWS, MOD_LATE_COLS), F32)],
        scratch_shapes=[pltpu.VMEM((hp, SEQ // RET_CHUNK, dk, dv), F32),
                        pltpu.VMEM((hp, RET_CHUNK, RET_CHUNK), F32),
                        pltpu.VMEM((hp, 2, RET_CHUNK, dk), F32)],
        compiler_params=pltpu.CompilerParams(vmem_limit_bytes=VMEM_LIMIT),
        name="ret",
    )(lg, q, kt, v, g, s, w_up, w_dn, w_out, cc, w_mod, b_mod)


def _ffn_kernel(x_ref, xp_ref, xn_ref, yh_ref, yhp_ref, yhn_ref, yr_ref, yrp_ref, yrn_ref,
                g1_ref, sh_ref, sc_ref, g2_ref, n2_ref, nf_ref,
                woh_ref, wor_ref, wup_ref, cw_ref, cb_ref, wdn_ref,
                o_ref, hb_scr, x1_scr, av_scr, ag_scr, act_scr):
    t = TOK_TILE
    i = pl.program_id(1)
    nt = pl.num_programs(1)
    th = t // 2
    tile_halves = (slice(0, th), slice(th, t))
    batch_row = pl.ds(pl.program_id(0), 1)
    gate1, shift2, scale2, gate2 = (r[batch_row, :] for r in (g1_ref, sh_ref, sc_ref, g2_ref))

    def mixed(xr, yh, yr, rows=slice(None)):
        return xr[0, rows, :] + gate1 * (_dot(yh[0, rows, :], woh_ref[...]) + _dot(yr[0, rows, :], wor_ref[...]))

    def hidden(x1):
        return _norm_mod(x1, n2_ref[...], shift2, scale2)

    half = (t + 2 * HALO) // 2
    up_halves = (slice(0, half), slice(half, 2 * half))

    def up(slot, cblk, rows):
        c0 = cblk * FFN_COLS
        hb = hb_scr[rows, :]
        av_scr[slot, rows, :] = _dot(hb, wup_ref[:, c0:c0 + FFN_COLS])
        ag_scr[slot, rows, :] = _dot(hb, wup_ref[:, D_FF + c0:D_FF + c0 + FFN_COLS])

    for rows in tile_halves:
        x1_scr[rows, :] = mixed(x_ref, yh_ref, yr_ref, rows)
    hp = hidden(mixed(xp_ref, yhp_ref, yrp_ref))
    hn = hidden(mixed(xn_ref, yhn_ref, yrn_ref))
    hb_scr[0:HALO, :] = jnp.where(i > 0, hp, 0.0).astype(BF16)
    hb_scr[HALO + t:2 * HALO + t, :] = jnp.where(i < nt - 1, hn, 0.0).astype(BF16)
    hb_scr[HALO:HALO + th, :] = hidden(x1_scr[tile_halves[0], :]).astype(BF16)
    up(0, 0, up_halves[0])
    hb_scr[HALO + th:HALO + t, :] = hidden(x1_scr[tile_halves[1], :]).astype(BF16)
    up(0, 0, up_halves[1])

    def conv(scr, slot, col):
        w = cw_ref[:, col:col + FFN_COLS]
        return (scr[slot, HALO - 1:HALO - 1 + t, :] * w[0:1] + scr[slot, HALO:HALO + t, :] * w[1:2]
                + scr[slot, HALO + 1:HALO + 1 + t, :] * w[2:3] + cb_ref[:, col:col + FFN_COLS])

    n_blk = D_FF // FFN_COLS
    for cblk in range(n_blk):
        c0 = cblk * FFN_COLS
        slot = cblk % 2
        if cblk + 1 < n_blk:
            for rows in up_halves:
                up(1 - slot, cblk + 1, rows)
        act_scr[:, c0:c0 + FFN_COLS] = (
            _silu(conv(ag_scr, slot, D_FF + c0)) * conv(av_scr, slot, c0)).astype(BF16)

    down = [_dot(act_scr[rows, :], wdn_ref[...]) for rows in tile_halves]
    for rows, ffn in zip(tile_halves, down):
        x2 = x1_scr[rows, :] + gate2 * ffn
        o_ref[0, rows, :] = x2 * lax.rsqrt(jnp.mean(x2 * x2, axis=-1, keepdims=True) + EPS) * nf_ref[...]


def _ffn_call(x, y_hy, y_ret, mod3, norm2, norm_f, w_oh, w_or, w_up, conv_w, conv_b, w_dn):
    t = TOK_TILE
    r = t // HALO
    last = SEQ // HALO - 1

    def main(width):
        return pl.BlockSpec((1, t, width), lambda b, i: (b, i, 0))

    def prev(width):
        return pl.BlockSpec((1, HALO, width), lambda b, i: (b, jnp.maximum(i * r - 1, 0), 0))

    def nxt(width):
        return pl.BlockSpec((1, HALO, width), lambda b, i: (b, jnp.minimum((i + 1) * r, last), 0))

    def modrow(k):
        return _mod_rows(0, k)

    return pl.pallas_call(
        _ffn_kernel,
        grid=(BATCH, SEQ // t),
        in_specs=[main(D_MODEL), prev(D_MODEL), nxt(D_MODEL),
                  main(HY_WIDTH), prev(HY_WIDTH), nxt(HY_WIDTH),
                  main(RET_WIDTH), prev(RET_WIDTH), nxt(RET_WIDTH),
                  modrow(0), modrow(1), modrow(2), modrow(3),
                  _resident((1, D_MODEL)), _resident((1, D_MODEL)),
                  _resident_rows(0, HY_WIDTH, D_MODEL), _resident_rows(HY_WIDTH, RET_WIDTH, D_MODEL),
                  _resident((D_MODEL, 2 * D_FF)),
                  _resident((3, 2 * D_FF)), _resident((1, 2 * D_FF)),
                  _resident((D_FF, D_MODEL))],
        out_specs=pl.BlockSpec((1, t, D_MODEL), lambda b, i: (b, i, 0)),
        out_shape=jax.ShapeDtypeStruct((BATCH, SEQ, D_MODEL), F32),
        scratch_shapes=[pltpu.VMEM((t + 2 * HALO, D_MODEL), BF16),
                        pltpu.VMEM((t, D_MODEL), F32),
                        pltpu.VMEM((2, t + 2 * HALO, FFN_COLS), F32),
                        pltpu.VMEM((2, t + 2 * HALO, FFN_COLS), F32),
                        pltpu.VMEM((t, D_FF), BF16)],
        compiler_params=pltpu.CompilerParams(vmem_limit_bytes=VMEM_LIMIT),
        name="ffn",
    )(x, x, x, y_hy, y_hy, y_hy, y_ret, y_ret, y_ret, mod3, mod3, mod3, mod3,
      norm2, norm_f, w_oh, w_or, w_up, conv_w, conv_b, w_dn)


def kernel(x, c, ctx, c_ctx, w_mod, b_mod, norm1, w_in, hy_conv_w, hy_conv_b, hy_w1, hy_b1, hy_f1,
           hy_w2, hy_b2, hy_f2, hy_w3, hy_bias, ret_logit_f, ret_logit_b, w_out, norm2,
           ffn_w_up, ffn_conv_w, ffn_conv_b, ffn_w_down, norm_f):
    layer = 0
    rope = tuple(jnp.asarray(a) for a in _rope_tables())
    zt, t_row, absdelta = (jnp.asarray(a) for a in _filter_features())
    fwd_np, inv_np = _dft_matrices()
    fwd_u = jnp.asarray(fwd_np[:, :CONV_BLOCK]).astype(BF16)
    inv = jnp.asarray(inv_np).astype(BF16)

    w_in_b = w_in[layer]
    w_kt = w_in_b[:, K_OFF:V_OFF].T
    row = lambda a: a.reshape(1, -1)
    col = lambda a: a.reshape(-1, 1)
    w1p = jnp.pad(hy_w1[layer], ((0, HY_FILTER_WIDTH - hy_w1.shape[1]), (0, 0)))
    lg = jnp.stack([jax.nn.log_sigmoid(ret_logit_f[layer].astype(F32)),
                    jax.nn.log_sigmoid(ret_logit_b[layer].astype(F32))])

    cc = jnp.concatenate([c, c_ctx[None, :], jnp.zeros((MOD_ROWS - BATCH - 1, D_MODEL), F32)], axis=0)
    assert BATCH <= SUBLANES
    mod3 = _mod_call(cc, w_mod[layer], row(b_mod[layer]))
    norm1_r = row(norm1[layer])

    s_ctx = _ctx_call(lg, ctx, mod3, norm1_r, w_kt, w_in_b)
    zhy, q, kt, v, g = _inproj_call(x, mod3, norm1_r, w_in_b, w_kt, rope)

    k_two_sided = _filter_mlp_call(zt, t_row, w1p.T, col(hy_b1[layer]), col(hy_f1[layer]), hy_w2[layer].T,
                                   col(hy_b2[layer]), col(hy_f2[layer]), hy_w3[layer].T, absdelta)
    g_all = _filter_dft_call(k_two_sided, fwd_u)
    conv_w, conv_b = hy_conv_w[layer], row(hy_conv_b[layer])
    y1 = _hyena_call(zhy, 0, zhy, 1, conv_w, conv_b, hy_bias[layer][0:1], g_all, 0, fwd_u, inv, True)
    y_hy = _hyena_call(y1, 0, zhy, 2, conv_w, conv_b, hy_bias[layer][1:2], g_all, 1, fwd_u, inv, False)

    y_ret, w_up, w_dn, w_out_b, mod_late = _ret_call(lg, q, kt, v, g, s_ctx, ffn_w_up[layer], ffn_w_down[layer],
                                                     w_out[layer], cc, w_mod[layer], row(b_mod[layer]))

    return _ffn_call(x, y_hy, y_ret, mod_late, row(norm2[layer]), row(norm_f), w_out_b, w_out_b, w_up,
                     ffn_conv_w[layer], row(ffn_conv_b[layer]), w_dn)
```

```python
import functools
import math

import numpy as np
import jax
import jax.numpy as jnp
from jax import lax
from jax.experimental import pallas as pl
from jax.experimental.pallas import tpu as pltpu

F32 = jnp.float32
BF16 = jnp.bfloat16

D_MODEL = 1024
BATCH = 8
SEQ = 2048
CTX_LEN = 256
GRID_W = 64
HY_WIDTH = 512
HY_ORDER = 2
HY_EMB_BANDS = 16
HY_FILTER_WIDTH = 64
HY_FAST_DECAY = 0.3
HY_SLOW_DECAY = 1.5
HY_TARGET = 1e-2
RET_WIDTH = 512
RET_HEADS = 4
RET_QK_DIM = 64
RET_V_DIM = 128
ROPE_BASE = 10000.0
D_FF = 2816
EPS = 1e-6
HY_COLS = (HY_ORDER + 1) * HY_WIDTH
Q_OFF = HY_COLS
K_OFF = Q_OFF + RET_HEADS * RET_QK_DIM
V_OFF = K_OFF + RET_HEADS * RET_QK_DIM
G_OFF = V_OFF + RET_WIDTH
K_SCALE = RET_QK_DIM ** -0.5

MOD_ROWS = 16
MOD_EARLY_COLS = 2 * D_MODEL
MOD_LATE_COLS = 4 * D_MODEL
TOK_TILE = 512
IN_TILE = 1024
IN_PIECE = 256
SUBLANES = 8
LANES = 128
HALO = 16
CONV_BLOCK = 512
N_CONV_BLOCKS = SEQ // CONV_BLOCK
FREQ_ROWS = 16
FREQ_LANES = 128
DFT_ROWS = 512
CTX_PER_STEP = 4
RET_CHUNK = 256
RET_HEADS_PER_STEP = 4
FFN_COLS = 256
VMEM_LIMIT = 56 * 1024 * 1024
HYENA_VMEM_LIMIT = 62 * 1024 * 1024

_NT = (((1,), (1,)), ((), ()))


def _dot(a, b):
    return jnp.dot(a, b, preferred_element_type=F32)


def _dot_hi(a, b):
    return jnp.dot(a, b, preferred_element_type=F32, precision=lax.Precision.HIGHEST)


def _silu(x):
    return x * (1.0 / (1.0 + jnp.exp(-x)))


def _norm_mod(x, gain, shift, scale):
    y = x * lax.rsqrt(jnp.mean(x * x, axis=-1, keepdims=True) + EPS)
    return (y * gain) * (1.0 + scale) + shift


def _resident(shape):
    nd = len(shape)
    return pl.BlockSpec(shape, lambda *_: (0,) * nd, pipeline_mode=pl.Buffered(1))


def _resident_cols(rows, col0, width):
    assert col0 % width == 0
    return pl.BlockSpec((rows, width), lambda *_: (0, col0 // width), pipeline_mode=pl.Buffered(1))


def _mod_rows(row_block, k):
    return pl.BlockSpec((SUBLANES, D_MODEL), lambda *_: (row_block, k), pipeline_mode=pl.Buffered(1))


def _resident_rows(row0, height, cols):
    assert row0 % height == 0
    return pl.BlockSpec((height, cols), lambda *_: (row0 // height, 0), pipeline_mode=pl.Buffered(1))


@functools.lru_cache(maxsize=None)
def _rope_tables():
    pos = np.arange(SEQ)
    row = (pos // GRID_W).astype(np.float64)
    col = (pos % GRID_W).astype(np.float64)
    quarter = RET_QK_DIM // 4
    inv_freq = ROPE_BASE ** (-np.arange(quarter, dtype=np.float64) / quarter)
    ang = np.concatenate([row[:, None] * inv_freq, col[:, None] * inv_freq], axis=-1)
    cos, sin = np.cos(ang), np.sin(ang)
    cos_h = np.concatenate([cos, cos], axis=-1)
    sin_h = np.concatenate([-sin, sin], axis=-1)
    cos_t = np.tile(cos_h, (1, RET_HEADS))
    sin_t = np.tile(sin_h, (1, RET_HEADS))
    return (cos_t.astype(np.float32), sin_t.astype(np.float32),
            np.ascontiguousarray((cos_t * K_SCALE).T).astype(np.float32),
            np.ascontiguousarray((sin_t * K_SCALE).T).astype(np.float32))


@functools.lru_cache(maxsize=None)
def _filter_features():
    lag = np.abs(np.arange(2 * SEQ) - SEQ).astype(np.float64)
    t = lag / (SEQ - 1)
    bands = np.linspace(1e-4, HY_EMB_BANDS - 1, HY_EMB_BANDS)
    ang = 2.0 * math.pi * lag[:, None] * bands[None, :] / SEQ
    z = np.concatenate([t[:, None], np.cos(ang), -np.sin(ang)], axis=-1)
    zp = np.zeros((2 * SEQ, HY_FILTER_WIDTH), np.float64)
    zp[:, :z.shape[1]] = z
    max_decay = math.log(HY_TARGET) / HY_FAST_DECAY
    min_decay = math.log(HY_TARGET) / HY_SLOW_DECAY
    absdelta = np.abs(np.linspace(min_decay, max_decay, HY_WIDTH))[:, None]
    return (np.ascontiguousarray(zp.T).astype(np.float32), t[None, :].astype(np.float32),
            absdelta.astype(np.float32))


@functools.lru_cache(maxsize=None)
def _dft_matrices():
    p = CONV_BLOCK
    n = 2 * p
    f = np.arange(p, dtype=np.float64)[:, None] + 0.5
    t = np.arange(n, dtype=np.float64)[None, :]
    theta = 2.0 * math.pi * f * t / n
    fwd = np.concatenate([np.cos(theta), -np.sin(theta)], axis=0)
    th_out = theta[:, p:].T
    inv = np.concatenate([np.cos(th_out), -np.sin(th_out)], axis=1) / p
    return fwd.astype(np.float32), inv.astype(np.float32)


def _mod_kernel(c_ref, w_ref, b_ref, o_ref):
    s = _silu(c_ref[...]).astype(BF16)
    o_ref[...] = _dot(s, w_ref[...].astype(BF16)) + b_ref[...]


def _mod_call(cc, w_mod, b_mod):
    ncol = MOD_EARLY_COLS
    blk = ncol // 2
    return pl.pallas_call(
        _mod_kernel,
        grid=(ncol // blk,),
        in_specs=[pl.BlockSpec((MOD_ROWS, D_MODEL), lambda j: (0, 0)),
                  pl.BlockSpec((D_MODEL, blk), lambda j: (0, j)),
                  pl.BlockSpec((1, blk), lambda j: (0, j))],
        out_specs=pl.BlockSpec((MOD_ROWS, blk), lambda j: (0, j)),
        out_shape=jax.ShapeDtypeStruct((MOD_ROWS, ncol), F32),
        compiler_params=pltpu.CompilerParams(vmem_limit_bytes=VMEM_LIMIT),
        name="mod",
    )(cc, w_mod, b_mod)


def _ctx_kernel(lg_ref, x_ref, sh_ref, sc_ref, n1_ref, wkt_ref, wv_ref, s_ref):
    pos = lax.broadcasted_iota(jnp.int32, (1, CTX_LEN), 1).astype(F32)
    w_f = [jnp.exp(lg_ref[0, hh] * (CTX_LEN - 1.0 - pos)) for hh in range(RET_HEADS)]
    w_b = [jnp.exp(lg_ref[1, hh] * pos) for hh in range(RET_HEADS)]
    wkt, wv = wkt_ref[...].astype(BF16), wv_ref[...].astype(BF16)
    for bb in range(CTX_PER_STEP):
        h = _norm_mod(x_ref[bb], n1_ref[...], sh_ref[0:1, :], sc_ref[0:1, :]).astype(BF16)
        kt = lax.dot_general(wkt, h, _NT, preferred_element_type=F32) * K_SCALE
        v = _dot(h, wv)
        for hh in range(RET_HEADS):
            kth = kt[hh * RET_QK_DIM:(hh + 1) * RET_QK_DIM, :]
            vh = v[:, hh * RET_V_DIM:(hh + 1) * RET_V_DIM].astype(BF16)
            s_ref[bb, hh, 0:RET_QK_DIM, :] = _dot((kth * w_f[hh]).astype(BF16), vh)
            s_ref[bb, hh, RET_QK_DIM:2 * RET_QK_DIM, :] = _dot((kth * w_b[hh]).astype(BF16), vh)


def _ctx_call(lg, ctx, mod3, norm1, w_kt, w_v):
    return pl.pallas_call(
        _ctx_kernel,
        grid=(BATCH // CTX_PER_STEP,),
        in_specs=[pl.BlockSpec(memory_space=pltpu.SMEM),
                  pl.BlockSpec((CTX_PER_STEP, CTX_LEN, D_MODEL), lambda b: (b, 0, 0)),
                  _mod_rows(BATCH // SUBLANES, 0),
                  _mod_rows(BATCH // SUBLANES, 1),
                  _resident((1, D_MODEL)),
                  _resident((RET_HEADS * RET_QK_DIM, D_MODEL)),
                  _resident_cols(D_MODEL, V_OFF, RET_WIDTH)],
        out_specs=pl.BlockSpec((CTX_PER_STEP, RET_HEADS, 2 * RET_QK_DIM, RET_V_DIM), lambda b: (b, 0, 0, 0)),
        out_shape=jax.ShapeDtypeStruct((BATCH, RET_HEADS, 2 * RET_QK_DIM, RET_V_DIM), F32),
        compiler_params=pltpu.CompilerParams(vmem_limit_bytes=VMEM_LIMIT),
        name="ctx",
    )(lg, ctx, mod3, mod3, norm1, w_kt, w_v)


def _swap_halves(x, axis):
    n = x.shape[axis]
    half = RET_QK_DIM // 2
    idx = lax.broadcasted_iota(jnp.int32, x.shape, axis)
    first = (idx & (RET_QK_DIM - 1)) < half
    return jnp.where(first, pltpu.roll(x, n - half, axis), pltpu.roll(x, half, axis))


def _inproj_kernel(x_ref, sh_ref, sc_ref, n1_ref, why_ref, wq_ref, wv_ref, wg_ref, wkt_ref,
                   cq_ref, sq_ref, ck_ref, sk_ref,
                   zhy_ref, q_ref, kt_ref, v_ref, g_ref):
    pieces = [slice(r, r + IN_PIECE) for r in range(0, IN_TILE, IN_PIECE)]
    why, wq, wv, wg, wkt = (r[...].astype(BF16) for r in (why_ref, wq_ref, wv_ref, wg_ref, wkt_ref))
    batch_row = pl.ds(pl.program_id(0), 1)
    shift, scale = sh_ref[batch_row, :], sc_ref[batch_row, :]
    hbs = [_norm_mod(x_ref[0, rows, :], n1_ref[...], shift, scale).astype(BF16) for rows in pieces]
    for rows, hb in zip(pieces, hbs):
        zhy_ref[0, rows, :] = _dot(hb, why).astype(BF16)
        v_ref[0, rows, :] = _dot(hb, wv).astype(BF16)
        g_ref[0, rows, :] = _dot(hb, wg).astype(BF16)
        q = _dot(hb, wq)
        q = q * cq_ref[rows, :] + _swap_halves(q, 1) * sq_ref[rows, :]
        for hh in range(RET_HEADS):
            q_ref[0, hh, rows, :] = q[:, hh * RET_QK_DIM:(hh + 1) * RET_QK_DIM].astype(BF16)
        kt = lax.dot_general(wkt, hb, _NT, preferred_element_type=F32)
        kt = kt * ck_ref[:, rows] + _swap_halves(kt, 0) * sk_ref[:, rows]
        kt_ref[0, :, rows] = kt.astype(BF16)


def _inproj_call(x, mod3, norm1, w_in, w_kt, rope):
    cq, sq, ck, sk = rope
    t = IN_TILE
    qk = RET_HEADS * RET_QK_DIM
    return pl.pallas_call(
        _inproj_kernel,
        grid=(BATCH, SEQ // t),
        in_specs=[pl.BlockSpec((1, t, D_MODEL), lambda b, i: (b, i, 0)),
                  _mod_rows(0, 0),
                  _mod_rows(0, 1),
                  _resident((1, D_MODEL)),
                  _resident_cols(D_MODEL, 0, HY_COLS),
                  _resident_cols(D_MODEL, Q_OFF, qk),
                  _resident_cols(D_MODEL, V_OFF, RET_WIDTH),
                  _resident_cols(D_MODEL, G_OFF, RET_WIDTH),
                  _resident((qk, D_MODEL)),
                  pl.BlockSpec((t, qk), lambda b, i: (i, 0)),
                  pl.BlockSpec((t, qk), lambda b, i: (i, 0)),
                  pl.BlockSpec((qk, t), lambda b, i: (0, i)),
                  pl.BlockSpec((qk, t), lambda b, i: (0, i))],
        out_specs=[pl.BlockSpec((1, t, HY_COLS), lambda b, i: (b, i, 0)),
                   pl.BlockSpec((1, RET_HEADS, t, RET_QK_DIM), lambda b, i: (b, 0, i, 0)),
                   pl.BlockSpec((1, qk, t), lambda b, i: (b, 0, i)),
                   pl.BlockSpec((1, t, RET_WIDTH), lambda b, i: (b, i, 0)),
                   pl.BlockSpec((1, t, RET_WIDTH), lambda b, i: (b, i, 0))],
        out_shape=[jax.ShapeDtypeStruct((BATCH, SEQ, HY_COLS), BF16),
                   jax.ShapeDtypeStruct((BATCH, RET_HEADS, SEQ, RET_QK_DIM), BF16),
                   jax.ShapeDtypeStruct((BATCH, qk, SEQ), BF16),
                   jax.ShapeDtypeStruct((BATCH, SEQ, RET_WIDTH), BF16),
                   jax.ShapeDtypeStruct((BATCH, SEQ, RET_WIDTH), BF16)],
        compiler_params=pltpu.CompilerParams(vmem_limit_bytes=VMEM_LIMIT),
        name="inproj",
    )(x, mod3, mod3, norm1, w_in, w_in, w_in, w_in, w_kt, cq, sq, ck, sk)


def _filter_mlp_kernel(zt_ref, t_ref, w1t_ref, b1_ref, f1_ref, w2t_ref, b2_ref, f2_ref, w3t_ref, adel_ref,
                       kt_ref):
    hid = jnp.sin(f1_ref[...] * (_dot_hi(w1t_ref[...], zt_ref[...]) + b1_ref[...]))
    hid = jnp.sin(f2_ref[...] * (_dot_hi(w2t_ref[...], hid) + b2_ref[...])).astype(BF16)
    c = HY_WIDTH
    for half in range(2):
        lags = slice(half * SEQ, (half + 1) * SEQ)
        window = jnp.exp(-adel_ref[...] * t_ref[:, lags])
        direction = 1 - half
        for o in range(HY_ORDER):
            r0 = (direction * HY_ORDER + o) * c
            w3 = w3t_ref[r0:r0 + c, :].astype(BF16)
            kt_ref[o, :, lags] = (_dot(w3, hid[:, lags]) * window).astype(BF16)


def _filter_mlp_call(zt, t_row, w1t, b1, f1, w2t, b2, f2, w3t, absdelta):
    fw = HY_FILTER_WIDTH
    n_out = 2 * HY_ORDER * HY_WIDTH
    return pl.pallas_call(
        _filter_mlp_kernel,
        grid=(1,),
        in_specs=[_resident((fw, 2 * SEQ)), _resident((1, 2 * SEQ)),
                  _resident((fw, fw)), _resident((fw, 1)), _resident((fw, 1)),
                  _resident((fw, fw)), _resident((fw, 1)), _resident((fw, 1)),
                  _resident((n_out, fw)), _resident((HY_WIDTH, 1))],
        out_specs=pl.BlockSpec((HY_ORDER, HY_WIDTH, 2 * SEQ), lambda i: (0, 0, 0)),
        out_shape=jax.ShapeDtypeStruct((HY_ORDER, HY_WIDTH, 2 * SEQ), BF16),
        compiler_params=pltpu.CompilerParams(vmem_limit_bytes=VMEM_LIMIT),
        name="filter_mlp",
    )(zt, t_row, w1t, b1, f1, w2t, b2, f2, w3t, absdelta)


_FILTER_PLANES = (
    {0: 1},
    {-1: 1, 0: -1},
    {1: 1, 0: -1},
    {-2: 1, 0: -1},
    {-3: 1, -1: -1, -2: -1, 0: 1},
    {-1: 1, 1: -1, -2: -1, 0: 1},
    {2: 1, 0: -1},
    {1: 1, -1: -1, 2: -1, 0: 1},
    {3: 1, 1: -1, 2: -1, 0: 1},
)
DFT_BLOCKS_PER_STEP = 4
PLANES_PER_STEP = 3


def _filter_dft_kernel(kt_ref, fwd_ref, h_ref, t_scr):
    p = CONV_BLOCK
    n_dft_steps = 2 * N_CONV_BLOCKS // DFT_BLOCKS_PER_STEP
    step = pl.program_id(1)

    for s in range(n_dft_steps):
        @pl.when(step == s)
        def _(s=s):
            for k in range(DFT_BLOCKS_PER_STEP):
                t_scr[s * DFT_BLOCKS_PER_STEP + k] = lax.dot_general(
                    fwd_ref[...], kt_ref[0, :, k * p:(k + 1) * p], _NT, preferred_element_type=F32)
            if s == 0:
                for k in range(PLANES_PER_STEP):
                    h_ref[0, k] = t_scr[0]

    row = lax.broadcasted_iota(jnp.int32, (p, 1), 0)
    sign = (1 - 2 * (row & 1)).astype(F32)
    re, im = slice(0, p), slice(p, 2 * p)

    def plane(k, coefs):
        def comb(shift, rows):
            acc = None
            for d, c in coefs.items():
                term = t_scr[d + N_CONV_BLOCKS - 1 + shift, rows, :]
                if acc is None:
                    acc = term
                else:
                    acc = acc + term if c > 0 else acc - term
            return acc

        h_ref[0, k, re, :] = comb(0, re) + sign * comb(1, im)
        h_ref[0, k, im, :] = comb(0, im) - sign * comb(1, re)

    for s in range(len(_FILTER_PLANES) // PLANES_PER_STEP):
        @pl.when(step == n_dft_steps + s)
        def _(s=s):
            for k in range(PLANES_PER_STEP):
                plane(k, _FILTER_PLANES[s * PLANES_PER_STEP + k])


def _filter_dft_call(kt, fwd_u):
    p = CONV_BLOCK
    n_blk = 2 * N_CONV_BLOCKS
    n_planes = len(_FILTER_PLANES)
    n_dft_steps = n_blk // DFT_BLOCKS_PER_STEP
    return pl.pallas_call(
        _filter_dft_kernel,
        grid=(HY_ORDER, n_dft_steps + n_planes // PLANES_PER_STEP),
        in_specs=[pl.BlockSpec((1, HY_WIDTH, DFT_BLOCKS_PER_STEP * p),
                               lambda o, s: (o, 0, jnp.minimum(s, n_dft_steps - 1))),
                  _resident((2 * p, p))],
        out_specs=pl.BlockSpec((1, PLANES_PER_STEP, 2 * p, HY_WIDTH),
                               lambda o, s: (o, jnp.maximum(s - n_dft_steps, 0), 0, 0)),
        out_shape=jax.ShapeDtypeStruct((HY_ORDER, n_planes, 2 * p, HY_WIDTH), F32),
        scratch_shapes=[pltpu.VMEM((n_blk, 2 * p, HY_WIDTH), F32)],
        compiler_params=pltpu.CompilerParams(vmem_limit_bytes=VMEM_LIMIT),
        name="filter_dft",
    )(kt, fwd_u)


def _conv3_rows(ref, j, n_blocks, rows, w, b):
    main = ref[0, j * rows:(j + 1) * rows, :].astype(F32)
    cols = main.shape[1]
    zeros = jnp.zeros((HALO, cols), F32)
    prev = ref[0, j * rows - HALO:j * rows, :].astype(F32) if j > 0 else zeros
    nxt = ref[0, (j + 1) * rows:(j + 1) * rows + HALO, :].astype(F32) if j < n_blocks - 1 else zeros
    ext = jnp.concatenate([prev, main, nxt], axis=0)
    n = rows + 2 * HALO
    before = pltpu.roll(ext, 1, 0)[HALO:HALO + rows]
    after = pltpu.roll(ext, n - 1, 0)[HALO:HALO + rows]
    return before * w[0:1] + main * w[1:2] + after * w[2:3] + b


def _hyena_kernel(u_ref, zg_ref, cwu_ref, cbu_ref, cwg_ref, cbg_ref, skip_ref, h_ref, fwd_ref, inv_ref,
                  o_ref, uf_scr, ub_scr, y_scr, *, conv_u):
    p = CONV_BLOCK
    nb = N_CONV_BLOCKS

    def prepare(j):
        if conv_u:
            ub_scr[j * p:(j + 1) * p, :] = _conv3_rows(u_ref, j, nb, p, cwu_ref[...], cbu_ref[...]).astype(BF16)

    def u_block(j):
        return ub_scr[j * p:(j + 1) * p, :] if conv_u else u_ref[0, j * p:(j + 1) * p, :]

    prepare(0)
    for j in range(nb):
        if j + 1 < nb:
            prepare(j + 1)
        for r in range(2 * p // DFT_ROWS):
            rows = slice(r * DFT_ROWS, (r + 1) * DFT_ROWS)
            uf_scr[j, rows, :] = _dot(fwd_ref[rows, :], u_block(j))

    def cadd(a, b):
        return a[0] + b[0], a[1] + b[1]

    def cmul(m, x):
        return m[0] * x[0] - m[1] * x[1], m[0] * x[1] + m[1] * x[0]

    def toeplitz2(k0, x0, x1, re, im, ln):
        m0, mu, ml = ((h_ref[0, k0 + t, re, ln], h_ref[0, k0 + t, im, ln]) for t in range(3))
        p1 = cmul(m0, cadd(x0, x1))
        return cadd(p1, cmul(mu, x1)), cadd(p1, cmul(ml, x0))

    assert nb == 4
    def spectral_rows(r0, ln):
        re = slice(r0, r0 + SUBLANES)
        im = slice(p + r0, p + r0 + SUBLANES)
        u = [(uf_scr[j, re, ln], uf_scr[j, im, ln]) for j in range(nb)]
        d0, d1 = toeplitz2(0, cadd(u[0], u[2]), cadd(u[1], u[3]), re, im, ln)
        b0, b1 = toeplitz2(3, u[2], u[3], re, im, ln)
        c0, c1 = toeplitz2(6, u[0], u[1], re, im, ln)
        return cadd(d0, b0), cadd(d1, b1), cadd(d0, c0), cadd(d1, c1)

    for r in range(p // FREQ_ROWS):
        re = slice(r * FREQ_ROWS, (r + 1) * FREQ_ROWS)
        im = slice(p + r * FREQ_ROWS, p + (r + 1) * FREQ_ROWS)
        for cb in range(HY_WIDTH // FREQ_LANES):
            ln = slice(cb * FREQ_LANES, (cb + 1) * FREQ_LANES)
            parts = [spectral_rows(r * FREQ_ROWS + r0, ln) for r0 in range(0, FREQ_ROWS, SUBLANES)]
            for i in range(nb):
                y_scr[i, re, ln] = jnp.concatenate([part[i][0] for part in parts], axis=0).astype(BF16)
                y_scr[i, im, ln] = jnp.concatenate([part[i][1] for part in parts], axis=0).astype(BF16)

    for i in range(nb):
        gate = _conv3_rows(zg_ref, i, nb, p, cwg_ref[...], cbg_ref[...])
        y = _dot(inv_ref[...], y_scr[i])
        o_ref[0, i * p:(i + 1) * p, :] = (
            gate * (y + u_block(i).astype(F32) * skip_ref[...])).astype(BF16)


def _hyena_call(u, u_col, zhy, gate_col, conv_w, conv_b, skip, g_all, order, fwd_u, inv, conv_u):
    p = CONV_BLOCK
    c = HY_WIDTH
    ucol = u_col if conv_u else 0
    cwu = conv_w[:, ucol * c:(ucol + 1) * c]
    cbu = conv_b[:, ucol * c:(ucol + 1) * c]
    cwg = conv_w[:, gate_col * c:(gate_col + 1) * c]
    cbg = conv_b[:, gate_col * c:(gate_col + 1) * c]
    return pl.pallas_call(
        functools.partial(_hyena_kernel, conv_u=conv_u),
        grid=(BATCH,),
        in_specs=[pl.BlockSpec((1, SEQ, c), lambda b: (b, 0, u_col)),
                  pl.BlockSpec((1, SEQ, c), lambda b: (b, 0, gate_col)),
                  _resident((3, c)), _resident((1, c)), _resident((3, c)), _resident((1, c)),
                  _resident((1, c)),
                  pl.BlockSpec((1, len(_FILTER_PLANES), 2 * p, c), lambda b: (order, 0, 0, 0),
                               pipeline_mode=pl.Buffered(1)),
                  _resident((2 * p, p)),
                  _resident((p, 2 * p))],
        out_specs=pl.BlockSpec((1, SEQ, c), lambda b: (b, 0, 0)),
        out_shape=jax.ShapeDtypeStruct((BATCH, SEQ, c), BF16),
        scratch_shapes=[pltpu.VMEM((N_CONV_BLOCKS, 2 * p, c), F32),
                        pltpu.VMEM((SEQ, c) if conv_u else (SUBLANES * 2, LANES), BF16),
                        pltpu.VMEM((N_CONV_BLOCKS, 2 * p, c), BF16)],
        compiler_params=pltpu.CompilerParams(vmem_limit_bytes=HYENA_VMEM_LIMIT),
        name="hyena%d" % order,
    )(u, zhy, cwu, cbu, cwg, cbg, skip, g_all, fwd_u, inv)


def _ret_kernel(lg_ref, q_ref, kt_ref, v_ref, g_ref, s_ref, wup_ref, wdn_ref, wout_ref, cc_ref, wmod_ref, bmod_ref,
                o_ref, wup_b_ref, wdn_b_ref, wout_b_ref, modl_ref, b_scr, decay_scr, qw_scr):
    wup_b_ref[...] = wup_ref[...].astype(BF16)
    wdn_b_ref[...] = wdn_ref[...].astype(BF16)
    wout_b_ref[...] = wout_ref[...].astype(BF16)
    modl_ref[...] = _dot(_silu(cc_ref[...]).astype(BF16), wmod_ref[...].astype(BF16)) + bmod_ref[...]
    c = RET_CHUNK
    nc = SEQ // c
    dk = RET_QK_DIM
    dv = RET_V_DIM
    heads = range(RET_HEADS_PER_STEP)
    lg_f = [lg_ref[0, pl.program_id(0) * RET_HEADS_PER_STEP + hh] for hh in heads]
    lg_b = [lg_ref[1, pl.program_id(0) * RET_HEADS_PER_STEP + hh] for hh in heads]

    @pl.when(pl.program_id(1) == 0)
    def _():
        ii = lax.broadcasted_iota(jnp.int32, (c, c), 0).astype(F32)
        jj = lax.broadcasted_iota(jnp.int32, (c, c), 1).astype(F32)
        dif = ii - jj
        pos_q = lax.broadcasted_iota(jnp.int32, (c, dk), 0).astype(F32)
        for hh in heads:
            decay_scr[hh] = jnp.where(dif >= 0.0, jnp.exp(lg_f[hh] * jnp.maximum(dif, 0.0)),
                                      jnp.exp(lg_b[hh] * jnp.maximum(-dif, 0.0)))
            qw_scr[hh, 0] = jnp.exp(lg_f[hh] * (pos_q + 1.0))
            qw_scr[hh, 1] = jnp.exp(lg_b[hh] * (c - pos_q))

    pos_r = lax.broadcasted_iota(jnp.int32, (1, c), 1).astype(F32)
    ones = jnp.ones((1, dv), F32)
    kw_f = [jnp.exp(lg_f[hh] * (c - 1.0 - pos_r)) for hh in heads]
    kw_b = [jnp.exp(lg_b[hh] * pos_r) for hh in heads]
    dec_f = [jnp.exp(lg_f[hh] * float(c) * ones) for hh in heads]
    dec_b = [jnp.exp(lg_b[hh] * float(c) * ones) for hh in heads]

    def kt_chunk(hh, n):
        return kt_ref[0, hh * dk:(hh + 1) * dk, n * c:(n + 1) * c]

    def v_chunk(hh, n):
        return v_ref[0, n * c:(n + 1) * c, hh * dv:(hh + 1) * dv]

    state = [s_ref[0, hh, dk:2 * dk, :] for hh in heads]
    for hh in heads:
        b_scr[hh, nc - 1] = state[hh]
    for n in range(nc - 1, 0, -1):
        for hh in heads:
            ktn = (kt_chunk(hh, n).astype(F32) * kw_b[hh]).astype(BF16)
            state[hh] = state[hh] * dec_b[hh] + _dot(ktn, v_chunk(hh, n))
            b_scr[hh, n - 1] = state[hh]

    state = [s_ref[0, hh, 0:dk, :] for hh in heads]
    for n in range(nc):
        for hh in heads:
            qn = q_ref[0, hh, n * c:(n + 1) * c, :]
            ktn = kt_chunk(hh, n)
            vn = v_chunk(hh, n)
            scores = (_dot(qn, ktn) * decay_scr[hh]).astype(BF16)
            qf = qn.astype(F32)
            o = _dot(scores, vn)
            o = o + _dot((qf * qw_scr[hh, 0]).astype(BF16), state[hh].astype(BF16))
            o = o + _dot((qf * qw_scr[hh, 1]).astype(BF16), b_scr[hh, n].astype(BF16))
            state[hh] = state[hh] * dec_f[hh] + _dot((ktn.astype(F32) * kw_f[hh]).astype(BF16), vn)
            o = o * lax.rsqrt(jnp.mean(o * o, axis=-1, keepdims=True) + EPS)
            gate = g_ref[0, n * c:(n + 1) * c, hh * dv:(hh + 1) * dv].astype(F32)
            o_ref[0, n * c:(n + 1) * c, hh * dv:(hh + 1) * dv] = (_silu(gate) * o).astype(BF16)


def _ret_call(lg, q, kt, v, g, s, w_up, w_dn, w_out, cc, w_mod, b_mod):
    dk, dv = RET_QK_DIM, RET_V_DIM
    hp = RET_HEADS_PER_STEP
    assert RET_HEADS == hp

    def row_slab(w):
        return pl.BlockSpec((w.shape[0] // BATCH, w.shape[1]), lambda h, b: (b, 0))

    up_slab, dn_slab, out_slab = row_slab(w_up), row_slab(w_dn), row_slab(w_out)
    mcols = MOD_LATE_COLS // BATCH
    mod_first = MOD_EARLY_COLS // mcols
    mod_slab = pl.BlockSpec((MOD_ROWS, mcols), lambda h, b: (0, b))
    return pl.pallas_call(
        _ret_kernel,
        grid=(RET_HEADS // hp, BATCH),
        in_specs=[pl.BlockSpec(memory_space=pltpu.SMEM),
                  pl.BlockSpec((1, hp, SEQ, dk), lambda h, b: (b, h, 0, 0)),
                  pl.BlockSpec((1, hp * dk, SEQ), lambda h, b: (b, h, 0)),
                  pl.BlockSpec((1, SEQ, hp * dv), lambda h, b: (b, 0, h)),
                  pl.BlockSpec((1, SEQ, hp * dv), lambda h, b: (b, 0, h)),
                  pl.BlockSpec((1, hp, 2 * dk, dv), lambda h, b: (b, h, 0, 0)),
                  up_slab, dn_slab, out_slab,
                  _resident((MOD_ROWS, D_MODEL)),
                  pl.BlockSpec((D_MODEL, mcols), lambda h, b: (0, mod_first + b)),
                  pl.BlockSpec((1, mcols), lambda h, b: (0, mod_first + b))],
        out_specs=[pl.BlockSpec((1, SEQ, hp * dv), lambda h, b: (b, 0, h)), up_slab, dn_slab, out_slab, mod_slab],
        out_shape=[jax.ShapeDtypeStruct((BATCH, SEQ, RET_WIDTH), BF16),
                   jax.ShapeDtypeStruct(w_up.shape, BF16),
                   jax.ShapeDtypeStruct(w_dn.shape, BF16),
                   jax.ShapeDtypeStruct(w_out.shape, BF16),
                   jax.ShapeDtypeStruct((MOD_ROWS, MOD_LATE_COLS), F32)],
        scratch_shapes=[pltpu.VMEM((hp, SEQ // RET_CHUNK, dk, dv), F32),
                        pltpu.VMEM((hp, RET_CHUNK, RET_CHUNK), F32),
                        pltpu.VMEM((hp, 2, RET_CHUNK, dk), F32)],
        compiler_params=pltpu.CompilerParams(vmem_limit_bytes=VMEM_LIMIT),
        name="ret",
    )(lg, q, kt, v, g, s, w_up, w_dn, w_out, cc, w_mod, b_mod)


def _ffn_kernel(x_ref, xp_ref, xn_ref, yh_ref, yhp_ref, yhn_ref, yr_ref, yrp_ref, yrn_ref,
                g1_ref, sh_ref, sc_ref, g2_ref, n2_ref, nf_ref,
                woh_ref, wor_ref, wup_ref, cw_ref, cb_ref, wdn_ref,
                o_ref, hb_scr, x1_scr, av_scr, ag_scr, act_scr):
    t = TOK_TILE
    i = pl.program_id(1)
    nt = pl.num_programs(1)
    th = t // 2
    tile_halves = (slice(0, th), slice(th, t))
    batch_row = pl.ds(pl.program_id(0), 1)
    gate1, shift2, scale2, gate2 = (r[batch_row, :] for r in (g1_ref, sh_ref, sc_ref, g2_ref))

    def mixed(xr, yh, yr, rows=slice(None)):
        return xr[0, rows, :] + gate1 * (_dot(yh[0, rows, :], woh_ref[...]) + _dot(yr[0, rows, :], wor_ref[...]))

    def hidden(x1):
        return _norm_mod(x1, n2_ref[...], shift2, scale2)

    half = (t + 2 * HALO) // 2
    up_halves = (slice(0, half), slice(half, 2 * half))

    def up(slot, cblk, rows):
        c0 = cblk * FFN_COLS
        hb = hb_scr[rows, :]
        av_scr[slot, rows, :] = _dot(hb, wup_ref[:, c0:c0 + FFN_COLS])
        ag_scr[slot, rows, :] = _dot(hb, wup_ref[:, D_FF + c0:D_FF + c0 + FFN_COLS])

    for rows in tile_halves:
        x1_scr[rows, :] = mixed(x_ref, yh_ref, yr_ref, rows)
    hp = hidden(mixed(xp_ref, yhp_ref, yrp_ref))
    hn = hidden(mixed(xn_ref, yhn_ref, yrn_ref))
    hb_scr[0:HALO, :] = jnp.where(i > 0, hp, 0.0).astype(BF16)
    hb_scr[HALO + t:2 * HALO + t, :] = jnp.where(i < nt - 1, hn, 0.0).astype(BF16)
    hb_scr[HALO:HALO + th, :] = hidden(x1_scr[tile_halves[0], :]).astype(BF16)
    up(0, 0, up_halves[0])
    hb_scr[HALO + th:HALO + t, :] = hidden(x1_scr[tile_halves[1], :]).astype(BF16)
    up(0, 0, up_halves[1])

    def conv(scr, slot, col):
        w = cw_ref[:, col:col + FFN_COLS]
        return (scr[slot, HALO - 1:HALO - 1 + t, :] * w[0:1] + scr[slot, HALO:HALO + t, :] * w[1:2]
                + scr[slot, HALO + 1:HALO + 1 + t, :] * w[2:3] + cb_ref[:, col:col + FFN_COLS])

    n_blk = D_FF // FFN_COLS
    for cblk in range(n_blk):
        c0 = cblk * FFN_COLS
        slot = cblk % 2
        if cblk + 1 < n_blk:
            for rows in up_halves:
                up(1 - slot, cblk + 1, rows)
        act_scr[:, c0:c0 + FFN_COLS] = (
            _silu(conv(ag_scr, slot, D_FF + c0)) * conv(av_scr, slot, c0)).astype(BF16)

    down = [_dot(act_scr[rows, :], wdn_ref[...]) for rows in tile_halves]
    for rows, ffn in zip(tile_halves, down):
        x2 = x1_scr[rows, :] + gate2 * ffn
        o_ref[0, rows, :] = x2 * lax.rsqrt(jnp.mean(x2 * x2, axis=-1, keepdims=True) + EPS) * nf_ref[...]


def _ffn_call(x, y_hy, y_ret, mod3, norm2, norm_f, w_oh, w_or, w_up, conv_w, conv_b, w_dn):
    t = TOK_TILE
    r = t // HALO
    last = SEQ // HALO - 1

    def main(width):
        return pl.BlockSpec((1, t, width), lambda b, i: (b, i, 0))

    def prev(width):
        return pl.BlockSpec((1, HALO, width), lambda b, i: (b, jnp.maximum(i * r - 1, 0), 0))

    def nxt(width):
        return pl.BlockSpec((1, HALO, width), lambda b, i: (b, jnp.minimum((i + 1) * r, last), 0))

    def modrow(k):
        return _mod_rows(0, k)

    return pl.pallas_call(
        _ffn_kernel,
        grid=(BATCH, SEQ // t),
        in_specs=[main(D_MODEL), prev(D_MODEL), nxt(D_MODEL),
                  main(HY_WIDTH), prev(HY_WIDTH), nxt(HY_WIDTH),
                  main(RET_WIDTH), prev(RET_WIDTH), nxt(RET_WIDTH),
                  modrow(0), modrow(1), modrow(2), modrow(3),
                  _resident((1, D_MODEL)), _resident((1, D_MODEL)),
                  _resident_rows(0, HY_WIDTH, D_MODEL), _resident_rows(HY_WIDTH, RET_WIDTH, D_MODEL),
                  _resident((D_MODEL, 2 * D_FF)),
                  _resident((3, 2 * D_FF)), _resident((1, 2 * D_FF)),
                  _resident((D_FF, D_MODEL))],
        out_specs=pl.BlockSpec((1, t, D_MODEL), lambda b, i: (b, i, 0)),
        out_shape=jax.ShapeDtypeStruct((BATCH, SEQ, D_MODEL), F32),
        scratch_shapes=[pltpu.VMEM((t + 2 * HALO, D_MODEL), BF16),
                        pltpu.VMEM((t, D_MODEL), F32),
                        pltpu.VMEM((2, t + 2 * HALO, FFN_COLS), F32),
                        pltpu.VMEM((2, t + 2 * HALO, FFN_COLS), F32),
                        pltpu.VMEM((t, D_FF), BF16)],
        compiler_params=pltpu.CompilerParams(vmem_limit_bytes=VMEM_LIMIT),
        name="ffn",
    )(x, x, x, y_hy, y_hy, y_hy, y_ret, y_ret, y_ret, mod3, mod3, mod3, mod3,
      norm2, norm_f, w_oh, w_or, w_up, conv_w, conv_b, w_dn)


def kernel(x, c, ctx, c_ctx, w_mod, b_mod, norm1, w_in, hy_conv_w, hy_conv_b, hy_w1, hy_b1, hy_f1,
           hy_w2, hy_b2, hy_f2, hy_w3, hy_bias, ret_logit_f, ret_logit_b, w_out, norm2,
           ffn_w_up, ffn_conv_w, ffn_conv_b, ffn_w_down, norm_f):
    layer = 0
    rope = tuple(jnp.asarray(a) for a in _rope_tables())
    zt, t_row, absdelta = (jnp.asarray(a) for a in _filter_features())
    fwd_np, inv_np = _dft_matrices()
    fwd_u = jnp.asarray(fwd_np[:, :CONV_BLOCK]).astype(BF16)
    inv = jnp.asarray(inv_np).astype(BF16)

    w_in_b = w_in[layer]
    w_kt = w_in_b[:, K_OFF:V_OFF].T
    row = lambda a: a.reshape(1, -1)
    col = lambda a: a.reshape(-1, 1)
    w1p = jnp.pad(hy_w1[layer], ((0, HY_FILTER_WIDTH - hy_w1.shape[1]), (0, 0)))
    lg = jnp.stack([jax.nn.log_sigmoid(ret_logit_f[layer].astype(F32)),
                    jax.nn.log_sigmoid(ret_logit_b[layer].astype(F32))])

    cc = jnp.concatenate([c, c_ctx[None, :], jnp.zeros((MOD_ROWS - BATCH - 1, D_MODEL), F32)], axis=0)
    assert BATCH <= SUBLANES
    mod3 = _mod_call(cc, w_mod[layer], row(b_mod[layer]))
    norm1_r = row(norm1[layer])

    s_ctx = _ctx_call(lg, ctx, mod3, norm1_r, w_kt, w_in_b)
    zhy, q, kt, v, g = _inproj_call(x, mod3, norm1_r, w_in_b, w_kt, rope)

    k_two_sided = _filter_mlp_call(zt, t_row, w1p.T, col(hy_b1[layer]), col(hy_f1[layer]), hy_w2[layer].T,
                                   col(hy_b2[layer]), col(hy_f2[layer]), hy_w3[layer].T, absdelta)
    g_all = _filter_dft_call(k_two_sided, fwd_u)
    conv_w, conv_b = hy_conv_w[layer], row(hy_conv_b[layer])
    y1 = _hyena_call(zhy, 0, zhy, 1, conv_w, conv_b, hy_bias[layer][0:1], g_all, 0, fwd_u, inv, True)
    y_hy = _hyena_call(y1, 0, zhy, 2, conv_w, conv_b, hy_bias[layer][1:2], g_all, 1, fwd_u, inv, False)

    y_ret, w_up, w_dn, w_out_b, mod_late = _ret_call(lg, q, kt, v, g, s_ctx, ffn_w_up[layer], ffn_w_down[layer],
                                                     w_out[layer], cc, w_mod[layer], row(b_mod[layer]))

    return _ffn_call(x, y_hy, y_ret, mod_late, row(norm2[layer]), row(norm_f), w_out_b, w_out_b, w_up,
                     ffn_conv_w[layer], row(ffn_conv_b[layer]), w_dn)
```
